```python
import math
import jax, jax.numpy as jnp
from jax import lax
import numpy as np

D_MODEL = 1024
BATCH = 16
SEQ = 256
DEPTH = 2
DEC_BATCH = 8
DEC_SEQ = 1024
PAST_LEN = 256

GRID_W = 64
N_EVEN = (DEPTH + 1) // 2
N_ODD = DEPTH // 2
N_MOD = 6
GDN_HEADS = 4
GDN_DK = 128
GDN_DV = 128
SHORT_CONV = 5
CHUNK = 64
DIFF_HEADS = 4
DIFF_DH = 64
MLA_HEADS = 8
Q_LORA = 256
KV_LORA = 256
QK_NOPE = 128
QK_ROPE = 64
V_HEAD = 128
N_EXPERTS = 32
TOP_K = 4
D_FF = 1024
SWIGLU_LIMIT = 7.0
SWIGLU_ALPHA = 1.702
ROPE_BASE = 10000.0
EPS = 1e-6
Q_BLOCK = 128
GDN_QK_W = GDN_HEADS * GDN_DK
GDN_V_W = GDN_HEADS * GDN_DV
DIFF_QK_W = DIFF_HEADS * 2 * DIFF_DH
DIFF_V_W = DIFF_HEADS * 2 * DIFF_DH
EVEN_SPLIT = (2 * GDN_QK_W + GDN_V_W, GDN_V_W, 2 * GDN_HEADS, 2 * GDN_HEADS, DIFF_QK_W, DIFF_QK_W, DIFF_V_W)
IN_EVEN = 2 * GDN_QK_W + 2 * GDN_V_W + 4 * GDN_HEADS + 2 * DIFF_QK_W + DIFF_V_W
MIX_EVEN = GDN_V_W + DIFF_V_W
IN_ODD = Q_LORA + KV_LORA + QK_ROPE
MIX_ODD = MLA_HEADS * V_HEAD

kernel_name = 'hybrid_flow_backbone_ctx_and_denoise'

F32 = jnp.float32


def split_last(x, sizes):
    return jnp.split(x, np.cumsum(sizes)[:-1].tolist(), axis=-1)


def rmsnorm(x, g):
    x32 = x.astype(F32)
    y = x32 * lax.rsqrt(jnp.mean(x32 * x32, axis=-1, keepdims=True) + EPS)
    return (y * g.astype(F32)).astype(x.dtype)


def l2norm(x):
    x32 = x.astype(F32)
    return (x32 * lax.rsqrt(jnp.sum(x32 * x32, axis=-1, keepdims=True) + EPS)).astype(x.dtype)


def modulation(cond, w_mod, b_mod):
    m = (jax.nn.silu(cond) @ w_mod + b_mod)[:, None, :]
    return jnp.split(m, N_MOD, axis=-1)


def modulate(x, g, shift, scale):
    return rmsnorm(x, g) * (1.0 + scale) + shift


def rope_2d_tables(n_tokens, rot_dim):
    rows = n_tokens // GRID_W
    r = jnp.repeat(jnp.arange(rows, dtype=F32), GRID_W)
    col = jnp.tile(jnp.arange(GRID_W, dtype=F32), rows)
    quarter = rot_dim // 4
    inv = ROPE_BASE ** (-jnp.arange(quarter, dtype=F32) / quarter)
    ar = r[:, None] * inv
    ac = col[:, None] * inv
    ang = jnp.concatenate([ar, ar, ac, ac], axis=-1)
    return jnp.cos(ang), jnp.sin(ang)


def apply_rope_2d(x, cos, sin):
    shape = (1, x.shape[1]) + (1,) * (x.ndim - 3) + (x.shape[-1],)
    cos = cos.reshape(shape).astype(x.dtype)
    sin = sin.reshape(shape).astype(x.dtype)
    x1, x2, x3, x4 = jnp.split(x, 4, axis=-1)
    rot = jnp.concatenate([-x2, x1, -x4, x3], axis=-1)
    return x * cos + rot * sin


def short_conv(x, w):
    width, ch = w.shape
    return lax.conv_general_dilated(x, w[:, None, :].astype(x.dtype), window_strides=(1,),
                                    padding=[(width // 2, width // 2)],
                                    dimension_numbers=('NWC', 'WIO', 'NWC'), feature_group_count=ch)


def map_query_blocks(fn, *qs):
    b, t = qs[0].shape[:2]
    n = t // Q_BLOCK
    blocks = tuple(jnp.moveaxis(q.reshape((b, n, Q_BLOCK) + q.shape[2:]), 1, 0) for q in qs)
    out = lax.map(lambda xs: fn(*xs), blocks)
    return jnp.moveaxis(out, 0, 1).reshape((b, t) + out.shape[3:])


def gated_delta_rule(q, k, v, g, beta, s0):
    dt = v.dtype
    b, t, h, dk = q.shape
    dv = v.shape[-1]
    n = t // CHUNK
    f = lambda a: jnp.swapaxes(a.astype(F32).reshape((b, n, CHUNK, h) + a.shape[3:]), 2, 3)
    q = f(q) * dk ** -0.5
    k = f(k)
    v = f(v)
    g = jnp.cumsum(f(g), axis=-1)
    beta = f(beta)
    idx = jnp.arange(CHUNK)
    causal = idx[:, None] >= idx[None, :]
    strict = idx[:, None] > idx[None, :]
    gdiff = g[..., :, None] - g[..., None, :]
    decay = jnp.where(causal, jnp.exp(jnp.where(causal, gdiff, 0.0)), 0.0)
    kb = k * beta[..., None]
    lower = jnp.where(strict, jnp.einsum('bnhid,bnhjd->bnhij', kb, k) * decay, 0.0)
    eye = jnp.eye(CHUNK, dtype=F32)
    tinv = lax.linalg.triangular_solve(eye + lower, jnp.broadcast_to(eye, lower.shape),
                                       left_side=True, lower=True, unit_diagonal=True)
    u = tinv @ (v * beta[..., None])
    w = tinv @ (kb * jnp.exp(g)[..., None])
    intra = jnp.einsum('bnhid,bnhjd->bnhij', q, k) * decay
    qg = q * jnp.exp(g)[..., None]
    kg = k * jnp.exp(g[..., -1:] - g)[..., None]
    glast = jnp.exp(g[..., -1])

    def step(s, xs):
        qg_i, kg_i, u_i, w_i, a_i, gl_i = xs
        v_new = u_i - w_i @ s
        o = qg_i @ s + a_i @ v_new
        s = s * gl_i[..., None, None] + jnp.swapaxes(kg_i, -1, -2) @ v_new
        return s, o

    xs = tuple(jnp.moveaxis(a, 1, 0) for a in (qg, kg, u, w, intra, glast))
    s, o = lax.scan(step, s0.astype(F32), xs)
    o = jnp.swapaxes(jnp.moveaxis(o, 0, 1), 2, 3).reshape(b, t, h, dv)
    return o.astype(dt), s.astype(dt)


def diff_attention(q, k, v, lam):
    scale = DIFF_DH ** -0.5

    def block(qb):
        s = jnp.einsum('bqhmd,bkhmd->bhmqk', qb, k).astype(F32) * scale
        p = jax.nn.softmax(s, axis=-1)
        p = (p[:, :, 0] - lam * p[:, :, 1]).astype(v.dtype)
        return jnp.einsum('bhqk,bkhe->bqhe', p, v)

    return map_query_blocks(block, q)


def delta_diff_mixer(h, w_in, conv_w, a_log, dt_bias, gdn_norm_g, diff_lam, diff_norm_g, w_out, lam_init, ctx, rope):
    b, t, _ = h.shape
    qkv, z, a, bt, qd, kd, vd = split_last(h @ w_in, EVEN_SPLIT)
    qkv = jax.nn.silu(short_conv(qkv, conv_w))
    qa, ka, va = split_last(qkv, (GDN_QK_W, GDN_QK_W, GDN_V_W))
    qa = l2norm(qa.reshape(b, t, GDN_HEADS, GDN_DK))
    ka = l2norm(ka.reshape(b, t, GDN_HEADS, GDN_DK))
    va = va.reshape(b, t, GDN_HEADS, GDN_DV)
    a32 = a.astype(F32).reshape(b, t, 2, GDN_HEADS)
    g = -jnp.exp(a_log.astype(F32)) * jax.nn.softplus(a32 + dt_bias.astype(F32))
    beta = jax.nn.sigmoid(bt.astype(F32)).reshape(b, t, 2, GDN_HEADS)
    if ctx is None:
        s0 = jnp.zeros((b, 2, GDN_HEADS, GDN_DK, GDN_DV), h.dtype)
    else:
        s0 = ctx[0]
    flip = lambda x: jnp.flip(x, axis=1)
    o_f, s_f = gated_delta_rule(qa, ka, va, g[:, :, 0], beta[:, :, 0], s0[:, 0])
    o_b, s_b = gated_delta_rule(flip(qa), flip(ka), flip(va), flip(g[:, :, 1]), flip(beta[:, :, 1]), s0[:, 1])
    o_a = rmsnorm(o_f + flip(o_b), gdn_norm_g) * jax.nn.silu(z.reshape(b, t, GDN_HEADS, GDN_DV))
    qd = qd.reshape(b, t, DIFF_HEADS, 2, DIFF_DH)
    kd = kd.reshape(b, t, DIFF_HEADS, 2, DIFF_DH)
    vd = vd.reshape(b, t, DIFF_HEADS, 2 * DIFF_DH)
    k_all, v_all = kd, vd
    if ctx is not None:
        cos, sin = rope
        qd = apply_rope_2d(qd, cos, sin)
        k_all = jnp.concatenate([ctx[1], apply_rope_2d(kd, cos, sin)], axis=1)
        v_all = jnp.concatenate([ctx[2], vd], axis=1)
    lq1, lk1, lq2, lk2 = diff_lam.astype(F32)
    lam = jnp.exp(jnp.sum(lq1 * lk1)) - jnp.exp(jnp.sum(lq2 * lk2)) + lam_init
    o_d = rmsnorm(diff_attention(qd, k_all, v_all, lam), diff_norm_g) * (1.0 - lam_init)
    out = jnp.concatenate([o_a.reshape(b, t, GDN_V_W), o_d.reshape(b, t, DIFF_V_W)], axis=-1) @ w_out
    return out, (jnp.stack([s_f, s_b], axis=1), kd, vd)


def mla_mixer(h, w_in, q_norm_g, w_uq, kv_norm_g, w_ukv, w_out, ctx, rope):
    b, t, _ = h.shape
    q_c, kv_c, k_pe = split_last(h @ w_in, (Q_LORA, KV_LORA, QK_ROPE))
    q = (rmsnorm(q_c, q_norm_g) @ w_uq).reshape(b, t, MLA_HEADS, QK_NOPE + QK_ROPE)
    q_nope, q_pe = q[..., :QK_NOPE], q[..., QK_NOPE:]
    ckv = rmsnorm(kv_c, kv_norm_g)
    ckv_all, kpe_all = ckv, k_pe
    if ctx is not None:
        cos, sin = rope
        q_pe = apply_rope_2d(q_pe, cos, sin)
        ckv_all = jnp.concatenate([ctx[0], ckv], axis=1)
        kpe_all = jnp.concatenate([ctx[1], apply_rope_2d(k_pe, cos, sin)], axis=1)
    kv = (ckv_all @ w_ukv).reshape(b, ckv_all.shape[1], MLA_HEADS, QK_NOPE + V_HEAD)
    k_nope, v = kv[..., :QK_NOPE], kv[..., QK_NOPE:]
    scale = (QK_NOPE + QK_ROPE) ** -0.5

    def block(qn, qp):
        s = (jnp.einsum('bqhd,bkhd->bhqk', qn, k_nope)
             + jnp.einsum('bqhr,bkr->bhqk', qp, kpe_all)).astype(F32) * scale
        p = jax.nn.softmax(s, axis=-1).astype(v.dtype)
        return jnp.einsum('bhqk,bkhe->bqhe', p, v)

    o = map_query_blocks(block, q_nope, q_pe)
    return o.reshape(b, t, MIX_ODD) @ w_out, (ckv, k_pe)


def moe(tok, w_router, b_router, w_gate_up, b_gate_up, w_down, b_down):
    logits = (tok @ w_router + b_router).astype(F32)
    top_v, top_i = lax.top_k(logits, TOP_K)
    probs = jax.nn.softmax(top_v, axis=-1)
    combine = jnp.sum(jax.nn.one_hot(top_i, N_EXPERTS, dtype=F32) * probs[..., None], axis=1).astype(tok.dtype)
    out = jnp.zeros_like(tok)
    for e in range(N_EXPERTS):
        gate, up = jnp.split(tok @ w_gate_up[e] + b_gate_up[e], 2, axis=-1)
        gate = jnp.minimum(gate, SWIGLU_LIMIT)
        up = jnp.clip(up, -SWIGLU_LIMIT, SWIGLU_LIMIT)
        act = (up + 1.0) * gate * jax.nn.sigmoid(SWIGLU_ALPHA * gate)
        out = out + combine[:, e:e + 1] * (act @ w_down[e] + b_down[e])
    return out


def setup_inputs(seed: int = 0) -> dict:
    key = jax.random.key(seed)
    ks = iter(jax.random.split(key, 64))
    D = D_MODEL

    def nrm(shape, scale=1.0):
        return jax.random.normal(next(ks), shape, F32) * scale

    def gain(shape):
        return 1.0 + 0.05 * jax.random.normal(next(ks), shape, F32)

    dt = jnp.exp(jax.random.uniform(next(ks), (N_EVEN, 2, GDN_HEADS), F32,
                                    minval=math.log(1e-3), maxval=math.log(1e-1)))
    return {
        'x_prompt': nrm((BATCH, SEQ, D)),
        'x_sample': nrm((DEC_BATCH, DEC_SEQ, D)),
        'c': nrm((DEC_BATCH, D)),
        'c_ctx': nrm((D,)),
        'state_delta': nrm((DEC_BATCH, N_EVEN, 2, GDN_HEADS, GDN_DK, GDN_DV), 0.1),
        'cache_diff_k': nrm((DEC_BATCH, N_EVEN, PAST_LEN, DIFF_HEADS, 2, DIFF_DH)),
        'cache_diff_v': nrm((DEC_BATCH, N_EVEN, PAST_LEN, DIFF_HEADS, 2 * DIFF_DH)),
        'cache_mla_ckv': nrm((DEC_BATCH, N_ODD, PAST_LEN, KV_LORA)),
        'cache_mla_kpe': nrm((DEC_BATCH, N_ODD, PAST_LEN, QK_ROPE)),
        'w_mod': nrm((DEPTH, D, N_MOD * D), 0.5 * D ** -0.5),
        'b_mod': nrm((DEPTH, N_MOD * D), 0.02),
        'norm_g': gain((DEPTH, 4, D)),
        'w_router': nrm((DEPTH, D, N_EXPERTS), D ** -0.5),
        'b_router': nrm((DEPTH, N_EXPERTS), 0.01),
        'w_gate_up': nrm((DEPTH, N_EXPERTS, D, 2 * D_FF), D ** -0.5),
        'b_gate_up': nrm((DEPTH, N_EXPERTS, 2 * D_FF), 0.02),
        'w_down': nrm((DEPTH, N_EXPERTS, D_FF, D), D_FF ** -0.5),
        'b_down': nrm((DEPTH, N_EXPERTS, D), 0.02),
        'w_in_even': nrm((N_EVEN, D, IN_EVEN), D ** -0.5),
        'conv_w': nrm((N_EVEN, SHORT_CONV, 2 * GDN_QK_W + GDN_V_W), SHORT_CONV ** -0.5),
        'a_log': jnp.log(jax.random.uniform(next(ks), (N_EVEN, 2, GDN_HEADS), F32, minval=1.0, maxval=16.0)),
        'dt_bias': dt + jnp.log(-jnp.expm1(-dt)),
        'gdn_norm_g': gain((N_EVEN, GDN_DV)),
        'diff_lambda': nrm((N_EVEN, 4, DIFF_DH), 0.1),
        'diff_norm_g': gain((N_EVEN, 2 * DIFF_DH)),
        'w_out_even': nrm((N_EVEN, MIX_EVEN, D), MIX_EVEN ** -0.5),
        'w_in_odd': nrm((N_ODD, D, IN_ODD), D ** -0.5),
        'q_norm_g': gain((N_ODD, Q_LORA)),
        'w_uq': nrm((N_ODD, Q_LORA, MLA_HEADS * (QK_NOPE + QK_ROPE)), Q_LORA ** -0.5),
        'kv_norm_g': gain((N_ODD, KV_LORA)),
        'w_ukv': nrm((N_ODD, KV_LORA, MLA_HEADS * (QK_NOPE + V_HEAD)), KV_LORA ** -0.5),
        'w_out_odd': nrm((N_ODD, MIX_ODD, D), MIX_ODD ** -0.5),
    }


def reference(x_prompt, x_sample, c, c_ctx, state_delta, cache_diff_k, cache_diff_v, cache_mla_ckv, cache_mla_kpe,
              w_mod, b_mod, norm_g, w_router, b_router, w_gate_up, b_gate_up, w_down, b_down,
              w_in_even, conv_w, a_log, dt_bias, gdn_norm_g, diff_lambda, diff_norm_g, w_out_even,
              w_in_odd, q_norm_g, w_uq, kv_norm_g, w_ukv, w_out_odd):
    n_lat = x_sample.shape[1]
    rope_diff = rope_2d_tables(n_lat, DIFF_DH)
    rope_mla = rope_2d_tables(n_lat, QK_ROPE)
    xp, xs = x_prompt, x_sample
    n_prompt_tok = xp.shape[0] * xp.shape[1]
    new_delta, new_dk, new_dv, new_ckv, new_kpe = [], [], [], [], []
    for i in range(DEPTH):
        mp = modulation(c_ctx[None, :], w_mod[i], b_mod[i])
        ms = modulation(c, w_mod[i], b_mod[i])
        hp = modulate(xp, norm_g[i, 0], mp[0], mp[1])
        hs = modulate(xs, norm_g[i, 0], ms[0], ms[1])
        j = i // 2
        if i % 2 == 0:
            lam_init = 0.8 - 0.6 * math.exp(-0.3 * i)
            params = (w_in_even[j], conv_w[j], a_log[j], dt_bias[j], gdn_norm_g[j], diff_lambda[j],
                      diff_norm_g[j], w_out_even[j], lam_init)
            yp, (st, kd, vd) = delta_diff_mixer(hp, *params, None, None)
            ys, _ = delta_diff_mixer(hs, *params, (state_delta[:, j], cache_diff_k[:, j], cache_diff_v[:, j]), rope_diff)
            new_delta.append(st)
            new_dk.append(kd)
            new_dv.append(vd)
        else:
            params = (w_in_odd[j], q_norm_g[j], w_uq[j], kv_norm_g[j], w_ukv[j], w_out_odd[j])
            yp, (ckv, kpe) = mla_mixer(hp, *params, None, None)
            ys, _ = mla_mixer(hs, *params, (cache_mla_ckv[:, j], cache_mla_kpe[:, j]), rope_mla)
            new_ckv.append(ckv)
            new_kpe.append(kpe)
        xp = xp + mp[2] * rmsnorm(yp, norm_g[i, 1])
        xs = xs + ms[2] * rmsnorm(ys, norm_g[i, 1])
        hp = modulate(xp, norm_g[i, 2], mp[3], mp[4])
        hs = modulate(xs, norm_g[i, 2], ms[3], ms[4])
        tok = jnp.concatenate([hp.reshape(-1, D_MODEL), hs.reshape(-1, D_MODEL)], axis=0)
        f = moe(tok, w_router[i], b_router[i], w_gate_up[i], b_gate_up[i], w_down[i], b_down[i])
        xp = xp + mp[5] * rmsnorm(f[:n_prompt_tok].reshape(xp.shape), norm_g[i, 3])
        xs = xs + ms[5] * rmsnorm(f[n_prompt_tok:].reshape(xs.shape), norm_g[i, 3])
    return (xp, xs, jnp.stack(new_delta, axis=1), jnp.stack(new_dk, axis=1), jnp.stack(new_dv, axis=1),
            jnp.stack(new_ckv, axis=1), jnp.stack(new_kpe, axis=1))
```

```python
import functools
import math

import jax
import jax.numpy as jnp
from jax import lax
from jax.experimental import pallas as pl
from jax.experimental.pallas import tpu as pltpu

F32 = jnp.float32
BF16 = jnp.bfloat16
HIGHEST = lax.Precision.HIGHEST

D = 1024
B_CTX, T_CTX = 16, 256
B_LAT, T_LAT = 8, 1024
PAST = 256
N_CTX = B_CTX * T_CTX
N_LAT = B_LAT * T_LAT
N_TOK = N_CTX + N_LAT
DEPTH = 2
N_MOD = 6
GRID_W = 64
GDN_H, GDN_DK, GDN_DV = 4, 128, 128
CONV_W = 5
CHUNK = 64
DIFF_H, DIFF_DH = 4, 64
MLA_H, Q_LORA, KV_LORA, QK_NOPE, QK_ROPE, V_HEAD = 8, 256, 256, 128, 64, 128
N_EXP, TOP_K, D_FF = 32, 4, 1024
SWIGLU_LIMIT, SWIGLU_ALPHA = 7.0, 1.702
ROPE_BASE = 10000.0
EPS = 1e-6

LANE = 128
TM = 256
N_TILES = N_TOK // TM
SUB_ROWS = 16
TILE_ROWS = TOP_K * TM + N_EXP * SUB_ROWS
MB = 256
SUB = MB // SUB_ROWS
N_MACRO = -(-((TOP_K * N_TOK) // SUB_ROWS + N_TILES * N_EXP + N_EXP * (SUB - 1)) // SUB)
D_EXT = D + LANE
VMEM_LIMIT = 48 * 1024 * 1024


def _cparams(sem, vmem=VMEM_LIMIT):
    return pltpu.CompilerParams(dimension_semantics=sem, vmem_limit_bytes=vmem)


def _mod_row(t):
    return jnp.where(t < N_CTX // TM, 0, 1 + (t - N_CTX // TM) // (T_LAT // TM))


def _rms(x, g):
    return x * lax.rsqrt(jnp.mean(x * x, axis=-1, keepdims=True) + EPS) * g


def _silu(x):
    return x * jax.nn.sigmoid(x)


def _dot(a, b):
    return jnp.dot(a, b, preferred_element_type=F32)


def _dot_nt(a, b):
    return lax.dot_general(a, b, (((1,), (1,)), ((), ())), preferred_element_type=F32)


def _mod_kernel(c_ref, w_ref, b_ref, o_ref):
    s = _silu(c_ref[...])
    o_ref[...] = _dot(s.astype(BF16), w_ref[...].astype(BF16)) + b_ref[...]


def _modulation(cond, w_mod, b_mod):
    nc = 1536
    return pl.pallas_call(
        _mod_kernel,
        grid=(DEPTH, N_MOD * D // nc),
        in_specs=[pl.BlockSpec((16, D), lambda i, j: (0, 0)),
                  pl.BlockSpec((None, D, nc), lambda i, j: (i, 0, j)),
                  pl.BlockSpec((None, 1, nc), lambda i, j: (i, 0, j))],
        out_specs=pl.BlockSpec((None, 16, nc), lambda i, j: (i, 0, j)),
        out_shape=jax.ShapeDtypeStruct((DEPTH, 16, N_MOD * D), F32),
        compiler_params=_cparams(("arbitrary", "arbitrary")),
        name="modulation",
    )(cond, w_mod, b_mod.reshape(DEPTH, 1, N_MOD * D))


def _mod_spec(layer, k):
    return pl.BlockSpec((None, None, 1, D), lambda t: (layer, _mod_row(t), 0, k))


EVEN_W = 2 * 512 + 512 + 512 + 3 * 512 + LANE


def _in_even_kernel(x_ref, g_ref, sh_ref, sc_ref, w_ref, qkv_ref, z_ref, qd_ref, kd_ref, vd_ref, ab_ref):
    h = _rms(x_ref[...], g_ref[...]) * (1.0 + sc_ref[...]) + sh_ref[...]
    y = _dot(h.astype(BF16), w_ref[...])
    qkv_ref[...] = y[:, 0:1536]
    z_ref[...] = y[:, 1536:2048]
    qd_ref[...] = y[:, 2048:2560]
    kd_ref[...] = y[:, 2560:3072]
    vd_ref[...] = y[:, 3072:3584]
    ab_ref[...] = y[:, 3584:3712]


def _in_even(x, g, mod4, layer, w):
    row = lambda c: pl.BlockSpec((TM, c), lambda t: (t, 0))
    shp = lambda c: jax.ShapeDtypeStruct((N_TOK, c), F32)
    return pl.pallas_call(
        _in_even_kernel,
        grid=(N_TILES,),
        in_specs=[row(D), pl.BlockSpec((1, D), lambda t: (0, 0)), _mod_spec(layer, 0), _mod_spec(layer, 1),
                  pl.BlockSpec((D, EVEN_W), lambda t: (0, 0))],
        out_specs=[row(1536), row(512), row(512), row(512), row(512), row(LANE)],
        out_shape=[shp(1536), shp(512), shp(512), shp(512), shp(512), shp(LANE)],
        compiler_params=_cparams(("arbitrary",)),
        name="in_proj_even",
    )(x, g, mod4, mod4, w)


def _split_bf16(a):
    hi = a.astype(BF16)
    lo = (a - hi.astype(F32)).astype(BF16)
    return hi, lo


def _mm3(a, b):
    ah, al = _split_bf16(a)
    bh, bl = _split_bf16(b)
    return _dot(ah, bh) + (_dot(ah, bl) + _dot(al, bh))


def _unit_tri_inverse(l, ii, jj, eye):
    blk = lambda s: (ii // s) == (jj // s)
    d8 = jnp.where(blk(8), l, 0.0)
    x2 = _mm3(d8, d8)
    t = eye - d8
    t = t + _mm3(t, x2)
    t = t + _mm3(t, _mm3(x2, x2))
    for s in (8, 16, 32):
        off = jnp.where(blk(2 * s) & jnp.logical_not(blk(s)), l, 0.0)
        t = t - _mm3(t, _mm3(off, t))
    return t


def _softplus(x):
    return jnp.maximum(x, 0.0) + jnp.log1p(jnp.exp(-jnp.abs(x)))


def _gdn_kernel(has_init, T, alog_ref, dtb_ref, q_ref, k_ref, v_ref, cwq_ref, cwk_ref, cwv_ref, z_ref,
                abc_ref, abr_ref, gn_ref, *rest):
    if has_init:
        s0_ref, o_ref, xp, qs, ks, vs, u_s, w_s, qg_s, kgt_s, in_s, gl_s, o_s = rest
        st_ref = None
    else:
        o_ref, st_ref, xp, qs, ks, vs, u_s, w_s, qg_s, kgt_s, in_s, gl_s, o_s = rest
        s0_ref = None
    n = T // CHUNK
    hd = pl.program_id(1)

    xp[0:8, :] = jnp.zeros((8, LANE), F32)
    xp[T + 8:T + 16, :] = jnp.zeros((8, LANE), F32)

    def conv_silu(x_ref, w_ref):
        xp[8:T + 8, :] = x_ref[...]
        w = w_ref[...]
        acc = xp[6:T + 6, :] * w[0:1, :]
        for i in range(1, CONV_W):
            acc = acc + xp[6 + i:T + 6 + i, :] * w[i:i + 1, :]
        return _silu(acc)

    def l2n(x):
        return x * lax.rsqrt(jnp.sum(x * x, axis=-1, keepdims=True) + EPS)

    qs[...] = l2n(conv_silu(q_ref, cwq_ref)) * (GDN_DK ** -0.5)
    ks[...] = l2n(conv_silu(k_ref, cwk_ref))
    vs[...] = conv_silu(v_ref, cwv_ref)

    ii = lax.broadcasted_iota(jnp.int32, (CHUNK, CHUNK), 0)
    jj = lax.broadcasted_iota(jnp.int32, (CHUNK, CHUNK), 1)
    eye = jnp.where(ii == jj, 1.0, 0.0).astype(F32)
    zeros_half = jnp.zeros((CHUNK, LANE), F32)

    def prep(c, carry):
        r0 = pl.multiple_of(c * CHUNK, CHUNK)
        qc = qs[pl.ds(r0, CHUNK), :]
        kc = ks[pl.ds(r0, CHUNK), :]
        vc = vs[pl.ds(r0, CHUNK), :]
        abc = abc_ref[c]
        abr = abr_ref[c]
        kb16 = kc.astype(BF16)
        kk = _dot_nt(kb16, kb16)
        qk = _dot_nt(qc.astype(BF16), kb16)
        for d in range(2):
            neg_a = -jnp.exp(jnp.full((1, 1), alog_ref[d, hd], F32))
            dtb = dtb_ref[d, hd]
            g_col = neg_a * _softplus(abc[:, d:d + 1] + dtb)
            g_row = neg_a * _softplus(abr[d:d + 1, :] + dtb)
            beta = jax.nn.sigmoid(abc[:, 2 + d:3 + d])
            incl = (ii >= jj) if d == 0 else (ii <= jj)
            strict = (ii > jj) if d == 0 else (ii < jj)
            gc_col = jnp.sum(jnp.where(incl, g_row, 0.0), axis=1, keepdims=True)
            gc_row = jnp.sum(jnp.where(incl, 0.0, g_col) + jnp.where(ii == jj, g_col, 0.0), axis=0, keepdims=True)
            decay = jnp.where(incl, jnp.exp(jnp.where(incl, gc_col - gc_row, 0.0)), 0.0)
            lm = jnp.where(strict, kk * beta * decay, 0.0)
            tm = _unit_tri_inverse(lm, ii, jj, eye)
            eg = jnp.exp(gc_col)
            rhs = jnp.concatenate([vc * beta, kc * (beta * eg)], axis=1)
            uw = _dot(tm.astype(BF16), rhs.astype(BF16))
            u_s[d, pl.ds(r0, CHUNK), :] = uw[:, :LANE]
            w_s[d, pl.ds(r0, CHUNK), :] = uw[:, LANE:]
            in_s[d, c] = qk * decay
            qg_s[d, pl.ds(r0, CHUNK), :] = qc * eg
            g_last = gc_col[CHUNK - 1:CHUNK, :] if d == 0 else gc_col[0:1, :]
            kg = kc * jnp.exp(g_last - gc_col)
            kgt_s[d, c] = jnp.concatenate([kg, zeros_half], axis=0).T
            gl_s[d, c] = jnp.broadcast_to(jnp.exp(g_last), (8, LANE))
        return carry

    lax.fori_loop(0, n, prep, 0)

    def scan(i, carry):
        new = []
        for d in range(2):
            s = carry[d]
            c = i if d == 0 else n - 1 - i
            r0 = pl.multiple_of(c * CHUNK, CHUNK)
            s16 = s.astype(BF16)
            v_new = u_s[d, pl.ds(r0, CHUNK), :] - _dot(w_s[d, pl.ds(r0, CHUNK), :].astype(BF16), s16)
            vn16 = v_new.astype(BF16)
            o = _dot(qg_s[d, pl.ds(r0, CHUNK), :].astype(BF16), s16) + _dot(in_s[d, c].astype(BF16), vn16)
            o_s[d, pl.ds(r0, CHUNK), :] = o
            vpad = jnp.concatenate([vn16, jnp.zeros((CHUNK, LANE), BF16)], axis=0)
            new.append(s * gl_s[d, c][0:1, :] + _dot(kgt_s[d, c].astype(BF16), vpad))
        return tuple(new)

    if has_init:
        init = (s0_ref[0], s0_ref[1])
    else:
        init = (jnp.zeros((GDN_DK, GDN_DV), F32), jnp.zeros((GDN_DK, GDN_DV), F32))
    sf, sb = lax.fori_loop(0, n, scan, init)
    if st_ref is not None:
        st_ref[0] = sf
        st_ref[1] = sb
    o_ref[...] = _rms(o_s[0] + o_s[1], gn_ref[...]) * _silu(z_ref[...])


def _gdn(qkv, z, ab, conv_w, a_log, dt_bias, gn, s0, nb, T, row0):
    n = T // CHUNK
    blk0 = row0 // T
    has_init = s0 is not None
    ab4 = ab[row0:row0 + nb * T, :16].reshape(nb, n, CHUNK, 4, GDN_H)
    abc = jnp.transpose(ab4, (0, 4, 1, 2, 3))
    abr = jnp.transpose(ab4, (0, 4, 1, 3, 2))
    smem = pl.BlockSpec(memory_space=pltpu.SMEM)
    col = lambda off: pl.BlockSpec((T, LANE), lambda b, h: (blk0 + b, off + h))
    cw = lambda off: pl.BlockSpec((CONV_W, LANE), lambda b, h: (0, off + h))
    in_specs = [smem, smem, col(0), col(4), col(8), cw(0), cw(4), cw(8),
                pl.BlockSpec((T, LANE), lambda b, h: (blk0 + b, h)),
                pl.BlockSpec((None, None, n, CHUNK, 4), lambda b, h: (b, h, 0, 0, 0)),
                pl.BlockSpec((None, None, n, 4, CHUNK), lambda b, h: (b, h, 0, 0, 0)),
                pl.BlockSpec((1, LANE), lambda b, h: (0, 0))]
    args = [a_log, dt_bias, qkv, qkv, qkv, conv_w, conv_w, conv_w, z, abc, abr, gn]
    st_spec = pl.BlockSpec((None, 2, None, GDN_DK, GDN_DV), lambda b, h: (b, 0, h, 0, 0))
    o_spec = pl.BlockSpec((T, LANE), lambda b, h: (b, h))
    o_shape = jax.ShapeDtypeStruct((nb * T, GDN_H * GDN_DV), F32)
    if has_init:
        in_specs.append(st_spec)
        args.append(s0)
        out_specs, out_shape = o_spec, o_shape
    else:
        out_specs = [o_spec, st_spec]
        out_shape = [o_shape, jax.ShapeDtypeStruct((nb, 2, GDN_H, GDN_DK, GDN_DV), F32)]
    seq = lambda: pltpu.VMEM((2, T, LANE), F32)
    scratch = [pltpu.VMEM((T + 16, LANE), F32), pltpu.VMEM((T, LANE), F32), pltpu.VMEM((T, LANE), F32),
               pltpu.VMEM((T, LANE), F32), seq(), seq(), seq(),
               pltpu.VMEM((2, n, LANE, LANE), F32), pltpu.VMEM((2, n, CHUNK, CHUNK), F32),
               pltpu.VMEM((2, n, 8, LANE), F32), seq()]
    return pl.pallas_call(
        functools.partial(_gdn_kernel, has_init, T),
        grid=(nb, GDN_H),
        in_specs=in_specs, out_specs=out_specs, out_shape=out_shape, scratch_shapes=scratch,
        compiler_params=_cparams(("arbitrary", "arbitrary")),
        name="gdn_lat" if has_init else "gdn_ctx",
    )(*args)


def _rope(x, cos, sin_signed):
    lane = lax.broadcasted_iota(jnp.int32, x.shape, 1)
    up = pltpu.roll(x, LANE - 16, axis=1)
    dn = pltpu.roll(x, 16, axis=1)
    rot = jnp.where((lane % 32) < 16, up, dn)
    return x * cos + rot * sin_signed


def _rope_tables():
    rows = T_LAT // GRID_W
    r = jnp.repeat(jnp.arange(rows, dtype=F32), GRID_W)
    col = jnp.tile(jnp.arange(GRID_W, dtype=F32), rows)
    quarter = 16
    inv = ROPE_BASE ** (-jnp.arange(quarter, dtype=F32) / quarter)
    ar = r[:, None] * inv
    ac = col[:, None] * inv
    ang = jnp.concatenate([ar, ar, ac, ac], axis=-1)
    sign = jnp.tile(jnp.concatenate([-jnp.ones(16, F32), jnp.ones(16, F32)]), 2)
    cos, sin = jnp.cos(ang), jnp.sin(ang) * sign
    return cos, sin


def _softmax_parts(parts):
    m = functools.reduce(jnp.maximum, [jnp.max(s, axis=-1, keepdims=True) for s in parts])
    es = [jnp.exp(s - m) for s in parts]
    den = functools.reduce(lambda a, b: a + b, [jnp.sum(e, axis=-1, keepdims=True) for e in es])
    return [e / den for e in es]


def _diff_kernel(latent, T, lam_init, q_ref, k_ref, v_ref, lam_ref, gn_ref, *rest):
    if latent:
        cos_ref, sin_ref, ck_ref, cv_ref, o_ref = rest
    else:
        (o_ref,) = rest
    scale = DIFF_DH ** -0.5
    lp = lam_ref[...]
    lam = (jnp.exp(jnp.sum(lp[0:1] * lp[1:2], axis=-1, keepdims=True))
           - jnp.exp(jnp.sum(lp[2:3] * lp[3:4], axis=-1, keepdims=True)) + lam_init)
    k = k_ref[...]
    if latent:
        k = _rope(k, cos_ref[...], sin_ref[...])
    k16 = k.astype(BF16)
    v16 = v_ref[...].astype(BF16)
    if latent:
        ck16 = ck_ref[...].astype(BF16)
        cv16 = cv_ref[...].astype(BF16)
    tq = 256
    lane = lax.broadcasted_iota(jnp.int32, (tq, LANE), 1)
    for qb in range(T // tq):
        q = q_ref[qb * tq:(qb + 1) * tq, :]
        if latent:
            q = _rope(q, cos_ref[qb * tq:(qb + 1) * tq, :], sin_ref[qb * tq:(qb + 1) * tq, :])
        ps = []
        for m in range(2):
            qm = jnp.where((lane // DIFF_DH) == m, q, 0.0).astype(BF16)
            parts = [_dot_nt(qm, k16) * scale]
            if latent:
                parts.insert(0, _dot_nt(qm, ck16) * scale)
            ps.append(_softmax_parts(parts))
        o = _dot((ps[0][-1] - lam * ps[1][-1]).astype(BF16), v16)
        if latent:
            o = o + _dot((ps[0][0] - lam * ps[1][0]).astype(BF16), cv16)
        o_ref[qb * tq:(qb + 1) * tq, :] = _rms(o, gn_ref[...]) * (1.0 - lam_init)


def _diff_attn(qd, kd, vd, lam_p, gn, lam_init, nb, T, row0, cache=None, rope=None):
    blk0 = row0 // T
    latent = cache is not None
    hcol = pl.BlockSpec((T, LANE), lambda b, h: (blk0 + b, h))
    in_specs = [hcol, hcol, hcol, pl.BlockSpec((4, DIFF_DH), lambda b, h: (0, 0)),
                pl.BlockSpec((1, LANE), lambda b, h: (0, 0))]
    args = [qd, kd, vd, lam_p, gn]
    if latent:
        tab = pl.BlockSpec((T, LANE), lambda b, h: (0, 0))
        cch = pl.BlockSpec((None, PAST, LANE), lambda b, h: (b, 0, h))
        in_specs += [tab, tab, cch, cch]
        args += [rope[0], rope[1], cache[0], cache[1]]
    return pl.pallas_call(
        functools.partial(_diff_kernel, latent, T, lam_init),
        grid=(nb, DIFF_H),
        in_specs=in_specs,
        out_specs=pl.BlockSpec((T, LANE), lambda b, h: (b, h)),
        out_shape=jax.ShapeDtypeStruct((nb * T, DIFF_H * 2 * DIFF_DH), F32),
        compiler_params=_cparams(("arbitrary", "arbitrary")),
        name="diff_lat" if latent else "diff_ctx",
    )(*args)


ODD_W = Q_LORA + KV_LORA + LANE
MLA_QW = 2 * LANE


def _in_odd_kernel(x_ref, g_ref, sh_ref, sc_ref, w_ref, qg_ref, wuq_ref, kvg_ref, wukv_ref,
                   q_ref, kv_ref, ckv_ref, kpe_ref):
    h = _rms(x_ref[...], g_ref[...]) * (1.0 + sc_ref[...]) + sh_ref[...]
    y = _dot(h.astype(BF16), w_ref[...])
    qn = _rms(y[:, 0:Q_LORA], qg_ref[...])
    q_ref[...] = _dot(qn.astype(BF16), wuq_ref[...])
    ckv = _rms(y[:, Q_LORA:Q_LORA + KV_LORA], kvg_ref[...])
    ckv_ref[...] = ckv
    kv_ref[...] = _dot(ckv.astype(BF16), wukv_ref[...])
    kpe_ref[...] = y[:, Q_LORA + KV_LORA:ODD_W]


def _in_odd(x, g, mod4, layer, w, qg, wuq, kvg, wukv):
    row = lambda c: pl.BlockSpec((TM, c), lambda t: (t, 0))
    full = lambda a: pl.BlockSpec(a.shape, lambda t: (0, 0))
    shp = lambda c: jax.ShapeDtypeStruct((N_TOK, c), F32)
    return pl.pallas_call(
        _in_odd_kernel,
        grid=(N_TILES,),
        in_specs=[row(D), full(g), _mod_spec(layer, 0), _mod_spec(layer, 1), full(w), full(qg), full(wuq),
                  full(kvg), full(wukv)],
        out_specs=[row(MLA_H * MLA_QW), row(MLA_H * MLA_QW), row(KV_LORA), row(LANE)],
        out_shape=[shp(MLA_H * MLA_QW), shp(MLA_H * MLA_QW), shp(KV_LORA), shp(LANE)],
        compiler_params=_cparams(("arbitrary",)),
        name="in_proj_odd",
    )(x, g, mod4, mod4, w, qg, wuq, kvg, wukv)


def _mm_kernel(x_ref, w_ref, o_ref):
    o_ref[...] = _dot(x_ref[...].astype(BF16), w_ref[...])


def _matmul(x, w):
    m, k = x.shape
    n = w.shape[1]
    return pl.pallas_call(
        _mm_kernel,
        grid=(m // TM,),
        in_specs=[pl.BlockSpec((TM, k), lambda i: (i, 0)), pl.BlockSpec((k, n), lambda i: (0, 0))],
        out_specs=pl.BlockSpec((TM, n), lambda i: (i, 0)),
        out_shape=jax.ShapeDtypeStruct((m, n), F32),
        compiler_params=_cparams(("arbitrary",)),
        name="cache_kv_proj",
    )(x, w)


def _mla_kernel(latent, T, q_ref, kv_ref, kpe_ref, *rest):
    if latent:
        cos_ref, sin_ref, ckv_ref, ckpe_ref, o_ref = rest
    else:
        (o_ref,) = rest
    scale = (QK_NOPE + QK_ROPE) ** -0.5
    kv = kv_ref[...]
    kn16 = kv[:, :LANE].astype(BF16)
    v16 = kv[:, LANE:].astype(BF16)
    kpe = kpe_ref[...]
    if latent:
        kpe = _rope(kpe, cos_ref[...], sin_ref[...])
        ckv = ckv_ref[...]
        ckn16 = ckv[:, :LANE].astype(BF16)
        cv16 = ckv[:, LANE:].astype(BF16)
        ckpe16 = ckpe_ref[...].astype(BF16)
    kpe16 = kpe.astype(BF16)
    tq = 256
    for qb in range(T // tq):
        rows = slice(qb * tq, (qb + 1) * tq)
        q = q_ref[rows, :]
        qn16 = q[:, :LANE].astype(BF16)
        qp = q[:, LANE:]
        if latent:
            qp = _rope(qp, cos_ref[rows, :], sin_ref[rows, :])
        qp16 = qp.astype(BF16)
        parts = [(_dot_nt(qn16, kn16) + _dot_nt(qp16, kpe16)) * scale]
        if latent:
            parts.insert(0, (_dot_nt(qn16, ckn16) + _dot_nt(qp16, ckpe16)) * scale)
        p = _softmax_parts(parts)
        o = _dot(p[-1].astype(BF16), v16)
        if latent:
            o = o + _dot(p[0].astype(BF16), cv16)
        o_ref[rows, :] = o


def _mla_attn(q, kv, kpe, nb, T, row0, cache=None, rope=None):
    blk0 = row0 // T
    latent = cache is not None
    head = pl.BlockSpec((T, MLA_QW), lambda b, h: (blk0 + b, h))
    in_specs = [head, head, pl.BlockSpec((T, LANE), lambda b, h: (blk0 + b, 0))]
    args = [q, kv, kpe]
    if latent:
        tab = pl.BlockSpec((T, LANE), lambda b, h: (0, 0))
        in_specs += [tab, tab, pl.BlockSpec((PAST, MLA_QW), lambda b, h: (b, h)),
                     pl.BlockSpec((PAST, LANE), lambda b, h: (b, 0))]
        args += [rope[0], rope[1], cache[0], cache[1]]
    return pl.pallas_call(
        functools.partial(_mla_kernel, latent, T),
        grid=(nb, MLA_H),
        in_specs=in_specs,
        out_specs=pl.BlockSpec((T, V_HEAD), lambda b, h: (b, h)),
        out_shape=jax.ShapeDtypeStruct((nb * T, MLA_H * V_HEAD), F32),
        compiler_params=_cparams(("arbitrary", "arbitrary")),
        name="mla_lat" if latent else "mla_ctx",
    )(*args)


def _route_kernel(n_mix, *refs):
    mix_refs = refs[:n_mix]
    w_refs = refs[n_mix:2 * n_mix]
    (x_ref, g1_ref, gate1_ref, g2_ref, sh2_ref, sc2_ref, wr_ref, br_ref,
     x1_ref, hx_ref, pw_ref, pwt_ref, cnt_ref) = refs[2 * n_mix:]
    y = _dot(mix_refs[0][...].astype(BF16), w_refs[0][...])
    for m in range(1, n_mix):
        y = y + _dot(mix_refs[m][...].astype(BF16), w_refs[m][...])
    x1 = x_ref[...] + gate1_ref[...] * _rms(y, g1_ref[...])
    x1_ref[...] = x1
    h2 = _rms(x1, g2_ref[...]) * (1.0 + sc2_ref[...]) + sh2_ref[...]

    logits = jnp.dot(h2, wr_ref[...], precision=HIGHEST, preferred_element_type=F32) + br_ref[...]
    lane_i = lax.broadcasted_iota(jnp.int32, (TM, LANE), 1)
    lane = lane_i.astype(F32)
    neg = jnp.float32(-jnp.inf)
    l = jnp.where(lane_i < N_EXP, logits, neg)
    sel = jnp.zeros((TM, LANE), F32)
    comb = jnp.zeros((TM, LANE), F32)
    wsum = jnp.zeros((TM, 1), F32)
    hots, wts = [], []
    for k in range(TOP_K):
        m = jnp.max(l, axis=-1, keepdims=True)
        idx = jnp.min(jnp.where(l == m, lane, float(LANE)), axis=-1, keepdims=True)
        hot = lane == idx
        if k == 0:
            m0 = m
        w = jnp.exp(m - m0)
        wsum = wsum + w
        hots.append(hot)
        wts.append(w)
        sel = sel + jnp.where(hot, 1.0, 0.0)
        comb = comb + jnp.where(hot, w, 0.0)
        l = jnp.where(hot, neg, l)
    inv = 1.0 / wsum
    comb = comb * inv

    cnt = jnp.sum(sel, axis=0, keepdims=True)
    cnt_ref[...] = cnt
    padded = jnp.floor((cnt + (SUB_ROWS - 1)) * (1.0 / SUB_ROWS)) * SUB_ROWS
    li = lax.broadcasted_iota(jnp.int32, (LANE, LANE), 0)
    lj = lax.broadcasted_iota(jnp.int32, (LANE, LANE), 1)
    before = jnp.where(li < lj, 1.0, 0.0).astype(BF16)
    start = _dot(jnp.broadcast_to(padded, (8, LANE)).astype(BF16), before)[0:1, :]
    ti = lax.broadcasted_iota(jnp.int32, (TM, TM), 0)
    tj = lax.broadcasted_iota(jnp.int32, (TM, TM), 1)
    earlier = jnp.where(tj < ti, 1.0, 0.0).astype(BF16)
    pos_full = _dot(earlier, sel.astype(BF16)) + start
    pw = jnp.zeros((TM, LANE), F32)
    for k in range(TOP_K):
        pos_k = jnp.sum(jnp.where(hots[k], pos_full, 0.0), axis=-1, keepdims=True)
        pw = pw + jnp.where(lane_i == k, pos_k, 0.0) + jnp.where(lane_i == TOP_K + k, wts[k] * inv, 0.0)
    pw_ref[...] = pw
    pwt_ref[...] = pw.T[0:8, :]

    chi, clo = _split_bf16(comb)
    ext = chi.astype(F32) + pltpu.roll(clo.astype(F32), N_EXP, axis=1)
    hx_ref[:, 0:D] = h2.astype(BF16)
    hx_ref[:, D:D_EXT] = ext.astype(BF16)


def _route(mixes, w_outs, x, norm_g, mod4, layer, w_router, b_router):
    n_mix = len(mixes)
    row = lambda c: pl.BlockSpec((TM, c), lambda t: (t, 0))
    full = lambda a: pl.BlockSpec(a.shape, lambda t: (0, 0))
    g = lambda k: pl.BlockSpec((None, 1, D), lambda t: (k, 0, 0))
    in_specs = ([row(m.shape[1]) for m in mixes] + [full(w) for w in w_outs]
                + [row(D), g(1), _mod_spec(layer, 2), g(2), _mod_spec(layer, 3), _mod_spec(layer, 4),
                   full(w_router), full(b_router)])
    ng = norm_g.reshape(4, 1, D)
    return pl.pallas_call(
        functools.partial(_route_kernel, n_mix),
        grid=(N_TILES,),
        in_specs=in_specs,
        out_specs=[row(D), row(D_EXT), row(LANE), pl.BlockSpec((None, 8, TM), lambda t: (t, 0, 0)),
                   pl.BlockSpec((None, 1, LANE), lambda t: (t, 0, 0))],
        out_shape=[jax.ShapeDtypeStruct((N_TOK, D), F32), jax.ShapeDtypeStruct((N_TOK, D_EXT), BF16),
                   jax.ShapeDtypeStruct((N_TOK, LANE), F32), jax.ShapeDtypeStruct((N_TILES, 8, TM), F32),
                   jax.ShapeDtypeStruct((N_TILES, 1, LANE), F32)],
        compiler_params=_cparams(("arbitrary",)),
        name="out_proj_route",
    )(*mixes, *w_outs, x, ng, mod4, ng, mod4, mod4, w_router, b_router)


def _dispatch_kernel(hx_ref, pwt_ref, s_ref):
    j = pl.program_id(0)

    @pl.when(j < N_TILES)
    def _():
        rio = lax.broadcasted_iota(jnp.int32, (TILE_ROWS, TM), 0).astype(F32)
        p = jnp.zeros((TILE_ROWS, TM), F32)
        for k in range(TOP_K):
            p = p + jnp.where(rio == pwt_ref[k:k + 1, :], 1.0, 0.0)
        s_ref[...] = _dot(p.astype(BF16), hx_ref[...]).astype(BF16)

    @pl.when(j == N_TILES)
    def _():
        s_ref[...] = jnp.zeros((TILE_ROWS, D_EXT), BF16)


def _dispatch(hx, pwt):
    clamp = lambda j: jnp.minimum(j, N_TILES - 1)
    return pl.pallas_call(
        _dispatch_kernel,
        grid=(N_TILES + 1,),
        in_specs=[pl.BlockSpec((TM, D_EXT), lambda j: (clamp(j), 0)),
                  pl.BlockSpec((None, 8, TM), lambda j: (clamp(j), 0, 0))],
        out_specs=pl.BlockSpec((TILE_ROWS, D_EXT), lambda j: (j, 0)),
        out_shape=jax.ShapeDtypeStruct(((N_TILES + 1) * TILE_ROWS, D_EXT), BF16),
        compiler_params=_cparams(("arbitrary",)),
        name="moe_dispatch",
    )(hx, pwt)


def _expert_kernel(blk_e_ref, src_ref, nact_ref, s_hbm, yinit_hbm, wgu_ref, bgu_ref, wd_ref, bd_ref, y_hbm,
                   xbuf, ybuf, wgu_s, wd_s, gsem, ssem):
    del yinit_hbm
    i = pl.program_id(0)
    nsteps = pl.num_programs(0)
    nact = nact_ref[0]
    slot = i % 2

    def gather_copy(step, sl, s):
        src = pl.multiple_of(src_ref[step * SUB + s], SUB_ROWS)
        return pltpu.make_async_copy(s_hbm.at[pl.ds(src, SUB_ROWS)],
                                     xbuf.at[sl, pl.ds(s * SUB_ROWS, SUB_ROWS)], gsem.at[sl])

    def scatter_copy(step, sl, s):
        dst = pl.multiple_of(src_ref[step * SUB + s], SUB_ROWS)
        return pltpu.make_async_copy(ybuf.at[sl, pl.ds(s * SUB_ROWS, SUB_ROWS)],
                                     y_hbm.at[pl.ds(dst, SUB_ROWS)], ssem.at[sl])

    def gather_start(step, sl):
        for s in range(SUB):
            gather_copy(step, sl, s).start()

    def gather_wait(step, sl):
        for s in range(SUB):
            gather_copy(step, sl, s).wait()

    def scatter_start(step, sl):
        for s in range(SUB):
            scatter_copy(step, sl, s).start()

    def scatter_wait(step, sl):
        for s in range(SUB):
            scatter_copy(step, sl, s).wait()

    @pl.when(i == 0)
    def _():
        gather_start(0, 0)

    @pl.when(i < nact)
    def _():
        gather_wait(i, slot)

        @pl.when(i + 1 < nact)
        def _():
            gather_start(i + 1, 1 - slot)

        e = blk_e_ref[i]
        e_prev = blk_e_ref[jnp.maximum(i - 1, 0)]

        @pl.when((i == 0) | (e != e_prev))
        def _():
            wgu_s[...] = wgu_ref[...].astype(BF16)
            wd_s[...] = wd_ref[...].astype(BF16)

        x = xbuf[slot]
        ext = x[:, D:D_EXT].astype(F32)
        lane = lax.broadcasted_iota(jnp.int32, (MB, LANE), 1)
        wr = jnp.sum(jnp.where((lane == e) | (lane == e + N_EXP), ext, 0.0), axis=-1, keepdims=True)
        hgu = _dot(x[:, 0:D], wgu_s[...]) + bgu_ref[...]
        gate = jnp.minimum(hgu[:, :D_FF], SWIGLU_LIMIT)
        up = jnp.clip(hgu[:, D_FF:], -SWIGLU_LIMIT, SWIGLU_LIMIT)
        act = (up + 1.0) * gate * jax.nn.sigmoid(SWIGLU_ALPHA * gate)
        y = (_dot(act.astype(BF16), wd_s[...]) + bd_ref[...]) * wr

        @pl.when(i >= 2)
        def _():
            scatter_wait(i - 2, slot)

        ybuf[slot] = y.astype(BF16)
        scatter_start(i, slot)

    @pl.when(i == nsteps - 1)
    def _():
        @pl.when(nact >= 2)
        def _():
            scatter_wait(nact - 2, nact % 2)

        scatter_wait(nact - 1, (nact - 1) % 2)


def _experts(blk_e, src, nact, s_sorted, w_gu, b_gu, w_dn, b_dn):
    rows = (N_TILES + 1) * TILE_ROWS
    y_init = jnp.zeros((rows, D), BF16)
    grid_spec = pltpu.PrefetchScalarGridSpec(
        num_scalar_prefetch=3,
        grid=(N_MACRO,),
        in_specs=[pl.BlockSpec(memory_space=pl.ANY), pl.BlockSpec(memory_space=pl.ANY),
                  pl.BlockSpec((None, D, 2 * D_FF), lambda i, be, sr, na: (be[i], 0, 0)),
                  pl.BlockSpec((None, 1, 2 * D_FF), lambda i, be, sr, na: (be[i], 0, 0)),
                  pl.BlockSpec((None, D_FF, D), lambda i, be, sr, na: (be[i], 0, 0)),
                  pl.BlockSpec((None, 1, D), lambda i, be, sr, na: (be[i], 0, 0))],
        out_specs=pl.BlockSpec(memory_space=pl.ANY),
        scratch_shapes=[pltpu.VMEM((2, MB, D_EXT), BF16), pltpu.VMEM((2, MB, D), BF16),
                        pltpu.VMEM((D, 2 * D_FF), BF16), pltpu.VMEM((D_FF, D), BF16),
                        pltpu.SemaphoreType.DMA((2,)), pltpu.SemaphoreType.DMA((2,))])
    return pl.pallas_call(
        _expert_kernel,
        grid_spec=grid_spec,
        out_shape=jax.ShapeDtypeStruct((rows, D), BF16),
        input_output_aliases={4: 0},
        compiler_params=_cparams(("arbitrary",), 56 * 1024 * 1024),
        name="moe_experts",
    )(blk_e, src, nact, s_sorted, y_init, w_gu, b_gu.reshape(N_EXP, 1, 2 * D_FF), w_dn,
      b_dn.reshape(N_EXP, 1, D))


def _combine_kernel(y_ref, pw_ref, x1_ref, g_ref, gate_ref, o_ref):
    lio = lax.broadcasted_iota(jnp.int32, (TM, TILE_ROWS), 1).astype(F32)
    pw = pw_ref[...]
    p = jnp.zeros((TM, TILE_ROWS), F32)
    for k in range(TOP_K):
        p = p + jnp.where(lio == pw[:, k:k + 1], 1.0, 0.0)
    f = _dot(p.astype(BF16), y_ref[...])
    o_ref[...] = x1_ref[...] + gate_ref[...] * _rms(f, g_ref[...])


def _combine(y_sorted, pw, x1, norm_g, mod4, layer):
    row = lambda c: pl.BlockSpec((TM, c), lambda t: (t, 0))
    return pl.pallas_call(
        _combine_kernel,
        grid=(N_TILES,),
        in_specs=[pl.BlockSpec((TILE_ROWS, D), lambda t: (t, 0)), row(LANE), row(D),
                  pl.BlockSpec((None, 1, D), lambda t: (3, 0, 0)), _mod_spec(layer, 5)],
        out_specs=row(D),
        out_shape=jax.ShapeDtypeStruct((N_TOK, D), F32),
        compiler_params=_cparams(("arbitrary",)),
        name="moe_combine",
    )(y_sorted, pw, x1, norm_g.reshape(4, 1, D), mod4)


def _block_tables(cnt):
    cnt = cnt[:, 0, :N_EXP].astype(jnp.int32)
    nb = (cnt + (SUB_ROWS - 1)) // SUB_ROWS
    lstart = (jnp.cumsum(nb, axis=1) - nb) * SUB_ROWS
    cum_nb = jnp.cumsum(nb, axis=0)
    nb_e = cum_nb[-1]
    mb_e = (nb_e + (SUB - 1)) // SUB
    mend = jnp.cumsum(mb_e)
    nact = mend[-1]
    i = jnp.arange(N_MACRO, dtype=jnp.int32)
    e_i = jnp.minimum(jnp.sum(mend[None, :] <= i[:, None], axis=1), N_EXP - 1).astype(jnp.int32)
    e_last = jnp.sum(mend <= nact - 1).astype(jnp.int32)
    blk_e = jnp.where(i < nact, e_i, e_last)
    mstart = mend - mb_e
    s = jnp.arange(SUB, dtype=jnp.int32)
    q = (i - mstart[blk_e])[:, None] * SUB + s[None, :]
    valid = (i < nact)[:, None] & (q < nb_e[blk_e][:, None])
    cum_t = cum_nb.T[blk_e]
    j = jnp.sum(cum_t[:, None, :] <= q[:, :, None], axis=2)
    j = jnp.minimum(j, N_TILES - 1)
    prev = jnp.take_along_axis(cum_t - nb.T[blk_e], j, axis=1)
    row = j * TILE_ROWS + lstart[j, blk_e[:, None]] + (q - prev) * SUB_ROWS
    spare = N_TILES * TILE_ROWS + ((i % 2)[:, None] * SUB + s[None, :]) * SUB_ROWS
    src = jnp.where(valid, row, spare).astype(jnp.int32).reshape(-1)
    return blk_e.astype(jnp.int32), src, nact.reshape(1).astype(jnp.int32)


def _moe(mixes, w_outs, x, norm_g, mod4, layer, w_router, b_router, w_gu, b_gu, w_dn, b_dn):
    wr = jnp.pad(w_router, ((0, 0), (0, LANE - N_EXP)))
    br = jnp.pad(b_router, (0, LANE - N_EXP)).reshape(1, LANE)
    x1, hx, pw, pwt, cnt = _route(mixes, w_outs, x, norm_g, mod4, layer, wr, br)
    blk_e, src, nact = _block_tables(cnt)
    s_sorted = _dispatch(hx, pwt)
    y_sorted = _experts(blk_e, src, nact, s_sorted, w_gu, b_gu, w_dn, b_dn)
    return _combine(y_sorted, pw, x1, norm_g, mod4, layer)


def kernel(x_prompt, x_sample, c, c_ctx, state_delta, cache_diff_k, cache_diff_v, cache_mla_ckv, cache_mla_kpe, w_mod, b_mod, norm_g, w_router, b_router, w_gate_up, b_gate_up, w_down, b_down, w_in_even, conv_w, a_log, dt_bias, gdn_norm_g, diff_lambda, diff_norm_g, w_out_even, w_in_odd, q_norm_g, w_uq, kv_norm_g, w_ukv, w_out_odd):
    x = jnp.concatenate([x_prompt.reshape(N_CTX, D), x_sample.reshape(N_LAT, D)], axis=0)
    cond = jnp.concatenate([c_ctx[None, :], c, jnp.zeros((16 - 1 - B_LAT, D), F32)], axis=0)
    mod4 = _modulation(cond, w_mod, b_mod).reshape(DEPTH, 16, 1, N_MOD * D)
    cos64, sin64 = _rope_tables()
    rope2 = (jnp.tile(cos64, (1, 2)), jnp.tile(sin64, (1, 2)))
    rope1 = (jnp.pad(cos64, ((0, 0), (0, 64))), jnp.pad(sin64, ((0, 0), (0, 64))))

    we = w_in_even[0]
    w_cat = jnp.concatenate([we[:, 0:2048], we[:, 2064:3600], we[:, 2048:2064],
                             jnp.zeros((D, LANE - 16), F32)], axis=1).astype(BF16)
    qkv, z, qd, kd, vd, ab = _in_even(x, norm_g[0, 0:1], mod4, 0, w_cat)
    gn = gdn_norm_g[0].reshape(1, GDN_DV)
    oa_c, new_state = _gdn(qkv, z, ab, conv_w[0], a_log[0], dt_bias[0], gn, None, B_CTX, T_CTX, 0)
    oa_l = _gdn(qkv, z, ab, conv_w[0], a_log[0], dt_bias[0], gn, state_delta[:, 0], B_LAT, T_LAT, N_CTX)
    lam_init = 0.8 - 0.6 * math.exp(-0.3 * 0)
    dgn = diff_norm_g[0].reshape(1, 2 * DIFF_DH)
    od_c = _diff_attn(qd, kd, vd, diff_lambda[0], dgn, lam_init, B_CTX, T_CTX, 0)
    cache_dk = cache_diff_k[:, 0].reshape(B_LAT, PAST, DIFF_H * 2 * DIFF_DH)
    cache_dv = cache_diff_v[:, 0].reshape(B_LAT, PAST, DIFF_H * 2 * DIFF_DH)
    od_l = _diff_attn(qd, kd, vd, diff_lambda[0], dgn, lam_init, B_LAT, T_LAT, N_CTX,
                      cache=(cache_dk, cache_dv), rope=rope2)
    o_a = jnp.concatenate([oa_c, oa_l], axis=0)
    o_d = jnp.concatenate([od_c, od_l], axis=0)
    wo = w_out_even[0].astype(BF16)
    x = _moe([o_a, o_d], [wo[:512], wo[512:]], x, norm_g[0], mod4, 0, w_router[0], b_router[0],
             w_gate_up[0], b_gate_up[0], w_down[0], b_down[0])
    new_dk = kd[:N_CTX].reshape(B_CTX, 1, T_CTX, DIFF_H, 2, DIFF_DH)
    new_dv = vd[:N_CTX].reshape(B_CTX, 1, T_CTX, DIFF_H, 2 * DIFF_DH)
    new_state = new_state.reshape(B_CTX, 1, 2, GDN_H, GDN_DK, GDN_DV)

    w_odd = jnp.pad(w_in_odd[0], ((0, 0), (0, ODD_W - (Q_LORA + KV_LORA + QK_ROPE)))).astype(BF16)
    wuq = w_uq[0].reshape(Q_LORA, MLA_H, QK_NOPE + QK_ROPE)
    wuq = jnp.pad(wuq, ((0, 0), (0, 0), (0, MLA_QW - (QK_NOPE + QK_ROPE)))).reshape(Q_LORA, MLA_H * MLA_QW)
    wukv = w_ukv[0].astype(BF16)
    q, kv, ckv, kpe = _in_odd(x, norm_g[1, 0:1], mod4, 1, w_odd, q_norm_g[0].reshape(1, Q_LORA),
                              wuq.astype(BF16), kv_norm_g[0].reshape(1, KV_LORA), wukv)
    om_c = _mla_attn(q, kv, kpe, B_CTX, T_CTX, 0)
    cache_kv = _matmul(cache_mla_ckv[:, 0].reshape(B_LAT * PAST, KV_LORA), wukv)
    cache_kpe = jnp.pad(cache_mla_kpe[:, 0].reshape(B_LAT * PAST, QK_ROPE), ((0, 0), (0, LANE - QK_ROPE)))
    om_l = _mla_attn(q, kv, kpe, B_LAT, T_LAT, N_CTX, cache=(cache_kv, cache_kpe), rope=rope1)
    o_m = jnp.concatenate([om_c, om_l], axis=0)
    x = _moe([o_m], [w_out_odd[0].astype(BF16)], x, norm_g[1], mod4, 1, w_router[1], b_router[1],
             w_gate_up[1], b_gate_up[1], w_down[1], b_down[1])
    new_ckv = ckv[:N_CTX].reshape(B_CTX, 1, T_CTX, KV_LORA)
    new_kpe = kpe[:N_CTX, :QK_ROPE].reshape(B_CTX, 1, T_CTX, QK_ROPE)

    return (x[:N_CTX].reshape(B_CTX, T_CTX, D), x[N_CTX:].reshape(B_LAT, T_LAT, D),
            new_state, new_dk, new_dv, new_ckv, new_kpe)
```

```python
import functools
import math

import jax
import jax.numpy as jnp
from jax import lax
from jax.experimental import pallas as pl
from jax.experimental.pallas import tpu as pltpu

F32 = jnp.float32
BF16 = jnp.bfloat16
HIGHEST = lax.Precision.HIGHEST

D = 1024
B_CTX, T_CTX = 16, 256
B_LAT, T_LAT = 8, 1024
PAST = 256
N_CTX = B_CTX * T_CTX
N_LAT = B_LAT * T_LAT
N_TOK = N_CTX + N_LAT
DEPTH = 2
N_MOD = 6
GRID_W = 64
GDN_H, GDN_DK, GDN_DV = 4, 128, 128
CONV_W = 5
CHUNK = 64
DIFF_H, DIFF_DH = 4, 64
MLA_H, Q_LORA, KV_LORA, QK_NOPE, QK_ROPE, V_HEAD = 8, 256, 256, 128, 64, 128
N_EXP, TOP_K, D_FF = 32, 4, 1024
SWIGLU_LIMIT, SWIGLU_ALPHA = 7.0, 1.702
ROPE_BASE = 10000.0
EPS = 1e-6

LANE = 128
TM = 256
N_TILES = N_TOK // TM
SUB_ROWS = 16
TILE_ROWS = TOP_K * TM + N_EXP * SUB_ROWS
MB = 256
SUB = MB // SUB_ROWS
N_MACRO = -(-((TOP_K * N_TOK) // SUB_ROWS + N_TILES * N_EXP + N_EXP * (SUB - 1)) // SUB)
D_EXT = D + LANE
VMEM_LIMIT = 48 * 1024 * 1024


def _cparams(sem, vmem=VMEM_LIMIT):
    return pltpu.CompilerParams(dimension_semantics=sem, vmem_limit_bytes=vmem)


def _mod_row(t):
    return jnp.where(t < N_CTX // TM, 0, 1 + (t - N_CTX // TM) // (T_LAT // TM))


def _rms(x, g):
    return x * lax.rsqrt(jnp.mean(x * x, axis=-1, keepdims=True) + EPS) * g


def _silu(x):
    return x * jax.nn.sigmoid(x)


def _dot(a, b):
    return jnp.dot(a, b, preferred_element_type=F32)


def _dot_nt(a, b):
    return lax.dot_general(a, b, (((1,), (1,)), ((), ())), preferred_element_type=F32)


def _mod_kernel(c_ref, w_ref, b_ref, o_ref):
    s = _silu(c_ref[...])
    o_ref[...] = _dot(s.astype(BF16), w_ref[...].astype(BF16)) + b_ref[...]


def _modulation(cond, w_mod, b_mod):
    nc = 1536
    return pl.pallas_call(
        _mod_kernel,
        grid=(DEPTH, N_MOD * D // nc),
        in_specs=[pl.BlockSpec((16, D), lambda i, j: (0, 0)),
                  pl.BlockSpec((None, D, nc), lambda i, j: (i, 0, j)),
                  pl.BlockSpec((None, 1, nc), lambda i, j: (i, 0, j))],
        out_specs=pl.BlockSpec((None, 16, nc), lambda i, j: (i, 0, j)),
        out_shape=jax.ShapeDtypeStruct((DEPTH, 16, N_MOD * D), F32),
        compiler_params=_cparams(("arbitrary", "arbitrary")),
        name="modulation",
    )(cond, w_mod, b_mod.reshape(DEPTH, 1, N_MOD * D))


def _mod_spec(layer, k):
    return pl.BlockSpec((None, None, 1, D), lambda t: (layer, _mod_row(t), 0, k))


EVEN_W = 2 * 512 + 512 + 512 + 3 * 512 + LANE


def _in_even_kernel(x_ref, g_ref, sh_ref, sc_ref, w_ref, qkv_ref, z_ref, qd_ref, kd_ref, vd_ref, ab_ref):
    h = _rms(x_ref[...], g_ref[...]) * (1.0 + sc_ref[...]) + sh_ref[...]
    y = _dot(h.astype(BF16), w_ref[...])
    qkv_ref[...] = y[:, 0:1536]
    z_ref[...] = y[:, 1536:2048]
    qd_ref[...] = y[:, 2048:2560]
    kd_ref[...] = y[:, 2560:3072]
    vd_ref[...] = y[:, 3072:3584]
    ab_ref[...] = y[:, 3584:3712]


def _in_even(x, g, mod4, layer, w):
    row = lambda c: pl.BlockSpec((TM, c), lambda t: (t, 0))
    shp = lambda c: jax.ShapeDtypeStruct((N_TOK, c), F32)
    return pl.pallas_call(
        _in_even_kernel,
        grid=(N_TILES,),
        in_specs=[row(D), pl.BlockSpec((1, D), lambda t: (0, 0)), _mod_spec(layer, 0), _mod_spec(layer, 1),
                  pl.BlockSpec((D, EVEN_W), lambda t: (0, 0))],
        out_specs=[row(1536), row(512), row(512), row(512), row(512), row(LANE)],
        out_shape=[shp(1536), shp(512), shp(512), shp(512), shp(512), shp(LANE)],
        compiler_params=_cparams(("arbitrary",)),
        name="in_proj_even",
    )(x, g, mod4, mod4, w)


def _split_bf16(a):
    hi = a.astype(BF16)
    lo = (a - hi.astype(F32)).astype(BF16)
    return hi, lo


def _mm3(a, b):
    return _dot(a[0], b[0]) + (_dot(a[0], b[1]) + _dot(a[1], b[0]))


def _unit_tri_inverses(ls, eye, blk, bd):
    rhs = lambda x: _split_bf16(bd(x))
    d8 = [jnp.where(blk(8), l, 0.0) for l in ls]
    x2 = [_mm3(_split_bf16(d), rhs(d)) for d in d8]
    x2r = [rhs(x) for x in x2]
    ts = [eye - d for d in d8]
    ts = [t + _mm3(_split_bf16(t), xr) for t, xr in zip(ts, x2r)]
    x4 = [_mm3(_split_bf16(x), xr) for x, xr in zip(x2, x2r)]
    ts = [t + _mm3(_split_bf16(t), rhs(x)) for t, x in zip(ts, x4)]
    for s in (8, 16, 32):
        offs = [jnp.where(blk(2 * s) & jnp.logical_not(blk(s)), l, 0.0) for l in ls]
        ys = [_mm3(_split_bf16(o), rhs(t)) for o, t in zip(offs, ts)]
        ts = [t - _mm3(_split_bf16(t), rhs(y)) for t, y in zip(ts, ys)]
    return ts


def _softplus(x):
    return jnp.maximum(x, 0.0) + jnp.log1p(jnp.exp(-jnp.abs(x)))


def _gdn_kernel(has_init, T, alog_ref, dtb_ref, qkv_ref, cw_ref, z_ref, ab_ref, gn_ref, *rest):
    if has_init:
        s0_ref, o_ref = rest[:2]
        st_ref = None
    else:
        o_ref, st_ref = rest[:2]
        s0_ref = None
    xp, qs, ks, vs, abt_s, uo_s, w_s, qg_s, kgt_s, in_s, gl_s, st_s = rest[2:]
    n = T // CHUNK
    H = GDN_H

    xp[0:8, :] = jnp.zeros((8, LANE), F32)
    xp[T + 8:T + 16, :] = jnp.zeros((8, LANE), F32)

    def l2n(x):
        return x * lax.rsqrt(jnp.sum(x * x, axis=-1, keepdims=True) + EPS)

    for blk in range(3 * H):
        cols = slice(blk * LANE, (blk + 1) * LANE)
        dst = slice((blk % H) * LANE, (blk % H + 1) * LANE)
        xp[8:T + 8, :] = qkv_ref[:, cols]
        w = cw_ref[:, cols]
        acc = xp[6:T + 6, :] * w[0:1, :]
        for i in range(1, CONV_W):
            acc = acc + xp[6 + i:T + 6 + i, :] * w[i:i + 1, :]
        y = _silu(acc)
        if blk < H:
            qs[:, dst] = l2n(y) * (GDN_DK ** -0.5)
        elif blk < 2 * H:
            ks[:, dst] = l2n(y)
        else:
            vs[:, dst] = y
    abt_s[...] = ab_ref[...].T[0:16, :]

    ii = lax.broadcasted_iota(jnp.int32, (CHUNK, LANE), 0)
    lane = lax.broadcasted_iota(jnp.int32, (CHUNK, LANE), 1)
    jm = lane & (CHUNK - 1)
    fwd = lane < CHUNK
    order = jnp.where(fwd, ii - jm, jm - ii)
    incl = order >= 0
    strict = order > 0
    incl_t = order <= 0
    eye = jnp.where(order == 0, 1.0, 0.0).astype(F32)
    blk_mask = lambda s: (ii // s) == (jm // s)
    zeros_c = jnp.zeros((CHUNK, LANE), F32)
    zeros_c16 = jnp.zeros((CHUNK, LANE), BF16)

    def bd(x):
        return jnp.concatenate([jnp.where(fwd, x, 0.0), jnp.where(fwd, 0.0, x)], axis=0)

    lane_row = lax.broadcasted_iota(jnp.int32, (1, LANE), 1)
    sub_col = lax.broadcasted_iota(jnp.int32, (16, 1), 0)
    a_lane, dt_lane = jnp.zeros((1, LANE), F32), jnp.zeros((1, LANE), F32)
    a_col, dt_col = jnp.zeros((16, 1), F32), jnp.zeros((16, 1), F32)
    for d in range(2):
        for h in range(H):
            a_lane = jnp.where(lane_row == d * H + h, alog_ref[d, h], a_lane)
            dt_lane = jnp.where(lane_row == d * H + h, dtb_ref[d, h], dt_lane)
            a_col = jnp.where(sub_col == d * H + h, alog_ref[d, h], a_col)
            dt_col = jnp.where(sub_col == d * H + h, dtb_ref[d, h], dt_col)
    nega_lane, nega_col = -jnp.exp(a_lane), -jnp.exp(a_col)

    def prep(p, carry):
        r128 = pl.multiple_of(p * LANE, LANE)
        g_rows = nega_col * _softplus(abt_s[:, pl.ds(r128, LANE)] + dt_col)
        sysm = []
        for cc in range(2):
            r0 = pl.multiple_of(r128 + cc * CHUNK, CHUNK)
            abc = ab_ref[pl.ds(r0, CHUNK), :]
            g_cols = nega_lane * _softplus(abc + dt_lane)
            b_cols = jax.nn.sigmoid(abc)
            for h in range(H):
                hs = slice(h * LANE, (h + 1) * LANE)
                tok = slice(cc * CHUNK, (cc + 1) * CHUNK)
                sysm.append(dict(
                    c=2 * p + cc, rows=pl.ds(r0, CHUNK), h=h,
                    qc=qs[pl.ds(r0, CHUNK), hs], kc=ks[pl.ds(r0, CHUNK), hs], vc=vs[pl.ds(r0, CHUNK), hs],
                    g_col=jnp.where(fwd, g_cols[:, h:h + 1], g_cols[:, H + h:H + h + 1]),
                    g_row=jnp.concatenate([g_rows[h:h + 1, tok], g_rows[H + h:H + h + 1, tok]], axis=1),
                    beta=(b_cols[:, 2 * H + h:2 * H + h + 1], b_cols[:, 3 * H + h:3 * H + h + 1])))
        for s in sysm:
            k16 = s["kc"].astype(BF16)
            kk16 = jnp.concatenate([k16, k16], axis=0)
            s["kk"] = _dot_nt(k16, kk16)
            s["qk"] = _dot_nt(s["qc"].astype(BF16), kk16)
        ls = []
        for s in sysm:
            m = jnp.where(incl, s["g_row"], 0.0)
            s["gcf"] = jnp.sum(jnp.where(fwd, m, 0.0), axis=1, keepdims=True)
            s["gcb"] = jnp.sum(jnp.where(fwd, 0.0, m), axis=1, keepdims=True)
            gc_col = jnp.where(fwd, s["gcf"], s["gcb"])
            gc_row = jnp.sum(jnp.where(incl_t, s["g_col"], 0.0), axis=0, keepdims=True)
            decay = jnp.where(incl, jnp.exp(jnp.where(incl, gc_col - gc_row, 0.0)), 0.0)
            beta = jnp.where(fwd, s["beta"][0], s["beta"][1])
            ls.append(jnp.where(strict, s["kk"] * beta * decay, 0.0))
            s["intra"] = s["qk"] * decay
        ts = _unit_tri_inverses(ls, eye, blk_mask, bd)
        for s, t in zip(sysm, ts):
            h, c, rows = s["h"], s["c"], s["rows"]
            egs = (jnp.exp(s["gcf"]), jnp.exp(s["gcb"]))
            r_f = jnp.concatenate([s["vc"] * s["beta"][0], s["kc"] * (s["beta"][0] * egs[0]), zeros_c, zeros_c], axis=1)
            r_b = jnp.concatenate([zeros_c, zeros_c, s["vc"] * s["beta"][1], s["kc"] * (s["beta"][1] * egs[1])], axis=1)
            uw = _dot(t.astype(BF16), jnp.concatenate([r_f, r_b], axis=0).astype(BF16))
            g_last = (s["gcf"][CHUNK - 1:CHUNK, :], s["gcb"][0:1, :])
            kgs = []
            for d in range(2):
                k = d * H + h
                uo_s[k, rows, :] = uw[:, 2 * d * LANE:(2 * d + 1) * LANE]
                w_s[k, rows, :] = uw[:, (2 * d + 1) * LANE:(2 * d + 2) * LANE].astype(BF16)
                qg_s[k, rows, :] = (s["qc"] * egs[d]).astype(BF16)
                gc = s["gcf"] if d == 0 else s["gcb"]
                kgs.append(s["kc"] * jnp.exp(g_last[d] - gc))
                gl_s[k, c] = jnp.broadcast_to(jnp.exp(g_last[d]), (8, LANE))
            in_s[h, c] = s["intra"].astype(BF16)
            kgt_s[h, c] = jnp.concatenate(kgs, axis=0).T.astype(BF16)
        return carry

    lax.fori_loop(0, T // LANE, prep, 0)

    for d in range(2):
        for h in range(H):
            st_s[d * H + h] = s0_ref[d, h] if has_init else jnp.zeros((GDN_DK, GDN_DV), F32)

    def scan(i, carry):
        cs = (i, n - 1 - i)
        ks_ = range(2 * H)
        c_of = [cs[k // H] for k in ks_]
        rows = [pl.ds(pl.multiple_of(c * CHUNK, CHUNK), CHUNK) for c in c_of]
        s16 = [st_s[k].astype(BF16) for k in ks_]
        ws = [_dot(w_s[k, rows[k], :], s16[k]) for k in ks_]
        qss = [_dot(qg_s[k, rows[k], :], s16[k]) for k in ks_]
        vn = [(uo_s[k, rows[k], :] - ws[k]).astype(BF16) for k in ks_]
        vpad = [jnp.concatenate([vn[k], zeros_c16] if k < H else [zeros_c16, vn[k]], axis=0) for k in ks_]
        o = [qss[k] + _dot(in_s[k % H, c_of[k]], vpad[k]) for k in ks_]
        snew = [st_s[k] * gl_s[k, c_of[k]][0:1, :] + _dot(kgt_s[k % H, c_of[k]], vpad[k]) for k in ks_]
        for k in ks_:
            uo_s[k, rows[k], :] = o[k]
            st_s[k] = snew[k]
        return carry

    lax.fori_loop(0, n, scan, 0)
    for h in range(H):
        hs = slice(h * LANE, (h + 1) * LANE)
        o_ref[:, hs] = _rms(uo_s[h] + uo_s[H + h], gn_ref[...]) * _silu(z_ref[:, hs])
        if st_ref is not None:
            st_ref[0, h] = st_s[h]
            st_ref[1, h] = st_s[H + h]


def _gdn(qkv, z, ab, conv_w, a_log, dt_bias, gn, s0, nb, T, row0):
    n = T // CHUNK
    blk0 = row0 // T
    has_init = s0 is not None
    H, W = GDN_H, GDN_H * LANE
    smem = pl.BlockSpec(memory_space=pltpu.SMEM)
    in_specs = [smem, smem,
                pl.BlockSpec((T, 3 * W), lambda b: (blk0 + b, 0), pipeline_mode=pl.Buffered(1)),
                pl.BlockSpec((CONV_W, 3 * W), lambda b: (0, 0)),
                pl.BlockSpec((T, W), lambda b: (blk0 + b, 0)),
                pl.BlockSpec((T, LANE), lambda b: (blk0 + b, 0)),
                pl.BlockSpec((1, LANE), lambda b: (0, 0))]
    args = [a_log, dt_bias, qkv, conv_w, z, ab, gn]
    st_spec = pl.BlockSpec((None, 2, H, GDN_DK, GDN_DV), lambda b: (b, 0, 0, 0, 0))
    o_spec = pl.BlockSpec((T, W), lambda b: (b, 0))
    o_shape = jax.ShapeDtypeStruct((nb * T, W), F32)
    if has_init:
        in_specs.append(st_spec)
        args.append(s0)
        out_specs, out_shape = o_spec, o_shape
    else:
        out_specs = [o_spec, st_spec]
        out_shape = [o_shape, jax.ShapeDtypeStruct((nb, 2, H, GDN_DK, GDN_DV), F32)]
    scratch = [pltpu.VMEM((T + 16, LANE), F32), pltpu.VMEM((T, W), F32), pltpu.VMEM((T, W), F32),
               pltpu.VMEM((T, W), F32), pltpu.VMEM((16, T), F32),
               pltpu.VMEM((2 * H, T, LANE), F32), pltpu.VMEM((2 * H, T, LANE), BF16),
               pltpu.VMEM((2 * H, T, LANE), BF16), pltpu.VMEM((H, n, LANE, LANE), BF16),
               pltpu.VMEM((H, n, CHUNK, LANE), BF16), pltpu.VMEM((2 * H, n, 8, LANE), F32),
               pltpu.VMEM((2 * H, GDN_DK, GDN_DV), F32)]
    return pl.pallas_call(
        functools.partial(_gdn_kernel, has_init, T),
        grid=(nb,),
        in_specs=in_specs, out_specs=out_specs, out_shape=out_shape, scratch_shapes=scratch,
        compiler_params=_cparams(("arbitrary",), 56 * 1024 * 1024),
        name="gdn_lat" if has_init else "gdn_ctx",
    )(*args)


def _rope(x, cos, sin_signed):
    lane = lax.broadcasted_iota(jnp.int32, x.shape, 1)
    up = pltpu.roll(x, LANE - 16, axis=1)
    dn = pltpu.roll(x, 16, axis=1)
    rot = jnp.where((lane % 32) < 16, up, dn)
    return x * cos + rot * sin_signed


def _rope_tables():
    rows = T_LAT // GRID_W
    r = jnp.repeat(jnp.arange(rows, dtype=F32), GRID_W)
    col = jnp.tile(jnp.arange(GRID_W, dtype=F32), rows)
    quarter = 16
    inv = ROPE_BASE ** (-jnp.arange(quarter, dtype=F32) / quarter)
    ar = r[:, None] * inv
    ac = col[:, None] * inv
    ang = jnp.concatenate([ar, ar, ac, ac], axis=-1)
    sign = jnp.tile(jnp.concatenate([-jnp.ones(16, F32), jnp.ones(16, F32)]), 2)
    cos, sin = jnp.cos(ang), jnp.sin(ang) * sign
    return cos, sin


def _softmax_parts(parts):
    m = functools.reduce(jnp.maximum, [jnp.max(s, axis=-1, keepdims=True) for s in parts])
    es = [jnp.exp(s - m) for s in parts]
    den = functools.reduce(lambda a, b: a + b, [jnp.sum(e, axis=-1, keepdims=True) for e in es])
    return [e / den for e in es]


def _diff_kernel(latent, T, lam_init, q_ref, k_ref, v_ref, lam_ref, gn_ref, *rest):
    if latent:
        cos_ref, sin_ref, ck_ref, cv_ref, o_ref = rest
    else:
        (o_ref,) = rest
    scale = DIFF_DH ** -0.5
    lp = lam_ref[...]
    lam = (jnp.exp(jnp.sum(lp[0:1] * lp[1:2], axis=-1, keepdims=True))
           - jnp.exp(jnp.sum(lp[2:3] * lp[3:4], axis=-1, keepdims=True)) + lam_init)
    k = k_ref[...]
    if latent:
        k = _rope(k, cos_ref[...], sin_ref[...])
    k16 = k.astype(BF16)
    v16 = v_ref[...].astype(BF16)
    if latent:
        ck16 = ck_ref[...].astype(BF16)
        cv16 = cv_ref[...].astype(BF16)
    tq = 256
    lane = lax.broadcasted_iota(jnp.int32, (tq, LANE), 1)
    for qb in range(T // tq):
        q = q_ref[qb * tq:(qb + 1) * tq, :]
        if latent:
            q = _rope(q, cos_ref[qb * tq:(qb + 1) * tq, :], sin_ref[qb * tq:(qb + 1) * tq, :])
        ps = []
        for m in range(2):
            qm = jnp.where((lane // DIFF_DH) == m, q, 0.0).astype(BF16)
            parts = [_dot_nt(qm, k16) * scale]
            if latent:
                parts.insert(0, _dot_nt(qm, ck16) * scale)
            ps.append(_softmax_parts(parts))
        o = _dot((ps[0][-1] - lam * ps[1][-1]).astype(BF16), v16)
        if latent:
            o = o + _dot((ps[0][0] - lam * ps[1][0]).astype(BF16), cv16)
        o_ref[qb * tq:(qb + 1) * tq, :] = _rms(o, gn_ref[...]) * (1.0 - lam_init)


def _diff_attn(qd, kd, vd, lam_p, gn, lam_init, nb, T, row0, cache=None, rope=None):
    blk0 = row0 // T
    latent = cache is not None
    hcol = pl.BlockSpec((T, LANE), lambda b, h: (blk0 + b, h))
    in_specs = [hcol, hcol, hcol, pl.BlockSpec((4, DIFF_DH), lambda b, h: (0, 0)),
                pl.BlockSpec((1, LANE), lambda b, h: (0, 0))]
    args = [qd, kd, vd, lam_p, gn]
    if latent:
        tab = pl.BlockSpec((T, LANE), lambda b, h: (0, 0))
        cch = pl.BlockSpec((None, PAST, LANE), lambda b, h: (b, 0, h))
        in_specs += [tab, tab, cch, cch]
        args += [rope[0], rope[1], cache[0], cache[1]]
    return pl.pallas_call(
        functools.partial(_diff_kernel, latent, T, lam_init),
        grid=(nb, DIFF_H),
        in_specs=in_specs,
        out_specs=pl.BlockSpec((T, LANE), lambda b, h: (b, h)),
        out_shape=jax.ShapeDtypeStruct((nb * T, DIFF_H * 2 * DIFF_DH), F32),
        compiler_params=_cparams(("arbitrary", "arbitrary")),
        name="diff_lat" if latent else "diff_ctx",
    )(*args)


ODD_W = Q_LORA + KV_LORA + LANE
MLA_QW = 2 * LANE


def _in_odd_kernel(x_ref, g_ref, sh_ref, sc_ref, w_ref, qg_ref, wuq_ref, kvg_ref, wukv_ref,
                   q_ref, kv_ref, ckv_ref, kpe_ref):
    h = _rms(x_ref[...], g_ref[...]) * (1.0 + sc_ref[...]) + sh_ref[...]
    y = _dot(h.astype(BF16), w_ref[...])
    qn = _rms(y[:, 0:Q_LORA], qg_ref[...])
    q_ref[...] = _dot(qn.astype(BF16), wuq_ref[...])
    ckv = _rms(y[:, Q_LORA:Q_LORA + KV_LORA], kvg_ref[...])
    ckv_ref[...] = ckv
    kv_ref[...] = _dot(ckv.astype(BF16), wukv_ref[...])
    kpe_ref[...] = y[:, Q_LORA + KV_LORA:ODD_W]


def _in_odd(x, g, mod4, layer, w, qg, wuq, kvg, wukv):
    row = lambda c: pl.BlockSpec((TM, c), lambda t: (t, 0))
    full = lambda a: pl.BlockSpec(a.shape, lambda t: (0, 0))
    shp = lambda c: jax.ShapeDtypeStruct((N_TOK, c), F32)
    return pl.pallas_call(
        _in_odd_kernel,
        grid=(N_TILES,),
        in_specs=[row(D), full(g), _mod_spec(layer, 0), _mod_spec(layer, 1), full(w), full(qg), full(wuq),
                  full(kvg), full(wukv)],
        out_specs=[row(MLA_H * MLA_QW), row(MLA_H * MLA_QW), row(KV_LORA), row(LANE)],
        out_shape=[shp(MLA_H * MLA_QW), shp(MLA_H * MLA_QW), shp(KV_LORA), shp(LANE)],
        compiler_params=_cparams(("arbitrary",)),
        name="in_proj_odd",
    )(x, g, mod4, mod4, w, qg, wuq, kvg, wukv)


def _mm_kernel(x_ref, w_ref, o_ref):
    o_ref[...] = _dot(x_ref[...].astype(BF16), w_ref[...])


def _matmul(x, w):
    m, k = x.shape
    n = w.shape[1]
    return pl.pallas_call(
        _mm_kernel,
        grid=(m // TM,),
        in_specs=[pl.BlockSpec((TM, k), lambda i: (i, 0)), pl.BlockSpec((k, n), lambda i: (0, 0))],
        out_specs=pl.BlockSpec((TM, n), lambda i: (i, 0)),
        out_shape=jax.ShapeDtypeStruct((m, n), F32),
        compiler_params=_cparams(("arbitrary",)),
        name="cache_kv_proj",
    )(x, w)


def _mla_kernel(latent, T, q_ref, kv_ref, kpe_ref, *rest):
    if latent:
        cos_ref, sin_ref, ckv_ref, ckpe_ref, o_ref = rest
    else:
        (o_ref,) = rest
    scale = (QK_NOPE + QK_ROPE) ** -0.5
    kv = kv_ref[...]
    kn16 = kv[:, :LANE].astype(BF16)
    v16 = kv[:, LANE:].astype(BF16)
    kpe = kpe_ref[...]
    if latent:
        kpe = _rope(kpe, cos_ref[...], sin_ref[...])
        ckv = ckv_ref[...]
        ckn16 = ckv[:, :LANE].astype(BF16)
        cv16 = ckv[:, LANE:].astype(BF16)
        ckpe16 = ckpe_ref[...].astype(BF16)
    kpe16 = kpe.astype(BF16)
    tq = 256
    for qb in range(T // tq):
        rows = slice(qb * tq, (qb + 1) * tq)
        q = q_ref[rows, :]
        qn16 = q[:, :LANE].astype(BF16)
        qp = q[:, LANE:]
        if latent:
            qp = _rope(qp, cos_ref[rows, :], sin_ref[rows, :])
        qp16 = qp.astype(BF16)
        parts = [(_dot_nt(qn16, kn16) + _dot_nt(qp16, kpe16)) * scale]
        if latent:
            parts.insert(0, (_dot_nt(qn16, ckn16) + _dot_nt(qp16, ckpe16)) * scale)
        p = _softmax_parts(parts)
        o = _dot(p[-1].astype(BF16), v16)
        if latent:
            o = o + _dot(p[0].astype(BF16), cv16)
        o_ref[rows, :] = o


def _mla_attn(q, kv, kpe, nb, T, row0, cache=None, rope=None):
    blk0 = row0 // T
    latent = cache is not None
    head = pl.BlockSpec((T, MLA_QW), lambda b, h: (blk0 + b, h))
    in_specs = [head, head, pl.BlockSpec((T, LANE), lambda b, h: (blk0 + b, 0))]
    args = [q, kv, kpe]
    if latent:
        tab = pl.BlockSpec((T, LANE), lambda b, h: (0, 0))
        in_specs += [tab, tab, pl.BlockSpec((PAST, MLA_QW), lambda b, h: (b, h)),
                     pl.BlockSpec((PAST, LANE), lambda b, h: (b, 0))]
        args += [rope[0], rope[1], cache[0], cache[1]]
    return pl.pallas_call(
        functools.partial(_mla_kernel, latent, T),
        grid=(nb, MLA_H),
        in_specs=in_specs,
        out_specs=pl.BlockSpec((T, V_HEAD), lambda b, h: (b, h)),
        out_shape=jax.ShapeDtypeStruct((nb * T, MLA_H * V_HEAD), F32),
        compiler_params=_cparams(("arbitrary", "arbitrary")),
        name="mla_lat" if latent else "mla_ctx",
    )(*args)


def _route_kernel(n_mix, *refs):
    mix_refs = refs[:n_mix]
    w_refs = refs[n_mix:2 * n_mix]
    (x_ref, g1_ref, gate1_ref, g2_ref, sh2_ref, sc2_ref, wr_ref, br_ref,
     x1_ref, hx_ref, pw_ref, pwt_ref, cnt_ref) = refs[2 * n_mix:]
    y = _dot(mix_refs[0][...].astype(BF16), w_refs[0][...])
    for m in range(1, n_mix):
        y = y + _dot(mix_refs[m][...].astype(BF16), w_refs[m][...])
    x1 = x_ref[...] + gate1_ref[...] * _rms(y, g1_ref[...])
    x1_ref[...] = x1
    h2 = _rms(x1, g2_ref[...]) * (1.0 + sc2_ref[...]) + sh2_ref[...]

    logits = jnp.dot(h2, wr_ref[...], precision=HIGHEST, preferred_element_type=F32) + br_ref[...]
    lane_i = lax.broadcasted_iota(jnp.int32, (TM, LANE), 1)
    lane = lane_i.astype(F32)
    neg = jnp.float32(-jnp.inf)
    l = jnp.where(lane_i < N_EXP, logits, neg)
    sel = jnp.zeros((TM, LANE), F32)
    comb = jnp.zeros((TM, LANE), F32)
    wsum = jnp.zeros((TM, 1), F32)
    hots, wts = [], []
    for k in range(TOP_K):
        m = jnp.max(l, axis=-1, keepdims=True)
        idx = jnp.min(jnp.where(l == m, lane, float(LANE)), axis=-1, keepdims=True)
        hot = lane == idx
        if k == 0:
            m0 = m
        w = jnp.exp(m - m0)
        wsum = wsum + w
        hots.append(hot)
        wts.append(w)
        sel = sel + jnp.where(hot, 1.0, 0.0)
        comb = comb + jnp.where(hot, w, 0.0)
        l = jnp.where(hot, neg, l)
    inv = 1.0 / wsum
    comb = comb * inv

    cnt = jnp.sum(sel, axis=0, keepdims=True)
    cnt_ref[...] = cnt
    padded = jnp.floor((cnt + (SUB_ROWS - 1)) * (1.0 / SUB_ROWS)) * SUB_ROWS
    li = lax.broadcasted_iota(jnp.int32, (LANE, LANE), 0)
    lj = lax.broadcasted_iota(jnp.int32, (LANE, LANE), 1)
    before = jnp.where(li < lj, 1.0, 0.0).astype(BF16)
    start = _dot(jnp.broadcast_to(padded, (8, LANE)).astype(BF16), before)[0:1, :]
    ti = lax.broadcasted_iota(jnp.int32, (TM, TM), 0)
    tj = lax.broadcasted_iota(jnp.int32, (TM, TM), 1)
    earlier = jnp.where(tj < ti, 1.0, 0.0).astype(BF16)
    pos_full = _dot(earlier, sel.astype(BF16)) + start
    pw = jnp.zeros((TM, LANE), F32)
    for k in range(TOP_K):
        pos_k = jnp.sum(jnp.where(hots[k], pos_full, 0.0), axis=-1, keepdims=True)
        pw = pw + jnp.where(lane_i == k, pos_k, 0.0) + jnp.where(lane_i == TOP_K + k, wts[k] * inv, 0.0)
    pw_ref[...] = pw
    pwt_ref[...] = pw.T[0:8, :]

    chi, clo = _split_bf16(comb)
    ext = chi.astype(F32) + pltpu.roll(clo.astype(F32), N_EXP, axis=1)
    hx_ref[:, 0:D] = h2.astype(BF16)
    hx_ref[:, D:D_EXT] = ext.astype(BF16)


def _route(mixes, w_outs, x, norm_g, mod4, layer, w_router, b_router):
    n_mix = len(mixes)
    row = lambda c: pl.BlockSpec((TM, c), lambda t: (t, 0))
    full = lambda a: pl.BlockSpec(a.shape, lambda t: (0, 0))
    g = lambda k: pl.BlockSpec((None, 1, D), lambda t: (k, 0, 0))
    in_specs = ([row(m.shape[1]) for m in mixes] + [full(w) for w in w_outs]
                + [row(D), g(1), _mod_spec(layer, 2), g(2), _mod_spec(layer, 3), _mod_spec(layer, 4),
                   full(w_router), full(b_router)])
    ng = norm_g.reshape(4, 1, D)
    return pl.pallas_call(
        functools.partial(_route_kernel, n_mix),
        grid=(N_TILES,),
        in_specs=in_specs,
        out_specs=[row(D), row(D_EXT), row(LANE), pl.BlockSpec((None, 8, TM), lambda t: (t, 0, 0)),
                   pl.BlockSpec((None, 1, LANE), lambda t: (t, 0, 0))],
        out_shape=[jax.ShapeDtypeStruct((N_TOK, D), F32), jax.ShapeDtypeStruct((N_TOK, D_EXT), BF16),
                   jax.ShapeDtypeStruct((N_TOK, LANE), F32), jax.ShapeDtypeStruct((N_TILES, 8, TM), F32),
                   jax.ShapeDtypeStruct((N_TILES, 1, LANE), F32)],
        compiler_params=_cparams(("arbitrary",)),
        name="out_proj_route",
    )(*mixes, *w_outs, x, ng, mod4, ng, mod4, mod4, w_router, b_router)


def _dispatch_kernel(hx_ref, pwt_ref, s_ref):
    j = pl.program_id(0)

    @pl.when(j < N_TILES)
    def _():
        rio = lax.broadcasted_iota(jnp.int32, (TILE_ROWS, TM), 0).astype(F32)
        p = jnp.zeros((TILE_ROWS, TM), F32)
        for k in range(TOP_K):
            p = p + jnp.where(rio == pwt_ref[k:k + 1, :], 1.0, 0.0)
        s_ref[...] = _dot(p.astype(BF16), hx_ref[...]).astype(BF16)

    @pl.when(j == N_TILES)
    def _():
        s_ref[...] = jnp.zeros((TILE_ROWS, D_EXT), BF16)


def _dispatch(hx, pwt):
    clamp = lambda j: jnp.minimum(j, N_TILES - 1)
    return pl.pallas_call(
        _dispatch_kernel,
        grid=(N_TILES + 1,),
        in_specs=[pl.BlockSpec((TM, D_EXT), lambda j: (clamp(j), 0)),
                  pl.BlockSpec((None, 8, TM), lambda j: (clamp(j), 0, 0))],
        out_specs=pl.BlockSpec((TILE_ROWS, D_EXT), lambda j: (j, 0)),
        out_shape=jax.ShapeDtypeStruct(((N_TILES + 1) * TILE_ROWS, D_EXT), BF16),
        compiler_params=_cparams(("arbitrary",)),
        name="moe_dispatch",
    )(hx, pwt)


def _expert_kernel(blk_e_ref, src_ref, nact_ref, s_hbm, yinit_hbm, wgu_ref, bgu_ref, wd_ref, bd_ref, y_hbm,
                   xbuf, ybuf, wgu_s, wd_s, gsem, ssem):
    del yinit_hbm
    i = pl.program_id(0)
    nsteps = pl.num_programs(0)
    nact = nact_ref[0]
    slot = i % 2

    def gather_copy(step, sl, s):
        src = pl.multiple_of(src_ref[step * SUB + s], SUB_ROWS)
        return pltpu.make_async_copy(s_hbm.at[pl.ds(src, SUB_ROWS)],
                                     xbuf.at[sl, pl.ds(s * SUB_ROWS, SUB_ROWS)], gsem.at[sl])

    def scatter_copy(step, sl, s):
        dst = pl.multiple_of(src_ref[step * SUB + s], SUB_ROWS)
        return pltpu.make_async_copy(ybuf.at[sl, pl.ds(s * SUB_ROWS, SUB_ROWS)],
                                     y_hbm.at[pl.ds(dst, SUB_ROWS)], ssem.at[sl])

    def gather_start(step, sl):
        for s in range(SUB):
            gather_copy(step, sl, s).start()

    def gather_wait(step, sl):
        for s in range(SUB):
            gather_copy(step, sl, s).wait()

    def scatter_start(step, sl):
        for s in range(SUB):
            scatter_copy(step, sl, s).start()

    def scatter_wait(step, sl):
        for s in range(SUB):
            scatter_copy(step, sl, s).wait()

    @pl.when(i == 0)
    def _():
        gather_start(0, 0)

    @pl.when(i < nact)
    def _():
        gather_wait(i, slot)

        @pl.when(i + 1 < nact)
        def _():
            gather_start(i + 1, 1 - slot)

        e = blk_e_ref[i]
        e_prev = blk_e_ref[jnp.maximum(i - 1, 0)]

        @pl.when((i == 0) | (e != e_prev))
        def _():
            wgu_s[...] = wgu_ref[...].astype(BF16)
            wd_s[...] = wd_ref[...].astype(BF16)

        x = xbuf[slot]
        ext = x[:, D:D_EXT].astype(F32)
        lane = lax.broadcasted_iota(jnp.int32, (MB, LANE), 1)
        wr = jnp.sum(jnp.where((lane == e) | (lane == e + N_EXP), ext, 0.0), axis=-1, keepdims=True)
        hgu = _dot(x[:, 0:D], wgu_s[...]) + bgu_ref[...]
        gate = jnp.minimum(hgu[:, :D_FF], SWIGLU_LIMIT)
        up = jnp.clip(hgu[:, D_FF:], -SWIGLU_LIMIT, SWIGLU_LIMIT)
        act = (up + 1.0) * gate * jax.nn.sigmoid(SWIGLU_ALPHA * gate)
        y = (_dot(act.astype(BF16), wd_s[...]) + bd_ref[...]) * wr

        @pl.when(i >= 2)
        def _():
            scatter_wait(i - 2, slot)

        ybuf[slot] = y.astype(BF16)
        scatter_start(i, slot)

    @pl.when(i == nsteps - 1)
    def _():
        @pl.when(nact >= 2)
        def _():
            scatter_wait(nact - 2, nact % 2)

        scatter_wait(nact - 1, (nact - 1) % 2)


def _experts(blk_e, src, nact, s_sorted, w_gu, b_gu, w_dn, b_dn):
    rows = (N_TILES + 1) * TILE_ROWS
    y_init = jnp.zeros((rows, D), BF16)
    grid_spec = pltpu.PrefetchScalarGridSpec(
        num_scalar_prefetch=3,
        grid=(N_MACRO,),
        in_specs=[pl.BlockSpec(memory_space=pl.ANY), pl.BlockSpec(memory_space=pl.ANY),
                  pl.BlockSpec((None, D, 2 * D_FF), lambda i, be, sr, na: (be[i], 0, 0)),
                  pl.BlockSpec((None, 1, 2 * D_FF), lambda i, be, sr, na: (be[i], 0, 0)),
                  pl.BlockSpec((None, D_FF, D), lambda i, be, sr, na: (be[i], 0, 0)),
                  pl.BlockSpec((None, 1, D), lambda i, be, sr, na: (be[i], 0, 0))],
        out_specs=pl.BlockSpec(memory_space=pl.ANY),
        scratch_shapes=[pltpu.VMEM((2, MB, D_EXT), BF16), pltpu.VMEM((2, MB, D), BF16),
                        pltpu.VMEM((D, 2 * D_FF), BF16), pltpu.VMEM((D_FF, D), BF16),
                        pltpu.SemaphoreType.DMA((2,)), pltpu.SemaphoreType.DMA((2,))])
    return pl.pallas_call(
        _expert_kernel,
        grid_spec=grid_spec,
        out_shape=jax.ShapeDtypeStruct((rows, D), BF16),
        input_output_aliases={4: 0},
        compiler_params=_cparams(("arbitrary",), 56 * 1024 * 1024),
        name="moe_experts",
    )(blk_e, src, nact, s_sorted, y_init, w_gu, b_gu.reshape(N_EXP, 1, 2 * D_FF), w_dn,
      b_dn.reshape(N_EXP, 1, D))


def _combine_kernel(y_ref, pw_ref, x1_ref, g_ref, gate_ref, o_ref):
    lio = lax.broadcasted_iota(jnp.int32, (TM, TILE_ROWS), 1).astype(F32)
    pw = pw_ref[...]
    p = jnp.zeros((TM, TILE_ROWS), F32)
    for k in range(TOP_K):
        p = p + jnp.where(lio == pw[:, k:k + 1], 1.0, 0.0)
    f = _dot(p.astype(BF16), y_ref[...])
    o_ref[...] = x1_ref[...] + gate_ref[...] * _rms(f, g_ref[...])


def _combine(y_sorted, pw, x1, norm_g, mod4, layer):
    row = lambda c: pl.BlockSpec((TM, c), lambda t: (t, 0))
    return pl.pallas_call(
        _combine_kernel,
        grid=(N_TILES,),
        in_specs=[pl.BlockSpec((TILE_ROWS, D), lambda t: (t, 0)), row(LANE), row(D),
                  pl.BlockSpec((None, 1, D), lambda t: (3, 0, 0)), _mod_spec(layer, 5)],
        out_specs=row(D),
        out_shape=jax.ShapeDtypeStruct((N_TOK, D), F32),
        compiler_params=_cparams(("arbitrary",)),
        name="moe_combine",
    )(y_sorted, pw, x1, norm_g.reshape(4, 1, D), mod4)


def _block_tables(cnt):
    cnt = cnt[:, 0, :N_EXP].astype(jnp.int32)
    nb = (cnt + (SUB_ROWS - 1)) // SUB_ROWS
    lstart = (jnp.cumsum(nb, axis=1) - nb) * SUB_ROWS
    cum_nb = jnp.cumsum(nb, axis=0)
    nb_e = cum_nb[-1]
    mb_e = (nb_e + (SUB - 1)) // SUB
    mend = jnp.cumsum(mb_e)
    nact = mend[-1]
    i = jnp.arange(N_MACRO, dtype=jnp.int32)
    e_i = jnp.minimum(jnp.sum(mend[None, :] <= i[:, None], axis=1), N_EXP - 1).astype(jnp.int32)
    e_last = jnp.sum(mend <= nact - 1).astype(jnp.int32)
    blk_e = jnp.where(i < nact, e_i, e_last)
    mstart = mend - mb_e
    s = jnp.arange(SUB, dtype=jnp.int32)
    q = (i - mstart[blk_e])[:, None] * SUB + s[None, :]
    valid = (i < nact)[:, None] & (q < nb_e[blk_e][:, None])
    cum_t = cum_nb.T[blk_e]
    j = jnp.sum(cum_t[:, None, :] <= q[:, :, None], axis=2)
    j = jnp.minimum(j, N_TILES - 1)
    prev = jnp.take_along_axis(cum_t - nb.T[blk_e], j, axis=1)
    row = j * TILE_ROWS + lstart[j, blk_e[:, None]] + (q - prev) * SUB_ROWS
    spare = N_TILES * TILE_ROWS + ((i % 2)[:, None] * SUB + s[None, :]) * SUB_ROWS
    src = jnp.where(valid, row, spare).astype(jnp.int32).reshape(-1)
    return blk_e.astype(jnp.int32), src, nact.reshape(1).astype(jnp.int32)


def _moe(mixes, w_outs, x, norm_g, mod4, layer, w_router, b_router, w_gu, b_gu, w_dn, b_dn):
    wr = jnp.pad(w_router, ((0, 0), (0, LANE - N_EXP)))
    br = jnp.pad(b_router, (0, LANE - N_EXP)).reshape(1, LANE)
    x1, hx, pw, pwt, cnt = _route(mixes, w_outs, x, norm_g, mod4, layer, wr, br)
    blk_e, src, nact = _block_tables(cnt)
    s_sorted = _dispatch(hx, pwt)
    y_sorted = _experts(blk_e, src, nact, s_sorted, w_gu, b_gu, w_dn, b_dn)
    return _combine(y_sorted, pw, x1, norm_g, mod4, layer)


def kernel(x_prompt, x_sample, c, c_ctx, state_delta, cache_diff_k, cache_diff_v, cache_mla_ckv, cache_mla_kpe, w_mod, b_mod, norm_g, w_router, b_router, w_gate_up, b_gate_up, w_down, b_down, w_in_even, conv_w, a_log, dt_bias, gdn_norm_g, diff_lambda, diff_norm_g, w_out_even, w_in_odd, q_norm_g, w_uq, kv_norm_g, w_ukv, w_out_odd):
    x = jnp.concatenate([x_prompt.reshape(N_CTX, D), x_sample.reshape(N_LAT, D)], axis=0)
    cond = jnp.concatenate([c_ctx[None, :], c, jnp.zeros((16 - 1 - B_LAT, D), F32)], axis=0)
    mod4 = _modulation(cond, w_mod, b_mod).reshape(DEPTH, 16, 1, N_MOD * D)
    cos64, sin64 = _rope_tables()
    rope2 = (jnp.tile(cos64, (1, 2)), jnp.tile(sin64, (1, 2)))
    rope1 = (jnp.pad(cos64, ((0, 0), (0, 64))), jnp.pad(sin64, ((0, 0), (0, 64))))

    we = w_in_even[0]
    w_cat = jnp.concatenate([we[:, 0:2048], we[:, 2064:3600], we[:, 2048:2064],
                             jnp.zeros((D, LANE - 16), F32)], axis=1).astype(BF16)
    qkv, z, qd, kd, vd, ab = _in_even(x, norm_g[0, 0:1], mod4, 0, w_cat)
    gn = gdn_norm_g[0].reshape(1, GDN_DV)
    oa_c, new_state = _gdn(qkv, z, ab, conv_w[0], a_log[0], dt_bias[0], gn, None, B_CTX, T_CTX, 0)
    oa_l = _gdn(qkv, z, ab, conv_w[0], a_log[0], dt_bias[0], gn, state_delta[:, 0], B_LAT, T_LAT, N_CTX)
    lam_init = 0.8 - 0.6 * math.exp(-0.3 * 0)
    dgn = diff_norm_g[0].reshape(1, 2 * DIFF_DH)
    od_c = _diff_attn(qd, kd, vd, diff_lambda[0], dgn, lam_init, B_CTX, T_CTX, 0)
    cache_dk = cache_diff_k[:, 0].reshape(B_LAT, PAST, DIFF_H * 2 * DIFF_DH)
    cache_dv = cache_diff_v[:, 0].reshape(B_LAT, PAST, DIFF_H * 2 * DIFF_DH)
    od_l = _diff_attn(qd, kd, vd, diff_lambda[0], dgn, lam_init, B_LAT, T_LAT, N_CTX,
                      cache=(cache_dk, cache_dv), rope=rope2)
    o_a = jnp.concatenate([oa_c, oa_l], axis=0)
    o_d = jnp.concatenate([od_c, od_l], axis=0)
    wo = w_out_even[0].astype(BF16)
    x = _moe([o_a, o_d], [wo[:512], wo[512:]], x, norm_g[0], mod4, 0, w_router[0], b_router[0],
             w_gate_up[0], b_gate_up[0], w_down[0], b_down[0])
    new_dk = kd[:N_CTX].reshape(B_CTX, 1, T_CTX, DIFF_H, 2, DIFF_DH)
    new_dv = vd[:N_CTX].reshape(B_CTX, 1, T_CTX, DIFF_H, 2 * DIFF_DH)
    new_state = new_state.reshape(B_CTX, 1, 2, GDN_H, GDN_DK, GDN_DV)

    w_odd = jnp.pad(w_in_odd[0], ((0, 0), (0, ODD_W - (Q_LORA + KV_LORA + QK_ROPE)))).astype(BF16)
    wuq = w_uq[0].reshape(Q_LORA, MLA_H, QK_NOPE + QK_ROPE)
    wuq = jnp.pad(wuq, ((0, 0), (0, 0), (0, MLA_QW - (QK_NOPE + QK_ROPE)))).reshape(Q_LORA, MLA_H * MLA_QW)
    wukv = w_ukv[0].astype(BF16)
    q, kv, ckv, kpe = _in_odd(x, norm_g[1, 0:1], mod4, 1, w_odd, q_norm_g[0].reshape(1, Q_LORA),
                              wuq.astype(BF16), kv_norm_g[0].reshape(1, KV_LORA), wukv)
    om_c = _mla_attn(q, kv, kpe, B_CTX, T_CTX, 0)
    cache_kv = _matmul(cache_mla_ckv[:, 0].reshape(B_LAT * PAST, KV_LORA), wukv)
    cache_kpe = jnp.pad(cache_mla_kpe[:, 0].reshape(B_LAT * PAST, QK_ROPE), ((0, 0), (0, LANE - QK_ROPE)))
    om_l = _mla_attn(q, kv, kpe, B_LAT, T_LAT, N_CTX, cache=(cache_kv, cache_kpe), rope=rope1)
    o_m = jnp.concatenate([om_c, om_l], axis=0)
    x = _moe([o_m], [w_out_odd[0].astype(BF16)], x, norm_g[1], mod4, 1, w_router[1], b_router[1],
             w_gate_up[1], b_gate_up[1], w_down[1], b_down[1])
    new_ckv = ckv[:N_CTX].reshape(B_CTX, 1, T_CTX, KV_LORA)
    new_kpe = kpe[:N_CTX, :QK_ROPE].reshape(B_CTX, 1, T_CTX, QK_ROPE)

    return (x[:N_CTX].reshape(B_CTX, T_CTX, D), x[N_CTX:].reshape(B_LAT, T_LAT, D),
            new_state, new_dk, new_dv, new_ckv, new_kpe)
```

```python
import functools
import math

import jax
import jax.numpy as jnp
from jax import lax
from jax.experimental import pallas as pl
from jax.experimental.pallas import tpu as pltpu

F32 = jnp.float32
BF16 = jnp.bfloat16
HIGHEST = lax.Precision.HIGHEST

D = 1024
B_CTX, T_CTX = 16, 256
B_LAT, T_LAT = 8, 1024
PAST = 256
N_CTX = B_CTX * T_CTX
N_LAT = B_LAT * T_LAT
N_TOK = N_CTX + N_LAT
DEPTH = 2
N_MOD = 6
GRID_W = 64
GDN_H, GDN_DK, GDN_DV = 4, 128, 128
CONV_W = 5
CHUNK = 64
DIFF_H, DIFF_DH = 4, 64
MLA_H, Q_LORA, KV_LORA, QK_NOPE, QK_ROPE, V_HEAD = 8, 256, 256, 128, 64, 128
N_EXP, TOP_K, D_FF = 32, 4, 1024
SWIGLU_LIMIT, SWIGLU_ALPHA = 7.0, 1.702
ROPE_BASE = 10000.0
EPS = 1e-6

LANE = 128
TM = 256
N_TILES = N_TOK // TM
SUB_ROWS = 16
TILE_ROWS = TOP_K * TM + N_EXP * SUB_ROWS
MB = 256
SUB = MB // SUB_ROWS
FF_CHUNK = 256
N_MACRO = -(-((TOP_K * N_TOK) // SUB_ROWS + N_TILES * N_EXP + N_EXP * (SUB - 1)) // SUB)
D_EXT = D + LANE
VMEM_LIMIT = 48 * 1024 * 1024


def _cparams(sem, vmem=VMEM_LIMIT):
    return pltpu.CompilerParams(dimension_semantics=sem, vmem_limit_bytes=vmem)


def _mod_row(t):
    return jnp.where(t < N_CTX // TM, 0, 1 + (t - N_CTX // TM) // (T_LAT // TM))


def _rms(x, g):
    return x * lax.rsqrt(jnp.mean(x * x, axis=-1, keepdims=True) + EPS) * g


def _silu(x):
    return x * jax.nn.sigmoid(x)


def _dot(a, b):
    return jnp.dot(a, b, preferred_element_type=F32)


def _dot_nt(a, b):
    return lax.dot_general(a, b, (((1,), (1,)), ((), ())), preferred_element_type=F32)


def _mod_kernel(c_ref, w_ref, b_ref, o_ref):
    s = _silu(c_ref[...])
    o_ref[...] = _dot(s.astype(BF16), w_ref[...].astype(BF16)) + b_ref[...]


def _modulation(cond, w_mod, b_mod):
    nc = 1536
    return pl.pallas_call(
        _mod_kernel,
        grid=(DEPTH, N_MOD * D // nc),
        in_specs=[pl.BlockSpec((16, D), lambda i, j: (0, 0)),
                  pl.BlockSpec((None, D, nc), lambda i, j: (i, 0, j)),
                  pl.BlockSpec((None, 1, nc), lambda i, j: (i, 0, j))],
        out_specs=pl.BlockSpec((None, 16, nc), lambda i, j: (i, 0, j)),
        out_shape=jax.ShapeDtypeStruct((DEPTH, 16, N_MOD * D), F32),
        compiler_params=_cparams(("arbitrary", "arbitrary")),
        name="modulation",
    )(cond, w_mod, b_mod.reshape(DEPTH, 1, N_MOD * D))


def _mod_spec(layer, k):
    return pl.BlockSpec((None, None, 1, D), lambda t: (layer, _mod_row(t), 0, k))


EVEN_W = 2 * 512 + 512 + 512 + 3 * 512 + LANE


def _in_even_kernel(x_ref, g_ref, sh_ref, sc_ref, w_ref, qkv_ref, z_ref, qd_ref, kd_ref, vd_ref, ab_ref):
    h = _rms(x_ref[...], g_ref[...]) * (1.0 + sc_ref[...]) + sh_ref[...]
    y = _dot(h.astype(BF16), w_ref[...])
    qkv_ref[...] = y[:, 0:1536]
    z_ref[...] = y[:, 1536:2048]
    qd_ref[...] = y[:, 2048:2560]
    kd_ref[...] = y[:, 2560:3072]
    vd_ref[...] = y[:, 3072:3584]
    ab_ref[...] = y[:, 3584:3712]


def _in_even(x, g, mod4, layer, w):
    row = lambda c: pl.BlockSpec((TM, c), lambda t: (t, 0))
    shp = lambda c: jax.ShapeDtypeStruct((N_TOK, c), F32)
    return pl.pallas_call(
        _in_even_kernel,
        grid=(N_TILES,),
        in_specs=[row(D), pl.BlockSpec((1, D), lambda t: (0, 0)), _mod_spec(layer, 0), _mod_spec(layer, 1),
                  pl.BlockSpec((D, EVEN_W), lambda t: (0, 0))],
        out_specs=[row(1536), row(512), row(512), row(512), row(512), row(LANE)],
        out_shape=[shp(1536), shp(512), shp(512), shp(512), shp(512), shp(LANE)],
        compiler_params=_cparams(("arbitrary",)),
        name="in_proj_even",
    )(x, g, mod4, mod4, w)


def _split_bf16(a):
    hi = a.astype(BF16)
    lo = (a - hi.astype(F32)).astype(BF16)
    return hi, lo


def _mm3(a, b):
    return _dot(a[0], b[0]) + (_dot(a[0], b[1]) + _dot(a[1], b[0]))


def _unit_tri_inverses(ls, eye, blk, bd):
    rhs = lambda x: _split_bf16(bd(x))
    d8 = [jnp.where(blk(8), l, 0.0) for l in ls]
    x2 = [_mm3(_split_bf16(d), rhs(d)) for d in d8]
    x2r = [rhs(x) for x in x2]
    ts = [eye - d for d in d8]
    ts = [t + _mm3(_split_bf16(t), xr) for t, xr in zip(ts, x2r)]
    x4 = [_mm3(_split_bf16(x), xr) for x, xr in zip(x2, x2r)]
    ts = [t + _mm3(_split_bf16(t), rhs(x)) for t, x in zip(ts, x4)]
    for s in (8, 16, 32):
        offs = [jnp.where(blk(2 * s) & jnp.logical_not(blk(s)), l, 0.0) for l in ls]
        ys = [_mm3(_split_bf16(o), rhs(t)) for o, t in zip(offs, ts)]
        ts = [t - _mm3(_split_bf16(t), rhs(y)) for t, y in zip(ts, ys)]
    return ts


def _softplus(x):
    return jnp.maximum(x, 0.0) + jnp.log1p(jnp.exp(-jnp.abs(x)))


def _gdn_kernel(has_init, T, alog_ref, dtb_ref, qkv_ref, cw_ref, z_ref, ab_ref, gn_ref, *rest):
    if has_init:
        s0_ref, o_ref = rest[:2]
        st_ref = None
    else:
        o_ref, st_ref = rest[:2]
        s0_ref = None
    xp, qs, ks, vs, abt_s, uo_s, w_s, qg_s, kgt_s, in_s, gl_s, st_s = rest[2:]
    n = T // CHUNK
    H = GDN_H

    xp[0:8, :] = jnp.zeros((8, LANE), F32)
    xp[T + 8:T + 16, :] = jnp.zeros((8, LANE), F32)

    def l2n(x):
        return x * lax.rsqrt(jnp.sum(x * x, axis=-1, keepdims=True) + EPS)

    for blk in range(3 * H):
        cols = slice(blk * LANE, (blk + 1) * LANE)
        dst = slice((blk % H) * LANE, (blk % H + 1) * LANE)
        xp[8:T + 8, :] = qkv_ref[:, cols]
        w = cw_ref[:, cols]
        acc = xp[6:T + 6, :] * w[0:1, :]
        for i in range(1, CONV_W):
            acc = acc + xp[6 + i:T + 6 + i, :] * w[i:i + 1, :]
        y = _silu(acc)
        if blk < H:
            qs[:, dst] = l2n(y) * (GDN_DK ** -0.5)
        elif blk < 2 * H:
            ks[:, dst] = l2n(y)
        else:
            vs[:, dst] = y
    abt_s[...] = ab_ref[...].T[0:16, :]

    ii = lax.broadcasted_iota(jnp.int32, (CHUNK, LANE), 0)
    lane = lax.broadcasted_iota(jnp.int32, (CHUNK, LANE), 1)
    jm = lane & (CHUNK - 1)
    fwd = lane < CHUNK
    order = jnp.where(fwd, ii - jm, jm - ii)
    incl = order >= 0
    strict = order > 0
    incl_t = order <= 0
    eye = jnp.where(order == 0, 1.0, 0.0).astype(F32)
    blk_mask = lambda s: (ii // s) == (jm // s)
    zeros_c = jnp.zeros((CHUNK, LANE), F32)
    zeros_c16 = jnp.zeros((CHUNK, LANE), BF16)

    def bd(x):
        return jnp.concatenate([jnp.where(fwd, x, 0.0), jnp.where(fwd, 0.0, x)], axis=0)

    lane_row = lax.broadcasted_iota(jnp.int32, (1, LANE), 1)
    sub_col = lax.broadcasted_iota(jnp.int32, (16, 1), 0)
    a_lane, dt_lane = jnp.zeros((1, LANE), F32), jnp.zeros((1, LANE), F32)
    a_col, dt_col = jnp.zeros((16, 1), F32), jnp.zeros((16, 1), F32)
    for d in range(2):
        for h in range(H):
            a_lane = jnp.where(lane_row == d * H + h, alog_ref[d, h], a_lane)
            dt_lane = jnp.where(lane_row == d * H + h, dtb_ref[d, h], dt_lane)
            a_col = jnp.where(sub_col == d * H + h, alog_ref[d, h], a_col)
            dt_col = jnp.where(sub_col == d * H + h, dtb_ref[d, h], dt_col)
    nega_lane, nega_col = -jnp.exp(a_lane), -jnp.exp(a_col)

    def prep(p, carry):
        r128 = pl.multiple_of(p * LANE, LANE)
        g_rows = nega_col * _softplus(abt_s[:, pl.ds(r128, LANE)] + dt_col)
        sysm = []
        for cc in range(2):
            r0 = pl.multiple_of(r128 + cc * CHUNK, CHUNK)
            abc = ab_ref[pl.ds(r0, CHUNK), :]
            g_cols = nega_lane * _softplus(abc + dt_lane)
            b_cols = jax.nn.sigmoid(abc)
            for h in range(H):
                hs = slice(h * LANE, (h + 1) * LANE)
                tok = slice(cc * CHUNK, (cc + 1) * CHUNK)
                sysm.append(dict(
                    c=2 * p + cc, rows=pl.ds(r0, CHUNK), h=h,
                    qc=qs[pl.ds(r0, CHUNK), hs], kc=ks[pl.ds(r0, CHUNK), hs], vc=vs[pl.ds(r0, CHUNK), hs],
                    g_col=jnp.where(fwd, g_cols[:, h:h + 1], g_cols[:, H + h:H + h + 1]),
                    g_row=jnp.concatenate([g_rows[h:h + 1, tok], g_rows[H + h:H + h + 1, tok]], axis=1),
                    beta=(b_cols[:, 2 * H + h:2 * H + h + 1], b_cols[:, 3 * H + h:3 * H + h + 1])))
        for s in sysm:
            k16 = s["kc"].astype(BF16)
            kk16 = jnp.concatenate([k16, k16], axis=0)
            s["kk"] = _dot_nt(k16, kk16)
            s["qk"] = _dot_nt(s["qc"].astype(BF16), kk16)
        ls = []
        for s in sysm:
            m = jnp.where(incl, s["g_row"], 0.0)
            s["gcf"] = jnp.sum(jnp.where(fwd, m, 0.0), axis=1, keepdims=True)
            s["gcb"] = jnp.sum(jnp.where(fwd, 0.0, m), axis=1, keepdims=True)
            gc_col = jnp.where(fwd, s["gcf"], s["gcb"])
            gc_row = jnp.sum(jnp.where(incl_t, s["g_col"], 0.0), axis=0, keepdims=True)
            decay = jnp.where(incl, jnp.exp(jnp.where(incl, gc_col - gc_row, 0.0)), 0.0)
            beta = jnp.where(fwd, s["beta"][0], s["beta"][1])
            ls.append(jnp.where(strict, s["kk"] * beta * decay, 0.0))
            s["intra"] = s["qk"] * decay
        ts = _unit_tri_inverses(ls, eye, blk_mask, bd)
        for s, t in zip(sysm, ts):
            h, c, rows = s["h"], s["c"], s["rows"]
            egs = (jnp.exp(s["gcf"]), jnp.exp(s["gcb"]))
            r_f = jnp.concatenate([s["vc"] * s["beta"][0], s["kc"] * (s["beta"][0] * egs[0]), zeros_c, zeros_c], axis=1)
            r_b = jnp.concatenate([zeros_c, zeros_c, s["vc"] * s["beta"][1], s["kc"] * (s["beta"][1] * egs[1])], axis=1)
            uw = _dot(t.astype(BF16), jnp.concatenate([r_f, r_b], axis=0).astype(BF16))
            g_last = (s["gcf"][CHUNK - 1:CHUNK, :], s["gcb"][0:1, :])
            kgs = []
            for d in range(2):
                k = d * H + h
                uo_s[k, rows, :] = uw[:, 2 * d * LANE:(2 * d + 1) * LANE]
                w_s[k, rows, :] = uw[:, (2 * d + 1) * LANE:(2 * d + 2) * LANE].astype(BF16)
                qg_s[k, rows, :] = (s["qc"] * egs[d]).astype(BF16)
                gc = s["gcf"] if d == 0 else s["gcb"]
                kgs.append(s["kc"] * jnp.exp(g_last[d] - gc))
                gl_s[k, c] = jnp.broadcast_to(jnp.exp(g_last[d]), (8, LANE))
            in_s[h, c] = s["intra"].astype(BF16)
            kgt_s[h, c] = jnp.concatenate(kgs, axis=0).T.astype(BF16)
        return carry

    lax.fori_loop(0, T // LANE, prep, 0)

    for d in range(2):
        for h in range(H):
            st_s[d * H + h] = s0_ref[d, h] if has_init else jnp.zeros((GDN_DK, GDN_DV), F32)

    def scan(i, carry):
        cs = (i, n - 1 - i)
        ks_ = range(2 * H)
        c_of = [cs[k // H] for k in ks_]
        rows = [pl.ds(pl.multiple_of(c * CHUNK, CHUNK), CHUNK) for c in c_of]
        s16 = [st_s[k].astype(BF16) for k in ks_]
        ws = [_dot(w_s[k, rows[k], :], s16[k]) for k in ks_]
        qss = [_dot(qg_s[k, rows[k], :], s16[k]) for k in ks_]
        vn = [(uo_s[k, rows[k], :] - ws[k]).astype(BF16) for k in ks_]
        vpad = [jnp.concatenate([vn[k], zeros_c16] if k < H else [zeros_c16, vn[k]], axis=0) for k in ks_]
        o = [qss[k] + _dot(in_s[k % H, c_of[k]], vpad[k]) for k in ks_]
        snew = [st_s[k] * gl_s[k, c_of[k]][0:1, :] + _dot(kgt_s[k % H, c_of[k]], vpad[k]) for k in ks_]
        for k in ks_:
            uo_s[k, rows[k], :] = o[k]
            st_s[k] = snew[k]
        return carry

    lax.fori_loop(0, n, scan, 0)
    for h in range(H):
        hs = slice(h * LANE, (h + 1) * LANE)
        o_ref[:, hs] = _rms(uo_s[h] + uo_s[H + h], gn_ref[...]) * _silu(z_ref[:, hs])
        if st_ref is not None:
            st_ref[0, h] = st_s[h]
            st_ref[1, h] = st_s[H + h]


def _gdn(qkv, z, ab, conv_w, a_log, dt_bias, gn, s0, nb, T, row0):
    n = T // CHUNK
    blk0 = row0 // T
    has_init = s0 is not None
    H, W = GDN_H, GDN_H * LANE
    smem = pl.BlockSpec(memory_space=pltpu.SMEM)
    in_specs = [smem, smem,
                pl.BlockSpec((T, 3 * W), lambda b: (blk0 + b, 0), pipeline_mode=pl.Buffered(1)),
                pl.BlockSpec((CONV_W, 3 * W), lambda b: (0, 0)),
                pl.BlockSpec((T, W), lambda b: (blk0 + b, 0)),
                pl.BlockSpec((T, LANE), lambda b: (blk0 + b, 0)),
                pl.BlockSpec((1, LANE), lambda b: (0, 0))]
    args = [a_log, dt_bias, qkv, conv_w, z, ab, gn]
    st_spec = pl.BlockSpec((None, 2, H, GDN_DK, GDN_DV), lambda b: (b, 0, 0, 0, 0))
    o_spec = pl.BlockSpec((T, W), lambda b: (b, 0))
    o_shape = jax.ShapeDtypeStruct((nb * T, W), F32)
    if has_init:
        in_specs.append(st_spec)
        args.append(s0)
        out_specs, out_shape = o_spec, o_shape
    else:
        out_specs = [o_spec, st_spec]
        out_shape = [o_shape, jax.ShapeDtypeStruct((nb, 2, H, GDN_DK, GDN_DV), F32)]
    scratch = [pltpu.VMEM((T + 16, LANE), F32), pltpu.VMEM((T, W), F32), pltpu.VMEM((T, W), F32),
               pltpu.VMEM((T, W), F32), pltpu.VMEM((16, T), F32),
               pltpu.VMEM((2 * H, T, LANE), F32), pltpu.VMEM((2 * H, T, LANE), BF16),
               pltpu.VMEM((2 * H, T, LANE), BF16), pltpu.VMEM((H, n, LANE, LANE), BF16),
               pltpu.VMEM((H, n, CHUNK, LANE), BF16), pltpu.VMEM((2 * H, n, 8, LANE), F32),
               pltpu.VMEM((2 * H, GDN_DK, GDN_DV), F32)]
    return pl.pallas_call(
        functools.partial(_gdn_kernel, has_init, T),
        grid=(nb,),
        in_specs=in_specs, out_specs=out_specs, out_shape=out_shape, scratch_shapes=scratch,
        compiler_params=_cparams(("arbitrary",), 56 * 1024 * 1024),
        name="gdn_lat" if has_init else "gdn_ctx",
    )(*args)


def _rope(x, cos, sin_signed):
    lane = lax.broadcasted_iota(jnp.int32, x.shape, 1)
    up = pltpu.roll(x, LANE - 16, axis=1)
    dn = pltpu.roll(x, 16, axis=1)
    rot = jnp.where((lane % 32) < 16, up, dn)
    return x * cos + rot * sin_signed


def _rope_tables():
    rows = T_LAT // GRID_W
    r = jnp.repeat(jnp.arange(rows, dtype=F32), GRID_W)
    col = jnp.tile(jnp.arange(GRID_W, dtype=F32), rows)
    quarter = 16
    inv = ROPE_BASE ** (-jnp.arange(quarter, dtype=F32) / quarter)
    ar = r[:, None] * inv
    ac = col[:, None] * inv
    ang = jnp.concatenate([ar, ar, ac, ac], axis=-1)
    sign = jnp.tile(jnp.concatenate([-jnp.ones(16, F32), jnp.ones(16, F32)]), 2)
    cos, sin = jnp.cos(ang), jnp.sin(ang) * sign
    return cos, sin


def _softmax_parts(parts):
    m = functools.reduce(jnp.maximum, [jnp.max(s, axis=-1, keepdims=True) for s in parts])
    es = [jnp.exp(s - m) for s in parts]
    den = functools.reduce(lambda a, b: a + b, [jnp.sum(e, axis=-1, keepdims=True) for e in es])
    return [e / den for e in es]


def _diff_kernel(latent, T, lam_init, q_ref, k_ref, v_ref, lam_ref, gn_ref, *rest):
    if latent:
        cos_ref, sin_ref, ck_ref, cv_ref, o_ref = rest
    else:
        (o_ref,) = rest
    scale = DIFF_DH ** -0.5
    lp = lam_ref[...]
    lam = (jnp.exp(jnp.sum(lp[0:1] * lp[1:2], axis=-1, keepdims=True))
           - jnp.exp(jnp.sum(lp[2:3] * lp[3:4], axis=-1, keepdims=True)) + lam_init)
    k = k_ref[...]
    if latent:
        k = _rope(k, cos_ref[...], sin_ref[...])
    k16 = k.astype(BF16)
    v16 = v_ref[...].astype(BF16)
    if latent:
        ck16 = ck_ref[...].astype(BF16)
        cv16 = cv_ref[...].astype(BF16)
    tq = 256
    lane = lax.broadcasted_iota(jnp.int32, (tq, LANE), 1)
    for qb in range(T // tq):
        q = q_ref[qb * tq:(qb + 1) * tq, :]
        if latent:
            q = _rope(q, cos_ref[qb * tq:(qb + 1) * tq, :], sin_ref[qb * tq:(qb + 1) * tq, :])
        ps = []
        for m in range(2):
            qm = jnp.where((lane // DIFF_DH) == m, q, 0.0).astype(BF16)
            parts = [_dot_nt(qm, k16) * scale]
            if latent:
                parts.insert(0, _dot_nt(qm, ck16) * scale)
            ps.append(_softmax_parts(parts))
        o = _dot((ps[0][-1] - lam * ps[1][-1]).astype(BF16), v16)
        if latent:
            o = o + _dot((ps[0][0] - lam * ps[1][0]).astype(BF16), cv16)
        o_ref[qb * tq:(qb + 1) * tq, :] = _rms(o, gn_ref[...]) * (1.0 - lam_init)


def _diff_attn(qd, kd, vd, lam_p, gn, lam_init, nb, T, row0, cache=None, rope=None):
    blk0 = row0 // T
    latent = cache is not None
    hcol = pl.BlockSpec((T, LANE), lambda b, h: (blk0 + b, h))
    in_specs = [hcol, hcol, hcol, pl.BlockSpec((4, DIFF_DH), lambda b, h: (0, 0)),
                pl.BlockSpec((1, LANE), lambda b, h: (0, 0))]
    args = [qd, kd, vd, lam_p, gn]
    if latent:
        tab = pl.BlockSpec((T, LANE), lambda b, h: (0, 0))
        cch = pl.BlockSpec((None, PAST, LANE), lambda b, h: (b, 0, h))
        in_specs += [tab, tab, cch, cch]
        args += [rope[0], rope[1], cache[0], cache[1]]
    return pl.pallas_call(
        functools.partial(_diff_kernel, latent, T, lam_init),
        grid=(nb, DIFF_H),
        in_specs=in_specs,
        out_specs=pl.BlockSpec((T, LANE), lambda b, h: (b, h)),
        out_shape=jax.ShapeDtypeStruct((nb * T, DIFF_H * 2 * DIFF_DH), F32),
        compiler_params=_cparams(("arbitrary", "arbitrary")),
        name="diff_lat" if latent else "diff_ctx",
    )(*args)


ODD_W = Q_LORA + KV_LORA + LANE
MLA_QW = 2 * LANE


def _in_odd_kernel(x_ref, g_ref, sh_ref, sc_ref, w_ref, qg_ref, wuq_ref, kvg_ref, wukv_ref,
                   q_ref, kv_ref, ckv_ref, kpe_ref):
    h = _rms(x_ref[...], g_ref[...]) * (1.0 + sc_ref[...]) + sh_ref[...]
    y = _dot(h.astype(BF16), w_ref[...])
    qn = _rms(y[:, 0:Q_LORA], qg_ref[...])
    q_ref[...] = _dot(qn.astype(BF16), wuq_ref[...])
    ckv = _rms(y[:, Q_LORA:Q_LORA + KV_LORA], kvg_ref[...])
    ckv_ref[...] = ckv
    kv_ref[...] = _dot(ckv.astype(BF16), wukv_ref[...])
    kpe_ref[...] = y[:, Q_LORA + KV_LORA:ODD_W]


def _in_odd(x, g, mod4, layer, w, qg, wuq, kvg, wukv):
    row = lambda c: pl.BlockSpec((TM, c), lambda t: (t, 0))
    full = lambda a: pl.BlockSpec(a.shape, lambda t: (0, 0))
    shp = lambda c: jax.ShapeDtypeStruct((N_TOK, c), F32)
    return pl.pallas_call(
        _in_odd_kernel,
        grid=(N_TILES,),
        in_specs=[row(D), full(g), _mod_spec(layer, 0), _mod_spec(layer, 1), full(w), full(qg), full(wuq),
                  full(kvg), full(wukv)],
        out_specs=[row(MLA_H * MLA_QW), row(MLA_H * MLA_QW), row(KV_LORA), row(LANE)],
        out_shape=[shp(MLA_H * MLA_QW), shp(MLA_H * MLA_QW), shp(KV_LORA), shp(LANE)],
        compiler_params=_cparams(("arbitrary",)),
        name="in_proj_odd",
    )(x, g, mod4, mod4, w, qg, wuq, kvg, wukv)


def _mm_kernel(x_ref, w_ref, o_ref):
    o_ref[...] = _dot(x_ref[...].astype(BF16), w_ref[...])


def _matmul(x, w):
    m, k = x.shape
    n = w.shape[1]
    return pl.pallas_call(
        _mm_kernel,
        grid=(m // TM,),
        in_specs=[pl.BlockSpec((TM, k), lambda i: (i, 0)), pl.BlockSpec((k, n), lambda i: (0, 0))],
        out_specs=pl.BlockSpec((TM, n), lambda i: (i, 0)),
        out_shape=jax.ShapeDtypeStruct((m, n), F32),
        compiler_params=_cparams(("arbitrary",)),
        name="cache_kv_proj",
    )(x, w)


def _mla_kernel(latent, T, q_ref, kv_ref, kpe_ref, *rest):
    if latent:
        cos_ref, sin_ref, ckv_ref, ckpe_ref, o_ref = rest
    else:
        (o_ref,) = rest
    scale = (QK_NOPE + QK_ROPE) ** -0.5
    kv = kv_ref[...]
    v16 = kv[:, LANE:].astype(BF16)
    kpe = kpe_ref[...]
    if latent:
        kpe = _rope(kpe, cos_ref[...], sin_ref[...])
        ckv = ckv_ref[...]
        ck16 = jnp.concatenate([ckv[:, :LANE], ckpe_ref[...]], axis=1).astype(BF16)
        cv16 = ckv[:, LANE:].astype(BF16)
    k16 = jnp.concatenate([kv[:, :LANE], kpe], axis=1).astype(BF16)
    tq = 256
    for qb in range(T // tq):
        rows = slice(qb * tq, (qb + 1) * tq)
        q = q_ref[rows, :]
        qp = q[:, LANE:]
        if latent:
            qp = _rope(qp, cos_ref[rows, :], sin_ref[rows, :])
        q16 = (jnp.concatenate([q[:, :LANE], qp], axis=1) * scale).astype(BF16)
        parts = [_dot_nt(q16, k16)]
        if latent:
            parts.insert(0, _dot_nt(q16, ck16))
        m = functools.reduce(jnp.maximum, [jnp.max(s, axis=-1, keepdims=True) for s in parts])
        es = [jnp.exp(s - m) for s in parts]
        den = functools.reduce(lambda a, b: a + b, [jnp.sum(e, axis=-1, keepdims=True) for e in es])
        o = _dot(es[-1].astype(BF16), v16)
        if latent:
            o = o + _dot(es[0].astype(BF16), cv16)
        o_ref[rows, :] = o * (1.0 / den)


def _mla_attn(q, kv, kpe, nb, T, row0, cache=None, rope=None):
    blk0 = row0 // T
    latent = cache is not None
    head = pl.BlockSpec((T, MLA_QW), lambda b, h: (blk0 + b, h))
    in_specs = [head, head, pl.BlockSpec((T, LANE), lambda b, h: (blk0 + b, 0))]
    args = [q, kv, kpe]
    if latent:
        tab = pl.BlockSpec((T, LANE), lambda b, h: (0, 0))
        in_specs += [tab, tab, pl.BlockSpec((PAST, MLA_QW), lambda b, h: (b, h)),
                     pl.BlockSpec((PAST, LANE), lambda b, h: (b, 0))]
        args += [rope[0], rope[1], cache[0], cache[1]]
    return pl.pallas_call(
        functools.partial(_mla_kernel, latent, T),
        grid=(nb, MLA_H),
        in_specs=in_specs,
        out_specs=pl.BlockSpec((T, V_HEAD), lambda b, h: (b, h)),
        out_shape=jax.ShapeDtypeStruct((nb * T, MLA_H * V_HEAD), F32),
        compiler_params=_cparams(("arbitrary", "arbitrary")),
        name="mla_lat" if latent else "mla_ctx",
    )(*args)


def _route_kernel(n_mix, *refs):
    ctx_refs = refs[:n_mix]
    lat_refs = refs[n_mix:2 * n_mix]
    w_refs = refs[2 * n_mix:3 * n_mix]
    (x_ref, g1_ref, gate1_ref, g2_ref, sh2_ref, sc2_ref, wrh_ref, wrl_ref, br_ref,
     x1_ref, hx_ref, pw_ref, pwt_ref, cnt_ref, y_s) = refs[3 * n_mix:]
    t = pl.program_id(0)

    def out_proj(mix_refs):
        y = _dot(mix_refs[0][...].astype(BF16), w_refs[0][...])
        for m in range(1, n_mix):
            y = y + _dot(mix_refs[m][...].astype(BF16), w_refs[m][...])
        y_s[...] = y

    pl.when(t < N_CTX // TM)(lambda: out_proj(ctx_refs))
    pl.when(t >= N_CTX // TM)(lambda: out_proj(lat_refs))
    x1 = x_ref[...] + gate1_ref[...] * _rms(y_s[...], g1_ref[...])
    x1_ref[...] = x1
    h2 = _rms(x1, g2_ref[...]) * (1.0 + sc2_ref[...]) + sh2_ref[...]

    logits = _mm3(_split_bf16(h2), (wrh_ref[...], wrl_ref[...])) + br_ref[...]
    lane_i = lax.broadcasted_iota(jnp.int32, (TM, LANE), 1)
    lane = lane_i.astype(F32)
    neg = jnp.float32(-jnp.inf)
    l = jnp.where(lane_i < N_EXP, logits, neg)
    sel = jnp.zeros((TM, LANE), F32)
    comb = jnp.zeros((TM, LANE), F32)
    wsum = jnp.zeros((TM, 1), F32)
    hots, wts = [], []
    for k in range(TOP_K):
        m = jnp.max(l, axis=-1, keepdims=True)
        idx = jnp.min(jnp.where(l == m, lane, float(LANE)), axis=-1, keepdims=True)
        hot = lane == idx
        if k == 0:
            m0 = m
        w = jnp.exp(m - m0)
        wsum = wsum + w
        hots.append(hot)
        wts.append(w)
        sel = sel + jnp.where(hot, 1.0, 0.0)
        comb = comb + jnp.where(hot, w, 0.0)
        l = jnp.where(hot, neg, l)
    inv = 1.0 / wsum
    comb = comb * inv

    cnt = jnp.sum(sel, axis=0, keepdims=True)
    cnt_ref[...] = cnt
    padded = jnp.floor((cnt + (SUB_ROWS - 1)) * (1.0 / SUB_ROWS)) * SUB_ROWS
    li = lax.broadcasted_iota(jnp.int32, (LANE, LANE), 0)
    lj = lax.broadcasted_iota(jnp.int32, (LANE, LANE), 1)
    before = jnp.where(li < lj, 1.0, 0.0).astype(BF16)
    start = _dot(jnp.broadcast_to(padded, (8, LANE)).astype(BF16), before)[0:1, :]
    ti = lax.broadcasted_iota(jnp.int32, (TM, TM), 0)
    tj = lax.broadcasted_iota(jnp.int32, (TM, TM), 1)
    earlier = jnp.where(tj < ti, 1.0, 0.0).astype(BF16)
    pos_full = _dot(earlier, sel.astype(BF16)) + start
    pw = jnp.zeros((TM, LANE), F32)
    for k in range(TOP_K):
        pos_k = jnp.sum(jnp.where(hots[k], pos_full, 0.0), axis=-1, keepdims=True)
        pw = pw + jnp.where(lane_i == k, pos_k, 0.0) + jnp.where(lane_i == TOP_K + k, wts[k] * inv, 0.0)
    pw_ref[...] = pw
    pwt_ref[...] = pw.T[0:8, :]

    chi, clo = _split_bf16(comb)
    ext = chi.astype(F32) + pltpu.roll(clo.astype(F32), N_EXP, axis=1)
    hx_ref[:, 0:D] = h2.astype(BF16)
    hx_ref[:, D:D_EXT] = ext.astype(BF16)


def _route(mixes, w_outs, x, norm_g, mod4, layer, w_router, b_router):
    n_mix = len(mixes)
    wr_hi, wr_lo = _split_bf16(w_router)
    row = lambda c: pl.BlockSpec((TM, c), lambda t: (t, 0))
    full = lambda a: pl.BlockSpec(a.shape, lambda t: (0, 0))
    g = lambda k: pl.BlockSpec((None, 1, D), lambda t: (k, 0, 0))
    n_ctx = N_CTX // TM
    ctx_row = lambda c: pl.BlockSpec((TM, c), lambda t: (jnp.minimum(t, n_ctx - 1), 0))
    lat_row = lambda c: pl.BlockSpec((TM, c), lambda t: (jnp.maximum(t - n_ctx, 0), 0))
    in_specs = ([ctx_row(m[0].shape[1]) for m in mixes] + [lat_row(m[1].shape[1]) for m in mixes]
                + [full(w) for w in w_outs]
                + [row(D), g(1), _mod_spec(layer, 2), g(2), _mod_spec(layer, 3), _mod_spec(layer, 4),
                   full(wr_hi), full(wr_lo), full(b_router)])
    ng = norm_g.reshape(4, 1, D)
    return pl.pallas_call(
        functools.partial(_route_kernel, n_mix),
        grid=(N_TILES,),
        in_specs=in_specs,
        out_specs=[row(D), row(D_EXT), row(LANE), pl.BlockSpec((None, 8, TM), lambda t: (t, 0, 0)),
                   pl.BlockSpec((None, 1, LANE), lambda t: (t, 0, 0))],
        out_shape=[jax.ShapeDtypeStruct((N_TOK, D), F32), jax.ShapeDtypeStruct((N_TOK, D_EXT), BF16),
                   jax.ShapeDtypeStruct((N_TOK, LANE), F32), jax.ShapeDtypeStruct((N_TILES, 8, TM), F32),
                   jax.ShapeDtypeStruct((N_TILES, 1, LANE), F32)],
        scratch_shapes=[pltpu.VMEM((TM, D), F32)],
        compiler_params=_cparams(("arbitrary",)),
        name="out_proj_route",
    )(*[m[0] for m in mixes], *[m[1] for m in mixes], *w_outs, x, ng, mod4, ng, mod4, mod4, wr_hi, wr_lo,
      b_router)


def _dispatch_kernel(hx_ref, pwt_ref, s_ref):
    j = pl.program_id(0)

    @pl.when(j < N_TILES)
    def _():
        rio = lax.broadcasted_iota(jnp.int32, (TILE_ROWS, TM), 0).astype(F32)
        p = jnp.zeros((TILE_ROWS, TM), F32)
        for k in range(TOP_K):
            p = p + jnp.where(rio == pwt_ref[k:k + 1, :], 1.0, 0.0)
        s_ref[...] = _dot(p.astype(BF16), hx_ref[...]).astype(BF16)

    @pl.when(j == N_TILES)
    def _():
        s_ref[...] = jnp.zeros((TILE_ROWS, D_EXT), BF16)


def _dispatch(hx, pwt):
    clamp = lambda j: jnp.minimum(j, N_TILES - 1)
    return pl.pallas_call(
        _dispatch_kernel,
        grid=(N_TILES + 1,),
        in_specs=[pl.BlockSpec((TM, D_EXT), lambda j: (clamp(j), 0)),
                  pl.BlockSpec((None, 8, TM), lambda j: (clamp(j), 0, 0))],
        out_specs=pl.BlockSpec((TILE_ROWS, D_EXT), lambda j: (j, 0)),
        out_shape=jax.ShapeDtypeStruct(((N_TILES + 1) * TILE_ROWS, D_EXT), BF16),
        compiler_params=_cparams(("arbitrary",)),
        name="moe_dispatch",
    )(hx, pwt)


def _expert_kernel(blk_e_ref, src_ref, nact_ref, tail_ref, s_hbm, wgu_ref, bgu_ref, wd_ref, bd_ref, y_hbm,
                   xbuf, ybuf, wgu_s, wd_s, zbuf, gsem, ssem, zsem):
    i = pl.program_id(0)
    nsteps = pl.num_programs(0)
    nact = nact_ref[0]
    slot = i % 2

    def gather_copy(step, sl, s):
        src = pl.multiple_of(src_ref[step * SUB + s], SUB_ROWS)
        return pltpu.make_async_copy(s_hbm.at[pl.ds(src, SUB_ROWS)],
                                     xbuf.at[sl, pl.ds(s * SUB_ROWS, SUB_ROWS)], gsem.at[sl])

    def scatter_copy(step, sl, s):
        dst = pl.multiple_of(src_ref[step * SUB + s], SUB_ROWS)
        return pltpu.make_async_copy(ybuf.at[sl, pl.ds(s * SUB_ROWS, SUB_ROWS)],
                                     y_hbm.at[pl.ds(dst, SUB_ROWS)], ssem.at[sl])

    def gather_start(step, sl):
        for s in range(SUB):
            gather_copy(step, sl, s).start()

    def gather_wait(step, sl):
        for s in range(SUB):
            gather_copy(step, sl, s).wait()

    def scatter_each(step, sl, wait):
        for s in range(SUB):
            @pl.when(src_ref[step * SUB + s] < N_TILES * TILE_ROWS)
            def _():
                cp = scatter_copy(step, sl, s)
                if wait:
                    cp.wait()
                else:
                    cp.start()

    def scatter_start(step, sl):
        scatter_each(step, sl, wait=False)

    def scatter_wait(step, sl):
        scatter_each(step, sl, wait=True)

    def tail_fill(wait):
        def tile(j, carry):
            first = tail_ref[j]

            def block(b, c):
                dst = pl.multiple_of(first + b * SUB_ROWS, SUB_ROWS)
                cp = pltpu.make_async_copy(zbuf, y_hbm.at[pl.ds(dst, SUB_ROWS)], zsem.at[0])
                if wait:
                    cp.wait()
                else:
                    cp.start()
                return c

            return lax.fori_loop(0, ((j + 1) * TILE_ROWS - first) // SUB_ROWS, block, carry)

        lax.fori_loop(0, N_TILES, tile, 0)

    @pl.when(i == 0)
    def _():
        gather_start(0, 0)
        zbuf[...] = jnp.zeros((SUB_ROWS, D), BF16)
        tail_fill(wait=False)

    @pl.when(i < nact)
    def _():
        gather_wait(i, slot)

        @pl.when(i + 1 < nact)
        def _():
            gather_start(i + 1, 1 - slot)

        e = blk_e_ref[i]
        e_prev = blk_e_ref[jnp.maximum(i - 1, 0)]

        @pl.when((i == 0) | (e != e_prev))
        def _():
            wgu_s[...] = wgu_ref[...].astype(BF16)
            wd_s[...] = wd_ref[...].astype(BF16)

        x = xbuf[slot]
        ext = x[:, D:D_EXT].astype(F32)
        lane = lax.broadcasted_iota(jnp.int32, (MB, LANE), 1)
        wr = jnp.sum(jnp.where((lane == e) | (lane == e + N_EXP), ext, 0.0), axis=-1, keepdims=True)
        xt = x[:, 0:D]
        n_chunks = D_FF // FF_CHUNK

        def gate_up(j):
            gc = slice(j * FF_CHUNK, (j + 1) * FF_CHUNK)
            uc = slice(D_FF + j * FF_CHUNK, D_FF + (j + 1) * FF_CHUNK)
            return _dot(xt, wgu_s[:, gc]) + bgu_ref[:, gc], _dot(xt, wgu_s[:, uc]) + bgu_ref[:, uc]

        y = bd_ref[...]
        nxt = gate_up(0)
        for j in range(n_chunks):
            gate, up = nxt
            if j + 1 < n_chunks:
                nxt = gate_up(j + 1)
            gate = jnp.minimum(gate, SWIGLU_LIMIT)
            up = jnp.clip(up, -SWIGLU_LIMIT, SWIGLU_LIMIT)
            act = (up + 1.0) * gate * jax.nn.sigmoid(SWIGLU_ALPHA * gate)
            y = y + _dot(act.astype(BF16), wd_s[j * FF_CHUNK:(j + 1) * FF_CHUNK, :])
        y = y * wr

        @pl.when(i >= 2)
        def _():
            scatter_wait(i - 2, slot)

        ybuf[slot] = y.astype(BF16)
        scatter_start(i, slot)

    @pl.when(i == nsteps - 1)
    def _():
        @pl.when(nact >= 2)
        def _():
            scatter_wait(nact - 2, nact % 2)

        scatter_wait(nact - 1, (nact - 1) % 2)
        tail_fill(wait=True)


def _experts(blk_e, src, nact, tail, s_sorted, layer, w_gu, b_gu, w_dn, b_dn):
    rows = N_TILES * TILE_ROWS
    per_expert = lambda r, c: pl.BlockSpec((None, None, r, c), lambda i, be, sr, na, tl: (layer, be[i], 0, 0))
    grid_spec = pltpu.PrefetchScalarGridSpec(
        num_scalar_prefetch=4,
        grid=(N_MACRO,),
        in_specs=[pl.BlockSpec(memory_space=pl.ANY), per_expert(D, 2 * D_FF), per_expert(1, 2 * D_FF),
                  per_expert(D_FF, D), per_expert(1, D)],
        out_specs=pl.BlockSpec(memory_space=pl.ANY),
        scratch_shapes=[pltpu.VMEM((2, MB, D_EXT), BF16), pltpu.VMEM((2, MB, D), BF16),
                        pltpu.VMEM((D, 2 * D_FF), BF16), pltpu.VMEM((D_FF, D), BF16),
                        pltpu.VMEM((SUB_ROWS, D), BF16),
                        pltpu.SemaphoreType.DMA((2,)), pltpu.SemaphoreType.DMA((2,)),
                        pltpu.SemaphoreType.DMA((1,))])
    return pl.pallas_call(
        _expert_kernel,
        grid_spec=grid_spec,
        out_shape=jax.ShapeDtypeStruct((rows, D), BF16),
        compiler_params=_cparams(("arbitrary",), 56 * 1024 * 1024),
        name="moe_experts",
    )(blk_e, src, nact, tail, s_sorted, w_gu, b_gu.reshape(DEPTH, N_EXP, 1, 2 * D_FF), w_dn,
      b_dn.reshape(DEPTH, N_EXP, 1, D))


def _combine_kernel(y_ref, pw_ref, x1_ref, g_ref, gate_ref, o_ref):
    lio = lax.broadcasted_iota(jnp.int32, (TM, TILE_ROWS), 1).astype(F32)
    pw = pw_ref[...]
    p = jnp.zeros((TM, TILE_ROWS), F32)
    for k in range(TOP_K):
        p = p + jnp.where(lio == pw[:, k:k + 1], 1.0, 0.0)
    f = _dot(p.astype(BF16), y_ref[...])
    o_ref[...] = x1_ref[...] + gate_ref[...] * _rms(f, g_ref[...])


def _combine(y_sorted, pw, x1, norm_g, mod4, layer):
    row = lambda c: pl.BlockSpec((TM, c), lambda t: (t, 0))
    return pl.pallas_call(
        _combine_kernel,
        grid=(N_TILES,),
        in_specs=[pl.BlockSpec((TILE_ROWS, D), lambda t: (t, 0)), row(LANE), row(D),
                  pl.BlockSpec((None, 1, D), lambda t: (3, 0, 0)), _mod_spec(layer, 5)],
        out_specs=row(D),
        out_shape=jax.ShapeDtypeStruct((N_TOK, D), F32),
        compiler_params=_cparams(("arbitrary",)),
        name="moe_combine",
    )(y_sorted, pw, x1, norm_g.reshape(4, 1, D), mod4)


def _block_tables(cnt):
    cnt = cnt[:, 0, :N_EXP].astype(jnp.int32)
    nb = (cnt + (SUB_ROWS - 1)) // SUB_ROWS
    lstart = (jnp.cumsum(nb, axis=1) - nb) * SUB_ROWS
    cum_nb = jnp.cumsum(nb, axis=0)
    nb_e = cum_nb[-1]
    mb_e = (nb_e + (SUB - 1)) // SUB
    mend = jnp.cumsum(mb_e)
    nact = mend[-1]
    i = jnp.arange(N_MACRO, dtype=jnp.int32)
    e_i = jnp.minimum(jnp.sum(mend[None, :] <= i[:, None], axis=1), N_EXP - 1).astype(jnp.int32)
    e_last = jnp.sum(mend <= nact - 1).astype(jnp.int32)
    blk_e = jnp.where(i < nact, e_i, e_last)
    mstart = mend - mb_e
    s = jnp.arange(SUB, dtype=jnp.int32)
    q = (i - mstart[blk_e])[:, None] * SUB + s[None, :]
    valid = (i < nact)[:, None] & (q < nb_e[blk_e][:, None])
    cum_t = cum_nb.T[blk_e]
    j = jnp.sum(cum_t[:, None, :] <= q[:, :, None], axis=2)
    j = jnp.minimum(j, N_TILES - 1)
    at_j = jnp.arange(N_TILES, dtype=jnp.int32)[None, None, :] == j[:, :, None]
    pick = lambda tab: jnp.sum(jnp.where(at_j, tab[:, None, :], 0), axis=2)
    prev = pick(cum_t - nb.T[blk_e])
    row = j * TILE_ROWS + pick(lstart.T[blk_e]) + (q - prev) * SUB_ROWS
    spare = N_TILES * TILE_ROWS + ((i % 2)[:, None] * SUB + s[None, :]) * SUB_ROWS
    src = jnp.where(valid, row, spare).astype(jnp.int32).reshape(-1)
    tail = jnp.arange(N_TILES, dtype=jnp.int32) * TILE_ROWS + jnp.sum(nb, axis=1) * SUB_ROWS
    return blk_e.astype(jnp.int32), src, nact.reshape(1).astype(jnp.int32), tail.astype(jnp.int32)


def _moe(mixes, w_outs, x, norm_g, mod4, layer, w_router, b_router, w_gu, b_gu, w_dn, b_dn):
    wr = jnp.pad(w_router, ((0, 0), (0, LANE - N_EXP)))
    br = jnp.pad(b_router, (0, LANE - N_EXP)).reshape(1, LANE)
    x1, hx, pw, pwt, cnt = _route(mixes, w_outs, x, norm_g, mod4, layer, wr, br)
    blk_e, src, nact, tail = _block_tables(cnt)
    s_sorted = _dispatch(hx, pwt)
    y_sorted = _experts(blk_e, src, nact, tail, s_sorted, layer, w_gu, b_gu, w_dn, b_dn)
    return _combine(y_sorted, pw, x1, norm_g, mod4, layer)


def kernel(x_prompt, x_sample, c, c_ctx, state_delta, cache_diff_k, cache_diff_v, cache_mla_ckv, cache_mla_kpe, w_mod, b_mod, norm_g, w_router, b_router, w_gate_up, b_gate_up, w_down, b_down, w_in_even, conv_w, a_log, dt_bias, gdn_norm_g, diff_lambda, diff_norm_g, w_out_even, w_in_odd, q_norm_g, w_uq, kv_norm_g, w_ukv, w_out_odd):
    x = jnp.concatenate([x_prompt.reshape(N_CTX, D), x_sample.reshape(N_LAT, D)], axis=0)
    cond = jnp.concatenate([c_ctx[None, :], c, jnp.zeros((16 - 1 - B_LAT, D), F32)], axis=0)
    mod4 = _modulation(cond, w_mod, b_mod).reshape(DEPTH, 16, 1, N_MOD * D)
    cos64, sin64 = _rope_tables()
    rope2 = (jnp.tile(cos64, (1, 2)), jnp.tile(sin64, (1, 2)))
    rope1 = (jnp.pad(cos64, ((0, 0), (0, 64))), jnp.pad(sin64, ((0, 0), (0, 64))))

    we = w_in_even[0]
    w_cat = jnp.concatenate([we[:, 0:2048], we[:, 2064:3600], we[:, 2048:2064],
                             jnp.zeros((D, LANE - 16), F32)], axis=1).astype(BF16)
    qkv, z, qd, kd, vd, ab = _in_even(x, norm_g[0, 0:1], mod4, 0, w_cat)
    gn = gdn_norm_g[0].reshape(1, GDN_DV)
    oa_c, new_state = _gdn(qkv, z, ab, conv_w[0], a_log[0], dt_bias[0], gn, None, B_CTX, T_CTX, 0)
    oa_l = _gdn(qkv, z, ab, conv_w[0], a_log[0], dt_bias[0], gn, state_delta[:, 0], B_LAT, T_LAT, N_CTX)
    lam_init = 0.8 - 0.6 * math.exp(-0.3 * 0)
    dgn = diff_norm_g[0].reshape(1, 2 * DIFF_DH)
    od_c = _diff_attn(qd, kd, vd, diff_lambda[0], dgn, lam_init, B_CTX, T_CTX, 0)
    cache_dk = cache_diff_k[:, 0].reshape(B_LAT, PAST, DIFF_H * 2 * DIFF_DH)
    cache_dv = cache_diff_v[:, 0].reshape(B_LAT, PAST, DIFF_H * 2 * DIFF_DH)
    od_l = _diff_attn(qd, kd, vd, diff_lambda[0], dgn, lam_init, B_LAT, T_LAT, N_CTX,
                      cache=(cache_dk, cache_dv), rope=rope2)
    wo = w_out_even[0].astype(BF16)
    x = _moe([(oa_c, oa_l), (od_c, od_l)], [wo[:512], wo[512:]], x, norm_g[0], mod4, 0, w_router[0], b_router[0],
             w_gate_up, b_gate_up, w_down, b_down)
    new_dk = kd[:N_CTX].reshape(B_CTX, 1, T_CTX, DIFF_H, 2, DIFF_DH)
    new_dv = vd[:N_CTX].reshape(B_CTX, 1, T_CTX, DIFF_H, 2 * DIFF_DH)
    new_state = new_state.reshape(B_CTX, 1, 2, GDN_H, GDN_DK, GDN_DV)

    w_odd = jnp.pad(w_in_odd[0], ((0, 0), (0, ODD_W - (Q_LORA + KV_LORA + QK_ROPE)))).astype(BF16)
    wuq = w_uq[0].reshape(Q_LORA, MLA_H, QK_NOPE + QK_ROPE)
    wuq = jnp.pad(wuq, ((0, 0), (0, 0), (0, MLA_QW - (QK_NOPE + QK_ROPE)))).reshape(Q_LORA, MLA_H * MLA_QW)
    wukv = w_ukv[0].astype(BF16)
    q, kv, ckv, kpe = _in_odd(x, norm_g[1, 0:1], mod4, 1, w_odd, q_norm_g[0].reshape(1, Q_LORA),
                              wuq.astype(BF16), kv_norm_g[0].reshape(1, KV_LORA), wukv)
    om_c = _mla_attn(q, kv, kpe, B_CTX, T_CTX, 0)
    cache_kv = _matmul(cache_mla_ckv[:, 0].reshape(B_LAT * PAST, KV_LORA), wukv)
    cache_kpe = jnp.pad(cache_mla_kpe[:, 0].reshape(B_LAT * PAST, QK_ROPE), ((0, 0), (0, LANE - QK_ROPE)))
    om_l = _mla_attn(q, kv, kpe, B_LAT, T_LAT, N_CTX, cache=(cache_kv, cache_kpe), rope=rope1)
    x = _moe([(om_c, om_l)], [w_out_odd[0].astype(BF16)], x, norm_g[1], mod4, 1, w_router[1], b_router[1],
             w_gate_up, b_gate_up, w_down, b_down)
    new_ckv = ckv[:N_CTX].reshape(B_CTX, 1, T_CTX, KV_LORA)
    new_kpe = kpe[:N_CTX, :QK_ROPE].reshape(B_CTX, 1, T_CTX, QK_ROPE)

    return (x[:N_CTX].reshape(B_CTX, T_CTX, D), x[N_CTX:].reshape(B_LAT, T_LAT, D),
            new_state, new_dk, new_dv, new_ckv, new_kpe)
```

```python
import functools
import math

import jax
import jax.numpy as jnp
from jax import lax
from jax.experimental import pallas as pl
from jax.experimental.pallas import tpu as pltpu

F32 = jnp.float32
BF16 = jnp.bfloat16
HIGHEST = lax.Precision.HIGHEST

D = 1024
B_CTX, T_CTX = 16, 256
B_LAT, T_LAT = 8, 1024
PAST = 256
N_CTX = B_CTX * T_CTX
N_LAT = B_LAT * T_LAT
N_TOK = N_CTX + N_LAT
DEPTH = 2
N_MOD = 6
GRID_W = 64
GDN_H, GDN_DK, GDN_DV = 4, 128, 128
CONV_W = 5
CHUNK = 64
DIFF_H, DIFF_DH = 4, 64
MLA_H, Q_LORA, KV_LORA, QK_NOPE, QK_ROPE, V_HEAD = 8, 256, 256, 128, 64, 128
N_EXP, TOP_K, D_FF = 32, 4, 1024
SWIGLU_LIMIT, SWIGLU_ALPHA = 7.0, 1.702
ROPE_BASE = 10000.0
EPS = 1e-6

LANE = 128
TM = 256
N_TILES = N_TOK // TM
SUB_ROWS = 16
TILE_ROWS = TOP_K * TM + N_EXP * SUB_ROWS
MB = 256
SUB = MB // SUB_ROWS
FF_CHUNK = 256
GDN_TRIP_ROWS = 256
N_MACRO = -(-((TOP_K * N_TOK) // SUB_ROWS + N_TILES * N_EXP + N_EXP * (SUB - 1)) // SUB)
D_EXT = D + LANE
VMEM_LIMIT = 48 * 1024 * 1024


def _cparams(sem, vmem=VMEM_LIMIT):
    return pltpu.CompilerParams(dimension_semantics=sem, vmem_limit_bytes=vmem)


def _mod_row(t):
    return jnp.where(t < N_CTX // TM, 0, 1 + (t - N_CTX // TM) // (T_LAT // TM))


def _rms(x, g):
    return x * lax.rsqrt(jnp.mean(x * x, axis=-1, keepdims=True) + EPS) * g


def _silu(x):
    return x * jax.nn.sigmoid(x)


def _dot(a, b):
    return jnp.dot(a, b, preferred_element_type=F32)


def _dot_nt(a, b):
    return lax.dot_general(a, b, (((1,), (1,)), ((), ())), preferred_element_type=F32)


def _mod_kernel(c_ref, w_ref, b_ref, o_ref):
    s = _silu(c_ref[...])
    o_ref[...] = _dot(s.astype(BF16), w_ref[...].astype(BF16)) + b_ref[...]


def _modulation(cond, w_mod, b_mod):
    nc = 1536
    return pl.pallas_call(
        _mod_kernel,
        grid=(DEPTH, N_MOD * D // nc),
        in_specs=[pl.BlockSpec((16, D), lambda i, j: (0, 0)),
                  pl.BlockSpec((None, D, nc), lambda i, j: (i, 0, j)),
                  pl.BlockSpec((None, 1, nc), lambda i, j: (i, 0, j))],
        out_specs=pl.BlockSpec((None, 16, nc), lambda i, j: (i, 0, j)),
        out_shape=jax.ShapeDtypeStruct((DEPTH, 16, N_MOD * D), F32),
        compiler_params=_cparams(("arbitrary", "arbitrary")),
        name="modulation",
    )(cond, w_mod, b_mod.reshape(DEPTH, 1, N_MOD * D))


def _mod_spec(layer, k):
    return pl.BlockSpec((None, None, 1, D), lambda t: (layer, _mod_row(t), 0, k))


EVEN_W = 2 * 512 + 512 + 512 + 3 * 512 + LANE


def _in_even_kernel(x_ref, g_ref, sh_ref, sc_ref, w_ref, qkv_ref, z_ref, qd_ref, kd_ref, vd_ref, ab_ref):
    h = _rms(x_ref[...], g_ref[...]) * (1.0 + sc_ref[...]) + sh_ref[...]
    y = _dot(h.astype(BF16), w_ref[...])
    qkv_ref[...] = y[:, 0:1536]
    z_ref[...] = y[:, 1536:2048]
    qd_ref[...] = y[:, 2048:2560]
    kd_ref[...] = y[:, 2560:3072]
    vd_ref[...] = y[:, 3072:3584]
    ab_ref[...] = y[:, 3584:3712]


def _in_even(x, g, mod4, layer, w):
    row = lambda c: pl.BlockSpec((TM, c), lambda t: (t, 0))
    shp = lambda c: jax.ShapeDtypeStruct((N_TOK, c), F32)
    return pl.pallas_call(
        _in_even_kernel,
        grid=(N_TILES,),
        in_specs=[row(D), pl.BlockSpec((1, D), lambda t: (0, 0)), _mod_spec(layer, 0), _mod_spec(layer, 1),
                  pl.BlockSpec((D, EVEN_W), lambda t: (0, 0))],
        out_specs=[row(1536), row(512), row(512), row(512), row(512), row(LANE)],
        out_shape=[shp(1536), shp(512), shp(512), shp(512), shp(512), shp(LANE)],
        compiler_params=_cparams(("arbitrary",)),
        name="in_proj_even",
    )(x, g, mod4, mod4, w)


def _split_bf16(a):
    hi = a.astype(BF16)
    lo = (a - hi.astype(F32)).astype(BF16)
    return hi, lo


def _mm3(a, b):
    return _dot(a[0], b[0]) + (_dot(a[0], b[1]) + _dot(a[1], b[0]))


def _unit_tri_inverses(ls, eye, blk, bd):
    sp = _split_bf16
    d8 = [jnp.where(blk(8), l, 0.0) for l in ls]
    d8s = [sp(d) for d in d8]
    x2s = [sp(_mm3(d, bd(d))) for d in d8s]
    x2r = [bd(x) for x in x2s]
    ts = [eye - d for d in d8]
    ts = [t + _mm3(sp(t), xr) for t, xr in zip(ts, x2r)]
    x4 = [_mm3(x, xr) for x, xr in zip(x2s, x2r)]
    ts = [t + _mm3(sp(t), bd(sp(x))) for t, x in zip(ts, x4)]
    for s in (8, 16, 32):
        offs = [sp(jnp.where(blk(2 * s) & jnp.logical_not(blk(s)), l, 0.0)) for l in ls]
        tss = [sp(t) for t in ts]
        ys = [_mm3(o, bd(t)) for o, t in zip(offs, tss)]
        ts = [t - _mm3(t2, bd(sp(y))) for t, t2, y in zip(ts, tss, ys)]
    return ts


def _softplus(x):
    return jnp.maximum(x, 0.0) + jnp.log1p(jnp.exp(-jnp.abs(x)))


def _gdn_kernel(has_init, T, alog_ref, dtb_ref, qkv_ref, cw_ref, z_ref, ab_ref, gn_ref, *rest):
    if has_init:
        s0_ref, o_ref = rest[:2]
        st_ref = None
    else:
        o_ref, st_ref = rest[:2]
        s0_ref = None
    xp, qs, ks, vs, abt_s, uo_s, w_s, qg_s, kgt_s, in_s, gl_s, st_s = rest[2:]
    n = T // CHUNK
    H = GDN_H

    xp[0:8, :] = jnp.zeros((8, LANE), F32)
    xp[T + 8:T + 16, :] = jnp.zeros((8, LANE), F32)

    def l2n(x):
        return x * lax.rsqrt(jnp.sum(x * x, axis=-1, keepdims=True) + EPS)

    for blk in range(3 * H):
        cols = slice(blk * LANE, (blk + 1) * LANE)
        dst = slice((blk % H) * LANE, (blk % H + 1) * LANE)
        xp[8:T + 8, :] = qkv_ref[:, cols]
        w = cw_ref[:, cols]
        acc = xp[6:T + 6, :] * w[0:1, :]
        for i in range(1, CONV_W):
            acc = acc + xp[6 + i:T + 6 + i, :] * w[i:i + 1, :]
        y = _silu(acc)
        if blk < H:
            qs[:, dst] = l2n(y) * (GDN_DK ** -0.5)
        elif blk < 2 * H:
            ks[:, dst] = l2n(y)
        else:
            vs[:, dst] = y
    abt_s[...] = ab_ref[...].T[0:16, :]

    ii = lax.broadcasted_iota(jnp.int32, (CHUNK, LANE), 0)
    lane = lax.broadcasted_iota(jnp.int32, (CHUNK, LANE), 1)
    jm = lane & (CHUNK - 1)
    fwd = lane < CHUNK
    order = jnp.where(fwd, ii - jm, jm - ii)
    incl = order >= 0
    strict = order > 0
    incl_t = order <= 0
    eye = jnp.where(order == 0, 1.0, 0.0).astype(F32)
    blk_mask = lambda s: (ii // s) == (jm // s)
    zeros_c = jnp.zeros((CHUNK, LANE), F32)
    zeros_c16 = jnp.zeros((CHUNK, LANE), BF16)

    keep_f = jnp.where(fwd, 1.0, 0.0).astype(BF16)
    keep_b = jnp.where(fwd, 0.0, 1.0).astype(BF16)

    def bd(parts):
        return tuple(jnp.concatenate([a * keep_f, a * keep_b], axis=0) for a in parts)

    lane_row = lax.broadcasted_iota(jnp.int32, (1, LANE), 1)
    sub_col = lax.broadcasted_iota(jnp.int32, (16, 1), 0)
    a_lane, dt_lane = jnp.zeros((1, LANE), F32), jnp.zeros((1, LANE), F32)
    a_col, dt_col = jnp.zeros((16, 1), F32), jnp.zeros((16, 1), F32)
    for d in range(2):
        for h in range(H):
            a_lane = jnp.where(lane_row == d * H + h, alog_ref[d, h], a_lane)
            dt_lane = jnp.where(lane_row == d * H + h, dtb_ref[d, h], dt_lane)
            a_col = jnp.where(sub_col == d * H + h, alog_ref[d, h], a_col)
            dt_col = jnp.where(sub_col == d * H + h, dtb_ref[d, h], dt_col)
    nega_lane, nega_col = -jnp.exp(a_lane), -jnp.exp(a_col)

    def prep(p, carry):
        rbase = pl.multiple_of(p * GDN_TRIP_ROWS, GDN_TRIP_ROWS)
        g_rows = nega_col * _softplus(abt_s[:, pl.ds(rbase, GDN_TRIP_ROWS)] + dt_col)
        sysm = []
        for cc in range(GDN_TRIP_ROWS // CHUNK):
            r0 = pl.multiple_of(rbase + cc * CHUNK, CHUNK)
            abc = ab_ref[pl.ds(r0, CHUNK), :]
            g_cols = nega_lane * _softplus(abc + dt_lane)
            b_cols = jax.nn.sigmoid(abc)
            for h in range(H):
                hs = slice(h * LANE, (h + 1) * LANE)
                tok = slice(cc * CHUNK, (cc + 1) * CHUNK)
                sysm.append(dict(
                    c=(GDN_TRIP_ROWS // CHUNK) * p + cc, rows=pl.ds(r0, CHUNK), h=h,
                    qc=qs[pl.ds(r0, CHUNK), hs], kc=ks[pl.ds(r0, CHUNK), hs], vc=vs[pl.ds(r0, CHUNK), hs],
                    g_col=jnp.where(fwd, g_cols[:, h:h + 1], g_cols[:, H + h:H + h + 1]),
                    g_row=jnp.concatenate([g_rows[h:h + 1, tok], g_rows[H + h:H + h + 1, tok]], axis=1),
                    beta=(b_cols[:, 2 * H + h:2 * H + h + 1], b_cols[:, 3 * H + h:3 * H + h + 1])))
        for s in sysm:
            k16 = s["kc"].astype(BF16)
            kk16 = jnp.concatenate([k16, k16], axis=0)
            s["kk"] = _dot_nt(k16, kk16)
            s["qk"] = _dot_nt(s["qc"].astype(BF16), kk16)
        ls = []
        for s in sysm:
            m = jnp.where(incl, s["g_row"], 0.0)
            s["gcf"] = jnp.sum(jnp.where(fwd, m, 0.0), axis=1, keepdims=True)
            s["gcb"] = jnp.sum(jnp.where(fwd, 0.0, m), axis=1, keepdims=True)
            gc_col = jnp.where(fwd, s["gcf"], s["gcb"])
            gc_row = jnp.sum(jnp.where(incl_t, s["g_col"], 0.0), axis=0, keepdims=True)
            decay = jnp.where(incl, jnp.exp(jnp.where(incl, gc_col - gc_row, 0.0)), 0.0)
            beta = jnp.where(fwd, s["beta"][0], s["beta"][1])
            ls.append(jnp.where(strict, s["kk"] * beta * decay, 0.0))
            s["intra"] = s["qk"] * decay
        ts = _unit_tri_inverses(ls, eye, blk_mask, bd)
        for s, t in zip(sysm, ts):
            h, c, rows = s["h"], s["c"], s["rows"]
            egs = (jnp.exp(s["gcf"]), jnp.exp(s["gcb"]))
            r_f = jnp.concatenate([s["vc"] * s["beta"][0], s["kc"] * (s["beta"][0] * egs[0]), zeros_c, zeros_c], axis=1)
            r_b = jnp.concatenate([zeros_c, zeros_c, s["vc"] * s["beta"][1], s["kc"] * (s["beta"][1] * egs[1])], axis=1)
            uw = _dot(t.astype(BF16), jnp.concatenate([r_f, r_b], axis=0).astype(BF16))
            g_last = (s["gcf"][CHUNK - 1:CHUNK, :], s["gcb"][0:1, :])
            kgs = []
            for d in range(2):
                k = d * H + h
                uo_s[k, rows, :] = uw[:, 2 * d * LANE:(2 * d + 1) * LANE]
                w_s[k, rows, :] = uw[:, (2 * d + 1) * LANE:(2 * d + 2) * LANE].astype(BF16)
                qg_s[k, rows, :] = (s["qc"] * egs[d]).astype(BF16)
                gc = s["gcf"] if d == 0 else s["gcb"]
                kgs.append(s["kc"] * jnp.exp(g_last[d] - gc))
                gl_s[k, c] = jnp.broadcast_to(jnp.exp(g_last[d]), (8, LANE))
            in_s[h, c] = s["intra"].astype(BF16)
            kgt_s[h, c] = jnp.concatenate(kgs, axis=0).T.astype(BF16)
        return carry

    lax.fori_loop(0, T // GDN_TRIP_ROWS, prep, 0)

    for d in range(2):
        for h in range(H):
            st_s[d * H + h] = s0_ref[d, h] if has_init else jnp.zeros((GDN_DK, GDN_DV), F32)

    def scan(i, carry):
        cs = (i, n - 1 - i)
        ks_ = range(2 * H)
        c_of = [cs[k // H] for k in ks_]
        rows = [pl.ds(pl.multiple_of(c * CHUNK, CHUNK), CHUNK) for c in c_of]
        s16 = [st_s[k].astype(BF16) for k in ks_]
        ws = [_dot(w_s[k, rows[k], :], s16[k]) for k in ks_]
        qss = [_dot(qg_s[k, rows[k], :], s16[k]) for k in ks_]
        vn = [(uo_s[k, rows[k], :] - ws[k]).astype(BF16) for k in ks_]
        vpad = [jnp.concatenate([vn[k], zeros_c16] if k < H else [zeros_c16, vn[k]], axis=0) for k in ks_]
        o = [qss[k] + _dot(in_s[k % H, c_of[k]], vpad[k]) for k in ks_]
        snew = [st_s[k] * gl_s[k, c_of[k]][0:1, :] + _dot(kgt_s[k % H, c_of[k]], vpad[k]) for k in ks_]
        for k in ks_:
            uo_s[k, rows[k], :] = o[k]
            st_s[k] = snew[k]
        return carry

    lax.fori_loop(0, n, scan, 0)
    for h in range(H):
        hs = slice(h * LANE, (h + 1) * LANE)
        o_ref[:, hs] = _rms(uo_s[h] + uo_s[H + h], gn_ref[...]) * _silu(z_ref[:, hs])
        if st_ref is not None:
            st_ref[0, h] = st_s[h]
            st_ref[1, h] = st_s[H + h]


def _gdn(qkv, z, ab, conv_w, a_log, dt_bias, gn, s0, nb, T, row0):
    n = T // CHUNK
    blk0 = row0 // T
    has_init = s0 is not None
    H, W = GDN_H, GDN_H * LANE
    smem = pl.BlockSpec(memory_space=pltpu.SMEM)
    in_specs = [smem, smem,
                pl.BlockSpec((T, 3 * W), lambda b: (blk0 + b, 0), pipeline_mode=pl.Buffered(1)),
                pl.BlockSpec((CONV_W, 3 * W), lambda b: (0, 0)),
                pl.BlockSpec((T, W), lambda b: (blk0 + b, 0), pipeline_mode=pl.Buffered(1)),
                pl.BlockSpec((T, LANE), lambda b: (blk0 + b, 0)),
                pl.BlockSpec((1, LANE), lambda b: (0, 0))]
    args = [a_log, dt_bias, qkv, conv_w, z, ab, gn]
    st_spec = pl.BlockSpec((None, 2, H, GDN_DK, GDN_DV), lambda b: (b, 0, 0, 0, 0))
    o_spec = pl.BlockSpec((T, W), lambda b: (b, 0))
    o_shape = jax.ShapeDtypeStruct((nb * T, W), F32)
    if has_init:
        in_specs.append(st_spec)
        args.append(s0)
        out_specs, out_shape = o_spec, o_shape
    else:
        out_specs = [o_spec, st_spec]
        out_shape = [o_shape, jax.ShapeDtypeStruct((nb, 2, H, GDN_DK, GDN_DV), F32)]
    scratch = [pltpu.VMEM((T + 16, LANE), F32), pltpu.VMEM((T, W), F32), pltpu.VMEM((T, W), F32),
               pltpu.VMEM((T, W), F32), pltpu.VMEM((16, T), F32),
               pltpu.VMEM((2 * H, T, LANE), F32), pltpu.VMEM((2 * H, T, LANE), BF16),
               pltpu.VMEM((2 * H, T, LANE), BF16), pltpu.VMEM((H, n, LANE, LANE), BF16),
               pltpu.VMEM((H, n, CHUNK, LANE), BF16), pltpu.VMEM((2 * H, n, 8, LANE), F32),
               pltpu.VMEM((2 * H, GDN_DK, GDN_DV), F32)]
    return pl.pallas_call(
        functools.partial(_gdn_kernel, has_init, T),
        grid=(nb,),
        in_specs=in_specs, out_specs=out_specs, out_shape=out_shape, scratch_shapes=scratch,
        compiler_params=_cparams(("arbitrary",), 56 * 1024 * 1024),
        name="gdn_lat" if has_init else "gdn_ctx",
    )(*args)


def _rope(x, cos, sin_signed):
    lane = lax.broadcasted_iota(jnp.int32, x.shape, 1)
    up = pltpu.roll(x, LANE - 16, axis=1)
    dn = pltpu.roll(x, 16, axis=1)
    rot = jnp.where((lane % 32) < 16, up, dn)
    return x * cos + rot * sin_signed


def _rope_tables():
    rows = T_LAT // GRID_W
    r = jnp.repeat(jnp.arange(rows, dtype=F32), GRID_W)
    col = jnp.tile(jnp.arange(GRID_W, dtype=F32), rows)
    quarter = 16
    inv = ROPE_BASE ** (-jnp.arange(quarter, dtype=F32) / quarter)
    ar = r[:, None] * inv
    ac = col[:, None] * inv
    ang = jnp.concatenate([ar, ar, ac, ac], axis=-1)
    sign = jnp.tile(jnp.concatenate([-jnp.ones(16, F32), jnp.ones(16, F32)]), 2)
    cos, sin = jnp.cos(ang), jnp.sin(ang) * sign
    return cos, sin


LOG2E = math.log2(math.e)


def _attend(parts, values):
    m = functools.reduce(jnp.maximum, [jnp.max(s, axis=-1, keepdims=True) for s in parts])
    es = [jnp.exp2(s - m) for s in parts]
    den = functools.reduce(lambda a, b: a + b, [jnp.sum(e, axis=-1, keepdims=True) for e in es])
    o = functools.reduce(lambda a, b: a + b, [_dot(e.astype(BF16), v) for e, v in zip(es, values)])
    return o * (1.0 / den)


def _diff_kernel(latent, T, lam_init, q_ref, k_ref, v_ref, lam_ref, gn_ref, *rest):
    if latent:
        cos_ref, sin_ref, ck_ref, cv_ref, o_ref = rest
    else:
        (o_ref,) = rest
    scale = DIFF_DH ** -0.5
    lp = lam_ref[...]
    lam = (jnp.exp(jnp.sum(lp[0:1] * lp[1:2], axis=-1, keepdims=True))
           - jnp.exp(jnp.sum(lp[2:3] * lp[3:4], axis=-1, keepdims=True)) + lam_init)
    k = k_ref[...]
    if latent:
        k = _rope(k, cos_ref[...], sin_ref[...])
    k16 = k.astype(BF16)
    v16 = v_ref[...].astype(BF16)
    if latent:
        ck16 = ck_ref[...].astype(BF16)
        cv16 = cv_ref[...].astype(BF16)
    tq = 256
    lane = lax.broadcasted_iota(jnp.int32, (tq, LANE), 1)
    for qb in range(T // tq):
        q = q_ref[qb * tq:(qb + 1) * tq, :]
        if latent:
            q = _rope(q, cos_ref[qb * tq:(qb + 1) * tq, :], sin_ref[qb * tq:(qb + 1) * tq, :])
        os = []
        for m in range(2):
            qm = jnp.where((lane // DIFF_DH) == m, q * (scale * LOG2E), 0.0).astype(BF16)
            parts = [_dot_nt(qm, k16)]
            if latent:
                parts.insert(0, _dot_nt(qm, ck16))
            os.append(_attend(parts, [cv16, v16] if latent else [v16]))
        o = os[0] - lam * os[1]
        o_ref[qb * tq:(qb + 1) * tq, :] = _rms(o, gn_ref[...]) * (1.0 - lam_init)


def _diff_attn(qd, kd, vd, lam_p, gn, lam_init, nb, T, row0, cache=None, rope=None):
    blk0 = row0 // T
    latent = cache is not None
    hcol = pl.BlockSpec((T, LANE), lambda b, h: (blk0 + b, h))
    in_specs = [hcol, hcol, hcol, pl.BlockSpec((4, DIFF_DH), lambda b, h: (0, 0)),
                pl.BlockSpec((1, LANE), lambda b, h: (0, 0))]
    args = [qd, kd, vd, lam_p, gn]
    if latent:
        tab = pl.BlockSpec((T, LANE), lambda b, h: (0, 0))
        cch = pl.BlockSpec((None, PAST, LANE), lambda b, h: (b, 0, h))
        in_specs += [tab, tab, cch, cch]
        args += [rope[0], rope[1], cache[0], cache[1]]
    return pl.pallas_call(
        functools.partial(_diff_kernel, latent, T, lam_init),
        grid=(nb, DIFF_H),
        in_specs=in_specs,
        out_specs=pl.BlockSpec((T, LANE), lambda b, h: (b, h)),
        out_shape=jax.ShapeDtypeStruct((nb * T, DIFF_H * 2 * DIFF_DH), F32),
        compiler_params=_cparams(("arbitrary", "arbitrary")),
        name="diff_lat" if latent else "diff_ctx",
    )(*args)


ODD_W = Q_LORA + KV_LORA + LANE
MLA_QW = 2 * LANE


def _in_odd_kernel(x_ref, g_ref, sh_ref, sc_ref, w_ref, qg_ref, wuq_ref, kvg_ref, wukv_ref,
                   q_ref, kv_ref, ckv_ref, kpe_ref):
    h = _rms(x_ref[...], g_ref[...]) * (1.0 + sc_ref[...]) + sh_ref[...]
    y = _dot(h.astype(BF16), w_ref[...])
    qn = _rms(y[:, 0:Q_LORA], qg_ref[...])
    q_ref[...] = _dot(qn.astype(BF16), wuq_ref[...])
    ckv = _rms(y[:, Q_LORA:Q_LORA + KV_LORA], kvg_ref[...])
    ckv_ref[...] = ckv
    kv_ref[...] = _dot(ckv.astype(BF16), wukv_ref[...])
    kpe_ref[...] = y[:, Q_LORA + KV_LORA:ODD_W]


def _in_odd(x, g, mod4, layer, w, qg, wuq, kvg, wukv):
    row = lambda c: pl.BlockSpec((TM, c), lambda t: (t, 0))
    full = lambda a: pl.BlockSpec(a.shape, lambda t: (0, 0))
    shp = lambda c: jax.ShapeDtypeStruct((N_TOK, c), F32)
    return pl.pallas_call(
        _in_odd_kernel,
        grid=(N_TILES,),
        in_specs=[row(D), full(g), _mod_spec(layer, 0), _mod_spec(layer, 1), full(w), full(qg), full(wuq),
                  full(kvg), full(wukv)],
        out_specs=[row(MLA_H * MLA_QW), row(MLA_H * MLA_QW), row(KV_LORA), row(LANE)],
        out_shape=[shp(MLA_H * MLA_QW), shp(MLA_H * MLA_QW), shp(KV_LORA), shp(LANE)],
        compiler_params=_cparams(("arbitrary",)),
        name="in_proj_odd",
    )(x, g, mod4, mod4, w, qg, wuq, kvg, wukv)


def _mm_kernel(x_ref, w_ref, o_ref):
    o_ref[...] = _dot(x_ref[...].astype(BF16), w_ref[...])


def _matmul(x, w):
    m, k = x.shape
    n = w.shape[1]
    return pl.pallas_call(
        _mm_kernel,
        grid=(m // TM,),
        in_specs=[pl.BlockSpec((TM, k), lambda i: (i, 0)), pl.BlockSpec((k, n), lambda i: (0, 0))],
        out_specs=pl.BlockSpec((TM, n), lambda i: (i, 0)),
        out_shape=jax.ShapeDtypeStruct((m, n), F32),
        compiler_params=_cparams(("arbitrary",)),
        name="cache_kv_proj",
    )(x, w)


def _mla_kernel(latent, T, q_ref, kv_ref, kpe_ref, *rest):
    if latent:
        cos_ref, sin_ref, ckv_ref, ckpe_ref, o_ref = rest
    else:
        (o_ref,) = rest
    scale = (QK_NOPE + QK_ROPE) ** -0.5
    kv = kv_ref[...]
    v16 = kv[:, LANE:].astype(BF16)
    kpe = kpe_ref[...]
    if latent:
        kpe = _rope(kpe, cos_ref[...], sin_ref[...])
        ckv = ckv_ref[...]
        ck16 = jnp.concatenate([ckv[:, :LANE], ckpe_ref[...]], axis=1).astype(BF16)
        cv16 = ckv[:, LANE:].astype(BF16)
    k16 = jnp.concatenate([kv[:, :LANE], kpe], axis=1).astype(BF16)
    tq = 256
    for qb in range(T // tq):
        rows = slice(qb * tq, (qb + 1) * tq)
        q = q_ref[rows, :]
        qp = q[:, LANE:]
        if latent:
            qp = _rope(qp, cos_ref[rows, :], sin_ref[rows, :])
        q16 = (jnp.concatenate([q[:, :LANE], qp], axis=1) * (scale * LOG2E)).astype(BF16)
        parts = [_dot_nt(q16, k16)]
        if latent:
            parts.insert(0, _dot_nt(q16, ck16))
        o_ref[rows, :] = _attend(parts, [cv16, v16] if latent else [v16])


def _mla_attn(q, kv, kpe, nb, T, row0, cache=None, rope=None):
    blk0 = row0 // T
    latent = cache is not None
    head = pl.BlockSpec((T, MLA_QW), lambda b, h: (blk0 + b, h))
    in_specs = [head, head, pl.BlockSpec((T, LANE), lambda b, h: (blk0 + b, 0))]
    args = [q, kv, kpe]
    if latent:
        tab = pl.BlockSpec((T, LANE), lambda b, h: (0, 0))
        in_specs += [tab, tab, pl.BlockSpec((PAST, MLA_QW), lambda b, h: (b, h)),
                     pl.BlockSpec((PAST, LANE), lambda b, h: (b, 0))]
        args += [rope[0], rope[1], cache[0], cache[1]]
    return pl.pallas_call(
        functools.partial(_mla_kernel, latent, T),
        grid=(nb, MLA_H),
        in_specs=in_specs,
        out_specs=pl.BlockSpec((T, V_HEAD), lambda b, h: (b, h)),
        out_shape=jax.ShapeDtypeStruct((nb * T, MLA_H * V_HEAD), F32),
        compiler_params=_cparams(("arbitrary", "arbitrary")),
        name="mla_lat" if latent else "mla_ctx",
    )(*args)


def _route_kernel(n_mix, *refs):
    ctx_refs = refs[:n_mix]
    lat_refs = refs[n_mix:2 * n_mix]
    w_refs = refs[2 * n_mix:3 * n_mix]
    (x_ref, g1_ref, gate1_ref, g2_ref, sh2_ref, sc2_ref, wrh_ref, wrl_ref, br_ref,
     x1_ref, hx_ref, pw_ref, pwt_ref, cnt_ref, y_s) = refs[3 * n_mix:]
    t = pl.program_id(0)

    def out_proj(mix_refs):
        y = _dot(mix_refs[0][...].astype(BF16), w_refs[0][...])
        for m in range(1, n_mix):
            y = y + _dot(mix_refs[m][...].astype(BF16), w_refs[m][...])
        y_s[...] = y

    pl.when(t < N_CTX // TM)(lambda: out_proj(ctx_refs))
    pl.when(t >= N_CTX // TM)(lambda: out_proj(lat_refs))
    x1 = x_ref[...] + gate1_ref[...] * _rms(y_s[...], g1_ref[...])
    x1_ref[...] = x1
    h2 = _rms(x1, g2_ref[...]) * (1.0 + sc2_ref[...]) + sh2_ref[...]

    logits = _mm3(_split_bf16(h2), (wrh_ref[...], wrl_ref[...])) + br_ref[...]
    lane_i = lax.broadcasted_iota(jnp.int32, (TM, LANE), 1)
    lane = lane_i.astype(F32)
    neg = jnp.float32(-jnp.inf)
    l = jnp.where(lane_i < N_EXP, logits, neg)
    sel = jnp.zeros((TM, LANE), F32)
    comb = jnp.zeros((TM, LANE), F32)
    wsum = jnp.zeros((TM, 1), F32)
    hots, wts = [], []
    for k in range(TOP_K):
        m = jnp.max(l, axis=-1, keepdims=True)
        idx = jnp.min(jnp.where(l == m, lane, float(LANE)), axis=-1, keepdims=True)
        hot = lane == idx
        if k == 0:
            m0 = m
        w = jnp.exp(m - m0)
        wsum = wsum + w
        hots.append(hot)
        wts.append(w)
        sel = sel + jnp.where(hot, 1.0, 0.0)
        comb = comb + jnp.where(hot, w, 0.0)
        l = jnp.where(hot, neg, l)
    inv = 1.0 / wsum
    comb = comb * inv

    cnt = jnp.sum(sel, axis=0, keepdims=True)
    cnt_ref[...] = cnt
    padded = jnp.floor((cnt + (SUB_ROWS - 1)) * (1.0 / SUB_ROWS)) * SUB_ROWS
    li = lax.broadcasted_iota(jnp.int32, (LANE, LANE), 0)
    lj = lax.broadcasted_iota(jnp.int32, (LANE, LANE), 1)
    before = jnp.where(li < lj, 1.0, 0.0).astype(BF16)
    start = _dot(jnp.broadcast_to(padded, (8, LANE)).astype(BF16), before)[0:1, :]
    ti = lax.broadcasted_iota(jnp.int32, (TM, TM), 0)
    tj = lax.broadcasted_iota(jnp.int32, (TM, TM), 1)
    earlier = jnp.where(tj < ti, 1.0, 0.0).astype(BF16)
    pos_full = _dot(earlier, sel.astype(BF16)) + start
    pw = jnp.zeros((TM, LANE), F32)
    for k in range(TOP_K):
        pos_k = jnp.sum(jnp.where(hots[k], pos_full, 0.0), axis=-1, keepdims=True)
        pw = pw + jnp.where(lane_i == k, pos_k, 0.0) + jnp.where(lane_i == TOP_K + k, wts[k] * inv, 0.0)
    pw_ref[...] = pw
    pwt_ref[...] = pw.T[0:8, :]

    chi, clo = _split_bf16(comb)
    ext = chi.astype(F32) + pltpu.roll(clo.astype(F32), N_EXP, axis=1)
    hx_ref[:, 0:D] = h2.astype(BF16)
    hx_ref[:, D:D_EXT] = ext.astype(BF16)


def _route(mixes, w_outs, x, norm_g, mod4, layer, w_router, b_router):
    n_mix = len(mixes)
    wr_hi, wr_lo = _split_bf16(w_router)
    row = lambda c: pl.BlockSpec((TM, c), lambda t: (t, 0))
    full = lambda a: pl.BlockSpec(a.shape, lambda t: (0, 0))
    g = lambda k: pl.BlockSpec((None, 1, D), lambda t: (k, 0, 0))
    n_ctx = N_CTX // TM
    ctx_row = lambda c: pl.BlockSpec((TM, c), lambda t: (jnp.minimum(t, n_ctx - 1), 0))
    lat_row = lambda c: pl.BlockSpec((TM, c), lambda t: (jnp.maximum(t - n_ctx, 0), 0))
    in_specs = ([ctx_row(m[0].shape[1]) for m in mixes] + [lat_row(m[1].shape[1]) for m in mixes]
                + [full(w) for w in w_outs]
                + [row(D), g(1), _mod_spec(layer, 2), g(2), _mod_spec(layer, 3), _mod_spec(layer, 4),
                   full(wr_hi), full(wr_lo), full(b_router)])
    ng = norm_g.reshape(4, 1, D)
    return pl.pallas_call(
        functools.partial(_route_kernel, n_mix),
        grid=(N_TILES,),
        in_specs=in_specs,
        out_specs=[row(D), row(D_EXT), row(LANE), pl.BlockSpec((None, 8, TM), lambda t: (t, 0, 0)),
                   pl.BlockSpec((None, 1, LANE), lambda t: (t, 0, 0))],
        out_shape=[jax.ShapeDtypeStruct((N_TOK, D), F32), jax.ShapeDtypeStruct((N_TOK, D_EXT), BF16),
                   jax.ShapeDtypeStruct((N_TOK, LANE), F32), jax.ShapeDtypeStruct((N_TILES, 8, TM), F32),
                   jax.ShapeDtypeStruct((N_TILES, 1, LANE), F32)],
        scratch_shapes=[pltpu.VMEM((TM, D), F32)],
        compiler_params=_cparams(("arbitrary",)),
        name="out_proj_route",
    )(*[m[0] for m in mixes], *[m[1] for m in mixes], *w_outs, x, ng, mod4, ng, mod4, mod4, wr_hi, wr_lo,
      b_router)


def _dispatch_kernel(hx_ref, pwt_ref, s_ref):
    j = pl.program_id(0)

    @pl.when(j < N_TILES)
    def _():
        rio = lax.broadcasted_iota(jnp.int32, (TILE_ROWS, TM), 0).astype(F32)
        p = jnp.zeros((TILE_ROWS, TM), F32)
        for k in range(TOP_K):
            p = p + jnp.where(rio == pwt_ref[k:k + 1, :], 1.0, 0.0)
        s_ref[...] = _dot(p.astype(BF16), hx_ref[...]).astype(BF16)

    @pl.when(j == N_TILES)
    def _():
        s_ref[...] = jnp.zeros((TILE_ROWS, D_EXT), BF16)


def _dispatch(hx, pwt):
    clamp = lambda j: jnp.minimum(j, N_TILES - 1)
    return pl.pallas_call(
        _dispatch_kernel,
        grid=(N_TILES + 1,),
        in_specs=[pl.BlockSpec((TM, D_EXT), lambda j: (clamp(j), 0)),
                  pl.BlockSpec((None, 8, TM), lambda j: (clamp(j), 0, 0))],
        out_specs=pl.BlockSpec((TILE_ROWS, D_EXT), lambda j: (j, 0)),
        out_shape=jax.ShapeDtypeStruct(((N_TILES + 1) * TILE_ROWS, D_EXT), BF16),
        compiler_params=_cparams(("arbitrary",)),
        name="moe_dispatch",
    )(hx, pwt)


def _expert_kernel(blk_e_ref, src_ref, nact_ref, tail_ref, s_hbm, wgu_ref, bgu_ref, wd_ref, bd_ref, y_hbm,
                   xbuf, ybuf, wgu_s, wd_s, zbuf, gsem, ssem, zsem):
    i = pl.program_id(0)
    nsteps = pl.num_programs(0)
    nact = nact_ref[0]
    slot = i % 2

    def gather_copy(step, sl, s):
        src = pl.multiple_of(src_ref[step * SUB + s], SUB_ROWS)
        return pltpu.make_async_copy(s_hbm.at[pl.ds(src, SUB_ROWS)],
                                     xbuf.at[sl, pl.ds(s * SUB_ROWS, SUB_ROWS)], gsem.at[sl])

    def scatter_copy(step, sl, s):
        dst = pl.multiple_of(src_ref[step * SUB + s], SUB_ROWS)
        return pltpu.make_async_copy(ybuf.at[sl, pl.ds(s * SUB_ROWS, SUB_ROWS)],
                                     y_hbm.at[pl.ds(dst, SUB_ROWS)], ssem.at[sl])

    def gather_start(step, sl):
        for s in range(SUB):
            gather_copy(step, sl, s).start()

    def gather_wait(step, sl):
        for s in range(SUB):
            gather_copy(step, sl, s).wait()

    def scatter_start(step, sl):
        for s in range(SUB):
            scatter_copy(step, sl, s).start()

    def scatter_wait(step, sl):
        for s in range(SUB):
            scatter_copy(step, sl, s).wait()

    def spare_fill():
        copies = [pltpu.make_async_copy(zbuf, y_hbm.at[pl.ds(N_TILES * TILE_ROWS + b * SUB_ROWS, SUB_ROWS)],
                                        zsem.at[1]) for b in range(2 * SUB)]
        for cp in copies:
            cp.start()
        for cp in copies:
            cp.wait()

    def tail_fill(wait):
        def tile(j, carry):
            first = tail_ref[j]

            def block(b, c):
                dst = pl.multiple_of(first + b * SUB_ROWS, SUB_ROWS)
                cp = pltpu.make_async_copy(zbuf, y_hbm.at[pl.ds(dst, SUB_ROWS)], zsem.at[0])
                if wait:
                    cp.wait()
                else:
                    cp.start()
                return c

            return lax.fori_loop(0, ((j + 1) * TILE_ROWS - first) // SUB_ROWS, block, carry)

        lax.fori_loop(0, N_TILES, tile, 0)

    @pl.when(i == 0)
    def _():
        gather_start(0, 0)
        zbuf[...] = jnp.zeros((SUB_ROWS, D), BF16)
        spare_fill()
        tail_fill(wait=False)

    @pl.when(i < nact)
    def _():
        gather_wait(i, slot)

        @pl.when(i + 1 < nact)
        def _():
            gather_start(i + 1, 1 - slot)

        e = blk_e_ref[i]
        e_prev = blk_e_ref[jnp.maximum(i - 1, 0)]

        @pl.when((i == 0) | (e != e_prev))
        def _():
            wgu_s[...] = wgu_ref[...].astype(BF16)
            wd_s[...] = wd_ref[...].astype(BF16)

        x = xbuf[slot]
        ext = x[:, D:D_EXT].astype(F32)
        lane = lax.broadcasted_iota(jnp.int32, (MB, LANE), 1)
        wr = jnp.sum(jnp.where((lane == e) | (lane == e + N_EXP), ext, 0.0), axis=-1, keepdims=True)
        xt = x[:, 0:D]
        n_chunks = D_FF // FF_CHUNK

        def gate_up(j):
            gc = slice(j * FF_CHUNK, (j + 1) * FF_CHUNK)
            uc = slice(D_FF + j * FF_CHUNK, D_FF + (j + 1) * FF_CHUNK)
            return _dot(xt, wgu_s[:, gc]) + bgu_ref[:, gc], _dot(xt, wgu_s[:, uc]) + bgu_ref[:, uc]

        y = bd_ref[...]
        nxt = gate_up(0)
        for j in range(n_chunks):
            gate, up = nxt
            if j + 1 < n_chunks:
                nxt = gate_up(j + 1)
            gate = jnp.minimum(gate, SWIGLU_LIMIT)
            up = jnp.clip(up, -SWIGLU_LIMIT, SWIGLU_LIMIT)
            act = (up + 1.0) * gate * jax.nn.sigmoid(SWIGLU_ALPHA * gate)
            y = y + _dot(act.astype(BF16), wd_s[j * FF_CHUNK:(j + 1) * FF_CHUNK, :])
        y = y * wr

        @pl.when(i >= 2)
        def _():
            scatter_wait(i - 2, slot)

        ybuf[slot] = y.astype(BF16)
        scatter_start(i, slot)

    @pl.when(i == nsteps - 1)
    def _():
        @pl.when(nact >= 2)
        def _():
            scatter_wait(nact - 2, nact % 2)

        scatter_wait(nact - 1, (nact - 1) % 2)
        tail_fill(wait=True)


def _experts(blk_e, src, nact, tail, s_sorted, layer, w_gu, b_gu, w_dn, b_dn):
    rows = N_TILES * TILE_ROWS + 2 * MB
    per_expert = lambda r, c: pl.BlockSpec((None, None, r, c), lambda i, be, sr, na, tl: (layer, be[i], 0, 0))
    grid_spec = pltpu.PrefetchScalarGridSpec(
        num_scalar_prefetch=4,
        grid=(N_MACRO,),
        in_specs=[pl.BlockSpec(memory_space=pl.ANY), per_expert(D, 2 * D_FF), per_expert(1, 2 * D_FF),
                  per_expert(D_FF, D), per_expert(1, D)],
        out_specs=pl.BlockSpec(memory_space=pl.ANY),
        scratch_shapes=[pltpu.VMEM((2, MB, D_EXT), BF16), pltpu.VMEM((2, MB, D), BF16),
                        pltpu.VMEM((D, 2 * D_FF), BF16), pltpu.VMEM((D_FF, D), BF16),
                        pltpu.VMEM((SUB_ROWS, D), BF16),
                        pltpu.SemaphoreType.DMA((2,)), pltpu.SemaphoreType.DMA((2,)),
                        pltpu.SemaphoreType.DMA((2,))])
    return pl.pallas_call(
        _expert_kernel,
        grid_spec=grid_spec,
        out_shape=jax.ShapeDtypeStruct((rows, D), BF16),
        compiler_params=_cparams(("arbitrary",), 56 * 1024 * 1024),
        name="moe_experts",
    )(blk_e, src, nact, tail, s_sorted, w_gu, b_gu.reshape(DEPTH, N_EXP, 1, 2 * D_FF), w_dn,
      b_dn.reshape(DEPTH, N_EXP, 1, D))


def _combine_kernel(y_ref, pw_ref, x1_ref, g_ref, gate_ref, o_ref):
    lio = lax.broadcasted_iota(jnp.int32, (TM, TILE_ROWS), 1).astype(F32)
    pw = pw_ref[...]
    p = jnp.zeros((TM, TILE_ROWS), F32)
    for k in range(TOP_K):
        p = p + jnp.where(lio == pw[:, k:k + 1], 1.0, 0.0)
    f = _dot(p.astype(BF16), y_ref[...])
    o_ref[...] = x1_ref[...] + gate_ref[...] * _rms(f, g_ref[...])


def _combine(y_sorted, pw, x1, norm_g, mod4, layer):
    row = lambda c: pl.BlockSpec((TM, c), lambda t: (t, 0))
    return pl.pallas_call(
        _combine_kernel,
        grid=(N_TILES,),
        in_specs=[pl.BlockSpec((TILE_ROWS, D), lambda t: (t, 0)), row(LANE), row(D),
                  pl.BlockSpec((None, 1, D), lambda t: (3, 0, 0)), _mod_spec(layer, 5)],
        out_specs=row(D),
        out_shape=jax.ShapeDtypeStruct((N_TOK, D), F32),
        compiler_params=_cparams(("arbitrary",)),
        name="moe_combine",
    )(y_sorted, pw, x1, norm_g.reshape(4, 1, D), mod4)


def _block_tables(cnt):
    cnt = cnt[:, 0, :N_EXP].astype(jnp.int32)
    nb = (cnt + (SUB_ROWS - 1)) // SUB_ROWS
    lstart = (jnp.cumsum(nb, axis=1) - nb) * SUB_ROWS
    cum_nb = jnp.cumsum(nb, axis=0)
    nb_e = cum_nb[-1]
    mb_e = (nb_e + (SUB - 1)) // SUB
    mend = jnp.cumsum(mb_e)
    nact = mend[-1]
    i = jnp.arange(N_MACRO, dtype=jnp.int32)
    e_i = jnp.minimum(jnp.sum(mend[None, :] <= i[:, None], axis=1), N_EXP - 1).astype(jnp.int32)
    e_last = jnp.sum(mend <= nact - 1).astype(jnp.int32)
    blk_e = jnp.where(i < nact, e_i, e_last)
    mstart = mend - mb_e
    s = jnp.arange(SUB, dtype=jnp.int32)
    q = (i - mstart[blk_e])[:, None] * SUB + s[None, :]
    valid = (i < nact)[:, None] & (q < nb_e[blk_e][:, None])
    cum_t = cum_nb.T[blk_e]
    j = jnp.sum(cum_t[:, None, :] <= q[:, :, None], axis=2)
    j = jnp.minimum(j, N_TILES - 1)
    at_j = jnp.arange(N_TILES, dtype=jnp.int32)[None, None, :] == j[:, :, None]
    pick = lambda tab: jnp.sum(jnp.where(at_j, tab[:, None, :], 0), axis=2)
    prev = pick(cum_t - nb.T[blk_e])
    row = j * TILE_ROWS + pick(lstart.T[blk_e]) + (q - prev) * SUB_ROWS
    spare = N_TILES * TILE_ROWS + ((i % 2)[:, None] * SUB + s[None, :]) * SUB_ROWS
    src = jnp.where(valid, row, spare).astype(jnp.int32).reshape(-1)
    tail = jnp.arange(N_TILES, dtype=jnp.int32) * TILE_ROWS + jnp.sum(nb, axis=1) * SUB_ROWS
    return blk_e.astype(jnp.int32), src, nact.reshape(1).astype(jnp.int32), tail.astype(jnp.int32)


def _moe(mixes, w_outs, x, norm_g, mod4, layer, w_router, b_router, w_gu, b_gu, w_dn, b_dn):
    wr = jnp.pad(w_router, ((0, 0), (0, LANE - N_EXP)))
    br = jnp.pad(b_router, (0, LANE - N_EXP)).reshape(1, LANE)
    x1, hx, pw, pwt, cnt = _route(mixes, w_outs, x, norm_g, mod4, layer, wr, br)
    blk_e, src, nact, tail = _block_tables(cnt)
    s_sorted = _dispatch(hx, pwt)
    y_sorted = _experts(blk_e, src, nact, tail, s_sorted, layer, w_gu, b_gu, w_dn, b_dn)
    return _combine(y_sorted, pw, x1, norm_g, mod4, layer)


def kernel(x_prompt, x_sample, c, c_ctx, state_delta, cache_diff_k, cache_diff_v, cache_mla_ckv, cache_mla_kpe, w_mod, b_mod, norm_g, w_router, b_router, w_gate_up, b_gate_up, w_down, b_down, w_in_even, conv_w, a_log, dt_bias, gdn_norm_g, diff_lambda, diff_norm_g, w_out_even, w_in_odd, q_norm_g, w_uq, kv_norm_g, w_ukv, w_out_odd):
    x = jnp.concatenate([x_prompt.reshape(N_CTX, D), x_sample.reshape(N_LAT, D)], axis=0)
    cond = jnp.concatenate([c_ctx[None, :], c, jnp.zeros((16 - 1 - B_LAT, D), F32)], axis=0)
    mod4 = _modulation(cond, w_mod, b_mod).reshape(DEPTH, 16, 1, N_MOD * D)
    cos64, sin64 = _rope_tables()
    rope2 = (jnp.tile(cos64, (1, 2)), jnp.tile(sin64, (1, 2)))
    rope1 = (jnp.pad(cos64, ((0, 0), (0, 64))), jnp.pad(sin64, ((0, 0), (0, 64))))

    we = w_in_even[0]
    w_cat = jnp.concatenate([we[:, 0:2048], we[:, 2064:3600], we[:, 2048:2064],
                             jnp.zeros((D, LANE - 16), F32)], axis=1).astype(BF16)
    qkv, z, qd, kd, vd, ab = _in_even(x, norm_g[0, 0:1], mod4, 0, w_cat)
    gn = gdn_norm_g[0].reshape(1, GDN_DV)
    oa_c, new_state = _gdn(qkv, z, ab, conv_w[0], a_log[0], dt_bias[0], gn, None, B_CTX, T_CTX, 0)
    oa_l = _gdn(qkv, z, ab, conv_w[0], a_log[0], dt_bias[0], gn, state_delta[:, 0], B_LAT, T_LAT, N_CTX)
    lam_init = 0.8 - 0.6 * math.exp(-0.3 * 0)
    dgn = diff_norm_g[0].reshape(1, 2 * DIFF_DH)
    od_c = _diff_attn(qd, kd, vd, diff_lambda[0], dgn, lam_init, B_CTX, T_CTX, 0)
    cache_dk = cache_diff_k[:, 0].reshape(B_LAT, PAST, DIFF_H * 2 * DIFF_DH)
    cache_dv = cache_diff_v[:, 0].reshape(B_LAT, PAST, DIFF_H * 2 * DIFF_DH)
    od_l = _diff_attn(qd, kd, vd, diff_lambda[0], dgn, lam_init, B_LAT, T_LAT, N_CTX,
                      cache=(cache_dk, cache_dv), rope=rope2)
    wo = w_out_even[0].astype(BF16)
    x = _moe([(oa_c, oa_l), (od_c, od_l)], [wo[:512], wo[512:]], x, norm_g[0], mod4, 0, w_router[0], b_router[0],
             w_gate_up, b_gate_up, w_down, b_down)
    new_dk = kd[:N_CTX].reshape(B_CTX, 1, T_CTX, DIFF_H, 2, DIFF_DH)
    new_dv = vd[:N_CTX].reshape(B_CTX, 1, T_CTX, DIFF_H, 2 * DIFF_DH)
    new_state = new_state.reshape(B_CTX, 1, 2, GDN_H, GDN_DK, GDN_DV)

    w_odd = jnp.pad(w_in_odd[0], ((0, 0), (0, ODD_W - (Q_LORA + KV_LORA + QK_ROPE)))).astype(BF16)
    wuq = w_uq[0].reshape(Q_LORA, MLA_H, QK_NOPE + QK_ROPE)
    wuq = jnp.pad(wuq, ((0, 0), (0, 0), (0, MLA_QW - (QK_NOPE + QK_ROPE)))).reshape(Q_LORA, MLA_H * MLA_QW)
    wukv = w_ukv[0].astype(BF16)
    q, kv, ckv, kpe = _in_odd(x, norm_g[1, 0:1], mod4, 1, w_odd, q_norm_g[0].reshape(1, Q_LORA),
                              wuq.astype(BF16), kv_norm_g[0].reshape(1, KV_LORA), wukv)
    om_c = _mla_attn(q, kv, kpe, B_CTX, T_CTX, 0)
    cache_kv = _matmul(cache_mla_ckv[:, 0].reshape(B_LAT * PAST, KV_LORA), wukv)
    cache_kpe = jnp.pad(cache_mla_kpe[:, 0].reshape(B_LAT * PAST, QK_ROPE), ((0, 0), (0, LANE - QK_ROPE)))
    om_l = _mla_attn(q, kv, kpe, B_LAT, T_LAT, N_CTX, cache=(cache_kv, cache_kpe), rope=rope1)
    x = _moe([(om_c, om_l)], [w_out_odd[0].astype(BF16)], x, norm_g[1], mod4, 1, w_router[1], b_router[1],
             w_gate_up, b_gate_up, w_down, b_down)
    new_ckv = ckv[:N_CTX].reshape(B_CTX, 1, T_CTX, KV_LORA)
    new_kpe = kpe[:N_CTX, :QK_ROPE].reshape(B_CTX, 1, T_CTX, QK_ROPE)

    return (x[:N_CTX].reshape(B_CTX, T_CTX, D), x[N_CTX:].reshape(B_LAT, T_LAT, D),
            new_state, new_dk, new_dv, new_ckv, new_kpe)
```

```python
import functools
import math

import jax
import jax.numpy as jnp
from jax import lax
from jax.experimental import pallas as pl
from jax.experimental.pallas import tpu as pltpu

F32 = jnp.float32
BF16 = jnp.bfloat16
HIGHEST = lax.Precision.HIGHEST

D = 1024
B_CTX, T_CTX = 16, 256
B_LAT, T_LAT = 8, 1024
PAST = 256
N_CTX = B_CTX * T_CTX
N_LAT = B_LAT * T_LAT
N_TOK = N_CTX + N_LAT
DEPTH = 2
N_MOD = 6
GRID_W = 64
GDN_H, GDN_DK, GDN_DV = 4, 128, 128
CONV_W = 5
CHUNK = 64
DIFF_H, DIFF_DH = 4, 64
MLA_H, Q_LORA, KV_LORA, QK_NOPE, QK_ROPE, V_HEAD = 8, 256, 256, 128, 64, 128
N_EXP, TOP_K, D_FF = 32, 4, 1024
SWIGLU_LIMIT, SWIGLU_ALPHA = 7.0, 1.702
ROPE_BASE = 10000.0
EPS = 1e-6

LANE = 128
TM = 256
N_TILES = N_TOK // TM
SUB_ROWS = 16
TILE_ROWS = TOP_K * TM + N_EXP * SUB_ROWS
MB = 256
SUB = MB // SUB_ROWS
FF_CHUNK = 256
GDN_TRIP_ROWS = 256
N_MACRO = -(-((TOP_K * N_TOK) // SUB_ROWS + N_TILES * N_EXP + N_EXP * (SUB - 1)) // SUB)
D_EXT = D + LANE
VMEM_LIMIT = 48 * 1024 * 1024


def _cparams(sem, vmem=VMEM_LIMIT):
    return pltpu.CompilerParams(dimension_semantics=sem, vmem_limit_bytes=vmem)


N_CTX_TILES = N_CTX // TM


def _mod_row(t):
    return jnp.where(t < N_CTX_TILES, 0, 1 + (t - N_CTX_TILES) // (T_LAT // TM))


def _ctx_spec(c):
    return pl.BlockSpec((TM, c), lambda t: (jnp.minimum(t, N_CTX_TILES - 1), 0))


def _lat_spec(c, first=0):
    return pl.BlockSpec((TM, c), lambda t: (jnp.maximum(t - N_CTX_TILES, 0) + first, 0))


def _token_pair(x):
    return (x[0], x[1], 0) if isinstance(x, tuple) else (x, x, N_CTX_TILES)


def _per_group(body, ctx_ref, lat_ref):
    t = pl.program_id(0)
    pl.when(t < N_CTX_TILES)(lambda: body(ctx_ref))
    pl.when(t >= N_CTX_TILES)(lambda: body(lat_ref))


def _rms(x, g):
    return x * lax.rsqrt(jnp.mean(x * x, axis=-1, keepdims=True) + EPS) * g


def _silu(x):
    return x * jax.nn.sigmoid(x)


def _dot(a, b):
    return jnp.dot(a, b, preferred_element_type=F32)


def _dot_nt(a, b):
    return lax.dot_general(a, b, (((1,), (1,)), ((), ())), preferred_element_type=F32)


def _mod_kernel(c_ref, w_ref, b_ref, o_ref):
    s = _silu(c_ref[...])
    o_ref[...] = _dot(s.astype(BF16), w_ref[...].astype(BF16)) + b_ref[...]


def _modulation(cond, w_mod, b_mod):
    nc = 1536
    return pl.pallas_call(
        _mod_kernel,
        grid=(DEPTH, N_MOD * D // nc),
        in_specs=[pl.BlockSpec((16, D), lambda i, j: (0, 0)),
                  pl.BlockSpec((None, D, nc), lambda i, j: (i, 0, j)),
                  pl.BlockSpec((None, 1, nc), lambda i, j: (i, 0, j))],
        out_specs=pl.BlockSpec((None, 16, nc), lambda i, j: (i, 0, j)),
        out_shape=jax.ShapeDtypeStruct((DEPTH, 16, N_MOD * D), F32),
        compiler_params=_cparams(("arbitrary", "arbitrary")),
        name="modulation",
    )(cond, w_mod, b_mod.reshape(DEPTH, 1, N_MOD * D))


def _mod_spec(layer, k):
    return pl.BlockSpec((None, None, 1, D), lambda t: (layer, _mod_row(t), 0, k))


EVEN_W = 2 * 512 + 512 + 512 + 3 * 512 + LANE


def _in_even_kernel(xc_ref, xl_ref, g_ref, sh_ref, sc_ref, w_ref, qkv_ref, z_ref, qd_ref, kd_ref, vd_ref, ab_ref):
    def body(x_ref):
        h = _rms(x_ref[...], g_ref[...]) * (1.0 + sc_ref[...]) + sh_ref[...]
        y = _dot(h.astype(BF16), w_ref[...])
        qkv_ref[...] = y[:, 0:1536]
        z_ref[...] = y[:, 1536:2048]
        qd_ref[...] = y[:, 2048:2560]
        kd_ref[...] = y[:, 2560:3072]
        vd_ref[...] = y[:, 3072:3584]
        ab_ref[...] = y[:, 3584:3712]

    _per_group(body, xc_ref, xl_ref)


def _in_even(x, g, mod4, layer, w):
    xc, xl, first = _token_pair(x)
    row = lambda c: pl.BlockSpec((TM, c), lambda t: (t, 0))
    shp = lambda c: jax.ShapeDtypeStruct((N_TOK, c), F32)
    return pl.pallas_call(
        _in_even_kernel,
        grid=(N_TILES,),
        in_specs=[_ctx_spec(D), _lat_spec(D, first), pl.BlockSpec((1, D), lambda t: (0, 0)), _mod_spec(layer, 0),
                  _mod_spec(layer, 1), pl.BlockSpec((D, EVEN_W), lambda t: (0, 0))],
        out_specs=[row(1536), row(512), row(512), row(512), row(512), row(LANE)],
        out_shape=[shp(1536), shp(512), shp(512), shp(512), shp(512), shp(LANE)],
        compiler_params=_cparams(("arbitrary",)),
        name="in_proj_even",
    )(xc, xl, g, mod4, mod4, w)


def _split_bf16(a):
    hi = a.astype(BF16)
    lo = (a - hi.astype(F32)).astype(BF16)
    return hi, lo


def _mm3(a, b):
    return _dot(a[0], b[0]) + (_dot(a[0], b[1]) + _dot(a[1], b[0]))


def _unit_tri_inverses(ls, eye, blk, bd):
    sp = _split_bf16
    d8 = [jnp.where(blk(8), l, 0.0) for l in ls]
    d8s = [sp(d) for d in d8]
    x2s = [sp(_mm3(d, bd(d))) for d in d8s]
    x2r = [bd(x) for x in x2s]
    ts = [eye - d for d in d8]
    ts = [t + _mm3(sp(t), xr) for t, xr in zip(ts, x2r)]
    x4 = [_mm3(x, xr) for x, xr in zip(x2s, x2r)]
    ts = [t + _mm3(sp(t), bd(sp(x))) for t, x in zip(ts, x4)]
    for s in (8, 16, 32):
        offs = [sp(jnp.where(blk(2 * s) & jnp.logical_not(blk(s)), l, 0.0)) for l in ls]
        tss = [sp(t) for t in ts]
        ys = [_mm3(o, bd(t)) for o, t in zip(offs, tss)]
        ts = [t - _mm3(t2, bd(sp(y))) for t, t2, y in zip(ts, tss, ys)]
    return ts


def _softplus(x):
    return jnp.maximum(x, 0.0) + jnp.log1p(jnp.exp(-jnp.abs(x)))


def _gdn_kernel(has_init, T, alog_ref, dtb_ref, qkv_ref, cw_ref, z_ref, ab_ref, gn_ref, *rest):
    if has_init:
        s0_ref, o_ref = rest[:2]
        st_ref = None
    else:
        o_ref, st_ref = rest[:2]
        s0_ref = None
    xp, qs, ks, vs, abt_s, uo_s, w_s, qg_s, kgt_s, in_s, gl_s, st_s = rest[2:]
    n = T // CHUNK
    H = GDN_H

    xp[0:8, :] = jnp.zeros((8, LANE), F32)
    xp[T + 8:T + 16, :] = jnp.zeros((8, LANE), F32)

    def l2n(x):
        return x * lax.rsqrt(jnp.sum(x * x, axis=-1, keepdims=True) + EPS)

    for blk in range(3 * H):
        cols = slice(blk * LANE, (blk + 1) * LANE)
        dst = slice((blk % H) * LANE, (blk % H + 1) * LANE)
        xp[8:T + 8, :] = qkv_ref[:, cols]
        w = cw_ref[:, cols]
        acc = xp[6:T + 6, :] * w[0:1, :]
        for i in range(1, CONV_W):
            acc = acc + xp[6 + i:T + 6 + i, :] * w[i:i + 1, :]
        y = _silu(acc)
        if blk < H:
            qs[:, dst] = l2n(y) * (GDN_DK ** -0.5)
        elif blk < 2 * H:
            ks[:, dst] = l2n(y)
        else:
            vs[:, dst] = y
    abt_s[...] = ab_ref[...].T[0:16, :]

    ii = lax.broadcasted_iota(jnp.int32, (CHUNK, LANE), 0)
    lane = lax.broadcasted_iota(jnp.int32, (CHUNK, LANE), 1)
    jm = lane & (CHUNK - 1)
    fwd = lane < CHUNK
    order = jnp.where(fwd, ii - jm, jm - ii)
    incl = order >= 0
    strict = order > 0
    incl_t = order <= 0
    eye = jnp.where(order == 0, 1.0, 0.0).astype(F32)
    blk_mask = lambda s: (ii // s) == (jm // s)
    zeros_c = jnp.zeros((CHUNK, LANE), F32)
    zeros_c16 = jnp.zeros((CHUNK, LANE), BF16)

    keep_f = jnp.where(fwd, 1.0, 0.0).astype(BF16)
    keep_b = jnp.where(fwd, 0.0, 1.0).astype(BF16)

    def bd(parts):
        return tuple(jnp.concatenate([a * keep_f, a * keep_b], axis=0) for a in parts)

    lane_row = lax.broadcasted_iota(jnp.int32, (1, LANE), 1)
    sub_col = lax.broadcasted_iota(jnp.int32, (16, 1), 0)
    a_lane, dt_lane = jnp.zeros((1, LANE), F32), jnp.zeros((1, LANE), F32)
    a_col, dt_col = jnp.zeros((16, 1), F32), jnp.zeros((16, 1), F32)
    for d in range(2):
        for h in range(H):
            a_lane = jnp.where(lane_row == d * H + h, alog_ref[d, h], a_lane)
            dt_lane = jnp.where(lane_row == d * H + h, dtb_ref[d, h], dt_lane)
            a_col = jnp.where(sub_col == d * H + h, alog_ref[d, h], a_col)
            dt_col = jnp.where(sub_col == d * H + h, dtb_ref[d, h], dt_col)
    nega_lane, nega_col = -jnp.exp(a_lane), -jnp.exp(a_col)

    def prep(p, carry):
        rbase = pl.multiple_of(p * GDN_TRIP_ROWS, GDN_TRIP_ROWS)
        g_rows = nega_col * _softplus(abt_s[:, pl.ds(rbase, GDN_TRIP_ROWS)] + dt_col)
        sysm = []
        for cc in range(GDN_TRIP_ROWS // CHUNK):
            r0 = pl.multiple_of(rbase + cc * CHUNK, CHUNK)
            abc = ab_ref[pl.ds(r0, CHUNK), :]
            g_cols = nega_lane * _softplus(abc + dt_lane)
            b_cols = jax.nn.sigmoid(abc)
            for h in range(H):
                hs = slice(h * LANE, (h + 1) * LANE)
                tok = slice(cc * CHUNK, (cc + 1) * CHUNK)
                sysm.append(dict(
                    c=(GDN_TRIP_ROWS // CHUNK) * p + cc, rows=pl.ds(r0, CHUNK), h=h,
                    qc=qs[pl.ds(r0, CHUNK), hs], kc=ks[pl.ds(r0, CHUNK), hs], vc=vs[pl.ds(r0, CHUNK), hs],
                    g_col=jnp.where(fwd, g_cols[:, h:h + 1], g_cols[:, H + h:H + h + 1]),
                    g_row=jnp.concatenate([g_rows[h:h + 1, tok], g_rows[H + h:H + h + 1, tok]], axis=1),
                    beta=(b_cols[:, 2 * H + h:2 * H + h + 1], b_cols[:, 3 * H + h:3 * H + h + 1])))
        for s in sysm:
            k16 = s["kc"].astype(BF16)
            kk16 = jnp.concatenate([k16, k16], axis=0)
            s["kk"] = _dot_nt(k16, kk16)
            s["qk"] = _dot_nt(s["qc"].astype(BF16), kk16)
        ls = []
        for s in sysm:
            m = jnp.where(incl, s["g_row"], 0.0)
            s["gcf"] = jnp.sum(jnp.where(fwd, m, 0.0), axis=1, keepdims=True)
            s["gcb"] = jnp.sum(jnp.where(fwd, 0.0, m), axis=1, keepdims=True)
            gc_col = jnp.where(fwd, s["gcf"], s["gcb"])
            gc_row = jnp.sum(jnp.where(incl_t, s["g_col"], 0.0), axis=0, keepdims=True)
            decay = jnp.where(incl, jnp.exp(jnp.where(incl, gc_col - gc_row, 0.0)), 0.0)
            beta = jnp.where(fwd, s["beta"][0], s["beta"][1])
            ls.append(jnp.where(strict, s["kk"] * beta * decay, 0.0))
            s["intra"] = s["qk"] * decay
        ts = _unit_tri_inverses(ls, eye, blk_mask, bd)
        for s, t in zip(sysm, ts):
            h, c, rows = s["h"], s["c"], s["rows"]
            egs = (jnp.exp(s["gcf"]), jnp.exp(s["gcb"]))
            r_f = jnp.concatenate([s["vc"] * s["beta"][0], s["kc"] * (s["beta"][0] * egs[0]), zeros_c, zeros_c], axis=1)
            r_b = jnp.concatenate([zeros_c, zeros_c, s["vc"] * s["beta"][1], s["kc"] * (s["beta"][1] * egs[1])], axis=1)
            uw = _dot(t.astype(BF16), jnp.concatenate([r_f, r_b], axis=0).astype(BF16))
            g_last = (s["gcf"][CHUNK - 1:CHUNK, :], s["gcb"][0:1, :])
            kgs = []
            for d in range(2):
                k = d * H + h
                uo_s[k, rows, :] = uw[:, 2 * d * LANE:(2 * d + 1) * LANE]
                w_s[k, rows, :] = uw[:, (2 * d + 1) * LANE:(2 * d + 2) * LANE].astype(BF16)
                qg_s[k, rows, :] = (s["qc"] * egs[d]).astype(BF16)
                gc = s["gcf"] if d == 0 else s["gcb"]
                kgs.append(s["kc"] * jnp.exp(g_last[d] - gc))
                gl_s[k, c] = jnp.broadcast_to(jnp.exp(g_last[d]), (8, LANE))
            in_s[h, c] = s["intra"].astype(BF16)
            kgt_s[h, c] = jnp.concatenate(kgs, axis=0).T.astype(BF16)
        return carry

    lax.fori_loop(0, T // GDN_TRIP_ROWS, prep, 0)

    for d in range(2):
        for h in range(H):
            st_s[d * H + h] = s0_ref[d, h] if has_init else jnp.zeros((GDN_DK, GDN_DV), F32)

    def scan(i, carry):
        cs = (i, n - 1 - i)
        ks_ = range(2 * H)
        c_of = [cs[k // H] for k in ks_]
        rows = [pl.ds(pl.multiple_of(c * CHUNK, CHUNK), CHUNK) for c in c_of]
        s16 = [st_s[k].astype(BF16) for k in ks_]
        ws = [_dot(w_s[k, rows[k], :], s16[k]) for k in ks_]
        qss = [_dot(qg_s[k, rows[k], :], s16[k]) for k in ks_]
        vn = [(uo_s[k, rows[k], :] - ws[k]).astype(BF16) for k in ks_]
        vpad = [jnp.concatenate([vn[k], zeros_c16] if k < H else [zeros_c16, vn[k]], axis=0) for k in ks_]
        o = [qss[k] + _dot(in_s[k % H, c_of[k]], vpad[k]) for k in ks_]
        snew = [st_s[k] * gl_s[k, c_of[k]][0:1, :] + _dot(kgt_s[k % H, c_of[k]], vpad[k]) for k in ks_]
        for k in ks_:
            uo_s[k, rows[k], :] = o[k]
            st_s[k] = snew[k]
        return carry

    lax.fori_loop(0, n, scan, 0)
    for h in range(H):
        hs = slice(h * LANE, (h + 1) * LANE)
        o_ref[:, hs] = _rms(uo_s[h] + uo_s[H + h], gn_ref[...]) * _silu(z_ref[:, hs])
        if st_ref is not None:
            st_ref[0, h] = st_s[h]
            st_ref[1, h] = st_s[H + h]


def _gdn(qkv, z, ab, conv_w, a_log, dt_bias, gn, s0, nb, T, row0):
    n = T // CHUNK
    blk0 = row0 // T
    has_init = s0 is not None
    H, W = GDN_H, GDN_H * LANE
    smem = pl.BlockSpec(memory_space=pltpu.SMEM)
    in_specs = [smem, smem,
                pl.BlockSpec((T, 3 * W), lambda b: (blk0 + b, 0), pipeline_mode=pl.Buffered(1)),
                pl.BlockSpec((CONV_W, 3 * W), lambda b: (0, 0)),
                pl.BlockSpec((T, W), lambda b: (blk0 + b, 0), pipeline_mode=pl.Buffered(1)),
                pl.BlockSpec((T, LANE), lambda b: (blk0 + b, 0)),
                pl.BlockSpec((1, LANE), lambda b: (0, 0))]
    args = [a_log, dt_bias, qkv, conv_w, z, ab, gn]
    st_spec = pl.BlockSpec((None, 2, H, GDN_DK, GDN_DV), lambda b: (b, 0, 0, 0, 0))
    o_spec = pl.BlockSpec((T, W), lambda b: (b, 0))
    o_shape = jax.ShapeDtypeStruct((nb * T, W), F32)
    if has_init:
        in_specs.append(st_spec)
        args.append(s0)
        out_specs, out_shape = o_spec, o_shape
    else:
        out_specs = [o_spec, st_spec]
        out_shape = [o_shape, jax.ShapeDtypeStruct((nb, 2, H, GDN_DK, GDN_DV), F32)]
    scratch = [pltpu.VMEM((T + 16, LANE), F32), pltpu.VMEM((T, W), F32), pltpu.VMEM((T, W), F32),
               pltpu.VMEM((T, W), F32), pltpu.VMEM((16, T), F32),
               pltpu.VMEM((2 * H, T, LANE), F32), pltpu.VMEM((2 * H, T, LANE), BF16),
               pltpu.VMEM((2 * H, T, LANE), BF16), pltpu.VMEM((H, n, LANE, LANE), BF16),
               pltpu.VMEM((H, n, CHUNK, LANE), BF16), pltpu.VMEM((2 * H, n, 8, LANE), F32),
               pltpu.VMEM((2 * H, GDN_DK, GDN_DV), F32)]
    return pl.pallas_call(
        functools.partial(_gdn_kernel, has_init, T),
        grid=(nb,),
        in_specs=in_specs, out_specs=out_specs, out_shape=out_shape, scratch_shapes=scratch,
        compiler_params=_cparams(("arbitrary",), 56 * 1024 * 1024),
        name="gdn_lat" if has_init else "gdn_ctx",
    )(*args)


def _rope(x, cos, sin_signed):
    lane = lax.broadcasted_iota(jnp.int32, x.shape, 1)
    up = pltpu.roll(x, LANE - 16, axis=1)
    dn = pltpu.roll(x, 16, axis=1)
    rot = jnp.where((lane % 32) < 16, up, dn)
    return x * cos + rot * sin_signed


def _rope_tables():
    rows = T_LAT // GRID_W
    r = jnp.repeat(jnp.arange(rows, dtype=F32), GRID_W)
    col = jnp.tile(jnp.arange(GRID_W, dtype=F32), rows)
    quarter = 16
    inv = ROPE_BASE ** (-jnp.arange(quarter, dtype=F32) / quarter)
    ar = r[:, None] * inv
    ac = col[:, None] * inv
    ang = jnp.concatenate([ar, ar, ac, ac], axis=-1)
    sign = jnp.tile(jnp.concatenate([-jnp.ones(16, F32), jnp.ones(16, F32)]), 2)
    cos, sin = jnp.cos(ang), jnp.sin(ang) * sign
    return cos, sin


LOG2E = math.log2(math.e)


def _attend(parts, values):
    m = functools.reduce(jnp.maximum, [jnp.max(s, axis=-1, keepdims=True) for s in parts])
    es = [jnp.exp2(s - m) for s in parts]
    den = functools.reduce(lambda a, b: a + b, [jnp.sum(e, axis=-1, keepdims=True) for e in es])
    o = functools.reduce(lambda a, b: a + b, [_dot(e.astype(BF16), v) for e, v in zip(es, values)])
    return o * (1.0 / den)


def _diff_kernel(latent, T, lam_init, q_ref, k_ref, v_ref, lam_ref, gn_ref, *rest):
    if latent:
        cos_ref, sin_ref, ck_ref, cv_ref, o_ref = rest
    else:
        (o_ref,) = rest
    scale = DIFF_DH ** -0.5
    lp = lam_ref[...]
    lam = (jnp.exp(jnp.sum(lp[0:1] * lp[1:2], axis=-1, keepdims=True))
           - jnp.exp(jnp.sum(lp[2:3] * lp[3:4], axis=-1, keepdims=True)) + lam_init)
    k = k_ref[...]
    if latent:
        k = _rope(k, cos_ref[...], sin_ref[...])
    k16 = k.astype(BF16)
    v16 = v_ref[...].astype(BF16)
    if latent:
        ck16 = ck_ref[...].astype(BF16)
        cv16 = cv_ref[...].astype(BF16)
    tq = 256
    lane = lax.broadcasted_iota(jnp.int32, (tq, LANE), 1)
    for qb in range(T // tq):
        q = q_ref[qb * tq:(qb + 1) * tq, :]
        if latent:
            q = _rope(q, cos_ref[qb * tq:(qb + 1) * tq, :], sin_ref[qb * tq:(qb + 1) * tq, :])
        os = []
        for m in range(2):
            qm = jnp.where((lane // DIFF_DH) == m, q * (scale * LOG2E), 0.0).astype(BF16)
            parts = [_dot_nt(qm, k16)]
            if latent:
                parts.insert(0, _dot_nt(qm, ck16))
            os.append(_attend(parts, [cv16, v16] if latent else [v16]))
        o = os[0] - lam * os[1]
        o_ref[qb * tq:(qb + 1) * tq, :] = _rms(o, gn_ref[...]) * (1.0 - lam_init)


def _diff_attn(qd, kd, vd, lam_p, gn, lam_init, nb, T, row0, cache=None, rope=None):
    blk0 = row0 // T
    latent = cache is not None
    hcol = pl.BlockSpec((T, LANE), lambda b, h: (blk0 + b, h))
    in_specs = [hcol, hcol, hcol, pl.BlockSpec((4, DIFF_DH), lambda b, h: (0, 0)),
                pl.BlockSpec((1, LANE), lambda b, h: (0, 0))]
    args = [qd, kd, vd, lam_p, gn]
    if latent:
        tab = pl.BlockSpec((T, LANE), lambda b, h: (0, 0))
        cch = pl.BlockSpec((None, PAST, LANE), lambda b, h: (b, 0, h))
        in_specs += [tab, tab, cch, cch]
        args += [rope[0], rope[1], cache[0], cache[1]]
    return pl.pallas_call(
        functools.partial(_diff_kernel, latent, T, lam_init),
        grid=(nb, DIFF_H),
        in_specs=in_specs,
        out_specs=pl.BlockSpec((T, LANE), lambda b, h: (b, h)),
        out_shape=jax.ShapeDtypeStruct((nb * T, DIFF_H * 2 * DIFF_DH), F32),
        compiler_params=_cparams(("arbitrary", "arbitrary")),
        name="diff_lat" if latent else "diff_ctx",
    )(*args)


ODD_W = Q_LORA + KV_LORA + LANE
MLA_QW = 2 * LANE


def _in_odd_kernel(x_ref, g_ref, sh_ref, sc_ref, w_ref, qg_ref, wuq_ref, kvg_ref, wukv_ref,
                   q_ref, kv_ref, ckv_ref, kpe_ref):
    h = _rms(x_ref[...], g_ref[...]) * (1.0 + sc_ref[...]) + sh_ref[...]
    y = _dot(h.astype(BF16), w_ref[...])
    qn = _rms(y[:, 0:Q_LORA], qg_ref[...])
    q_ref[...] = _dot(qn.astype(BF16), wuq_ref[...])
    ckv = _rms(y[:, Q_LORA:Q_LORA + KV_LORA], kvg_ref[...])
    ckv_ref[...] = ckv
    kv_ref[...] = _dot(ckv.astype(BF16), wukv_ref[...])
    kpe_ref[...] = y[:, Q_LORA + KV_LORA:ODD_W]


def _in_odd(x, g, mod4, layer, w, qg, wuq, kvg, wukv):
    row = lambda c: pl.BlockSpec((TM, c), lambda t: (t, 0))
    full = lambda a: pl.BlockSpec(a.shape, lambda t: (0, 0))
    shp = lambda c: jax.ShapeDtypeStruct((N_TOK, c), F32)
    return pl.pallas_call(
        _in_odd_kernel,
        grid=(N_TILES,),
        in_specs=[row(D), full(g), _mod_spec(layer, 0), _mod_spec(layer, 1), full(w), full(qg), full(wuq),
                  full(kvg), full(wukv)],
        out_specs=[row(MLA_H * MLA_QW), row(MLA_H * MLA_QW), row(KV_LORA), row(LANE)],
        out_shape=[shp(MLA_H * MLA_QW), shp(MLA_H * MLA_QW), shp(KV_LORA), shp(LANE)],
        compiler_params=_cparams(("arbitrary",)),
        name="in_proj_odd",
    )(x, g, mod4, mod4, w, qg, wuq, kvg, wukv)


def _mm_kernel(x_ref, w_ref, o_ref):
    o_ref[...] = _dot(x_ref[...].astype(BF16), w_ref[...])


def _matmul(x, w):
    m, k = x.shape
    n = w.shape[1]
    return pl.pallas_call(
        _mm_kernel,
        grid=(m // TM,),
        in_specs=[pl.BlockSpec((TM, k), lambda i: (i, 0)), pl.BlockSpec((k, n), lambda i: (0, 0))],
        out_specs=pl.BlockSpec((TM, n), lambda i: (i, 0)),
        out_shape=jax.ShapeDtypeStruct((m, n), F32),
        compiler_params=_cparams(("arbitrary",)),
        name="cache_kv_proj",
    )(x, w)


def _mla_kernel(latent, T, q_ref, kv_ref, kpe_ref, *rest):
    if latent:
        cos_ref, sin_ref, ckv_ref, ckpe_ref, o_ref = rest
    else:
        (o_ref,) = rest
    scale = (QK_NOPE + QK_ROPE) ** -0.5
    kv = kv_ref[...]
    v16 = kv[:, LANE:].astype(BF16)
    kpe = kpe_ref[...]
    if latent:
        kpe = _rope(kpe, cos_ref[...], sin_ref[...])
        ckv = ckv_ref[...]
        ck16 = jnp.concatenate([ckv[:, :LANE], ckpe_ref[...]], axis=1).astype(BF16)
        cv16 = ckv[:, LANE:].astype(BF16)
    k16 = jnp.concatenate([kv[:, :LANE], kpe], axis=1).astype(BF16)
    tq = 256
    for qb in range(T // tq):
        rows = slice(qb * tq, (qb + 1) * tq)
        q = q_ref[rows, :]
        qp = q[:, LANE:]
        if latent:
            qp = _rope(qp, cos_ref[rows, :], sin_ref[rows, :])
        q16 = (jnp.concatenate([q[:, :LANE], qp], axis=1) * (scale * LOG2E)).astype(BF16)
        parts = [_dot_nt(q16, k16)]
        if latent:
            parts.insert(0, _dot_nt(q16, ck16))
        o_ref[rows, :] = _attend(parts, [cv16, v16] if latent else [v16])


def _mla_attn(q, kv, kpe, nb, T, row0, cache=None, rope=None):
    blk0 = row0 // T
    latent = cache is not None
    head = pl.BlockSpec((T, MLA_QW), lambda b, h: (blk0 + b, h))
    in_specs = [head, head, pl.BlockSpec((T, LANE), lambda b, h: (blk0 + b, 0))]
    args = [q, kv, kpe]
    if latent:
        tab = pl.BlockSpec((T, LANE), lambda b, h: (0, 0))
        in_specs += [tab, tab, pl.BlockSpec((PAST, MLA_QW), lambda b, h: (b, h)),
                     pl.BlockSpec((PAST, LANE), lambda b, h: (b, 0))]
        args += [rope[0], rope[1], cache[0], cache[1]]
    return pl.pallas_call(
        functools.partial(_mla_kernel, latent, T),
        grid=(nb, MLA_H),
        in_specs=in_specs,
        out_specs=pl.BlockSpec((T, V_HEAD), lambda b, h: (b, h)),
        out_shape=jax.ShapeDtypeStruct((nb * T, MLA_H * V_HEAD), F32),
        compiler_params=_cparams(("arbitrary", "arbitrary")),
        name="mla_lat" if latent else "mla_ctx",
    )(*args)


def _route_kernel(n_mix, *refs):
    ctx_refs = refs[:n_mix]
    lat_refs = refs[n_mix:2 * n_mix]
    w_refs = refs[2 * n_mix:3 * n_mix]
    (xc_ref, xl_ref, g1_ref, gate1_ref, g2_ref, sh2_ref, sc2_ref, wrh_ref, wrl_ref, br_ref,
     x1_ref, hx_ref, pw_ref, pwt_ref, cnt_ref, y_s, x_s) = refs[3 * n_mix:]

    def out_proj(group):
        mix_refs, x_ref = group
        y = _dot(mix_refs[0][...].astype(BF16), w_refs[0][...])
        for m in range(1, n_mix):
            y = y + _dot(mix_refs[m][...].astype(BF16), w_refs[m][...])
        y_s[...] = y
        x_s[...] = x_ref[...]

    _per_group(out_proj, (ctx_refs, xc_ref), (lat_refs, xl_ref))
    x1 = x_s[...] + gate1_ref[...] * _rms(y_s[...], g1_ref[...])
    x1_ref[...] = x1
    h2 = _rms(x1, g2_ref[...]) * (1.0 + sc2_ref[...]) + sh2_ref[...]

    logits = _mm3(_split_bf16(h2), (wrh_ref[...], wrl_ref[...])) + br_ref[...]
    lane_i = lax.broadcasted_iota(jnp.int32, (TM, LANE), 1)
    lane = lane_i.astype(F32)
    neg = jnp.float32(-jnp.inf)
    l = jnp.where(lane_i < N_EXP, logits, neg)
    sel = jnp.zeros((TM, LANE), F32)
    comb = jnp.zeros((TM, LANE), F32)
    wsum = jnp.zeros((TM, 1), F32)
    hots, wts = [], []
    for k in range(TOP_K):
        m = jnp.max(l, axis=-1, keepdims=True)
        idx = jnp.min(jnp.where(l == m, lane, float(LANE)), axis=-1, keepdims=True)
        hot = lane == idx
        if k == 0:
            m0 = m
        w = jnp.exp(m - m0)
        wsum = wsum + w
        hots.append(hot)
        wts.append(w)
        sel = sel + jnp.where(hot, 1.0, 0.0)
        comb = comb + jnp.where(hot, w, 0.0)
        l = jnp.where(hot, neg, l)
    inv = 1.0 / wsum
    comb = comb * inv

    cnt = jnp.sum(sel, axis=0, keepdims=True)
    cnt_ref[...] = cnt
    padded = jnp.floor((cnt + (SUB_ROWS - 1)) * (1.0 / SUB_ROWS)) * SUB_ROWS
    li = lax.broadcasted_iota(jnp.int32, (LANE, LANE), 0)
    lj = lax.broadcasted_iota(jnp.int32, (LANE, LANE), 1)
    before = jnp.where(li < lj, 1.0, 0.0).astype(BF16)
    start = _dot(jnp.broadcast_to(padded, (8, LANE)).astype(BF16), before)[0:1, :]
    ti = lax.broadcasted_iota(jnp.int32, (TM, TM), 0)
    tj = lax.broadcasted_iota(jnp.int32, (TM, TM), 1)
    earlier = jnp.where(tj < ti, 1.0, 0.0).astype(BF16)
    pos_full = _dot(earlier, sel.astype(BF16)) + start
    pw = jnp.zeros((TM, LANE), F32)
    for k in range(TOP_K):
        pos_k = jnp.sum(jnp.where(hots[k], pos_full, 0.0), axis=-1, keepdims=True)
        pw = pw + jnp.where(lane_i == k, pos_k, 0.0) + jnp.where(lane_i == TOP_K + k, wts[k] * inv, 0.0)
    pw_ref[...] = pw
    pwt_ref[...] = pw.T[0:8, :]

    chi, clo = _split_bf16(comb)
    ext = chi.astype(F32) + pltpu.roll(clo.astype(F32), N_EXP, axis=1)
    hx_ref[:, 0:D] = h2.astype(BF16)
    hx_ref[:, D:D_EXT] = ext.astype(BF16)


def _route(mixes, w_outs, x, norm_g, mod4, layer, w_router, b_router):
    n_mix = len(mixes)
    wr_hi, wr_lo = _split_bf16(w_router)
    row = lambda c: pl.BlockSpec((TM, c), lambda t: (t, 0))
    full = lambda a: pl.BlockSpec(a.shape, lambda t: (0, 0))
    g = lambda k: pl.BlockSpec((None, 1, D), lambda t: (k, 0, 0))
    xc, xl, first = _token_pair(x)
    in_specs = ([_ctx_spec(m[0].shape[1]) for m in mixes] + [_lat_spec(m[1].shape[1]) for m in mixes]
                + [full(w) for w in w_outs]
                + [_ctx_spec(D), _lat_spec(D, first), g(1), _mod_spec(layer, 2), g(2), _mod_spec(layer, 3),
                   _mod_spec(layer, 4), full(wr_hi), full(wr_lo), full(b_router)])
    ng = norm_g.reshape(4, 1, D)
    return pl.pallas_call(
        functools.partial(_route_kernel, n_mix),
        grid=(N_TILES,),
        in_specs=in_specs,
        out_specs=[row(D), row(D_EXT), row(LANE), pl.BlockSpec((None, 8, TM), lambda t: (t, 0, 0)),
                   pl.BlockSpec((None, 1, LANE), lambda t: (t, 0, 0))],
        out_shape=[jax.ShapeDtypeStruct((N_TOK, D), F32), jax.ShapeDtypeStruct((N_TOK, D_EXT), BF16),
                   jax.ShapeDtypeStruct((N_TOK, LANE), F32), jax.ShapeDtypeStruct((N_TILES, 8, TM), F32),
                   jax.ShapeDtypeStruct((N_TILES, 1, LANE), F32)],
        scratch_shapes=[pltpu.VMEM((TM, D), F32), pltpu.VMEM((TM, D), F32)],
        compiler_params=_cparams(("arbitrary",)),
        name="out_proj_route",
    )(*[m[0] for m in mixes], *[m[1] for m in mixes], *w_outs, xc, xl, ng, mod4, ng, mod4, mod4, wr_hi, wr_lo,
      b_router)


def _dispatch_kernel(hx_ref, pwt_ref, s_ref):
    j = pl.program_id(0)

    @pl.when(j < N_TILES)
    def _():
        rio = lax.broadcasted_iota(jnp.int32, (TILE_ROWS, TM), 0).astype(F32)
        p = jnp.zeros((TILE_ROWS, TM), F32)
        for k in range(TOP_K):
            p = p + jnp.where(rio == pwt_ref[k:k + 1, :], 1.0, 0.0)
        s_ref[...] = _dot(p.astype(BF16), hx_ref[...]).astype(BF16)

    @pl.when(j == N_TILES)
    def _():
        s_ref[...] = jnp.zeros((TILE_ROWS, D_EXT), BF16)


def _dispatch(hx, pwt):
    clamp = lambda j: jnp.minimum(j, N_TILES - 1)
    return pl.pallas_call(
        _dispatch_kernel,
        grid=(N_TILES + 1,),
        in_specs=[pl.BlockSpec((TM, D_EXT), lambda j: (clamp(j), 0)),
                  pl.BlockSpec((None, 8, TM), lambda j: (clamp(j), 0, 0))],
        out_specs=pl.BlockSpec((TILE_ROWS, D_EXT), lambda j: (j, 0)),
        out_shape=jax.ShapeDtypeStruct(((N_TILES + 1) * TILE_ROWS, D_EXT), BF16),
        compiler_params=_cparams(("arbitrary",)),
        name="moe_dispatch",
    )(hx, pwt)


def _expert_kernel(layer, blk_e_ref, next_e_ref, src_ref, nact_ref, tail_ref, s_hbm, wgu_hbm, bgu_ref, wd_hbm,
                   bd_ref, y_hbm, xbuf, ybuf, wgu_f, wd_f, wgu_s, wd_s, zbuf, gsem, ssem, zsem, wsem):
    i = pl.program_id(0)
    nsteps = pl.num_programs(0)
    nact = nact_ref[0]
    slot = i % 2

    def gather_copy(step, sl, s):
        src = pl.multiple_of(src_ref[step * SUB + s], SUB_ROWS)
        return pltpu.make_async_copy(s_hbm.at[pl.ds(src, SUB_ROWS)],
                                     xbuf.at[sl, pl.ds(s * SUB_ROWS, SUB_ROWS)], gsem.at[sl])

    def scatter_copy(step, sl, s):
        dst = pl.multiple_of(src_ref[step * SUB + s], SUB_ROWS)
        return pltpu.make_async_copy(ybuf.at[sl, pl.ds(s * SUB_ROWS, SUB_ROWS)],
                                     y_hbm.at[pl.ds(dst, SUB_ROWS)], ssem.at[sl])

    def gather_start(step, sl):
        for s in range(SUB):
            gather_copy(step, sl, s).start()

    def gather_wait(step, sl):
        for s in range(SUB):
            gather_copy(step, sl, s).wait()

    def scatter_start(step, sl):
        for s in range(SUB):
            scatter_copy(step, sl, s).start()

    def scatter_wait(step, sl):
        for s in range(SUB):
            scatter_copy(step, sl, s).wait()

    def spare_fill():
        copies = [pltpu.make_async_copy(zbuf, y_hbm.at[pl.ds(N_TILES * TILE_ROWS + b * SUB_ROWS, SUB_ROWS)],
                                        zsem.at[1]) for b in range(2 * SUB)]
        for cp in copies:
            cp.start()
        for cp in copies:
            cp.wait()

    def tail_fill(wait):
        def tile(j, carry):
            first = tail_ref[j]

            def block(b, c):
                dst = pl.multiple_of(first + b * SUB_ROWS, SUB_ROWS)
                cp = pltpu.make_async_copy(zbuf, y_hbm.at[pl.ds(dst, SUB_ROWS)], zsem.at[0])
                if wait:
                    cp.wait()
                else:
                    cp.start()
                return c

            return lax.fori_loop(0, ((j + 1) * TILE_ROWS - first) // SUB_ROWS, block, carry)

        lax.fori_loop(0, N_TILES, tile, 0)

    def weight_copies(e, wsl):
        return (pltpu.make_async_copy(wgu_hbm.at[layer, e], wgu_f.at[wsl], wsem.at[wsl]),
                pltpu.make_async_copy(wd_hbm.at[layer, e], wd_f.at[wsl], wsem.at[wsl]))

    @pl.when(i == 0)
    def _():
        for cp in weight_copies(blk_e_ref[0], next_e_ref[0] % 2):
            cp.start()
        gather_start(0, 0)
        zbuf[...] = jnp.zeros((SUB_ROWS, D), BF16)
        spare_fill()
        tail_fill(wait=False)

    @pl.when(i < nact)
    def _():
        gather_wait(i, slot)

        @pl.when(i + 1 < nact)
        def _():
            gather_start(i + 1, 1 - slot)

        e = blk_e_ref[i]
        e_prev = blk_e_ref[jnp.maximum(i - 1, 0)]

        @pl.when((i == 0) | (e != e_prev))
        def _():
            wsl = next_e_ref[i] % 2
            for cp in weight_copies(e, wsl):
                cp.wait()
            nxt = next_e_ref[i] // 2

            @pl.when(nxt < N_EXP)
            def _():
                for cp in weight_copies(nxt, 1 - wsl):
                    cp.start()

            wgu_s[...] = wgu_f[wsl].astype(BF16)
            wd_s[...] = wd_f[wsl].astype(BF16)

        x = xbuf[slot]
        ext = x[:, D:D_EXT].astype(F32)
        lane = lax.broadcasted_iota(jnp.int32, (MB, LANE), 1)
        wr = jnp.sum(jnp.where((lane == e) | (lane == e + N_EXP), ext, 0.0), axis=-1, keepdims=True)
        xt = x[:, 0:D]
        n_chunks = D_FF // FF_CHUNK

        def gate_up(j):
            gc = slice(j * FF_CHUNK, (j + 1) * FF_CHUNK)
            uc = slice(D_FF + j * FF_CHUNK, D_FF + (j + 1) * FF_CHUNK)
            return _dot(xt, wgu_s[:, gc]) + bgu_ref[:, gc], _dot(xt, wgu_s[:, uc]) + bgu_ref[:, uc]

        y = bd_ref[...]
        nxt = gate_up(0)
        for j in range(n_chunks):
            gate, up = nxt
            if j + 1 < n_chunks:
                nxt = gate_up(j + 1)
            gate = jnp.minimum(gate, SWIGLU_LIMIT)
            up = jnp.clip(up, -SWIGLU_LIMIT, SWIGLU_LIMIT)
            act = (up + 1.0) * gate * jax.nn.sigmoid(SWIGLU_ALPHA * gate)
            y = y + _dot(act.astype(BF16), wd_s[j * FF_CHUNK:(j + 1) * FF_CHUNK, :])
        y = y * wr

        @pl.when(i >= 2)
        def _():
            scatter_wait(i - 2, slot)

        ybuf[slot] = y.astype(BF16)
        scatter_start(i, slot)

    @pl.when(i == nsteps - 1)
    def _():
        @pl.when(nact >= 2)
        def _():
            scatter_wait(nact - 2, nact % 2)

        scatter_wait(nact - 1, (nact - 1) % 2)
        tail_fill(wait=True)


def _experts(blk_e, next_e, src, nact, tail, s_sorted, layer, w_gu, b_gu, w_dn, b_dn):
    rows = N_TILES * TILE_ROWS + 2 * MB
    per_expert = lambda c: pl.BlockSpec((None, None, 1, c), lambda i, be, ne, sr, na, tl: (layer, be[i], 0, 0))
    hbm = pl.BlockSpec(memory_space=pl.ANY)
    grid_spec = pltpu.PrefetchScalarGridSpec(
        num_scalar_prefetch=5,
        grid=(N_MACRO,),
        in_specs=[hbm, hbm, per_expert(2 * D_FF), hbm, per_expert(D)],
        out_specs=hbm,
        scratch_shapes=[pltpu.VMEM((2, MB, D_EXT), BF16), pltpu.VMEM((2, MB, D), BF16),
                        pltpu.VMEM((2, D, 2 * D_FF), F32), pltpu.VMEM((2, D_FF, D), F32),
                        pltpu.VMEM((D, 2 * D_FF), BF16), pltpu.VMEM((D_FF, D), BF16),
                        pltpu.VMEM((SUB_ROWS, D), BF16),
                        pltpu.SemaphoreType.DMA((2,)), pltpu.SemaphoreType.DMA((2,)),
                        pltpu.SemaphoreType.DMA((2,)), pltpu.SemaphoreType.DMA((2,))])
    return pl.pallas_call(
        functools.partial(_expert_kernel, layer),
        grid_spec=grid_spec,
        out_shape=jax.ShapeDtypeStruct((rows, D), BF16),
        compiler_params=_cparams(("arbitrary",), 56 * 1024 * 1024),
        name="moe_experts",
    )(blk_e, next_e, src, nact, tail, s_sorted, w_gu, b_gu.reshape(DEPTH, N_EXP, 1, 2 * D_FF), w_dn,
      b_dn.reshape(DEPTH, N_EXP, 1, D))


def _combine_kernel(y_ref, pw_ref, x1_ref, g_ref, gate_ref, *o_refs):
    lio = lax.broadcasted_iota(jnp.int32, (TM, TILE_ROWS), 1).astype(F32)
    pw = pw_ref[...]
    p = jnp.zeros((TM, TILE_ROWS), F32)
    for k in range(TOP_K):
        p = p + jnp.where(lio == pw[:, k:k + 1], 1.0, 0.0)
    f = _dot(p.astype(BF16), y_ref[...])
    x2 = x1_ref[...] + gate_ref[...] * _rms(f, g_ref[...])
    if len(o_refs) == 1:
        o_refs[0][...] = x2
    else:
        def store(o_ref):
            o_ref[...] = x2

        _per_group(store, *o_refs)


def _combine(y_sorted, pw, x1, norm_g, mod4, layer, split):
    row = lambda c: pl.BlockSpec((TM, c), lambda t: (t, 0))
    if split:
        out_specs = [_ctx_spec(D), _lat_spec(D)]
        out_shape = [jax.ShapeDtypeStruct((N_CTX, D), F32), jax.ShapeDtypeStruct((N_LAT, D), F32)]
    else:
        out_specs, out_shape = row(D), jax.ShapeDtypeStruct((N_TOK, D), F32)
    return pl.pallas_call(
        _combine_kernel,
        grid=(N_TILES,),
        in_specs=[pl.BlockSpec((TILE_ROWS, D), lambda t: (t, 0)), row(LANE), row(D),
                  pl.BlockSpec((None, 1, D), lambda t: (3, 0, 0)), _mod_spec(layer, 5)],
        out_specs=out_specs,
        out_shape=out_shape,
        compiler_params=_cparams(("arbitrary",)),
        name="moe_combine",
    )(y_sorted, pw, x1, norm_g.reshape(4, 1, D), mod4)


N_MACRO_PAD = -(-N_MACRO // 8) * 8


def _dot_exact(sel, tab):
    hi, lo = _split_bf16(tab)
    s16 = sel.astype(BF16)
    return _dot_nt(s16, hi) + _dot_nt(s16, lo)


def _tables_kernel(cnt_ref, blk_ref, src_ref, meta_ref):
    cnt = cnt_ref[...]
    nb = jnp.floor((cnt + (SUB_ROWS - 1)) * (1.0 / SUB_ROWS))
    li = lax.broadcasted_iota(jnp.int32, (LANE, LANE), 0)
    lj = lax.broadcasted_iota(jnp.int32, (LANE, LANE), 1)
    nb16 = nb.astype(BF16)
    lstart = _dot(nb16, jnp.where(li < lj, 1.0, 0.0).astype(BF16)) * SUB_ROWS
    ti = lax.broadcasted_iota(jnp.int32, (N_TILES, N_TILES), 0)
    tj = lax.broadcasted_iota(jnp.int32, (N_TILES, N_TILES), 1)
    cum_nb = _dot(jnp.where(tj <= ti, 1.0, 0.0).astype(BF16), nb16)
    nb_e = cum_nb[N_TILES - 1:N_TILES, :]
    mb_e = jnp.floor((nb_e + (SUB - 1)) * (1.0 / SUB))
    mb_hi, mb_lo = _split_bf16(jnp.broadcast_to(mb_e, (8, LANE)))
    upto = jnp.where(li <= lj, 1.0, 0.0).astype(BF16)
    mend = (_dot(mb_hi, upto) + _dot(mb_lo, upto))[0:1, :]
    lane_row = lax.broadcasted_iota(jnp.int32, (1, LANE), 1)
    nact = jnp.sum(jnp.where(lane_row == N_EXP - 1, mend, 0.0), axis=1, keepdims=True)
    mstart = mend - mb_e

    lane = lax.broadcasted_iota(jnp.int32, (N_MACRO_PAD, LANE), 1)
    i_col = lax.broadcasted_iota(jnp.int32, (N_MACRO_PAD, 1), 0).astype(F32)
    real = lane < N_EXP
    e_i = jnp.sum(jnp.where(real & (mend <= i_col), 1.0, 0.0), axis=1, keepdims=True)
    e_last = jnp.sum(jnp.where((lane_row < N_EXP) & (mend <= nact - 1.0), 1.0, 0.0), axis=1, keepdims=True)
    blk = jnp.where(i_col < nact, jnp.minimum(e_i, N_EXP - 1.0), e_last)
    hot = jnp.where(lane.astype(F32) == blk, 1.0, 0.0)
    mstart_i = jnp.sum(hot * mstart, axis=1, keepdims=True)
    nbe_i = jnp.sum(hot * nb_e, axis=1, keepdims=True)
    cum_t = _dot_exact(hot, cum_nb)
    nb_t = _dot_exact(hot, nb)
    lst_t = _dot_exact(hot, lstart)
    tile_lane = lax.broadcasted_iota(jnp.int32, (N_MACRO_PAD, N_TILES), 1).astype(F32)
    parity = i_col - 2.0 * jnp.floor(i_col * 0.5)
    src = jnp.zeros((N_MACRO_PAD, LANE), F32)
    for s in range(SUB):
        q = (i_col - mstart_i) * SUB + s
        valid = (i_col < nact) & (q < nbe_i)
        le = cum_t <= q
        j = jnp.minimum(jnp.sum(jnp.where(le, 1.0, 0.0), axis=1, keepdims=True), N_TILES - 1.0)
        prev = jnp.sum(jnp.where(le, nb_t, 0.0), axis=1, keepdims=True)
        lst = jnp.sum(jnp.where(tile_lane == j, lst_t, 0.0), axis=1, keepdims=True)
        row = j * TILE_ROWS + lst + (q - prev) * SUB_ROWS
        spare = N_TILES * TILE_ROWS + (parity * SUB + s) * SUB_ROWS
        src = jnp.where(lane == s, jnp.where(valid, row, spare), src)
    src_ref[...] = src.astype(jnp.int32)
    used = real & (mb_e > 0.0)
    lane_f = lane.astype(F32)
    rank = jnp.sum(jnp.where(used & (lane_f < blk), 1.0, 0.0), axis=1, keepdims=True)
    nxt = jnp.min(jnp.where(used & (lane_f > blk), lane_f, float(N_EXP)), axis=1, keepdims=True)
    next_code = 2.0 * nxt + (rank - 2.0 * jnp.floor(rank * 0.5))
    blk_ref[...] = jnp.where(lane == 0, blk, jnp.where(lane == 1, next_code, 0.0)).astype(jnp.int32)
    tile_col = lax.broadcasted_iota(jnp.int32, (N_TILES, 1), 0).astype(F32)
    tail = tile_col * TILE_ROWS + jnp.sum(nb, axis=1, keepdims=True) * SUB_ROWS
    lane_t = lax.broadcasted_iota(jnp.int32, (N_TILES, LANE), 1)
    meta_ref[...] = jnp.where(lane_t == 0, tail, jnp.where(lane_t == 1, nact, 0.0)).astype(jnp.int32)


def _block_tables(cnt):
    i32 = lambda r: jax.ShapeDtypeStruct((r, LANE), jnp.int32)
    blk, src, meta = pl.pallas_call(
        _tables_kernel,
        out_shape=[i32(N_MACRO_PAD), i32(N_MACRO_PAD), i32(N_TILES)],
        compiler_params=_cparams(None),
        name="moe_tables",
    )(cnt.reshape(N_TILES, LANE))
    return blk[:N_MACRO, 0], blk[:N_MACRO, 1], src[:N_MACRO, :SUB].reshape(-1), meta[0:1, 1].reshape(1), meta[:, 0]


def _moe(mixes, w_outs, x, norm_g, mod4, layer, w_router, b_router, w_gu, b_gu, w_dn, b_dn, split=False):
    wr = jnp.pad(w_router, ((0, 0), (0, LANE - N_EXP)))
    br = jnp.pad(b_router, (0, LANE - N_EXP)).reshape(1, LANE)
    x1, hx, pw, pwt, cnt = _route(mixes, w_outs, x, norm_g, mod4, layer, wr, br)
    blk_e, next_e, src, nact, tail = _block_tables(cnt)
    s_sorted = _dispatch(hx, pwt)
    y_sorted = _experts(blk_e, next_e, src, nact, tail, s_sorted, layer, w_gu, b_gu, w_dn, b_dn)
    return _combine(y_sorted, pw, x1, norm_g, mod4, layer, split)


def kernel(x_prompt, x_sample, c, c_ctx, state_delta, cache_diff_k, cache_diff_v, cache_mla_ckv, cache_mla_kpe, w_mod, b_mod, norm_g, w_router, b_router, w_gate_up, b_gate_up, w_down, b_down, w_in_even, conv_w, a_log, dt_bias, gdn_norm_g, diff_lambda, diff_norm_g, w_out_even, w_in_odd, q_norm_g, w_uq, kv_norm_g, w_ukv, w_out_odd):
    x = (x_prompt.reshape(N_CTX, D), x_sample.reshape(N_LAT, D))
    cond =jnp.concatenate([c_ctx[None, :], c, jnp.zeros((16 - 1 - B_LAT, D), F32)], axis=0)
    mod4 = _modulation(cond, w_mod, b_mod).reshape(DEPTH, 16, 1, N_MOD * D)
    cos64, sin64 = _rope_tables()
    rope2 = (jnp.tile(cos64, (1, 2)), jnp.tile(sin64, (1, 2)))
    rope1 = (jnp.pad(cos64, ((0, 0), (0, 64))), jnp.pad(sin64, ((0, 0), (0, 64))))

    we = w_in_even[0]
    w_cat = jnp.concatenate([we[:, 0:2048], we[:, 2064:3600], we[:, 2048:2064],
                             jnp.zeros((D, LANE - 16), F32)], axis=1).astype(BF16)
    qkv, z, qd, kd, vd, ab = _in_even(x, norm_g[0, 0:1], mod4, 0, w_cat)
    gn = gdn_norm_g[0].reshape(1, GDN_DV)
    oa_c, new_state = _gdn(qkv, z, ab, conv_w[0], a_log[0], dt_bias[0], gn, None, B_CTX, T_CTX, 0)
    oa_l = _gdn(qkv, z, ab, conv_w[0], a_log[0], dt_bias[0], gn, state_delta[:, 0], B_LAT, T_LAT, N_CTX)
    lam_init = 0.8 - 0.6 * math.exp(-0.3 * 0)
    dgn = diff_norm_g[0].reshape(1, 2 * DIFF_DH)
    od_c = _diff_attn(qd, kd, vd, diff_lambda[0], dgn, lam_init, B_CTX, T_CTX, 0)
    cache_dk = cache_diff_k[:, 0].reshape(B_LAT, PAST, DIFF_H * 2 * DIFF_DH)
    cache_dv = cache_diff_v[:, 0].reshape(B_LAT, PAST, DIFF_H * 2 * DIFF_DH)
    od_l = _diff_attn(qd, kd, vd, diff_lambda[0], dgn, lam_init, B_LAT, T_LAT, N_CTX,
                      cache=(cache_dk, cache_dv), rope=rope2)
    wo = w_out_even[0].astype(BF16)
    x = _moe([(oa_c, oa_l), (od_c, od_l)], [wo[:512], wo[512:]], x, norm_g[0], mod4, 0, w_router[0], b_router[0],
             w_gate_up, b_gate_up, w_down, b_down)
    new_dk = kd[:N_CTX].reshape(B_CTX, 1, T_CTX, DIFF_H, 2, DIFF_DH)
    new_dv = vd[:N_CTX].reshape(B_CTX, 1, T_CTX, DIFF_H, 2 * DIFF_DH)
    new_state = new_state.reshape(B_CTX, 1, 2, GDN_H, GDN_DK, GDN_DV)

    w_odd = jnp.pad(w_in_odd[0], ((0, 0), (0, ODD_W - (Q_LORA + KV_LORA + QK_ROPE)))).astype(BF16)
    wuq = w_uq[0].reshape(Q_LORA, MLA_H, QK_NOPE + QK_ROPE)
    wuq = jnp.pad(wuq, ((0, 0), (0, 0), (0, MLA_QW - (QK_NOPE + QK_ROPE)))).reshape(Q_LORA, MLA_H * MLA_QW)
    wukv = w_ukv[0].astype(BF16)
    q, kv, ckv, kpe = _in_odd(x, norm_g[1, 0:1], mod4, 1, w_odd, q_norm_g[0].reshape(1, Q_LORA),
                              wuq.astype(BF16), kv_norm_g[0].reshape(1, KV_LORA), wukv)
    om_c = _mla_attn(q, kv, kpe, B_CTX, T_CTX, 0)
    cache_kv = _matmul(cache_mla_ckv[:, 0].reshape(B_LAT * PAST, KV_LORA), wukv)
    cache_kpe = jnp.pad(cache_mla_kpe[:, 0].reshape(B_LAT * PAST, QK_ROPE), ((0, 0), (0, LANE - QK_ROPE)))
    om_l = _mla_attn(q, kv, kpe, B_LAT, T_LAT, N_CTX, cache=(cache_kv, cache_kpe), rope=rope1)
    x_ctx, x_lat = _moe([(om_c, om_l)], [w_out_odd[0].astype(BF16)], x, norm_g[1], mod4, 1, w_router[1],
                        b_router[1], w_gate_up, b_gate_up, w_down, b_down, split=True)
    new_ckv = ckv[:N_CTX].reshape(B_CTX, 1, T_CTX, KV_LORA)
    new_kpe = kpe[:N_CTX, :QK_ROPE].reshape(B_CTX, 1, T_CTX, QK_ROPE)

    return (x_ctx.reshape(B_CTX, T_CTX, D), x_lat.reshape(B_LAT, T_LAT, D),
            new_state, new_dk, new_dv, new_ckv, new_kpe)
```

```python
import functools
import math

import jax
import jax.numpy as jnp
from jax import lax
from jax.experimental import pallas as pl
from jax.experimental.pallas import tpu as pltpu

F32 = jnp.float32
BF16 = jnp.bfloat16
HIGHEST = lax.Precision.HIGHEST

D = 1024
B_CTX, T_CTX = 16, 256
B_LAT, T_LAT = 8, 1024
PAST = 256
N_CTX = B_CTX * T_CTX
N_LAT = B_LAT * T_LAT
N_TOK = N_CTX + N_LAT
DEPTH = 2
N_MOD = 6
GRID_W = 64
GDN_H, GDN_DK, GDN_DV = 4, 128, 128
CONV_W = 5
CHUNK = 64
DIFF_H, DIFF_DH = 4, 64
MLA_H, Q_LORA, KV_LORA, QK_NOPE, QK_ROPE, V_HEAD = 8, 256, 256, 128, 64, 128
N_EXP, TOP_K, D_FF = 32, 4, 1024
SWIGLU_LIMIT, SWIGLU_ALPHA = 7.0, 1.702
ROPE_BASE = 10000.0
EPS = 1e-6

LANE = 128
TM = 256
N_TILES = N_TOK // TM
SUB_ROWS = 16
TILE_ROWS = TOP_K * TM + N_EXP * SUB_ROWS
MB = 256
SUB = MB // SUB_ROWS
FF_CHUNK = 256
GDN_TRIP_ROWS = 256
N_MACRO = -(-((TOP_K * N_TOK) // SUB_ROWS + N_TILES * N_EXP + N_EXP * (SUB - 1)) // SUB)
D_EXT = D + LANE
VMEM_LIMIT = 48 * 1024 * 1024


def _cparams(sem, vmem=VMEM_LIMIT):
    return pltpu.CompilerParams(dimension_semantics=sem, vmem_limit_bytes=vmem)


N_CTX_TILES = N_CTX // TM


def _mod_row(t):
    return jnp.where(t < N_CTX_TILES, 0, 1 + (t - N_CTX_TILES) // (T_LAT // TM))


def _ctx_spec(c):
    return pl.BlockSpec((TM, c), lambda t: (jnp.minimum(t, N_CTX_TILES - 1), 0))


def _lat_spec(c, first=0):
    return pl.BlockSpec((TM, c), lambda t: (jnp.maximum(t - N_CTX_TILES, 0) + first, 0))


def _token_pair(x):
    return (x[0], x[1], 0) if isinstance(x, tuple) else (x, x, N_CTX_TILES)


def _per_group(body, ctx_ref, lat_ref):
    t = pl.program_id(0)
    pl.when(t < N_CTX_TILES)(lambda: body(ctx_ref))
    pl.when(t >= N_CTX_TILES)(lambda: body(lat_ref))


def _rms(x, g):
    return x * lax.rsqrt(jnp.mean(x * x, axis=-1, keepdims=True) + EPS) * g


def _silu(x):
    return x * jax.nn.sigmoid(x)


def _dot(a, b):
    return jnp.dot(a, b, preferred_element_type=F32)


def _dot_nt(a, b):
    return lax.dot_general(a, b, (((1,), (1,)), ((), ())), preferred_element_type=F32)


def _mod_kernel(c_ref, w_ref, b_ref, o_ref):
    s = _silu(c_ref[...])
    o_ref[...] = _dot(s.astype(BF16), w_ref[...].astype(BF16)) + b_ref[...]


def _modulation(cond, w_mod, b_mod):
    nc = 1536
    return pl.pallas_call(
        _mod_kernel,
        grid=(DEPTH, N_MOD * D // nc),
        in_specs=[pl.BlockSpec((16, D), lambda i, j: (0, 0)),
                  pl.BlockSpec((None, D, nc), lambda i, j: (i, 0, j)),
                  pl.BlockSpec((None, 1, nc), lambda i, j: (i, 0, j))],
        out_specs=pl.BlockSpec((None, 16, nc), lambda i, j: (i, 0, j)),
        out_shape=jax.ShapeDtypeStruct((DEPTH, 16, N_MOD * D), F32),
        compiler_params=_cparams(("arbitrary", "arbitrary")),
        name="modulation",
    )(cond, w_mod, b_mod.reshape(DEPTH, 1, N_MOD * D))


def _mod_spec(layer, k):
    return pl.BlockSpec((None, None, 1, D), lambda t: (layer, _mod_row(t), 0, k))


EVEN_W = 2 * 512 + 512 + 512 + 3 * 512 + LANE


def _in_even_kernel(xc_ref, xl_ref, g_ref, sh_ref, sc_ref, w_ref, qkv_ref, z_ref, qd_ref, kd_ref, vd_ref, ab_ref):
    def body(x_ref):
        h = _rms(x_ref[...], g_ref[...]) * (1.0 + sc_ref[...]) + sh_ref[...]
        y = _dot(h.astype(BF16), w_ref[...])
        qkv_ref[...] = y[:, 0:1536]
        z_ref[...] = y[:, 1536:2048]
        qd_ref[...] = y[:, 2048:2560]
        kd_ref[...] = y[:, 2560:3072]
        vd_ref[...] = y[:, 3072:3584]
        ab_ref[...] = y[:, 3584:3712]

    _per_group(body, xc_ref, xl_ref)


def _in_even(x, g, mod4, layer, w):
    xc, xl, first = _token_pair(x)
    row = lambda c: pl.BlockSpec((TM, c), lambda t: (t, 0))
    shp = lambda c: jax.ShapeDtypeStruct((N_TOK, c), F32)
    return pl.pallas_call(
        _in_even_kernel,
        grid=(N_TILES,),
        in_specs=[_ctx_spec(D), _lat_spec(D, first), pl.BlockSpec((1, D), lambda t: (0, 0)), _mod_spec(layer, 0),
                  _mod_spec(layer, 1), pl.BlockSpec((D, EVEN_W), lambda t: (0, 0))],
        out_specs=[row(1536), row(512), row(512), row(512), row(512), row(LANE)],
        out_shape=[shp(1536), shp(512), shp(512), shp(512), shp(512), shp(LANE)],
        compiler_params=_cparams(("arbitrary",)),
        name="in_proj_even",
    )(xc, xl, g, mod4, mod4, w)


def _split_bf16(a):
    hi = a.astype(BF16)
    lo = (a - hi.astype(F32)).astype(BF16)
    return hi, lo


def _mm3(a, b):
    m = a[0].shape[0]
    top = _dot(jnp.concatenate([a[0], a[1]], axis=0), b[0])
    return top[:m] + (top[m:] + _dot(a[0], b[1]))


def _unit_tri_inverses(ls, eye, blk, bd):
    sp = _split_bf16
    d8 = [jnp.where(blk(8), l, 0.0) for l in ls]
    d8s = [sp(d) for d in d8]
    x2s = [sp(_mm3(d, bd(d))) for d in d8s]
    x2r = [bd(x) for x in x2s]
    ts = [eye - d for d in d8]
    ts = [t + _mm3(sp(t), xr) for t, xr in zip(ts, x2r)]
    x4 = [_mm3(x, xr) for x, xr in zip(x2s, x2r)]
    ts = [t + _mm3(sp(t), bd(sp(x))) for t, x in zip(ts, x4)]
    for s in (8, 16, 32):
        offs = [sp(jnp.where(blk(2 * s) & jnp.logical_not(blk(s)), l, 0.0)) for l in ls]
        tss = [sp(t) for t in ts]
        ys = [_mm3(o, bd(t)) for o, t in zip(offs, tss)]
        ts = [t - _mm3(t2, bd(sp(y))) for t, t2, y in zip(ts, tss, ys)]
    return ts


def _softplus(x):
    return jnp.maximum(x, 0.0) + jnp.log1p(jnp.exp(-jnp.abs(x)))


def _gdn_kernel(has_init, T, alog_ref, dtb_ref, qkv_ref, cw_ref, z_ref, ab_ref, gn_ref, *rest):
    if has_init:
        s0_ref, o_ref = rest[:2]
        st_ref = None
    else:
        o_ref, st_ref = rest[:2]
        s0_ref = None
    xp, qs, ks, vs, abt_s, uo_s, w_s, qg_s, kgt_s, in_s, gl_s, st_s = rest[2:]
    n = T // CHUNK
    H = GDN_H

    xp[0:8, :] = jnp.zeros((8, LANE), F32)
    xp[T + 8:T + 16, :] = jnp.zeros((8, LANE), F32)

    def l2n(x):
        return x * lax.rsqrt(jnp.sum(x * x, axis=-1, keepdims=True) + EPS)

    for blk in range(3 * H):
        cols = slice(blk * LANE, (blk + 1) * LANE)
        dst = slice((blk % H) * LANE, (blk % H + 1) * LANE)
        xp[8:T + 8, :] = qkv_ref[:, cols]
        w = cw_ref[:, cols]
        acc = xp[6:T + 6, :] * w[0:1, :]
        for i in range(1, CONV_W):
            acc = acc + xp[6 + i:T + 6 + i, :] * w[i:i + 1, :]
        y = _silu(acc)
        if blk < H:
            qs[:, dst] = l2n(y) * (GDN_DK ** -0.5)
        elif blk < 2 * H:
            ks[:, dst] = l2n(y)
        else:
            vs[:, dst] = y
    abt_s[...] = ab_ref[...].T[0:16, :]

    ii = lax.broadcasted_iota(jnp.int32, (CHUNK, LANE), 0)
    lane = lax.broadcasted_iota(jnp.int32, (CHUNK, LANE), 1)
    jm = lane & (CHUNK - 1)
    fwd = lane < CHUNK
    order = jnp.where(fwd, ii - jm, jm - ii)
    incl = order >= 0
    strict = order > 0
    incl_t = order <= 0
    eye = jnp.where(order == 0, 1.0, 0.0).astype(F32)
    blk_mask = lambda s: (ii // s) == (jm // s)
    zeros_c = jnp.zeros((CHUNK, LANE), F32)
    zeros_c16 = jnp.zeros((CHUNK, LANE), BF16)

    keep_f = jnp.where(fwd, 1.0, 0.0).astype(BF16)
    keep_b = jnp.where(fwd, 0.0, 1.0).astype(BF16)

    def bd(parts):
        return tuple(jnp.concatenate([a * keep_f, a * keep_b], axis=0) for a in parts)

    lane_row = lax.broadcasted_iota(jnp.int32, (1, LANE), 1)
    sub_col = lax.broadcasted_iota(jnp.int32, (16, 1), 0)
    a_lane, dt_lane = jnp.zeros((1, LANE), F32), jnp.zeros((1, LANE), F32)
    a_col, dt_col = jnp.zeros((16, 1), F32), jnp.zeros((16, 1), F32)
    for d in range(2):
        for h in range(H):
            a_lane = jnp.where(lane_row == d * H + h, alog_ref[d, h], a_lane)
            dt_lane = jnp.where(lane_row == d * H + h, dtb_ref[d, h], dt_lane)
            a_col = jnp.where(sub_col == d * H + h, alog_ref[d, h], a_col)
            dt_col = jnp.where(sub_col == d * H + h, dtb_ref[d, h], dt_col)
    nega_lane, nega_col = -jnp.exp(a_lane), -jnp.exp(a_col)

    def prep(p, carry):
        rbase = pl.multiple_of(p * GDN_TRIP_ROWS, GDN_TRIP_ROWS)
        g_rows = nega_col * _softplus(abt_s[:, pl.ds(rbase, GDN_TRIP_ROWS)] + dt_col)
        sysm = []
        for cc in range(GDN_TRIP_ROWS // CHUNK):
            r0 = pl.multiple_of(rbase + cc * CHUNK, CHUNK)
            abc = ab_ref[pl.ds(r0, CHUNK), :]
            g_cols = nega_lane * _softplus(abc + dt_lane)
            b_cols = jax.nn.sigmoid(abc)
            for h in range(H):
                hs = slice(h * LANE, (h + 1) * LANE)
                tok = slice(cc * CHUNK, (cc + 1) * CHUNK)
                sysm.append(dict(
                    c=(GDN_TRIP_ROWS // CHUNK) * p + cc, rows=pl.ds(r0, CHUNK), h=h,
                    qc=qs[pl.ds(r0, CHUNK), hs], kc=ks[pl.ds(r0, CHUNK), hs], vc=vs[pl.ds(r0, CHUNK), hs],
                    g_col=jnp.where(fwd, g_cols[:, h:h + 1], g_cols[:, H + h:H + h + 1]),
                    g_row=jnp.concatenate([g_rows[h:h + 1, tok], g_rows[H + h:H + h + 1, tok]], axis=1),
                    beta=(b_cols[:, 2 * H + h:2 * H + h + 1], b_cols[:, 3 * H + h:3 * H + h + 1])))
        for s in sysm:
            k16 = s["kc"].astype(BF16)
            kk16 = jnp.concatenate([k16, k16], axis=0)
            kq = _dot_nt(jnp.concatenate([k16, s["qc"].astype(BF16)], axis=0), kk16)
            s["kk"], s["qk"] = kq[:CHUNK], kq[CHUNK:]
        ls = []
        for s in sysm:
            m = jnp.where(incl, s["g_row"], 0.0)
            s["gcf"] = jnp.sum(jnp.where(fwd, m, 0.0), axis=1, keepdims=True)
            s["gcb"] = jnp.sum(jnp.where(fwd, 0.0, m), axis=1, keepdims=True)
            gc_col = jnp.where(fwd, s["gcf"], s["gcb"])
            gc_row = jnp.sum(jnp.where(incl_t, s["g_col"], 0.0), axis=0, keepdims=True)
            decay = jnp.where(incl, jnp.exp(jnp.where(incl, gc_col - gc_row, 0.0)), 0.0)
            beta = jnp.where(fwd, s["beta"][0], s["beta"][1])
            ls.append(jnp.where(strict, s["kk"] * beta * decay, 0.0))
            s["intra"] = s["qk"] * decay
        ts = _unit_tri_inverses(ls, eye, blk_mask, bd)
        for s, t in zip(sysm, ts):
            h, c, rows = s["h"], s["c"], s["rows"]
            egs = (jnp.exp(s["gcf"]), jnp.exp(s["gcb"]))
            r_f = jnp.concatenate([s["vc"] * s["beta"][0], s["kc"] * (s["beta"][0] * egs[0]), zeros_c, zeros_c], axis=1)
            r_b = jnp.concatenate([zeros_c, zeros_c, s["vc"] * s["beta"][1], s["kc"] * (s["beta"][1] * egs[1])], axis=1)
            uw = _dot(t.astype(BF16), jnp.concatenate([r_f, r_b], axis=0).astype(BF16))
            g_last = (s["gcf"][CHUNK - 1:CHUNK, :], s["gcb"][0:1, :])
            kgs = []
            for d in range(2):
                k = d * H + h
                uo_s[k, rows, :] = uw[:, 2 * d * LANE:(2 * d + 1) * LANE]
                w_s[k, rows, :] = uw[:, (2 * d + 1) * LANE:(2 * d + 2) * LANE].astype(BF16)
                qg_s[k, rows, :] = (s["qc"] * egs[d]).astype(BF16)
                gc = s["gcf"] if d == 0 else s["gcb"]
                kgs.append(s["kc"] * jnp.exp(g_last[d] - gc))
                gl_s[k, c] = jnp.broadcast_to(jnp.exp(g_last[d]), (8, LANE))
            in_s[h, c] = s["intra"].astype(BF16)
            kgt_s[h, c] = jnp.concatenate(kgs, axis=0).T.astype(BF16)
        return carry

    lax.fori_loop(0, T // GDN_TRIP_ROWS, prep, 0)

    for d in range(2):
        for h in range(H):
            st_s[d * H + h] = s0_ref[d, h] if has_init else jnp.zeros((GDN_DK, GDN_DV), F32)

    def scan(i, carry):
        cs = (i, n - 1 - i)
        ks_ = range(2 * H)
        c_of = [cs[k // H] for k in ks_]
        rows = [pl.ds(pl.multiple_of(c * CHUNK, CHUNK), CHUNK) for c in c_of]
        s16 = [st_s[k].astype(BF16) for k in ks_]
        wq = [_dot(jnp.concatenate([w_s[k, rows[k], :], qg_s[k, rows[k], :]], axis=0), s16[k]) for k in ks_]
        vn = [(uo_s[k, rows[k], :] - wq[k][:CHUNK]).astype(BF16) for k in ks_]
        vpad = [jnp.concatenate([vn[k], zeros_c16] if k < H else [zeros_c16, vn[k]], axis=0) for k in ks_]
        ik = [_dot(jnp.concatenate([in_s[k % H, c_of[k]], kgt_s[k % H, c_of[k]]], axis=0), vpad[k]) for k in ks_]
        for k in ks_:
            uo_s[k, rows[k], :] = wq[k][CHUNK:] + ik[k][:CHUNK]
            st_s[k] = st_s[k] * gl_s[k, c_of[k]][0:1, :] + ik[k][CHUNK:]
        return carry

    lax.fori_loop(0, n, scan, 0)
    for h in range(H):
        hs = slice(h * LANE, (h + 1) * LANE)
        o_ref[:, hs] = _rms(uo_s[h] + uo_s[H + h], gn_ref[...]) * _silu(z_ref[:, hs])
        if st_ref is not None:
            st_ref[0, h] = st_s[h]
            st_ref[1, h] = st_s[H + h]


def _gdn(qkv, z, ab, conv_w, a_log, dt_bias, gn, s0, nb, T, row0):
    n = T // CHUNK
    blk0 = row0 // T
    has_init = s0 is not None
    H, W = GDN_H, GDN_H * LANE
    smem = pl.BlockSpec(memory_space=pltpu.SMEM)
    in_specs = [smem, smem,
                pl.BlockSpec((T, 3 * W), lambda b: (blk0 + b, 0), pipeline_mode=pl.Buffered(1)),
                pl.BlockSpec((CONV_W, 3 * W), lambda b: (0, 0)),
                pl.BlockSpec((T, W), lambda b: (blk0 + b, 0), pipeline_mode=pl.Buffered(1)),
                pl.BlockSpec((T, LANE), lambda b: (blk0 + b, 0)),
                pl.BlockSpec((1, LANE), lambda b: (0, 0))]
    args = [a_log, dt_bias, qkv, conv_w, z, ab, gn]
    st_spec = pl.BlockSpec((None, 2, H, GDN_DK, GDN_DV), lambda b: (b, 0, 0, 0, 0))
    o_spec = pl.BlockSpec((T, W), lambda b: (b, 0))
    o_shape = jax.ShapeDtypeStruct((nb * T, W), F32)
    if has_init:
        in_specs.append(st_spec)
        args.append(s0)
        out_specs, out_shape = o_spec, o_shape
    else:
        out_specs = [o_spec, st_spec]
        out_shape = [o_shape, jax.ShapeDtypeStruct((nb, 2, H, GDN_DK, GDN_DV), F32)]
    scratch = [pltpu.VMEM((T + 16, LANE), F32), pltpu.VMEM((T, W), F32), pltpu.VMEM((T, W), F32),
               pltpu.VMEM((T, W), F32), pltpu.VMEM((16, T), F32),
               pltpu.VMEM((2 * H, T, LANE), F32), pltpu.VMEM((2 * H, T, LANE), BF16),
               pltpu.VMEM((2 * H, T, LANE), BF16), pltpu.VMEM((H, n, LANE, LANE), BF16),
               pltpu.VMEM((H, n, CHUNK, LANE), BF16), pltpu.VMEM((2 * H, n, 8, LANE), F32),
               pltpu.VMEM((2 * H, GDN_DK, GDN_DV), F32)]
    return pl.pallas_call(
        functools.partial(_gdn_kernel, has_init, T),
        grid=(nb,),
        in_specs=in_specs, out_specs=out_specs, out_shape=out_shape, scratch_shapes=scratch,
        compiler_params=_cparams(("arbitrary",), 56 * 1024 * 1024),
        name="gdn_lat" if has_init else "gdn_ctx",
    )(*args)


def _rope(x, cos, sin_signed):
    lane = lax.broadcasted_iota(jnp.int32, x.shape, 1)
    up = pltpu.roll(x, LANE - 16, axis=1)
    dn = pltpu.roll(x, 16, axis=1)
    rot = jnp.where((lane % 32) < 16, up, dn)
    return x * cos + rot * sin_signed


def _rope_tables():
    rows = T_LAT // GRID_W
    r = jnp.repeat(jnp.arange(rows, dtype=F32), GRID_W)
    col = jnp.tile(jnp.arange(GRID_W, dtype=F32), rows)
    quarter = 16
    inv = ROPE_BASE ** (-jnp.arange(quarter, dtype=F32) / quarter)
    ar = r[:, None] * inv
    ac = col[:, None] * inv
    ang = jnp.concatenate([ar, ar, ac, ac], axis=-1)
    sign = jnp.tile(jnp.concatenate([-jnp.ones(16, F32), jnp.ones(16, F32)]), 2)
    cos, sin = jnp.cos(ang), jnp.sin(ang) * sign
    return cos, sin


LOG2E = math.log2(math.e)


def _attend(parts, values):
    m = functools.reduce(jnp.maximum, [jnp.max(s, axis=-1, keepdims=True) for s in parts])
    es = [jnp.exp2(s - m) for s in parts]
    den = functools.reduce(lambda a, b: a + b, [jnp.sum(e, axis=-1, keepdims=True) for e in es])
    o = functools.reduce(lambda a, b: a + b, [_dot(e.astype(BF16), v) for e, v in zip(es, values)])
    return o * (1.0 / den)


def _diff_kernel(latent, T, lam_init, q_ref, k_ref, v_ref, lam_ref, gn_ref, *rest):
    if latent:
        cos_ref, sin_ref, ck_ref, cv_ref, o_ref = rest
    else:
        (o_ref,) = rest
    scale = DIFF_DH ** -0.5
    lp = lam_ref[...]
    lam = (jnp.exp(jnp.sum(lp[0:1] * lp[1:2], axis=-1, keepdims=True))
           - jnp.exp(jnp.sum(lp[2:3] * lp[3:4], axis=-1, keepdims=True)) + lam_init)
    k = k_ref[...]
    if latent:
        k = _rope(k, cos_ref[...], sin_ref[...])
    k16 = k.astype(BF16)
    v16 = v_ref[...].astype(BF16)
    if latent:
        ck16 = ck_ref[...].astype(BF16)
        cv16 = cv_ref[...].astype(BF16)
    tq = 256
    lane = lax.broadcasted_iota(jnp.int32, (tq, LANE), 1)
    for qb in range(T // tq):
        q = q_ref[qb * tq:(qb + 1) * tq, :]
        if latent:
            q = _rope(q, cos_ref[qb * tq:(qb + 1) * tq, :], sin_ref[qb * tq:(qb + 1) * tq, :])
        qs = q * (scale * LOG2E)
        q2 = jnp.concatenate([jnp.where((lane // DIFF_DH) == m, qs, 0.0) for m in range(2)], axis=0).astype(BF16)
        parts = [_dot_nt(q2, k16)]
        if latent:
            parts.insert(0, _dot_nt(q2, ck16))
        o2 = _attend(parts, [cv16, v16] if latent else [v16])
        o = o2[:tq] - lam * o2[tq:]
        o_ref[qb * tq:(qb + 1) * tq, :] = _rms(o, gn_ref[...]) * (1.0 - lam_init)


def _diff_attn(qd, kd, vd, lam_p, gn, lam_init, nb, T, row0, cache=None, rope=None):
    blk0 = row0 // T
    latent = cache is not None
    hcol = pl.BlockSpec((T, LANE), lambda b, h: (blk0 + b, h))
    in_specs = [hcol, hcol, hcol, pl.BlockSpec((4, DIFF_DH), lambda b, h: (0, 0)),
                pl.BlockSpec((1, LANE), lambda b, h: (0, 0))]
    args = [qd, kd, vd, lam_p, gn]
    if latent:
        tab = pl.BlockSpec((T, LANE), lambda b, h: (0, 0))
        cch = pl.BlockSpec((None, PAST, LANE), lambda b, h: (b, 0, h))
        in_specs += [tab, tab, cch, cch]
        args += [rope[0], rope[1], cache[0], cache[1]]
    return pl.pallas_call(
        functools.partial(_diff_kernel, latent, T, lam_init),
        grid=(nb, DIFF_H),
        in_specs=in_specs,
        out_specs=pl.BlockSpec((T, LANE), lambda b, h: (b, h)),
        out_shape=jax.ShapeDtypeStruct((nb * T, DIFF_H * 2 * DIFF_DH), F32),
        compiler_params=_cparams(("arbitrary", "arbitrary")),
        name="diff_lat" if latent else "diff_ctx",
    )(*args)


ODD_W = Q_LORA + KV_LORA + LANE
MLA_QW = 2 * LANE


def _in_odd_kernel(x_ref, g_ref, sh_ref, sc_ref, w_ref, qg_ref, wuq_ref, kvg_ref, wukv_ref,
                   q_ref, kv_ref, ckv_ref, kpe_ref):
    h = _rms(x_ref[...], g_ref[...]) * (1.0 + sc_ref[...]) + sh_ref[...]
    y = _dot(h.astype(BF16), w_ref[...])
    qn = _rms(y[:, 0:Q_LORA], qg_ref[...])
    q_ref[...] = _dot(qn.astype(BF16), wuq_ref[...])
    ckv = _rms(y[:, Q_LORA:Q_LORA + KV_LORA], kvg_ref[...])
    ckv_ref[...] = ckv
    kv_ref[...] = _dot(ckv.astype(BF16), wukv_ref[...])
    kpe_ref[...] = y[:, Q_LORA + KV_LORA:ODD_W]


def _in_odd(x, g, mod4, layer, w, qg, wuq, kvg, wukv):
    row = lambda c: pl.BlockSpec((TM, c), lambda t: (t, 0))
    full = lambda a: pl.BlockSpec(a.shape, lambda t: (0, 0))
    shp = lambda c: jax.ShapeDtypeStruct((N_TOK, c), F32)
    return pl.pallas_call(
        _in_odd_kernel,
        grid=(N_TILES,),
        in_specs=[row(D), full(g), _mod_spec(layer, 0), _mod_spec(layer, 1), full(w), full(qg), full(wuq),
                  full(kvg), full(wukv)],
        out_specs=[row(MLA_H * MLA_QW), row(MLA_H * MLA_QW), row(KV_LORA), row(LANE)],
        out_shape=[shp(MLA_H * MLA_QW), shp(MLA_H * MLA_QW), shp(KV_LORA), shp(LANE)],
        compiler_params=_cparams(("arbitrary",)),
        name="in_proj_odd",
    )(x, g, mod4, mod4, w, qg, wuq, kvg, wukv)


def _mm_kernel(x_ref, w_ref, o_ref):
    o_ref[...] = _dot(x_ref[...].astype(BF16), w_ref[...])


def _matmul(x, w):
    m, k = x.shape
    n = w.shape[1]
    return pl.pallas_call(
        _mm_kernel,
        grid=(m // TM,),
        in_specs=[pl.BlockSpec((TM, k), lambda i: (i, 0)), pl.BlockSpec((k, n), lambda i: (0, 0))],
        out_specs=pl.BlockSpec((TM, n), lambda i: (i, 0)),
        out_shape=jax.ShapeDtypeStruct((m, n), F32),
        compiler_params=_cparams(("arbitrary",)),
        name="cache_kv_proj",
    )(x, w)


def _mla_kernel(latent, T, q_ref, kv_ref, kpe_ref, *rest):
    if latent:
        cos_ref, sin_ref, ckv_ref, ckpe_ref, o_ref = rest
    else:
        (o_ref,) = rest
    scale = (QK_NOPE + QK_ROPE) ** -0.5
    kv = kv_ref[...]
    v16 = kv[:, LANE:].astype(BF16)
    kpe = kpe_ref[...]
    if latent:
        kpe = _rope(kpe, cos_ref[...], sin_ref[...])
        ckv = ckv_ref[...]
        ck16 = jnp.concatenate([ckv[:, :LANE], ckpe_ref[...]], axis=1).astype(BF16)
        cv16 = ckv[:, LANE:].astype(BF16)
    k16 = jnp.concatenate([kv[:, :LANE], kpe], axis=1).astype(BF16)
    tq = 256
    for qb in range(T // tq):
        rows = slice(qb * tq, (qb + 1) * tq)
        q = q_ref[rows, :]
        qp = q[:, LANE:]
        if latent:
            qp = _rope(qp, cos_ref[rows, :], sin_ref[rows, :])
        q16 = (jnp.concatenate([q[:, :LANE], qp], axis=1) * (scale * LOG2E)).astype(BF16)
        parts = [_dot_nt(q16, k16)]
        if latent:
            parts.insert(0, _dot_nt(q16, ck16))
        o_ref[rows, :] = _attend(parts, [cv16, v16] if latent else [v16])


def _mla_attn(q, kv, kpe, nb, T, row0, cache=None, rope=None):
    blk0 = row0 // T
    latent = cache is not None
    head = pl.BlockSpec((T, MLA_QW), lambda b, h: (blk0 + b, h))
    in_specs = [head, head, pl.BlockSpec((T, LANE), lambda b, h: (blk0 + b, 0))]
    args = [q, kv, kpe]
    if latent:
        tab = pl.BlockSpec((T, LANE), lambda b, h: (0, 0))
        in_specs += [tab, tab, pl.BlockSpec((PAST, MLA_QW), lambda b, h: (b, h)),
                     pl.BlockSpec((PAST, LANE), lambda b, h: (b, 0))]
        args += [rope[0], rope[1], cache[0], cache[1]]
    return pl.pallas_call(
        functools.partial(_mla_kernel, latent, T),
        grid=(nb, MLA_H),
        in_specs=in_specs,
        out_specs=pl.BlockSpec((T, V_HEAD), lambda b, h: (b, h)),
        out_shape=jax.ShapeDtypeStruct((nb * T, MLA_H * V_HEAD), F32),
        compiler_params=_cparams(("arbitrary", "arbitrary")),
        name="mla_lat" if latent else "mla_ctx",
    )(*args)


def _route_kernel(n_mix, *refs):
    ctx_refs = refs[:n_mix]
    lat_refs = refs[n_mix:2 * n_mix]
    w_refs = refs[2 * n_mix:3 * n_mix]
    (xc_ref, xl_ref, g1_ref, gate1_ref, g2_ref, sh2_ref, sc2_ref, wrh_ref, wrl_ref, br_ref,
     x1_ref, hx_ref, pw_ref, pwt_ref, cnt_ref, y_s, x_s) = refs[3 * n_mix:]

    def out_proj(group):
        mix_refs, x_ref = group
        y = _dot(mix_refs[0][...].astype(BF16), w_refs[0][...])
        for m in range(1, n_mix):
            y = y + _dot(mix_refs[m][...].astype(BF16), w_refs[m][...])
        y_s[...] = y
        x_s[...] = x_ref[...]

    _per_group(out_proj, (ctx_refs, xc_ref), (lat_refs, xl_ref))
    x1 = x_s[...] + gate1_ref[...] * _rms(y_s[...], g1_ref[...])
    x1_ref[...] = x1
    h2 = _rms(x1, g2_ref[...]) * (1.0 + sc2_ref[...]) + sh2_ref[...]

    logits = _mm3(_split_bf16(h2), (wrh_ref[...], wrl_ref[...])) + br_ref[...]
    lane_i = lax.broadcasted_iota(jnp.int32, (TM, LANE), 1)
    lane = lane_i.astype(F32)
    neg = jnp.float32(-jnp.inf)
    l = jnp.where(lane_i < N_EXP, logits, neg)
    sel = jnp.zeros((TM, LANE), F32)
    comb = jnp.zeros((TM, LANE), F32)
    wsum = jnp.zeros((TM, 1), F32)
    hots, wts = [], []
    for k in range(TOP_K):
        m = jnp.max(l, axis=-1, keepdims=True)
        idx = jnp.min(jnp.where(l == m, lane, float(LANE)), axis=-1, keepdims=True)
        hot = lane == idx
        if k == 0:
            m0 = m
        w = jnp.exp(m - m0)
        wsum = wsum + w
        hots.append(hot)
        wts.append(w)
        sel = sel + jnp.where(hot, 1.0, 0.0)
        comb = comb + jnp.where(hot, w, 0.0)
        l = jnp.where(hot, neg, l)
    inv = 1.0 / wsum
    comb = comb * inv

    cnt = jnp.sum(sel, axis=0, keepdims=True)
    cnt_ref[...] = cnt
    padded = jnp.floor((cnt + (SUB_ROWS - 1)) * (1.0 / SUB_ROWS)) * SUB_ROWS
    li = lax.broadcasted_iota(jnp.int32, (LANE, LANE), 0)
    lj = lax.broadcasted_iota(jnp.int32, (LANE, LANE), 1)
    before = jnp.where(li < lj, 1.0, 0.0).astype(BF16)
    start = _dot(jnp.broadcast_to(padded, (8, LANE)).astype(BF16), before)[0:1, :]
    ti = lax.broadcasted_iota(jnp.int32, (TM, TM), 0)
    tj = lax.broadcasted_iota(jnp.int32, (TM, TM), 1)
    earlier = jnp.where(tj < ti, 1.0, 0.0).astype(BF16)
    pos_full = _dot(earlier, sel.astype(BF16)) + start
    pw = jnp.zeros((TM, LANE), F32)
    for k in range(TOP_K):
        pos_k = jnp.sum(jnp.where(hots[k], pos_full, 0.0), axis=-1, keepdims=True)
        pw = pw + jnp.where(lane_i == k, pos_k, 0.0) + jnp.where(lane_i == TOP_K + k, wts[k] * inv, 0.0)
    pw_ref[...] = pw
    pwt_ref[...] = pw.T[0:8, :]

    chi, clo = _split_bf16(comb)
    ext = chi.astype(F32) + pltpu.roll(clo.astype(F32), N_EXP, axis=1)
    hx_ref[:, 0:D] = h2.astype(BF16)
    hx_ref[:, D:D_EXT] = ext.astype(BF16)


def _route(mixes, w_outs, x, norm_g, mod4, layer, w_router, b_router):
    n_mix = len(mixes)
    wr_hi, wr_lo = _split_bf16(w_router)
    row = lambda c: pl.BlockSpec((TM, c), lambda t: (t, 0))
    full = lambda a: pl.BlockSpec(a.shape, lambda t: (0, 0))
    g = lambda k: pl.BlockSpec((None, 1, D), lambda t: (k, 0, 0))
    xc, xl, first = _token_pair(x)
    in_specs = ([_ctx_spec(m[0].shape[1]) for m in mixes] + [_lat_spec(m[1].shape[1]) for m in mixes]
                + [full(w) for w in w_outs]
                + [_ctx_spec(D), _lat_spec(D, first), g(1), _mod_spec(layer, 2), g(2), _mod_spec(layer, 3),
                   _mod_spec(layer, 4), full(wr_hi), full(wr_lo), full(b_router)])
    ng = norm_g.reshape(4, 1, D)
    return pl.pallas_call(
        functools.partial(_route_kernel, n_mix),
        grid=(N_TILES,),
        in_specs=in_specs,
        out_specs=[row(D), row(D_EXT), row(LANE), pl.BlockSpec((None, 8, TM), lambda t: (t, 0, 0)),
                   pl.BlockSpec((None, 1, LANE), lambda t: (t, 0, 0))],
        out_shape=[jax.ShapeDtypeStruct((N_TOK, D), F32), jax.ShapeDtypeStruct((N_TOK, D_EXT), BF16),
                   jax.ShapeDtypeStruct((N_TOK, LANE), F32), jax.ShapeDtypeStruct((N_TILES, 8, TM), F32),
                   jax.ShapeDtypeStruct((N_TILES, 1, LANE), F32)],
        scratch_shapes=[pltpu.VMEM((TM, D), F32), pltpu.VMEM((TM, D), F32)],
        compiler_params=_cparams(("arbitrary",)),
        name="out_proj_route",
    )(*[m[0] for m in mixes], *[m[1] for m in mixes], *w_outs, xc, xl, ng, mod4, ng, mod4, mod4, wr_hi, wr_lo,
      b_router)


def _dispatch_kernel(hx_ref, pwt_ref, s_ref):
    j = pl.program_id(0)

    @pl.when(j < N_TILES)
    def _():
        rio = lax.broadcasted_iota(jnp.int32, (TILE_ROWS, TM), 0).astype(F32)
        p = jnp.zeros((TILE_ROWS, TM), F32)
        for k in range(TOP_K):
            p = p + jnp.where(rio == pwt_ref[k:k + 1, :], 1.0, 0.0)
        s_ref[...] = _dot(p.astype(BF16), hx_ref[...]).astype(BF16)

    @pl.when(j == N_TILES)
    def _():
        s_ref[...] = jnp.zeros((TILE_ROWS, D_EXT), BF16)


def _dispatch(hx, pwt):
    clamp = lambda j: jnp.minimum(j, N_TILES - 1)
    return pl.pallas_call(
        _dispatch_kernel,
        grid=(N_TILES + 1,),
        in_specs=[pl.BlockSpec((TM, D_EXT), lambda j: (clamp(j), 0)),
                  pl.BlockSpec((None, 8, TM), lambda j: (clamp(j), 0, 0))],
        out_specs=pl.BlockSpec((TILE_ROWS, D_EXT), lambda j: (j, 0)),
        out_shape=jax.ShapeDtypeStruct(((N_TILES + 1) * TILE_ROWS, D_EXT), BF16),
        compiler_params=_cparams(("arbitrary",)),
        name="moe_dispatch",
    )(hx, pwt)


def _expert_kernel(layer, blk_e_ref, next_e_ref, src_ref, nact_ref, tail_ref, s_hbm, wgu_hbm, bgu_ref, wd_hbm,
                   bd_ref, y_hbm, xbuf, ybuf, wgu_f, wd_f, wgu_s, wd_s, zbuf, gsem, ssem, zsem, wsem):
    i = pl.program_id(0)
    nsteps = pl.num_programs(0)
    nact = nact_ref[0]
    slot = i % 2

    def gather_copy(step, sl, s):
        src = pl.multiple_of(src_ref[step * SUB + s], SUB_ROWS)
        return pltpu.make_async_copy(s_hbm.at[pl.ds(src, SUB_ROWS)],
                                     xbuf.at[sl, pl.ds(s * SUB_ROWS, SUB_ROWS)], gsem.at[sl])

    def scatter_copy(step, sl, s):
        dst = pl.multiple_of(src_ref[step * SUB + s], SUB_ROWS)
        return pltpu.make_async_copy(ybuf.at[sl, pl.ds(s * SUB_ROWS, SUB_ROWS)],
                                     y_hbm.at[pl.ds(dst, SUB_ROWS)], ssem.at[sl])

    def gather_start(step, sl):
        for s in range(SUB):
            gather_copy(step, sl, s).start()

    def gather_wait(step, sl):
        for s in range(SUB):
            gather_copy(step, sl, s).wait()

    def scatter_start(step, sl):
        for s in range(SUB):
            scatter_copy(step, sl, s).start()

    def scatter_wait(step, sl):
        for s in range(SUB):
            scatter_copy(step, sl, s).wait()

    def spare_fill():
        copies = [pltpu.make_async_copy(zbuf, y_hbm.at[pl.ds(N_TILES * TILE_ROWS + b * SUB_ROWS, SUB_ROWS)],
                                        zsem.at[1]) for b in range(2 * SUB)]
        for cp in copies:
            cp.start()
        for cp in copies:
            cp.wait()

    def tail_fill(wait):
        def tile(j, carry):
            first = tail_ref[j]

            def block(b, c):
                dst = pl.multiple_of(first + b * SUB_ROWS, SUB_ROWS)
                cp = pltpu.make_async_copy(zbuf, y_hbm.at[pl.ds(dst, SUB_ROWS)], zsem.at[0])
                if wait:
                    cp.wait()
                else:
                    cp.start()
                return c

            return lax.fori_loop(0, ((j + 1) * TILE_ROWS - first) // SUB_ROWS, block, carry)

        lax.fori_loop(0, N_TILES, tile, 0)

    def weight_copies(e, wsl):
        return (pltpu.make_async_copy(wgu_hbm.at[layer, e], wgu_f.at[wsl], wsem.at[wsl]),
                pltpu.make_async_copy(wd_hbm.at[layer, e], wd_f.at[wsl], wsem.at[wsl]))

    @pl.when(i == 0)
    def _():
        for cp in weight_copies(blk_e_ref[0], next_e_ref[0] % 2):
            cp.start()
        gather_start(0, 0)
        zbuf[...] = jnp.zeros((SUB_ROWS, D), BF16)
        spare_fill()
        tail_fill(wait=False)

    @pl.when(i < nact)
    def _():
        gather_wait(i, slot)

        @pl.when(i >= 2)
        def _():
            scatter_wait(i - 2, slot)

        @pl.when(i + 1 < nact)
        def _():
            gather_start(i + 1, 1 - slot)

        e = blk_e_ref[i]
        e_prev = blk_e_ref[jnp.maximum(i - 1, 0)]

        @pl.when((i == 0) | (e != e_prev))
        def _():
            wsl = next_e_ref[i] % 2
            for cp in weight_copies(e, wsl):
                cp.wait()
            nxt = next_e_ref[i] // 2

            @pl.when(nxt < N_EXP)
            def _():
                for cp in weight_copies(nxt, 1 - wsl):
                    cp.start()

            wgu_s[...] = wgu_f[wsl].astype(BF16)
            wd_s[...] = wd_f[wsl].astype(BF16)

        x = xbuf[slot]
        ext = x[:, D:D_EXT].astype(F32)
        lane = lax.broadcasted_iota(jnp.int32, (MB, LANE), 1)
        wr = jnp.sum(jnp.where((lane == e) | (lane == e + N_EXP), ext, 0.0), axis=-1, keepdims=True)
        xt = x[:, 0:D]
        n_chunks = D_FF // FF_CHUNK

        def gate_up(j):
            gc = slice(j * FF_CHUNK, (j + 1) * FF_CHUNK)
            uc = slice(D_FF + j * FF_CHUNK, D_FF + (j + 1) * FF_CHUNK)
            return _dot(xt, wgu_s[:, gc]) + bgu_ref[:, gc], _dot(xt, wgu_s[:, uc]) + bgu_ref[:, uc]

        y = bd_ref[...]
        nxt = gate_up(0)
        for j in range(n_chunks):
            gate, up = nxt
            if j + 1 < n_chunks:
                nxt = gate_up(j + 1)
            gate = jnp.minimum(gate, SWIGLU_LIMIT)
            up = jnp.clip(up, -SWIGLU_LIMIT, SWIGLU_LIMIT)
            act = (up + 1.0) * gate * jax.nn.sigmoid(SWIGLU_ALPHA * gate)
            y = y + _dot(act.astype(BF16), wd_s[j * FF_CHUNK:(j + 1) * FF_CHUNK, :])
        y = y * wr
        ybuf[slot] = y.astype(BF16)
        scatter_start(i, slot)

    @pl.when(i == nsteps - 1)
    def _():
        @pl.when(nact >= 2)
        def _():
            scatter_wait(nact - 2, nact % 2)

        scatter_wait(nact - 1, (nact - 1) % 2)
        tail_fill(wait=True)


def _experts(blk_e, next_e, src, nact, tail, s_sorted, layer, w_gu, b_gu, w_dn, b_dn):
    rows = N_TILES * TILE_ROWS + 2 * MB
    per_expert = lambda c: pl.BlockSpec((None, None, 1, c), lambda i, be, ne, sr, na, tl: (layer, be[i], 0, 0))
    hbm = pl.BlockSpec(memory_space=pl.ANY)
    grid_spec = pltpu.PrefetchScalarGridSpec(
        num_scalar_prefetch=5,
        grid=(N_MACRO,),
        in_specs=[hbm, hbm, per_expert(2 * D_FF), hbm, per_expert(D)],
        out_specs=hbm,
        scratch_shapes=[pltpu.VMEM((2, MB, D_EXT), BF16), pltpu.VMEM((2, MB, D), BF16),
                        pltpu.VMEM((2, D, 2 * D_FF), F32), pltpu.VMEM((2, D_FF, D), F32),
                        pltpu.VMEM((D, 2 * D_FF), BF16), pltpu.VMEM((D_FF, D), BF16),
                        pltpu.VMEM((SUB_ROWS, D), BF16),
                        pltpu.SemaphoreType.DMA((2,)), pltpu.SemaphoreType.DMA((2,)),
                        pltpu.SemaphoreType.DMA((2,)), pltpu.SemaphoreType.DMA((2,))])
    return pl.pallas_call(
        functools.partial(_expert_kernel, layer),
        grid_spec=grid_spec,
        out_shape=jax.ShapeDtypeStruct((rows, D), BF16),
        compiler_params=_cparams(("arbitrary",), 56 * 1024 * 1024),
        name="moe_experts",
    )(blk_e, next_e, src, nact, tail, s_sorted, w_gu, b_gu.reshape(DEPTH, N_EXP, 1, 2 * D_FF), w_dn,
      b_dn.reshape(DEPTH, N_EXP, 1, D))


def _combine_kernel(y_ref, pw_ref, x1_ref, g_ref, gate_ref, *o_refs):
    lio = lax.broadcasted_iota(jnp.int32, (TM, TILE_ROWS), 1).astype(F32)
    pw = pw_ref[...]
    p = jnp.zeros((TM, TILE_ROWS), F32)
    for k in range(TOP_K):
        p = p + jnp.where(lio == pw[:, k:k + 1], 1.0, 0.0)
    f = _dot(p.astype(BF16), y_ref[...])
    x2 = x1_ref[...] + gate_ref[...] * _rms(f, g_ref[...])
    if len(o_refs) == 1:
        o_refs[0][...] = x2
    else:
        def store(o_ref):
            o_ref[...] = x2

        _per_group(store, *o_refs)


def _combine(y_sorted, pw, x1, norm_g, mod4, layer, split):
    row = lambda c: pl.BlockSpec((TM, c), lambda t: (t, 0))
    if split:
        out_specs = [_ctx_spec(D), _lat_spec(D)]
        out_shape = [jax.ShapeDtypeStruct((N_CTX, D), F32), jax.ShapeDtypeStruct((N_LAT, D), F32)]
    else:
        out_specs, out_shape = row(D), jax.ShapeDtypeStruct((N_TOK, D), F32)
    return pl.pallas_call(
        _combine_kernel,
        grid=(N_TILES,),
        in_specs=[pl.BlockSpec((TILE_ROWS, D), lambda t: (t, 0)), row(LANE), row(D),
                  pl.BlockSpec((None, 1, D), lambda t: (3, 0, 0)), _mod_spec(layer, 5)],
        out_specs=out_specs,
        out_shape=out_shape,
        compiler_params=_cparams(("arbitrary",)),
        name="moe_combine",
    )(y_sorted, pw, x1, norm_g.reshape(4, 1, D), mod4)


N_MACRO_PAD = -(-N_MACRO // 8) * 8


def _dot_exact(sel, tab):
    hi, lo = _split_bf16(tab)
    s16 = sel.astype(BF16)
    return _dot_nt(s16, hi) + _dot_nt(s16, lo)


def _tables_kernel(cnt_ref, blk_ref, src_ref, meta_ref):
    cnt = cnt_ref[...]
    nb = jnp.floor((cnt + (SUB_ROWS - 1)) * (1.0 / SUB_ROWS))
    li = lax.broadcasted_iota(jnp.int32, (LANE, LANE), 0)
    lj = lax.broadcasted_iota(jnp.int32, (LANE, LANE), 1)
    nb16 = nb.astype(BF16)
    lstart = _dot(nb16, jnp.where(li < lj, 1.0, 0.0).astype(BF16)) * SUB_ROWS
    ti = lax.broadcasted_iota(jnp.int32, (N_TILES, N_TILES), 0)
    tj = lax.broadcasted_iota(jnp.int32, (N_TILES, N_TILES), 1)
    cum_nb = _dot(jnp.where(tj <= ti, 1.0, 0.0).astype(BF16), nb16)
    nb_e = cum_nb[N_TILES - 1:N_TILES, :]
    mb_e = jnp.floor((nb_e + (SUB - 1)) * (1.0 / SUB))
    mb_hi, mb_lo = _split_bf16(jnp.broadcast_to(mb_e, (8, LANE)))
    upto = jnp.where(li <= lj, 1.0, 0.0).astype(BF16)
    mend = (_dot(mb_hi, upto) + _dot(mb_lo, upto))[0:1, :]
    lane_row = lax.broadcasted_iota(jnp.int32, (1, LANE), 1)
    nact = jnp.sum(jnp.where(lane_row == N_EXP - 1, mend, 0.0), axis=1, keepdims=True)
    mstart = mend - mb_e

    lane = lax.broadcasted_iota(jnp.int32, (N_MACRO_PAD, LANE), 1)
    i_col = lax.broadcasted_iota(jnp.int32, (N_MACRO_PAD, 1), 0).astype(F32)
    real = lane < N_EXP
    e_i = jnp.sum(jnp.where(real & (mend <= i_col), 1.0, 0.0), axis=1, keepdims=True)
    e_last = jnp.sum(jnp.where((lane_row < N_EXP) & (mend <= nact - 1.0), 1.0, 0.0), axis=1, keepdims=True)
    blk = jnp.where(i_col < nact, jnp.minimum(e_i, N_EXP - 1.0), e_last)
    hot = jnp.where(lane.astype(F32) == blk, 1.0, 0.0)
    mstart_i = jnp.sum(hot * mstart, axis=1, keepdims=True)
    nbe_i = jnp.sum(hot * nb_e, axis=1, keepdims=True)
    cum_t = _dot_exact(hot, cum_nb)
    nb_t = _dot_exact(hot, nb)
    lst_t = _dot_exact(hot, lstart)
    tile_lane = lax.broadcasted_iota(jnp.int32, (N_MACRO_PAD, N_TILES), 1).astype(F32)
    parity = i_col - 2.0 * jnp.floor(i_col * 0.5)
    src = jnp.zeros((N_MACRO_PAD, LANE), F32)
    for s in range(SUB):
        q = (i_col - mstart_i) * SUB + s
        valid = (i_col < nact) & (q < nbe_i)
        le = cum_t <= q
        j = jnp.minimum(jnp.sum(jnp.where(le, 1.0, 0.0), axis=1, keepdims=True), N_TILES - 1.0)
        prev = jnp.sum(jnp.where(le, nb_t, 0.0), axis=1, keepdims=True)
        lst = jnp.sum(jnp.where(tile_lane == j, lst_t, 0.0), axis=1, keepdims=True)
        row = j * TILE_ROWS + lst + (q - prev) * SUB_ROWS
        spare = N_TILES * TILE_ROWS + (parity * SUB + s) * SUB_ROWS
        src = jnp.where(lane == s, jnp.where(valid, row, spare), src)
    src_ref[...] = src.astype(jnp.int32)
    used = real & (mb_e > 0.0)
    lane_f = lane.astype(F32)
    rank = jnp.sum(jnp.where(used & (lane_f < blk), 1.0, 0.0), axis=1, keepdims=True)
    nxt = jnp.min(jnp.where(used & (lane_f > blk), lane_f, float(N_EXP)), axis=1, keepdims=True)
    next_code = 2.0 * nxt + (rank - 2.0 * jnp.floor(rank * 0.5))
    blk_ref[...] = jnp.where(lane == 0, blk, jnp.where(lane == 1, next_code, 0.0)).astype(jnp.int32)
    tile_col = lax.broadcasted_iota(jnp.int32, (N_TILES, 1), 0).astype(F32)
    tail = tile_col * TILE_ROWS + jnp.sum(nb, axis=1, keepdims=True) * SUB_ROWS
    lane_t = lax.broadcasted_iota(jnp.int32, (N_TILES, LANE), 1)
    meta_ref[...] = jnp.where(lane_t == 0, tail, jnp.where(lane_t == 1, nact, 0.0)).astype(jnp.int32)


def _block_tables(cnt):
    i32 = lambda r: jax.ShapeDtypeStruct((r, LANE), jnp.int32)
    blk, src, meta = pl.pallas_call(
        _tables_kernel,
        out_shape=[i32(N_MACRO_PAD), i32(N_MACRO_PAD), i32(N_TILES)],
        compiler_params=_cparams(None),
        name="moe_tables",
    )(cnt.reshape(N_TILES, LANE))
    return blk[:N_MACRO, 0], blk[:N_MACRO, 1], src[:N_MACRO, :SUB].reshape(-1), meta[0:1, 1].reshape(1), meta[:, 0]


def _moe(mixes, w_outs, x, norm_g, mod4, layer, w_router, b_router, w_gu, b_gu, w_dn, b_dn, split=False):
    wr = jnp.pad(w_router, ((0, 0), (0, LANE - N_EXP)))
    br = jnp.pad(b_router, (0, LANE - N_EXP)).reshape(1, LANE)
    x1, hx, pw, pwt, cnt = _route(mixes, w_outs, x, norm_g, mod4, layer, wr, br)
    blk_e, next_e, src, nact, tail = _block_tables(cnt)
    s_sorted = _dispatch(hx, pwt)
    y_sorted = _experts(blk_e, next_e, src, nact, tail, s_sorted, layer, w_gu, b_gu, w_dn, b_dn)
    return _combine(y_sorted, pw, x1, norm_g, mod4, layer, split)


def kernel(x_prompt, x_sample, c, c_ctx, state_delta, cache_diff_k, cache_diff_v, cache_mla_ckv, cache_mla_kpe, w_mod, b_mod, norm_g, w_router, b_router, w_gate_up, b_gate_up, w_down, b_down, w_in_even, conv_w, a_log, dt_bias, gdn_norm_g, diff_lambda, diff_norm_g, w_out_even, w_in_odd, q_norm_g, w_uq, kv_norm_g, w_ukv, w_out_odd):
    x = (x_prompt.reshape(N_CTX, D), x_sample.reshape(N_LAT, D))
    cond =jnp.concatenate([c_ctx[None, :], c, jnp.zeros((16 - 1 - B_LAT, D), F32)], axis=0)
    mod4 = _modulation(cond, w_mod, b_mod).reshape(DEPTH, 16, 1, N_MOD * D)
    cos64, sin64 = _rope_tables()
    rope2 = (jnp.tile(cos64, (1, 2)), jnp.tile(sin64, (1, 2)))
    rope1 = (jnp.pad(cos64, ((0, 0), (0, 64))), jnp.pad(sin64, ((0, 0), (0, 64))))

    we = w_in_even[0]
    w_cat = jnp.concatenate([we[:, 0:2048], we[:, 2064:3600], we[:, 2048:2064],
                             jnp.zeros((D, LANE - 16), F32)], axis=1).astype(BF16)
    qkv, z, qd, kd, vd, ab = _in_even(x, norm_g[0, 0:1], mod4, 0, w_cat)
    gn = gdn_norm_g[0].reshape(1, GDN_DV)
    oa_c, new_state = _gdn(qkv, z, ab, conv_w[0], a_log[0], dt_bias[0], gn, None, B_CTX, T_CTX, 0)
    oa_l = _gdn(qkv, z, ab, conv_w[0], a_log[0], dt_bias[0], gn, state_delta[:, 0], B_LAT, T_LAT, N_CTX)
    lam_init = 0.8 - 0.6 * math.exp(-0.3 * 0)
    dgn = diff_norm_g[0].reshape(1, 2 * DIFF_DH)
    od_c = _diff_attn(qd, kd, vd, diff_lambda[0], dgn, lam_init, B_CTX, T_CTX, 0)
    cache_dk = cache_diff_k[:, 0].reshape(B_LAT, PAST, DIFF_H * 2 * DIFF_DH)
    cache_dv = cache_diff_v[:, 0].reshape(B_LAT, PAST, DIFF_H * 2 * DIFF_DH)
    od_l = _diff_attn(qd, kd, vd, diff_lambda[0], dgn, lam_init, B_LAT, T_LAT, N_CTX,
                      cache=(cache_dk, cache_dv), rope=rope2)
    wo = w_out_even[0].astype(BF16)
    x = _moe([(oa_c, oa_l), (od_c, od_l)], [wo[:512], wo[512:]], x, norm_g[0], mod4, 0, w_router[0], b_router[0],
             w_gate_up, b_gate_up, w_down, b_down)
    new_dk = kd[:N_CTX].reshape(B_CTX, 1, T_CTX, DIFF_H, 2, DIFF_DH)
    new_dv = vd[:N_CTX].reshape(B_CTX, 1, T_CTX, DIFF_H, 2 * DIFF_DH)
    new_state = new_state.reshape(B_CTX, 1, 2, GDN_H, GDN_DK, GDN_DV)

    w_odd = jnp.pad(w_in_odd[0], ((0, 0), (0, ODD_W - (Q_LORA + KV_LORA + QK_ROPE)))).astype(BF16)
    wuq = w_uq[0].reshape(Q_LORA, MLA_H, QK_NOPE + QK_ROPE)
    wuq = jnp.pad(wuq, ((0, 0), (0, 0), (0, MLA_QW - (QK_NOPE + QK_ROPE)))).reshape(Q_LORA, MLA_H * MLA_QW)
    wukv = w_ukv[0].astype(BF16)
    q, kv, ckv, kpe = _in_odd(x, norm_g[1, 0:1], mod4, 1, w_odd, q_norm_g[0].reshape(1, Q_LORA),
                              wuq.astype(BF16), kv_norm_g[0].reshape(1, KV_LORA), wukv)
    om_c = _mla_attn(q, kv, kpe, B_CTX, T_CTX, 0)
    cache_kv = _matmul(cache_mla_ckv[:, 0].reshape(B_LAT * PAST, KV_LORA), wukv)
    cache_kpe = jnp.pad(cache_mla_kpe[:, 0].reshape(B_LAT * PAST, QK_ROPE), ((0, 0), (0, LANE - QK_ROPE)))
    om_l = _mla_attn(q, kv, kpe, B_LAT, T_LAT, N_CTX, cache=(cache_kv, cache_kpe), rope=rope1)
    x_ctx, x_lat = _moe([(om_c, om_l)], [w_out_odd[0].astype(BF16)], x, norm_g[1], mod4, 1, w_router[1],
                        b_router[1], w_gate_up, b_gate_up, w_down, b_down, split=True)
    new_ckv = ckv[:N_CTX].reshape(B_CTX, 1, T_CTX, KV_LORA)
    new_kpe = kpe[:N_CTX, :QK_ROPE].reshape(B_CTX, 1, T_CTX, QK_ROPE)

    return (x_ctx.reshape(B_CTX, T_CTX, D), x_lat.reshape(B_LAT, T_LAT, D),
            new_state, new_dk, new_dv, new_ckv, new_kpe)
```

```python
import functools
import math

import jax
import jax.numpy as jnp
from jax import lax
from jax.experimental import pallas as pl
from jax.experimental.pallas import tpu as pltpu

F32 = jnp.float32
BF16 = jnp.bfloat16
HIGHEST = lax.Precision.HIGHEST

D = 1024
B_CTX, T_CTX = 16, 256
B_LAT, T_LAT = 8, 1024
PAST = 256
N_CTX = B_CTX * T_CTX
N_LAT = B_LAT * T_LAT
N_TOK = N_CTX + N_LAT
DEPTH = 2
N_MOD = 6
GRID_W = 64
GDN_H, GDN_DK, GDN_DV = 4, 128, 128
CONV_W = 5
CHUNK = 64
DIFF_H, DIFF_DH = 4, 64
MLA_H, Q_LORA, KV_LORA, QK_NOPE, QK_ROPE, V_HEAD = 8, 256, 256, 128, 64, 128
N_EXP, TOP_K, D_FF = 32, 4, 1024
SWIGLU_LIMIT, SWIGLU_ALPHA = 7.0, 1.702
ROPE_BASE = 10000.0
EPS = 1e-6

LANE = 128
TM = 256
N_TILES = N_TOK // TM
SUB_ROWS = 16
TILE_ROWS = TOP_K * TM + N_EXP * SUB_ROWS
MB = 256
SUB = MB // SUB_ROWS
FF_CHUNK = 256
GDN_TRIP_ROWS = 256
N_MACRO = -(-((TOP_K * N_TOK) // SUB_ROWS + N_TILES * N_EXP + N_EXP * (SUB - 1)) // SUB)
D_EXT = D + LANE
VMEM_LIMIT = 48 * 1024 * 1024


def _cparams(sem, vmem=VMEM_LIMIT):
    return pltpu.CompilerParams(dimension_semantics=sem, vmem_limit_bytes=vmem)


N_CTX_TILES = N_CTX // TM


def _mod_row(t):
    return jnp.where(t < N_CTX_TILES, 0, 1 + (t - N_CTX_TILES) // (T_LAT // TM))


def _ctx_spec(c):
    return pl.BlockSpec((TM, c), lambda t: (jnp.minimum(t, N_CTX_TILES - 1), 0))


def _lat_spec(c, first=0):
    return pl.BlockSpec((TM, c), lambda t: (jnp.maximum(t - N_CTX_TILES, 0) + first, 0))


def _token_pair(x):
    return (x[0], x[1], 0) if isinstance(x, tuple) else (x, x, N_CTX_TILES)


def _per_group(body, ctx_ref, lat_ref):
    t = pl.program_id(0)
    pl.when(t < N_CTX_TILES)(lambda: body(ctx_ref))
    pl.when(t >= N_CTX_TILES)(lambda: body(lat_ref))


def _rms(x, g):
    return x * lax.rsqrt(jnp.mean(x * x, axis=-1, keepdims=True) + EPS) * g


def _silu(x):
    return x * jax.nn.sigmoid(x)


def _dot(a, b):
    return jnp.dot(a, b, preferred_element_type=F32)


def _dot_nt(a, b):
    return lax.dot_general(a, b, (((1,), (1,)), ((), ())), preferred_element_type=F32)


def _mod_kernel(c_ref, w_ref, b_ref, o_ref):
    s = _silu(c_ref[...])
    o_ref[...] = _dot(s.astype(BF16), w_ref[...].astype(BF16)) + b_ref[...]


def _modulation(cond, w_mod, b_mod):
    nc = 1536
    return pl.pallas_call(
        _mod_kernel,
        grid=(DEPTH, N_MOD * D // nc),
        in_specs=[pl.BlockSpec((16, D), lambda i, j: (0, 0)),
                  pl.BlockSpec((None, D, nc), lambda i, j: (i, 0, j)),
                  pl.BlockSpec((None, 1, nc), lambda i, j: (i, 0, j))],
        out_specs=pl.BlockSpec((None, 16, nc), lambda i, j: (i, 0, j)),
        out_shape=jax.ShapeDtypeStruct((DEPTH, 16, N_MOD * D), F32),
        compiler_params=_cparams(("arbitrary", "arbitrary")),
        name="modulation",
    )(cond, w_mod, b_mod.reshape(DEPTH, 1, N_MOD * D))


def _mod_spec(layer, k):
    return pl.BlockSpec((None, None, 1, D), lambda t: (layer, _mod_row(t), 0, k))


EVEN_W = 2 * 512 + 512 + 512 + 3 * 512 + LANE


def _in_even_kernel(xc_ref, xl_ref, g_ref, sh_ref, sc_ref, w_ref, qkv_ref, z_ref, qd_ref, kd_ref, vd_ref, ab_ref):
    def body(x_ref):
        h = _rms(x_ref[...], g_ref[...]) * (1.0 + sc_ref[...]) + sh_ref[...]
        y = _dot(h.astype(BF16), w_ref[...])
        qkv_ref[...] = y[:, 0:1536]
        z_ref[...] = y[:, 1536:2048]
        qd_ref[...] = y[:, 2048:2560]
        kd_ref[...] = y[:, 2560:3072]
        vd_ref[...] = y[:, 3072:3584]
        ab_ref[...] = y[:, 3584:3712]

    _per_group(body, xc_ref, xl_ref)


def _in_even(x, g, mod4, layer, w):
    xc, xl, first = _token_pair(x)
    row = lambda c: pl.BlockSpec((TM, c), lambda t: (t, 0))
    shp = lambda c: jax.ShapeDtypeStruct((N_TOK, c), F32)
    return pl.pallas_call(
        _in_even_kernel,
        grid=(N_TILES,),
        in_specs=[_ctx_spec(D), _lat_spec(D, first), pl.BlockSpec((1, D), lambda t: (0, 0)), _mod_spec(layer, 0),
                  _mod_spec(layer, 1), pl.BlockSpec((D, EVEN_W), lambda t: (0, 0))],
        out_specs=[row(1536), row(512), row(512), row(512), row(512), row(LANE)],
        out_shape=[shp(1536), shp(512), shp(512), shp(512), shp(512), shp(LANE)],
        compiler_params=_cparams(("arbitrary",)),
        name="in_proj_even",
    )(xc, xl, g, mod4, mod4, w)


def _split_bf16(a):
    hi = a.astype(BF16)
    lo = (a - hi.astype(F32)).astype(BF16)
    return hi, lo


def _mm3(a, b):
    m = a[0].shape[0]
    top = _dot(jnp.concatenate([a[0], a[1]], axis=0), b[0])
    return top[:m] + (top[m:] + _dot(a[0], b[1]))


def _unit_tri_inverses(ls, eye, blk, bd):
    sp = _split_bf16
    d8 = [jnp.where(blk(8), l, 0.0) for l in ls]
    d8s = [sp(d) for d in d8]
    x2s = [sp(_mm3(d, bd(d))) for d in d8s]
    x2r = [bd(x) for x in x2s]
    ts = [eye - d for d in d8]
    ts = [t + _mm3(sp(t), xr) for t, xr in zip(ts, x2r)]
    x4 = [_mm3(x, xr) for x, xr in zip(x2s, x2r)]
    ts = [t + _mm3(sp(t), bd(sp(x))) for t, x in zip(ts, x4)]
    for s in (8, 16, 32):
        offs = [sp(jnp.where(blk(2 * s) & jnp.logical_not(blk(s)), l, 0.0)) for l in ls]
        tss = [sp(t) for t in ts]
        ys = [_mm3(o, bd(t)) for o, t in zip(offs, tss)]
        ts = [t - _mm3(t2, bd(sp(y))) for t, t2, y in zip(ts, tss, ys)]
    return ts


def _softplus(x):
    return jnp.maximum(x, 0.0) + jnp.log1p(jnp.exp(-jnp.abs(x)))


def _gdn_kernel(has_init, T, alog_ref, dtb_ref, qkv_ref, cw_ref, z_ref, ab_ref, gn_ref, *rest):
    if has_init:
        s0_ref, o_ref = rest[:2]
        st_ref = None
    else:
        o_ref, st_ref = rest[:2]
        s0_ref = None
    xp, qs, ks, vs, abt_s, uo_s, w_s, qg_s, kgt_s, in_s, gl_s, st_s = rest[2:]
    n = T // CHUNK
    H = GDN_H

    xp[0:8, :] = jnp.zeros((8, LANE), F32)
    xp[T + 8:T + 16, :] = jnp.zeros((8, LANE), F32)

    def l2n(x):
        return x * lax.rsqrt(jnp.sum(x * x, axis=-1, keepdims=True) + EPS)

    for blk in range(3 * H):
        cols = slice(blk * LANE, (blk + 1) * LANE)
        dst = slice((blk % H) * LANE, (blk % H + 1) * LANE)
        xp[8:T + 8, :] = qkv_ref[:, cols]
        w = cw_ref[:, cols]
        acc = xp[6:T + 6, :] * w[0:1, :]
        for i in range(1, CONV_W):
            acc = acc + xp[6 + i:T + 6 + i, :] * w[i:i + 1, :]
        y = _silu(acc)
        if blk < H:
            qs[:, dst] = l2n(y) * (GDN_DK ** -0.5)
        elif blk < 2 * H:
            ks[:, dst] = l2n(y)
        else:
            vs[:, dst] = y
    abt_s[...] = ab_ref[...].T[0:16, :]

    ii = lax.broadcasted_iota(jnp.int32, (CHUNK, LANE), 0)
    lane = lax.broadcasted_iota(jnp.int32, (CHUNK, LANE), 1)
    jm = lane & (CHUNK - 1)
    fwd = lane < CHUNK
    order = jnp.where(fwd, ii - jm, jm - ii)
    incl = order >= 0
    strict = order > 0
    incl_t = order <= 0
    eye = jnp.where(order == 0, 1.0, 0.0).astype(F32)
    blk_mask = lambda s: (ii // s) == (jm // s)
    zeros_c = jnp.zeros((CHUNK, LANE), F32)
    zeros_c16 = jnp.zeros((CHUNK, LANE), BF16)

    keep_f = jnp.where(fwd, 1.0, 0.0).astype(BF16)
    keep_b = jnp.where(fwd, 0.0, 1.0).astype(BF16)

    def bd(parts):
        return tuple(jnp.concatenate([a * keep_f, a * keep_b], axis=0) for a in parts)

    lane_row = lax.broadcasted_iota(jnp.int32, (1, LANE), 1)
    sub_col = lax.broadcasted_iota(jnp.int32, (16, 1), 0)
    a_lane, dt_lane = jnp.zeros((1, LANE), F32), jnp.zeros((1, LANE), F32)
    a_col, dt_col = jnp.zeros((16, 1), F32), jnp.zeros((16, 1), F32)
    for d in range(2):
        for h in range(H):
            a_lane = jnp.where(lane_row == d * H + h, alog_ref[d, h], a_lane)
            dt_lane = jnp.where(lane_row == d * H + h, dtb_ref[d, h], dt_lane)
            a_col = jnp.where(sub_col == d * H + h, alog_ref[d, h], a_col)
            dt_col = jnp.where(sub_col == d * H + h, dtb_ref[d, h], dt_col)
    nega_lane, nega_col = -jnp.exp(a_lane), -jnp.exp(a_col)

    def prep(p, carry):
        rbase = pl.multiple_of(p * GDN_TRIP_ROWS, GDN_TRIP_ROWS)
        g_rows = nega_col * _softplus(abt_s[:, pl.ds(rbase, GDN_TRIP_ROWS)] + dt_col)
        sysm = []
        for cc in range(GDN_TRIP_ROWS // CHUNK):
            r0 = pl.multiple_of(rbase + cc * CHUNK, CHUNK)
            abc = ab_ref[pl.ds(r0, CHUNK), :]
            g_cols = nega_lane * _softplus(abc + dt_lane)
            b_cols = jax.nn.sigmoid(abc)
            for h in range(H):
                hs = slice(h * LANE, (h + 1) * LANE)
                tok = slice(cc * CHUNK, (cc + 1) * CHUNK)
                sysm.append(dict(
                    c=(GDN_TRIP_ROWS // CHUNK) * p + cc, rows=pl.ds(r0, CHUNK), h=h,
                    qc=qs[pl.ds(r0, CHUNK), hs], kc=ks[pl.ds(r0, CHUNK), hs], vc=vs[pl.ds(r0, CHUNK), hs],
                    g_col=jnp.where(fwd, g_cols[:, h:h + 1], g_cols[:, H + h:H + h + 1]),
                    g_row=jnp.concatenate([g_rows[h:h + 1, tok], g_rows[H + h:H + h + 1, tok]], axis=1),
                    beta=(b_cols[:, 2 * H + h:2 * H + h + 1], b_cols[:, 3 * H + h:3 * H + h + 1])))
        for s in sysm:
            k16 = s["kc"].astype(BF16)
            kk16 = jnp.concatenate([k16, k16], axis=0)
            kq = _dot_nt(jnp.concatenate([k16, s["qc"].astype(BF16)], axis=0), kk16)
            s["kk"], s["qk"] = kq[:CHUNK], kq[CHUNK:]
        ls = []
        for s in sysm:
            m = jnp.where(incl, s["g_row"], 0.0)
            s["gcf"] = jnp.sum(jnp.where(fwd, m, 0.0), axis=1, keepdims=True)
            s["gcb"] = jnp.sum(jnp.where(fwd, 0.0, m), axis=1, keepdims=True)
            gc_col = jnp.where(fwd, s["gcf"], s["gcb"])
            gc_row = jnp.sum(jnp.where(incl_t, s["g_col"], 0.0), axis=0, keepdims=True)
            decay = jnp.where(incl, jnp.exp(jnp.where(incl, gc_col - gc_row, 0.0)), 0.0)
            beta = jnp.where(fwd, s["beta"][0], s["beta"][1])
            ls.append(jnp.where(strict, s["kk"] * beta * decay, 0.0))
            s["intra"] = s["qk"] * decay
        ts = _unit_tri_inverses(ls, eye, blk_mask, bd)
        for s, t in zip(sysm, ts):
            h, c, rows = s["h"], s["c"], s["rows"]
            egs = (jnp.exp(s["gcf"]), jnp.exp(s["gcb"]))
            r_f = jnp.concatenate([s["vc"] * s["beta"][0], s["kc"] * (s["beta"][0] * egs[0]), zeros_c, zeros_c], axis=1)
            r_b = jnp.concatenate([zeros_c, zeros_c, s["vc"] * s["beta"][1], s["kc"] * (s["beta"][1] * egs[1])], axis=1)
            uw = _dot(t.astype(BF16), jnp.concatenate([r_f, r_b], axis=0).astype(BF16))
            g_last = (s["gcf"][CHUNK - 1:CHUNK, :], s["gcb"][0:1, :])
            kgs = []
            for d in range(2):
                k = d * H + h
                uo_s[k, rows, :] = uw[:, 2 * d * LANE:(2 * d + 1) * LANE]
                w_s[k, rows, :] = uw[:, (2 * d + 1) * LANE:(2 * d + 2) * LANE].astype(BF16)
                qg_s[k, rows, :] = (s["qc"] * egs[d]).astype(BF16)
                gc = s["gcf"] if d == 0 else s["gcb"]
                kgs.append(s["kc"] * jnp.exp(g_last[d] - gc))
                gl_s[k, c] = jnp.broadcast_to(jnp.exp(g_last[d]), (8, LANE))
            in_s[h, c] = s["intra"].astype(BF16)
            kgt_s[h, c] = jnp.concatenate(kgs, axis=0).T.astype(BF16)
        return carry

    lax.fori_loop(0, T // GDN_TRIP_ROWS, prep, 0)

    for d in range(2):
        for h in range(H):
            st_s[d * H + h] = s0_ref[d, h] if has_init else jnp.zeros((GDN_DK, GDN_DV), F32)

    def scan(i, carry):
        cs = (i, n - 1 - i)
        ks_ = range(2 * H)
        c_of = [cs[k // H] for k in ks_]
        rows = [pl.ds(pl.multiple_of(c * CHUNK, CHUNK), CHUNK) for c in c_of]
        s16 = [st_s[k].astype(BF16) for k in ks_]
        wq = [_dot(jnp.concatenate([w_s[k, rows[k], :], qg_s[k, rows[k], :]], axis=0), s16[k]) for k in ks_]
        vn = [(uo_s[k, rows[k], :] - wq[k][:CHUNK]).astype(BF16) for k in ks_]
        vpad = [jnp.concatenate([vn[k], zeros_c16] if k < H else [zeros_c16, vn[k]], axis=0) for k in ks_]
        ik = [_dot(jnp.concatenate([in_s[k % H, c_of[k]], kgt_s[k % H, c_of[k]]], axis=0), vpad[k]) for k in ks_]
        for k in ks_:
            uo_s[k, rows[k], :] = wq[k][CHUNK:] + ik[k][:CHUNK]
            st_s[k] = st_s[k] * gl_s[k, c_of[k]][0:1, :] + ik[k][CHUNK:]
        return carry

    lax.fori_loop(0, n, scan, 0)
    for h in range(H):
        hs = slice(h * LANE, (h + 1) * LANE)
        o_ref[:, hs] = _rms(uo_s[h] + uo_s[H + h], gn_ref[...]) * _silu(z_ref[:, hs])
        if st_ref is not None:
            st_ref[0, h] = st_s[h]
            st_ref[1, h] = st_s[H + h]


def _gdn(qkv, z, ab, conv_w, a_log, dt_bias, gn, s0, nb, T, row0):
    n = T // CHUNK
    blk0 = row0 // T
    has_init = s0 is not None
    H, W = GDN_H, GDN_H * LANE
    smem = pl.BlockSpec(memory_space=pltpu.SMEM)
    in_specs = [smem, smem,
                pl.BlockSpec((T, 3 * W), lambda b: (blk0 + b, 0), pipeline_mode=pl.Buffered(1)),
                pl.BlockSpec((CONV_W, 3 * W), lambda b: (0, 0)),
                pl.BlockSpec((T, W), lambda b: (blk0 + b, 0), pipeline_mode=pl.Buffered(1)),
                pl.BlockSpec((T, LANE), lambda b: (blk0 + b, 0)),
                pl.BlockSpec((1, LANE), lambda b: (0, 0))]
    args = [a_log, dt_bias, qkv, conv_w, z, ab, gn]
    st_spec = pl.BlockSpec((None, 2, H, GDN_DK, GDN_DV), lambda b: (b, 0, 0, 0, 0))
    o_spec = pl.BlockSpec((T, W), lambda b: (b, 0))
    o_shape = jax.ShapeDtypeStruct((nb * T, W), F32)
    if has_init:
        in_specs.append(st_spec)
        args.append(s0)
        out_specs, out_shape = o_spec, o_shape
    else:
        out_specs = [o_spec, st_spec]
        out_shape = [o_shape, jax.ShapeDtypeStruct((nb, 2, H, GDN_DK, GDN_DV), F32)]
    scratch = [pltpu.VMEM((T + 16, LANE), F32), pltpu.VMEM((T, W), F32), pltpu.VMEM((T, W), F32),
               pltpu.VMEM((T, W), F32), pltpu.VMEM((16, T), F32),
               pltpu.VMEM((2 * H, T, LANE), F32), pltpu.VMEM((2 * H, T, LANE), BF16),
               pltpu.VMEM((2 * H, T, LANE), BF16), pltpu.VMEM((H, n, LANE, LANE), BF16),
               pltpu.VMEM((H, n, CHUNK, LANE), BF16), pltpu.VMEM((2 * H, n, 8, LANE), F32),
               pltpu.VMEM((2 * H, GDN_DK, GDN_DV), F32)]
    return pl.pallas_call(
        functools.partial(_gdn_kernel, has_init, T),
        grid=(nb,),
        in_specs=in_specs, out_specs=out_specs, out_shape=out_shape, scratch_shapes=scratch,
        compiler_params=_cparams(("arbitrary",), 56 * 1024 * 1024),
        name="gdn_lat" if has_init else "gdn_ctx",
    )(*args)


def _rope(x, cos, sin_signed):
    lane = lax.broadcasted_iota(jnp.int32, x.shape, 1)
    up = pltpu.roll(x, LANE - 16, axis=1)
    dn = pltpu.roll(x, 16, axis=1)
    rot = jnp.where((lane % 32) < 16, up, dn)
    return x * cos + rot * sin_signed


def _rope_tables():
    rows = T_LAT // GRID_W
    r = jnp.repeat(jnp.arange(rows, dtype=F32), GRID_W)
    col = jnp.tile(jnp.arange(GRID_W, dtype=F32), rows)
    quarter = 16
    inv = ROPE_BASE ** (-jnp.arange(quarter, dtype=F32) / quarter)
    ar = r[:, None] * inv
    ac = col[:, None] * inv
    ang = jnp.concatenate([ar, ar, ac, ac], axis=-1)
    sign = jnp.tile(jnp.concatenate([-jnp.ones(16, F32), jnp.ones(16, F32)]), 2)
    cos, sin = jnp.cos(ang), jnp.sin(ang) * sign
    return cos, sin


LOG2E = math.log2(math.e)


def _attend(parts, values):
    m = functools.reduce(jnp.maximum, [jnp.max(s, axis=-1, keepdims=True) for s in parts])
    es = [jnp.exp2(s - m) for s in parts]
    den = functools.reduce(lambda a, b: a + b, [jnp.sum(e, axis=-1, keepdims=True) for e in es])
    o = functools.reduce(lambda a, b: a + b, [_dot(e.astype(BF16), v) for e, v in zip(es, values)])
    return o * (1.0 / den)


def _diff_kernel(latent, T, lam_init, q_ref, k_ref, v_ref, lam_ref, gn_ref, *rest):
    if latent:
        cos_ref, sin_ref, ck_ref, cv_ref, o_ref = rest
    else:
        (o_ref,) = rest
    scale = DIFF_DH ** -0.5
    lp = lam_ref[...]
    lam = (jnp.exp(jnp.sum(lp[0:1] * lp[1:2], axis=-1, keepdims=True))
           - jnp.exp(jnp.sum(lp[2:3] * lp[3:4], axis=-1, keepdims=True)) + lam_init)
    tq = 256
    lane = lax.broadcasted_iota(jnp.int32, (tq, LANE), 1)
    for hh in range(q_ref.shape[1] // LANE):
        cs = slice(hh * LANE, (hh + 1) * LANE)
        k = k_ref[:, cs]
        if latent:
            k = _rope(k, cos_ref[...], sin_ref[...])
        k16 = k.astype(BF16)
        v16 = v_ref[:, cs].astype(BF16)
        if latent:
            ck16 = ck_ref[:, cs].astype(BF16)
            cv16 = cv_ref[:, cs].astype(BF16)
        for qb in range(T // tq):
            rows = slice(qb * tq, (qb + 1) * tq)
            q = q_ref[rows, cs]
            if latent:
                q = _rope(q, cos_ref[rows, :], sin_ref[rows, :])
            qs = q * (scale * LOG2E)
            q2 = jnp.concatenate([jnp.where((lane // DIFF_DH) == m, qs, 0.0) for m in range(2)], axis=0)
            parts = [_dot_nt(q2.astype(BF16), k16)]
            if latent:
                parts.insert(0, _dot_nt(q2.astype(BF16), ck16))
            o2 = _attend(parts, [cv16, v16] if latent else [v16])
            o = o2[:tq] - lam * o2[tq:]
            o_ref[rows, cs] = _rms(o, gn_ref[...]) * (1.0 - lam_init)


def _diff_attn(qd, kd, vd, lam_p, gn, lam_init, nb, T, row0, cache=None, rope=None):
    blk0 = row0 // T
    latent = cache is not None
    hps = 1 if latent else DIFF_H
    hcol = pl.BlockSpec((T, hps * LANE), lambda b, h: (blk0 + b, h))
    in_specs = [hcol, hcol, hcol, pl.BlockSpec((4, DIFF_DH), lambda b, h: (0, 0)),
                pl.BlockSpec((1, LANE), lambda b, h: (0, 0))]
    args = [qd, kd, vd, lam_p, gn]
    if latent:
        tab = pl.BlockSpec((T, LANE), lambda b, h: (0, 0))
        cch = pl.BlockSpec((None, PAST, hps * LANE), lambda b, h: (b, 0, h))
        in_specs += [tab, tab, cch, cch]
        args += [rope[0], rope[1], cache[0], cache[1]]
    return pl.pallas_call(
        functools.partial(_diff_kernel, latent, T, lam_init),
        grid=(nb, DIFF_H // hps),
        in_specs=in_specs,
        out_specs=pl.BlockSpec((T, hps * LANE), lambda b, h: (b, h)),
        out_shape=jax.ShapeDtypeStruct((nb * T, DIFF_H * 2 * DIFF_DH), F32),
        compiler_params=_cparams(("arbitrary", "arbitrary")),
        name="diff_lat" if latent else "diff_ctx",
    )(*args)


ODD_W = Q_LORA + KV_LORA + LANE
MLA_QW = 2 * LANE


def _in_odd_kernel(x_ref, g_ref, sh_ref, sc_ref, w_ref, qg_ref, wuq_ref, kvg_ref, wukv_ref,
                   q_ref, kv_ref, ckv_ref, kpe_ref):
    h = _rms(x_ref[...], g_ref[...]) * (1.0 + sc_ref[...]) + sh_ref[...]
    y = _dot(h.astype(BF16), w_ref[...])
    qn = _rms(y[:, 0:Q_LORA], qg_ref[...])
    q_ref[...] = _dot(qn.astype(BF16), wuq_ref[...])
    ckv = _rms(y[:, Q_LORA:Q_LORA + KV_LORA], kvg_ref[...])
    ckv_ref[...] = ckv
    kv_ref[...] = _dot(ckv.astype(BF16), wukv_ref[...])
    kpe_ref[...] = y[:, Q_LORA + KV_LORA:ODD_W]


def _in_odd(x, g, mod4, layer, w, qg, wuq, kvg, wukv):
    row = lambda c: pl.BlockSpec((TM, c), lambda t: (t, 0))
    full = lambda a: pl.BlockSpec(a.shape, lambda t: (0, 0))
    shp = lambda c: jax.ShapeDtypeStruct((N_TOK, c), F32)
    return pl.pallas_call(
        _in_odd_kernel,
        grid=(N_TILES,),
        in_specs=[row(D), full(g), _mod_spec(layer, 0), _mod_spec(layer, 1), full(w), full(qg), full(wuq),
                  full(kvg), full(wukv)],
        out_specs=[row(MLA_H * MLA_QW), row(MLA_H * MLA_QW), row(KV_LORA), row(LANE)],
        out_shape=[shp(MLA_H * MLA_QW), shp(MLA_H * MLA_QW), shp(KV_LORA), shp(LANE)],
        compiler_params=_cparams(("arbitrary",)),
        name="in_proj_odd",
    )(x, g, mod4, mod4, w, qg, wuq, kvg, wukv)


def _mm_kernel(x_ref, w_ref, o_ref):
    o_ref[...] = _dot(x_ref[...].astype(BF16), w_ref[...])


def _matmul(x, w):
    m, k = x.shape
    n = w.shape[1]
    return pl.pallas_call(
        _mm_kernel,
        grid=(m // TM,),
        in_specs=[pl.BlockSpec((TM, k), lambda i: (i, 0)), pl.BlockSpec((k, n), lambda i: (0, 0))],
        out_specs=pl.BlockSpec((TM, n), lambda i: (i, 0)),
        out_shape=jax.ShapeDtypeStruct((m, n), F32),
        compiler_params=_cparams(("arbitrary",)),
        name="cache_kv_proj",
    )(x, w)


def _mla_kernel(latent, T, q_ref, kv_ref, kpe_ref, *rest):
    if latent:
        cos_ref, sin_ref, ckv_ref, ckpe_ref, o_ref = rest
    else:
        (o_ref,) = rest
    scale = (QK_NOPE + QK_ROPE) ** -0.5
    kpe = kpe_ref[...]
    if latent:
        kpe = _rope(kpe, cos_ref[...], sin_ref[...])
    tq = 256
    for hh in range(q_ref.shape[1] // MLA_QW):
        cs = slice(hh * MLA_QW, (hh + 1) * MLA_QW)
        kv = kv_ref[:, cs]
        v16 = kv[:, LANE:].astype(BF16)
        if latent:
            ckv = ckv_ref[:, cs]
            ck16 = jnp.concatenate([ckv[:, :LANE], ckpe_ref[...]], axis=1).astype(BF16)
            cv16 = ckv[:, LANE:].astype(BF16)
        k16 = jnp.concatenate([kv[:, :LANE], kpe], axis=1).astype(BF16)
        for qb in range(T // tq):
            rows = slice(qb * tq, (qb + 1) * tq)
            q = q_ref[rows, cs]
            qp = q[:, LANE:]
            if latent:
                qp = _rope(qp, cos_ref[rows, :], sin_ref[rows, :])
            q16 = (jnp.concatenate([q[:, :LANE], qp], axis=1) * (scale * LOG2E)).astype(BF16)
            parts = [_dot_nt(q16, k16)]
            if latent:
                parts.insert(0, _dot_nt(q16, ck16))
            o_ref[rows, hh * V_HEAD:(hh + 1) * V_HEAD] = _attend(parts, [cv16, v16] if latent else [v16])


def _mla_attn(q, kv, kpe, nb, T, row0, cache=None, rope=None):
    blk0 = row0 // T
    latent = cache is not None
    hps = 1 if latent else 4
    head = pl.BlockSpec((T, hps * MLA_QW), lambda b, h: (blk0 + b, h))
    in_specs = [head, head, pl.BlockSpec((T, LANE), lambda b, h: (blk0 + b, 0))]
    args = [q, kv, kpe]
    if latent:
        tab = pl.BlockSpec((T, LANE), lambda b, h: (0, 0))
        in_specs += [tab, tab, pl.BlockSpec((PAST, hps * MLA_QW), lambda b, h: (b, h)),
                     pl.BlockSpec((PAST, LANE), lambda b, h: (b, 0))]
        args += [rope[0], rope[1], cache[0], cache[1]]
    return pl.pallas_call(
        functools.partial(_mla_kernel, latent, T),
        grid=(nb, MLA_H // hps),
        in_specs=in_specs,
        out_specs=pl.BlockSpec((T, hps * V_HEAD), lambda b, h: (b, h)),
        out_shape=jax.ShapeDtypeStruct((nb * T, MLA_H * V_HEAD), F32),
        compiler_params=_cparams(("arbitrary", "arbitrary")),
        name="mla_lat" if latent else "mla_ctx",
    )(*args)


def _route_kernel(n_mix, *refs):
    ctx_refs = refs[:n_mix]
    lat_refs = refs[n_mix:2 * n_mix]
    w_refs = refs[2 * n_mix:3 * n_mix]
    (xc_ref, xl_ref, g1_ref, gate1_ref, g2_ref, sh2_ref, sc2_ref, wrh_ref, wrl_ref, br_ref,
     x1_ref, hx_ref, pw_ref, pwt_ref, cnt_ref, y_s, x_s) = refs[3 * n_mix:]

    def out_proj(group):
        mix_refs, x_ref = group
        y = _dot(mix_refs[0][...].astype(BF16), w_refs[0][...])
        for m in range(1, n_mix):
            y = y + _dot(mix_refs[m][...].astype(BF16), w_refs[m][...])
        y_s[...] = y
        x_s[...] = x_ref[...]

    _per_group(out_proj, (ctx_refs, xc_ref), (lat_refs, xl_ref))
    x1 = x_s[...] + gate1_ref[...] * _rms(y_s[...], g1_ref[...])
    x1_ref[...] = x1
    h2 = _rms(x1, g2_ref[...]) * (1.0 + sc2_ref[...]) + sh2_ref[...]

    logits = _mm3(_split_bf16(h2), (wrh_ref[...], wrl_ref[...])) + br_ref[...]
    lane_i = lax.broadcasted_iota(jnp.int32, (TM, LANE), 1)
    lane = lane_i.astype(F32)
    neg = jnp.float32(-jnp.inf)
    l = jnp.where(lane_i < N_EXP, logits, neg)
    sel = jnp.zeros((TM, LANE), F32)
    comb = jnp.zeros((TM, LANE), F32)
    wsum = jnp.zeros((TM, 1), F32)
    hots, wts = [], []
    for k in range(TOP_K):
        m = jnp.max(l, axis=-1, keepdims=True)
        idx = jnp.min(jnp.where(l == m, lane, float(LANE)), axis=-1, keepdims=True)
        hot = lane == idx
        if k == 0:
            m0 = m
        w = jnp.exp(m - m0)
        wsum = wsum + w
        hots.append(hot)
        wts.append(w)
        sel = sel + jnp.where(hot, 1.0, 0.0)
        comb = comb + jnp.where(hot, w, 0.0)
        l = jnp.where(hot, neg, l)
    inv = 1.0 / wsum
    comb = comb * inv

    cnt = jnp.sum(sel, axis=0, keepdims=True)
    cnt_ref[...] = cnt
    padded = jnp.floor((cnt + (SUB_ROWS - 1)) * (1.0 / SUB_ROWS)) * SUB_ROWS
    li = lax.broadcasted_iota(jnp.int32, (LANE, LANE), 0)
    lj = lax.broadcasted_iota(jnp.int32, (LANE, LANE), 1)
    before = jnp.where(li < lj, 1.0, 0.0).astype(BF16)
    start = _dot(jnp.broadcast_to(padded, (8, LANE)).astype(BF16), before)[0:1, :]
    ti = lax.broadcasted_iota(jnp.int32, (TM, TM), 0)
    tj = lax.broadcasted_iota(jnp.int32, (TM, TM), 1)
    earlier = jnp.where(tj < ti, 1.0, 0.0).astype(BF16)
    pos_full = _dot(earlier, sel.astype(BF16)) + start
    pw = jnp.zeros((TM, LANE), F32)
    for k in range(TOP_K):
        pos_k = jnp.sum(jnp.where(hots[k], pos_full, 0.0), axis=-1, keepdims=True)
        pw = pw + jnp.where(lane_i == k, pos_k, 0.0) + jnp.where(lane_i == TOP_K + k, wts[k] * inv, 0.0)
    pw_ref[...] = pw
    pwt_ref[...] = pw.T[0:8, :]

    chi, clo = _split_bf16(comb)
    ext = chi.astype(F32) + pltpu.roll(clo.astype(F32), N_EXP, axis=1)
    hx_ref[:, 0:D] = h2.astype(BF16)
    hx_ref[:, D:D_EXT] = ext.astype(BF16)


def _route(mixes, w_outs, x, norm_g, mod4, layer, w_router, b_router):
    n_mix = len(mixes)
    wr_hi, wr_lo = _split_bf16(w_router)
    row = lambda c: pl.BlockSpec((TM, c), lambda t: (t, 0))
    full = lambda a: pl.BlockSpec(a.shape, lambda t: (0, 0))
    g = lambda k: pl.BlockSpec((None, 1, D), lambda t: (k, 0, 0))
    xc, xl, first = _token_pair(x)
    in_specs = ([_ctx_spec(m[0].shape[1]) for m in mixes] + [_lat_spec(m[1].shape[1]) for m in mixes]
                + [full(w) for w in w_outs]
                + [_ctx_spec(D), _lat_spec(D, first), g(1), _mod_spec(layer, 2), g(2), _mod_spec(layer, 3),
                   _mod_spec(layer, 4), full(wr_hi), full(wr_lo), full(b_router)])
    ng = norm_g.reshape(4, 1, D)
    return pl.pallas_call(
        functools.partial(_route_kernel, n_mix),
        grid=(N_TILES,),
        in_specs=in_specs,
        out_specs=[row(D), row(D_EXT), row(LANE), pl.BlockSpec((None, 8, TM), lambda t: (t, 0, 0)),
                   pl.BlockSpec((None, 1, LANE), lambda t: (t, 0, 0))],
        out_shape=[jax.ShapeDtypeStruct((N_TOK, D), F32), jax.ShapeDtypeStruct((N_TOK, D_EXT), BF16),
                   jax.ShapeDtypeStruct((N_TOK, LANE), F32), jax.ShapeDtypeStruct((N_TILES, 8, TM), F32),
                   jax.ShapeDtypeStruct((N_TILES, 1, LANE), F32)],
        scratch_shapes=[pltpu.VMEM((TM, D), F32), pltpu.VMEM((TM, D), F32)],
        compiler_params=_cparams(("arbitrary",)),
        name="out_proj_route",
    )(*[m[0] for m in mixes], *[m[1] for m in mixes], *w_outs, xc, xl, ng, mod4, ng, mod4, mod4, wr_hi, wr_lo,
      b_router)


def _dispatch_kernel(hx_ref, pwt_ref, s_ref):
    j = pl.program_id(0)

    @pl.when(j < N_TILES)
    def _():
        rio = lax.broadcasted_iota(jnp.int32, (TILE_ROWS, TM), 0).astype(F32)
        p = jnp.zeros((TILE_ROWS, TM), F32)
        for k in range(TOP_K):
            p = p + jnp.where(rio == pwt_ref[k:k + 1, :], 1.0, 0.0)
        s_ref[...] = _dot(p.astype(BF16), hx_ref[...]).astype(BF16)

    @pl.when(j == N_TILES)
    def _():
        s_ref[...] = jnp.zeros((TILE_ROWS, D_EXT), BF16)


def _dispatch(hx, pwt):
    clamp = lambda j: jnp.minimum(j, N_TILES - 1)
    return pl.pallas_call(
        _dispatch_kernel,
        grid=(N_TILES + 1,),
        in_specs=[pl.BlockSpec((TM, D_EXT), lambda j: (clamp(j), 0)),
                  pl.BlockSpec((None, 8, TM), lambda j: (clamp(j), 0, 0))],
        out_specs=pl.BlockSpec((TILE_ROWS, D_EXT), lambda j: (j, 0)),
        out_shape=jax.ShapeDtypeStruct(((N_TILES + 1) * TILE_ROWS, D_EXT), BF16),
        compiler_params=_cparams(("arbitrary",)),
        name="moe_dispatch",
    )(hx, pwt)


def _expert_kernel(layer, blk_e_ref, next_e_ref, src_ref, nact_ref, tail_ref, s_hbm, wgu_hbm, bgu_ref, wd_hbm,
                   bd_ref, y_hbm, xbuf, ybuf, wgu_f, wd_f, wgu_s, wd_s, zbuf, gsem, ssem, zsem, wsem):
    i = pl.program_id(0)
    nsteps = pl.num_programs(0)
    nact = nact_ref[0]
    slot = i % 2

    def gather_copy(step, sl, s):
        src = pl.multiple_of(src_ref[step * SUB + s], SUB_ROWS)
        return pltpu.make_async_copy(s_hbm.at[pl.ds(src, SUB_ROWS)],
                                     xbuf.at[sl, pl.ds(s * SUB_ROWS, SUB_ROWS)], gsem.at[sl])

    def scatter_copy(step, sl, s):
        dst = pl.multiple_of(src_ref[step * SUB + s], SUB_ROWS)
        return pltpu.make_async_copy(ybuf.at[sl, pl.ds(s * SUB_ROWS, SUB_ROWS)],
                                     y_hbm.at[pl.ds(dst, SUB_ROWS)], ssem.at[sl])

    def gather_start(step, sl):
        for s in range(SUB):
            gather_copy(step, sl, s).start()

    def gather_wait(step, sl):
        for s in range(SUB):
            gather_copy(step, sl, s).wait()

    def scatter_start(step, sl):
        for s in range(SUB):
            scatter_copy(step, sl, s).start()

    def scatter_wait(step, sl):
        for s in range(SUB):
            scatter_copy(step, sl, s).wait()

    def spare_fill():
        copies = [pltpu.make_async_copy(zbuf, y_hbm.at[pl.ds(N_TILES * TILE_ROWS + b * SUB_ROWS, SUB_ROWS)],
                                        zsem.at[1]) for b in range(2 * SUB)]
        for cp in copies:
            cp.start()
        for cp in copies:
            cp.wait()

    def tail_fill(wait):
        def tile(j, carry):
            first = tail_ref[j]

            def block(b, c):
                dst = pl.multiple_of(first + b * SUB_ROWS, SUB_ROWS)
                cp = pltpu.make_async_copy(zbuf, y_hbm.at[pl.ds(dst, SUB_ROWS)], zsem.at[0])
                if wait:
                    cp.wait()
                else:
                    cp.start()
                return c

            return lax.fori_loop(0, ((j + 1) * TILE_ROWS - first) // SUB_ROWS, block, carry)

        lax.fori_loop(0, N_TILES, tile, 0)

    def weight_copies(e, wsl):
        return (pltpu.make_async_copy(wgu_hbm.at[layer, e], wgu_f.at[wsl], wsem.at[wsl]),
                pltpu.make_async_copy(wd_hbm.at[layer, e], wd_f.at[wsl], wsem.at[wsl]))

    @pl.when(i == 0)
    def _():
        for cp in weight_copies(blk_e_ref[0], next_e_ref[0] % 2):
            cp.start()
        gather_start(0, 0)
        zbuf[...] = jnp.zeros((SUB_ROWS, D), BF16)
        spare_fill()
        tail_fill(wait=False)

    @pl.when(i < nact)
    def _():
        gather_wait(i, slot)

        @pl.when(i >= 2)
        def _():
            scatter_wait(i - 2, slot)

        @pl.when(i + 1 < nact)
        def _():
            gather_start(i + 1, 1 - slot)

        e = blk_e_ref[i]
        e_prev = blk_e_ref[jnp.maximum(i - 1, 0)]

        @pl.when((i == 0) | (e != e_prev))
        def _():
            wsl = next_e_ref[i] % 2
            for cp in weight_copies(e, wsl):
                cp.wait()
            nxt = next_e_ref[i] // 2

            @pl.when(nxt < N_EXP)
            def _():
                for cp in weight_copies(nxt, 1 - wsl):
                    cp.start()

            wgu_s[...] = wgu_f[wsl].astype(BF16)
            wd_s[...] = wd_f[wsl].astype(BF16)

        x = xbuf[slot]
        ext = x[:, D:D_EXT].astype(F32)
        lane = lax.broadcasted_iota(jnp.int32, (MB, LANE), 1)
        wr = jnp.sum(jnp.where((lane == e) | (lane == e + N_EXP), ext, 0.0), axis=-1, keepdims=True)
        xt = x[:, 0:D]
        n_chunks = D_FF // FF_CHUNK

        def gate_up(j):
            gc = slice(j * FF_CHUNK, (j + 1) * FF_CHUNK)
            uc = slice(D_FF + j * FF_CHUNK, D_FF + (j + 1) * FF_CHUNK)
            return _dot(xt, wgu_s[:, gc]) + bgu_ref[:, gc], _dot(xt, wgu_s[:, uc]) + bgu_ref[:, uc]

        y = bd_ref[...]
        nxt = gate_up(0)
        for j in range(n_chunks):
            gate, up = nxt
            if j + 1 < n_chunks:
                nxt = gate_up(j + 1)
            gate = jnp.minimum(gate, SWIGLU_LIMIT)
            up = jnp.clip(up, -SWIGLU_LIMIT, SWIGLU_LIMIT)
            act = (up + 1.0) * gate * jax.nn.sigmoid(SWIGLU_ALPHA * gate)
            y = y + _dot(act.astype(BF16), wd_s[j * FF_CHUNK:(j + 1) * FF_CHUNK, :])
        y = y * wr
        ybuf[slot] = y.astype(BF16)
        scatter_start(i, slot)

    @pl.when(i == nsteps - 1)
    def _():
        @pl.when(nact >= 2)
        def _():
            scatter_wait(nact - 2, nact % 2)

        scatter_wait(nact - 1, (nact - 1) % 2)
        tail_fill(wait=True)


def _experts(blk_e, next_e, src, nact, tail, s_sorted, layer, w_gu, b_gu, w_dn, b_dn):
    rows = N_TILES * TILE_ROWS + 2 * MB
    per_expert = lambda c: pl.BlockSpec((None, None, 1, c), lambda i, be, ne, sr, na, tl: (layer, be[i], 0, 0))
    hbm = pl.BlockSpec(memory_space=pl.ANY)
    grid_spec = pltpu.PrefetchScalarGridSpec(
        num_scalar_prefetch=5,
        grid=(N_MACRO,),
        in_specs=[hbm, hbm, per_expert(2 * D_FF), hbm, per_expert(D)],
        out_specs=hbm,
        scratch_shapes=[pltpu.VMEM((2, MB, D_EXT), BF16), pltpu.VMEM((2, MB, D), BF16),
                        pltpu.VMEM((2, D, 2 * D_FF), F32), pltpu.VMEM((2, D_FF, D), F32),
                        pltpu.VMEM((D, 2 * D_FF), BF16), pltpu.VMEM((D_FF, D), BF16),
                        pltpu.VMEM((SUB_ROWS, D), BF16),
                        pltpu.SemaphoreType.DMA((2,)), pltpu.SemaphoreType.DMA((2,)),
                        pltpu.SemaphoreType.DMA((2,)), pltpu.SemaphoreType.DMA((2,))])
    return pl.pallas_call(
        functools.partial(_expert_kernel, layer),
        grid_spec=grid_spec,
        out_shape=jax.ShapeDtypeStruct((rows, D), BF16),
        compiler_params=_cparams(("arbitrary",), 56 * 1024 * 1024),
        name="moe_experts",
    )(blk_e, next_e, src, nact, tail, s_sorted, w_gu, b_gu.reshape(DEPTH, N_EXP, 1, 2 * D_FF), w_dn,
      b_dn.reshape(DEPTH, N_EXP, 1, D))


def _combine_kernel(y_ref, pw_ref, x1_ref, g_ref, gate_ref, *o_refs):
    lio = lax.broadcasted_iota(jnp.int32, (TM, TILE_ROWS), 1).astype(F32)
    pw = pw_ref[...]
    p = jnp.zeros((TM, TILE_ROWS), F32)
    for k in range(TOP_K):
        p = p + jnp.where(lio == pw[:, k:k + 1], 1.0, 0.0)
    f = _dot(p.astype(BF16), y_ref[...])
    x2 = x1_ref[...] + gate_ref[...] * _rms(f, g_ref[...])
    if len(o_refs) == 1:
        o_refs[0][...] = x2
    else:
        def store(o_ref):
            o_ref[...] = x2

        _per_group(store, *o_refs)


def _combine(y_sorted, pw, x1, norm_g, mod4, layer, split):
    row = lambda c: pl.BlockSpec((TM, c), lambda t: (t, 0))
    if split:
        out_specs = [_ctx_spec(D), _lat_spec(D)]
        out_shape = [jax.ShapeDtypeStruct((N_CTX, D), F32), jax.ShapeDtypeStruct((N_LAT, D), F32)]
    else:
        out_specs, out_shape = row(D), jax.ShapeDtypeStruct((N_TOK, D), F32)
    return pl.pallas_call(
        _combine_kernel,
        grid=(N_TILES,),
        in_specs=[pl.BlockSpec((TILE_ROWS, D), lambda t: (t, 0)), row(LANE), row(D),
                  pl.BlockSpec((None, 1, D), lambda t: (3, 0, 0)), _mod_spec(layer, 5)],
        out_specs=out_specs,
        out_shape=out_shape,
        compiler_params=_cparams(("arbitrary",)),
        name="moe_combine",
    )(y_sorted, pw, x1, norm_g.reshape(4, 1, D), mod4)


N_MACRO_PAD = -(-N_MACRO // 8) * 8


def _dot_exact(sel, tab):
    hi, lo = _split_bf16(tab)
    s16 = sel.astype(BF16)
    return _dot_nt(s16, hi) + _dot_nt(s16, lo)


def _tables_kernel(cnt_ref, blk_ref, src_ref, meta_ref):
    cnt = cnt_ref[...]
    nb = jnp.floor((cnt + (SUB_ROWS - 1)) * (1.0 / SUB_ROWS))
    li = lax.broadcasted_iota(jnp.int32, (LANE, LANE), 0)
    lj = lax.broadcasted_iota(jnp.int32, (LANE, LANE), 1)
    nb16 = nb.astype(BF16)
    lstart = _dot(nb16, jnp.where(li < lj, 1.0, 0.0).astype(BF16)) * SUB_ROWS
    ti = lax.broadcasted_iota(jnp.int32, (N_TILES, N_TILES), 0)
    tj = lax.broadcasted_iota(jnp.int32, (N_TILES, N_TILES), 1)
    cum_nb = _dot(jnp.where(tj <= ti, 1.0, 0.0).astype(BF16), nb16)
    nb_e = cum_nb[N_TILES - 1:N_TILES, :]
    mb_e = jnp.floor((nb_e + (SUB - 1)) * (1.0 / SUB))
    mb_hi, mb_lo = _split_bf16(jnp.broadcast_to(mb_e, (8, LANE)))
    upto = jnp.where(li <= lj, 1.0, 0.0).astype(BF16)
    mend = (_dot(mb_hi, upto) + _dot(mb_lo, upto))[0:1, :]
    lane_row = lax.broadcasted_iota(jnp.int32, (1, LANE), 1)
    nact = jnp.sum(jnp.where(lane_row == N_EXP - 1, mend, 0.0), axis=1, keepdims=True)
    mstart = mend - mb_e

    lane = lax.broadcasted_iota(jnp.int32, (N_MACRO_PAD, LANE), 1)
    i_col = lax.broadcasted_iota(jnp.int32, (N_MACRO_PAD, 1), 0).astype(F32)
    real = lane < N_EXP
    e_i = jnp.sum(jnp.where(real & (mend <= i_col), 1.0, 0.0), axis=1, keepdims=True)
    e_last = jnp.sum(jnp.where((lane_row < N_EXP) & (mend <= nact - 1.0), 1.0, 0.0), axis=1, keepdims=True)
    blk = jnp.where(i_col < nact, jnp.minimum(e_i, N_EXP - 1.0), e_last)
    hot = jnp.where(lane.astype(F32) == blk, 1.0, 0.0)
    mstart_i = jnp.sum(hot * mstart, axis=1, keepdims=True)
    nbe_i = jnp.sum(hot * nb_e, axis=1, keepdims=True)
    cum_t = _dot_exact(hot, cum_nb)
    nb_t = _dot_exact(hot, nb)
    lst_t = _dot_exact(hot, lstart)
    tile_lane = lax.broadcasted_iota(jnp.int32, (N_MACRO_PAD, N_TILES), 1).astype(F32)
    parity = i_col - 2.0 * jnp.floor(i_col * 0.5)
    src = jnp.zeros((N_MACRO_PAD, LANE), F32)
    for s in range(SUB):
        q = (i_col - mstart_i) * SUB + s
        valid = (i_col < nact) & (q < nbe_i)
        le = cum_t <= q
        j = jnp.minimum(jnp.sum(jnp.where(le, 1.0, 0.0), axis=1, keepdims=True), N_TILES - 1.0)
        prev = jnp.sum(jnp.where(le, nb_t, 0.0), axis=1, keepdims=True)
        lst = jnp.sum(jnp.where(tile_lane == j, lst_t, 0.0), axis=1, keepdims=True)
        row = j * TILE_ROWS + lst + (q - prev) * SUB_ROWS
        spare = N_TILES * TILE_ROWS + (parity * SUB + s) * SUB_ROWS
        src = jnp.where(lane == s, jnp.where(valid, row, spare), src)
    src_ref[...] = src.astype(jnp.int32)
    used = real & (mb_e > 0.0)
    lane_f = lane.astype(F32)
    rank = jnp.sum(jnp.where(used & (lane_f < blk), 1.0, 0.0), axis=1, keepdims=True)
    nxt = jnp.min(jnp.where(used & (lane_f > blk), lane_f, float(N_EXP)), axis=1, keepdims=True)
    next_code = 2.0 * nxt + (rank - 2.0 * jnp.floor(rank * 0.5))
    blk_ref[...] = jnp.where(lane == 0, blk, jnp.where(lane == 1, next_code, 0.0)).astype(jnp.int32)
    tile_col = lax.broadcasted_iota(jnp.int32, (N_TILES, 1), 0).astype(F32)
    tail = tile_col * TILE_ROWS + jnp.sum(nb, axis=1, keepdims=True) * SUB_ROWS
    lane_t = lax.broadcasted_iota(jnp.int32, (N_TILES, LANE), 1)
    meta_ref[...] = jnp.where(lane_t == 0, tail, jnp.where(lane_t == 1, nact, 0.0)).astype(jnp.int32)


def _block_tables(cnt):
    i32 = lambda r: jax.ShapeDtypeStruct((r, LANE), jnp.int32)
    blk, src, meta = pl.pallas_call(
        _tables_kernel,
        out_shape=[i32(N_MACRO_PAD), i32(N_MACRO_PAD), i32(N_TILES)],
        compiler_params=_cparams(None),
        name="moe_tables",
    )(cnt.reshape(N_TILES, LANE))
    return blk[:N_MACRO, 0], blk[:N_MACRO, 1], src[:N_MACRO, :SUB].reshape(-1), meta[0:1, 1].reshape(1), meta[:, 0]


def _moe(mixes, w_outs, x, norm_g, mod4, layer, w_router, b_router, w_gu, b_gu, w_dn, b_dn, split=False):
    wr = jnp.pad(w_router, ((0, 0), (0, LANE - N_EXP)))
    br = jnp.pad(b_router, (0, LANE - N_EXP)).reshape(1, LANE)
    x1, hx, pw, pwt, cnt = _route(mixes, w_outs, x, norm_g, mod4, layer, wr, br)
    blk_e, next_e, src, nact, tail = _block_tables(cnt)
    s_sorted = _dispatch(hx, pwt)
    y_sorted = _experts(blk_e, next_e, src, nact, tail, s_sorted, layer, w_gu, b_gu, w_dn, b_dn)
    return _combine(y_sorted, pw, x1, norm_g, mod4, layer, split)


def kernel(x_prompt, x_sample, c, c_ctx, state_delta, cache_diff_k, cache_diff_v, cache_mla_ckv, cache_mla_kpe, w_mod, b_mod, norm_g, w_router, b_router, w_gate_up, b_gate_up, w_down, b_down, w_in_even, conv_w, a_log, dt_bias, gdn_norm_g, diff_lambda, diff_norm_g, w_out_even, w_in_odd, q_norm_g, w_uq, kv_norm_g, w_ukv, w_out_odd):
    x = (x_prompt.reshape(N_CTX, D), x_sample.reshape(N_LAT, D))
    cond =jnp.concatenate([c_ctx[None, :], c, jnp.zeros((16 - 1 - B_LAT, D), F32)], axis=0)
    mod4 = _modulation(cond, w_mod, b_mod).reshape(DEPTH, 16, 1, N_MOD * D)
    cos64, sin64 = _rope_tables()
    rope2 = (jnp.tile(cos64, (1, 2)), jnp.tile(sin64, (1, 2)))
    rope1 = (jnp.pad(cos64, ((0, 0), (0, 64))), jnp.pad(sin64, ((0, 0), (0, 64))))

    we = w_in_even[0]
    w_cat = jnp.concatenate([we[:, 0:2048], we[:, 2064:3600], we[:, 2048:2064],
                             jnp.zeros((D, LANE - 16), F32)], axis=1).astype(BF16)
    qkv, z, qd, kd, vd, ab = _in_even(x, norm_g[0, 0:1], mod4, 0, w_cat)
    gn = gdn_norm_g[0].reshape(1, GDN_DV)
    oa_c, new_state = _gdn(qkv, z, ab, conv_w[0], a_log[0], dt_bias[0], gn, None, B_CTX, T_CTX, 0)
    oa_l = _gdn(qkv, z, ab, conv_w[0], a_log[0], dt_bias[0], gn, state_delta[:, 0], B_LAT, T_LAT, N_CTX)
    lam_init = 0.8 - 0.6 * math.exp(-0.3 * 0)
    dgn = diff_norm_g[0].reshape(1, 2 * DIFF_DH)
    od_c = _diff_attn(qd, kd, vd, diff_lambda[0], dgn, lam_init, B_CTX, T_CTX, 0)
    cache_dk = cache_diff_k[:, 0].reshape(B_LAT, PAST, DIFF_H * 2 * DIFF_DH)
    cache_dv = cache_diff_v[:, 0].reshape(B_LAT, PAST, DIFF_H * 2 * DIFF_DH)
    od_l = _diff_attn(qd, kd, vd, diff_lambda[0], dgn, lam_init, B_LAT, T_LAT, N_CTX,
                      cache=(cache_dk, cache_dv), rope=rope2)
    wo = w_out_even[0].astype(BF16)
    x = _moe([(oa_c, oa_l), (od_c, od_l)], [wo[:512], wo[512:]], x, norm_g[0], mod4, 0, w_router[0], b_router[0],
             w_gate_up, b_gate_up, w_down, b_down)
    new_dk = kd[:N_CTX].reshape(B_CTX, 1, T_CTX, DIFF_H, 2, DIFF_DH)
    new_dv = vd[:N_CTX].reshape(B_CTX, 1, T_CTX, DIFF_H, 2 * DIFF_DH)
    new_state = new_state.reshape(B_CTX, 1, 2, GDN_H, GDN_DK, GDN_DV)

    w_odd = jnp.pad(w_in_odd[0], ((0, 0), (0, ODD_W - (Q_LORA + KV_LORA + QK_ROPE)))).astype(BF16)
    wuq = w_uq[0].reshape(Q_LORA, MLA_H, QK_NOPE + QK_ROPE)
    wuq = jnp.pad(wuq, ((0, 0), (0, 0), (0, MLA_QW - (QK_NOPE + QK_ROPE)))).reshape(Q_LORA, MLA_H * MLA_QW)
    wukv = w_ukv[0].astype(BF16)
    q, kv, ckv, kpe = _in_odd(x, norm_g[1, 0:1], mod4, 1, w_odd, q_norm_g[0].reshape(1, Q_LORA),
                              wuq.astype(BF16), kv_norm_g[0].reshape(1, KV_LORA), wukv)
    om_c = _mla_attn(q, kv, kpe, B_CTX, T_CTX, 0)
    cache_kv = _matmul(cache_mla_ckv[:, 0].reshape(B_LAT * PAST, KV_LORA), wukv)
    cache_kpe = jnp.pad(cache_mla_kpe[:, 0].reshape(B_LAT * PAST, QK_ROPE), ((0, 0), (0, LANE - QK_ROPE)))
    om_l = _mla_attn(q, kv, kpe, B_LAT, T_LAT, N_CTX, cache=(cache_kv, cache_kpe), rope=rope1)
    x_ctx, x_lat = _moe([(om_c, om_l)], [w_out_odd[0].astype(BF16)], x, norm_g[1], mod4, 1, w_router[1],
                        b_router[1], w_gate_up, b_gate_up, w_down, b_down, split=True)
    new_ckv = ckv[:N_CTX].reshape(B_CTX, 1, T_CTX, KV_LORA)
    new_kpe = kpe[:N_CTX, :QK_ROPE].reshape(B_CTX, 1, T_CTX, QK_ROPE)

    return (x_ctx.reshape(B_CTX, T_CTX, D), x_lat.reshape(B_LAT, T_LAT, D),
            new_state, new_dk, new_dv, new_ckv, new_kpe)
```

```python
import functools
import math

import jax
import jax.numpy as jnp
from jax import lax
from jax.experimental import pallas as pl
from jax.experimental.pallas import tpu as pltpu

F32 = jnp.float32
BF16 = jnp.bfloat16
HIGHEST = lax.Precision.HIGHEST

D = 1024
B_CTX, T_CTX = 16, 256
B_LAT, T_LAT = 8, 1024
PAST = 256
N_CTX = B_CTX * T_CTX
N_LAT = B_LAT * T_LAT
N_TOK = N_CTX + N_LAT
DEPTH = 2
N_MOD = 6
GRID_W = 64
GDN_H, GDN_DK, GDN_DV = 4, 128, 128
CONV_W = 5
CHUNK = 64
DIFF_H, DIFF_DH = 4, 64
MLA_H, Q_LORA, KV_LORA, QK_NOPE, QK_ROPE, V_HEAD = 8, 256, 256, 128, 64, 128
N_EXP, TOP_K, D_FF = 32, 4, 1024
SWIGLU_LIMIT, SWIGLU_ALPHA = 7.0, 1.702
ROPE_BASE = 10000.0
EPS = 1e-6

LANE = 128
TM = 256
N_TILES = N_TOK // TM
SUB_ROWS = 16
TILE_ROWS = TOP_K * TM + N_EXP * SUB_ROWS
MB = 256
SUB = MB // SUB_ROWS
FF_CHUNK = 256
SORT_BLOCK = 256
GDN_TRIP_ROWS = 256
N_MACRO = -(-((TOP_K * N_TOK) // SUB_ROWS + N_TILES * N_EXP + N_EXP * (SUB - 1)) // SUB)
D_EXT = D + LANE
VMEM_LIMIT = 48 * 1024 * 1024


def _cparams(sem, vmem=VMEM_LIMIT):
    return pltpu.CompilerParams(dimension_semantics=sem, vmem_limit_bytes=vmem)


N_CTX_TILES = N_CTX // TM


def _mod_row(t):
    return jnp.where(t < N_CTX_TILES, 0, 1 + (t - N_CTX_TILES) // (T_LAT // TM))


def _ctx_spec(c):
    return pl.BlockSpec((TM, c), lambda t: (jnp.minimum(t, N_CTX_TILES - 1), 0))


def _lat_spec(c, first=0):
    return pl.BlockSpec((TM, c), lambda t: (jnp.maximum(t - N_CTX_TILES, 0) + first, 0))


def _token_pair(x):
    return (x[0], x[1], 0) if isinstance(x, tuple) else (x, x, N_CTX_TILES)


def _per_group(body, ctx_ref, lat_ref):
    t = pl.program_id(0)
    pl.when(t < N_CTX_TILES)(lambda: body(ctx_ref))
    pl.when(t >= N_CTX_TILES)(lambda: body(lat_ref))


def _rms(x, g):
    return x * lax.rsqrt(jnp.mean(x * x, axis=-1, keepdims=True) + EPS) * g


def _silu(x):
    return x * jax.nn.sigmoid(x)


def _dot(a, b):
    return jnp.dot(a, b, preferred_element_type=F32)


def _dot_nt(a, b):
    return lax.dot_general(a, b, (((1,), (1,)), ((), ())), preferred_element_type=F32)


def _mod_kernel(c_ref, w_ref, b_ref, o_ref):
    s = _silu(c_ref[...])
    o_ref[...] = _dot(s.astype(BF16), w_ref[...].astype(BF16)) + b_ref[...]


def _modulation(cond, w_mod, b_mod):
    nc = 1536
    return pl.pallas_call(
        _mod_kernel,
        grid=(DEPTH, N_MOD * D // nc),
        in_specs=[pl.BlockSpec((16, D), lambda i, j: (0, 0)),
                  pl.BlockSpec((None, D, nc), lambda i, j: (i, 0, j)),
                  pl.BlockSpec((None, 1, nc), lambda i, j: (i, 0, j))],
        out_specs=pl.BlockSpec((None, 16, nc), lambda i, j: (i, 0, j)),
        out_shape=jax.ShapeDtypeStruct((DEPTH, 16, N_MOD * D), F32),
        compiler_params=_cparams(("arbitrary", "arbitrary")),
        name="modulation",
    )(cond, w_mod, b_mod.reshape(DEPTH, 1, N_MOD * D))


def _mod_spec(layer, k):
    return pl.BlockSpec((None, None, 1, D), lambda t: (layer, _mod_row(t), 0, k))


EVEN_W = 2 * 512 + 512 + 512 + 3 * 512 + LANE


def _in_even_kernel(xc_ref, xl_ref, g_ref, sh_ref, sc_ref, w_ref, qkv_ref, z_ref, qd_ref, kd_ref, vd_ref, ab_ref):
    def body(x_ref):
        h = _rms(x_ref[...], g_ref[...]) * (1.0 + sc_ref[...]) + sh_ref[...]
        y = _dot(h.astype(BF16), w_ref[...])
        qkv_ref[...] = y[:, 0:1536]
        z_ref[...] = y[:, 1536:2048]
        qd_ref[...] = y[:, 2048:2560]
        kd_ref[...] = y[:, 2560:3072]
        vd_ref[...] = y[:, 3072:3584]
        ab_ref[...] = y[:, 3584:3712]

    _per_group(body, xc_ref, xl_ref)


def _in_even(x, g, mod4, layer, w):
    xc, xl, first = _token_pair(x)
    row = lambda c: pl.BlockSpec((TM, c), lambda t: (t, 0))
    shp = lambda c: jax.ShapeDtypeStruct((N_TOK, c), F32)
    return pl.pallas_call(
        _in_even_kernel,
        grid=(N_TILES,),
        in_specs=[_ctx_spec(D), _lat_spec(D, first), pl.BlockSpec((1, D), lambda t: (0, 0)), _mod_spec(layer, 0),
                  _mod_spec(layer, 1), pl.BlockSpec((D, EVEN_W), lambda t: (0, 0))],
        out_specs=[row(1536), row(512), row(512), row(512), row(512), row(LANE)],
        out_shape=[shp(1536), shp(512), shp(512), shp(512), shp(512), shp(LANE)],
        compiler_params=_cparams(("arbitrary",)),
        name="in_proj_even",
    )(xc, xl, g, mod4, mod4, w)


def _split_bf16(a):
    hi = a.astype(BF16)
    lo = (a - hi.astype(F32)).astype(BF16)
    return hi, lo


def _mm3(a, b):
    m = a[0].shape[0]
    top = _dot(jnp.concatenate([a[0], a[1]], axis=0), b[0])
    return top[:m] + (top[m:] + _dot(a[0], b[1]))


def _unit_tri_inverses(ls, eye, blk, bd):
    sp = _split_bf16
    d8 = [jnp.where(blk(8), l, 0.0) for l in ls]
    d8s = [sp(d) for d in d8]
    x2s = [sp(_mm3(d, bd(d))) for d in d8s]
    x2r = [bd(x) for x in x2s]
    ts = [eye - d for d in d8]
    ts = [t + _mm3(sp(t), xr) for t, xr in zip(ts, x2r)]
    x4 = [_mm3(x, xr) for x, xr in zip(x2s, x2r)]
    ts = [t + _mm3(sp(t), bd(sp(x))) for t, x in zip(ts, x4)]
    for s in (8, 16, 32):
        offs = [sp(jnp.where(blk(2 * s) & jnp.logical_not(blk(s)), l, 0.0)) for l in ls]
        tss = [sp(t) for t in ts]
        ys = [_mm3(o, bd(t)) for o, t in zip(offs, tss)]
        ts = [t - _mm3(t2, bd(sp(y))) for t, t2, y in zip(ts, tss, ys)]
    return ts


def _softplus(x):
    return jnp.maximum(x, 0.0) + jnp.log1p(jnp.exp(-jnp.abs(x)))


def _gdn_kernel(has_init, T, alog_ref, dtb_ref, qkv_ref, cw_ref, z_ref, ab_ref, gn_ref, *rest):
    if has_init:
        s0_ref, o_ref = rest[:2]
        st_ref = None
    else:
        o_ref, st_ref = rest[:2]
        s0_ref = None
    xp, qs, ks, vs, abt_s, uo_s, w_s, qg_s, kgt_s, in_s, gl_s, st_s = rest[2:]
    n = T // CHUNK
    H = GDN_H

    xp[0:8, :] = jnp.zeros((8, LANE), F32)
    xp[T + 8:T + 16, :] = jnp.zeros((8, LANE), F32)

    def l2n(x):
        return x * lax.rsqrt(jnp.sum(x * x, axis=-1, keepdims=True) + EPS)

    for blk in range(3 * H):
        cols = slice(blk * LANE, (blk + 1) * LANE)
        dst = slice((blk % H) * LANE, (blk % H + 1) * LANE)
        xp[8:T + 8, :] = qkv_ref[:, cols]
        w = cw_ref[:, cols]
        acc = xp[6:T + 6, :] * w[0:1, :]
        for i in range(1, CONV_W):
            acc = acc + xp[6 + i:T + 6 + i, :] * w[i:i + 1, :]
        y = _silu(acc)
        if blk < H:
            qs[:, dst] = l2n(y) * (GDN_DK ** -0.5)
        elif blk < 2 * H:
            ks[:, dst] = l2n(y)
        else:
            vs[:, dst] = y
    abt_s[...] = ab_ref[...].T[0:16, :]

    ii = lax.broadcasted_iota(jnp.int32, (CHUNK, LANE), 0)
    lane = lax.broadcasted_iota(jnp.int32, (CHUNK, LANE), 1)
    jm = lane & (CHUNK - 1)
    fwd = lane < CHUNK
    order = jnp.where(fwd, ii - jm, jm - ii)
    incl = order >= 0
    strict = order > 0
    incl_t = order <= 0
    eye = jnp.where(order == 0, 1.0, 0.0).astype(F32)
    blk_mask = lambda s: (ii // s) == (jm // s)
    zeros_c = jnp.zeros((CHUNK, LANE), F32)
    zeros_c16 = jnp.zeros((CHUNK, LANE), BF16)

    keep_f = jnp.where(fwd, 1.0, 0.0).astype(BF16)
    keep_b = jnp.where(fwd, 0.0, 1.0).astype(BF16)

    def bd(parts):
        return tuple(jnp.concatenate([a * keep_f, a * keep_b], axis=0) for a in parts)

    lane_row = lax.broadcasted_iota(jnp.int32, (1, LANE), 1)
    sub_col = lax.broadcasted_iota(jnp.int32, (16, 1), 0)
    a_lane, dt_lane = jnp.zeros((1, LANE), F32), jnp.zeros((1, LANE), F32)
    a_col, dt_col = jnp.zeros((16, 1), F32), jnp.zeros((16, 1), F32)
    for d in range(2):
        for h in range(H):
            a_lane = jnp.where(lane_row == d * H + h, alog_ref[d, h], a_lane)
            dt_lane = jnp.where(lane_row == d * H + h, dtb_ref[d, h], dt_lane)
            a_col = jnp.where(sub_col == d * H + h, alog_ref[d, h], a_col)
            dt_col = jnp.where(sub_col == d * H + h, dtb_ref[d, h], dt_col)
    nega_lane, nega_col = -jnp.exp(a_lane), -jnp.exp(a_col)

    def prep(p, carry):
        rbase = pl.multiple_of(p * GDN_TRIP_ROWS, GDN_TRIP_ROWS)
        g_rows = nega_col * _softplus(abt_s[:, pl.ds(rbase, GDN_TRIP_ROWS)] + dt_col)
        sysm = []
        for cc in range(GDN_TRIP_ROWS // CHUNK):
            r0 = pl.multiple_of(rbase + cc * CHUNK, CHUNK)
            abc = ab_ref[pl.ds(r0, CHUNK), :]
            g_cols = nega_lane * _softplus(abc + dt_lane)
            b_cols = jax.nn.sigmoid(abc)
            for h in range(H):
                hs = slice(h * LANE, (h + 1) * LANE)
                tok = slice(cc * CHUNK, (cc + 1) * CHUNK)
                sysm.append(dict(
                    c=(GDN_TRIP_ROWS // CHUNK) * p + cc, rows=pl.ds(r0, CHUNK), h=h,
                    qc=qs[pl.ds(r0, CHUNK), hs], kc=ks[pl.ds(r0, CHUNK), hs], vc=vs[pl.ds(r0, CHUNK), hs],
                    g_col=jnp.where(fwd, g_cols[:, h:h + 1], g_cols[:, H + h:H + h + 1]),
                    g_row=jnp.concatenate([g_rows[h:h + 1, tok], g_rows[H + h:H + h + 1, tok]], axis=1),
                    beta=(b_cols[:, 2 * H + h:2 * H + h + 1], b_cols[:, 3 * H + h:3 * H + h + 1])))
        for s in sysm:
            k16 = s["kc"].astype(BF16)
            kk16 = jnp.concatenate([k16, k16], axis=0)
            kq = _dot_nt(jnp.concatenate([k16, s["qc"].astype(BF16)], axis=0), kk16)
            s["kk"], s["qk"] = kq[:CHUNK], kq[CHUNK:]
        ls = []
        for s in sysm:
            m = jnp.where(incl, s["g_row"], 0.0)
            s["gcf"] = jnp.sum(jnp.where(fwd, m, 0.0), axis=1, keepdims=True)
            s["gcb"] = jnp.sum(jnp.where(fwd, 0.0, m), axis=1, keepdims=True)
            gc_col = jnp.where(fwd, s["gcf"], s["gcb"])
            gc_row = jnp.sum(jnp.where(incl_t, s["g_col"], 0.0), axis=0, keepdims=True)
            decay = jnp.where(incl, jnp.exp(jnp.where(incl, gc_col - gc_row, 0.0)), 0.0)
            beta = jnp.where(fwd, s["beta"][0], s["beta"][1])
            ls.append(jnp.where(strict, s["kk"] * beta * decay, 0.0))
            s["intra"] = s["qk"] * decay
        ts = _unit_tri_inverses(ls, eye, blk_mask, bd)
        for s, t in zip(sysm, ts):
            h, c, rows = s["h"], s["c"], s["rows"]
            egs = (jnp.exp(s["gcf"]), jnp.exp(s["gcb"]))
            r_f = jnp.concatenate([s["vc"] * s["beta"][0], s["kc"] * (s["beta"][0] * egs[0]), zeros_c, zeros_c], axis=1)
            r_b = jnp.concatenate([zeros_c, zeros_c, s["vc"] * s["beta"][1], s["kc"] * (s["beta"][1] * egs[1])], axis=1)
            uw = _dot(t.astype(BF16), jnp.concatenate([r_f, r_b], axis=0).astype(BF16))
            g_last = (s["gcf"][CHUNK - 1:CHUNK, :], s["gcb"][0:1, :])
            kgs = []
            for d in range(2):
                k = d * H + h
                uo_s[k, rows, :] = uw[:, 2 * d * LANE:(2 * d + 1) * LANE]
                w_s[k, rows, :] = uw[:, (2 * d + 1) * LANE:(2 * d + 2) * LANE].astype(BF16)
                qg_s[k, rows, :] = (s["qc"] * egs[d]).astype(BF16)
                gc = s["gcf"] if d == 0 else s["gcb"]
                kgs.append(s["kc"] * jnp.exp(g_last[d] - gc))
                gl_s[k, c] = jnp.broadcast_to(jnp.exp(g_last[d]), (8, LANE))
            in_s[h, c] = s["intra"].astype(BF16)
            kgt_s[h, c] = jnp.concatenate(kgs, axis=0).T.astype(BF16)
        return carry

    lax.fori_loop(0, T // GDN_TRIP_ROWS, prep, 0)

    for d in range(2):
        for h in range(H):
            st_s[d * H + h] = s0_ref[d, h] if has_init else jnp.zeros((GDN_DK, GDN_DV), F32)

    def scan(i, carry):
        cs = (i, n - 1 - i)
        ks_ = range(2 * H)
        c_of = [cs[k // H] for k in ks_]
        rows = [pl.ds(pl.multiple_of(c * CHUNK, CHUNK), CHUNK) for c in c_of]
        s16 = [st_s[k].astype(BF16) for k in ks_]
        wq = [_dot(jnp.concatenate([w_s[k, rows[k], :], qg_s[k, rows[k], :]], axis=0), s16[k]) for k in ks_]
        vn = [(uo_s[k, rows[k], :] - wq[k][:CHUNK]).astype(BF16) for k in ks_]
        vpad = [jnp.concatenate([vn[k], zeros_c16] if k < H else [zeros_c16, vn[k]], axis=0) for k in ks_]
        ik = [_dot(jnp.concatenate([in_s[k % H, c_of[k]], kgt_s[k % H, c_of[k]]], axis=0), vpad[k]) for k in ks_]
        for k in ks_:
            uo_s[k, rows[k], :] = wq[k][CHUNK:] + ik[k][:CHUNK]
            st_s[k] = st_s[k] * gl_s[k, c_of[k]][0:1, :] + ik[k][CHUNK:]
        return carry

    lax.fori_loop(0, n, scan, 0)
    for h in range(H):
        hs = slice(h * LANE, (h + 1) * LANE)
        o_ref[:, hs] = _rms(uo_s[h] + uo_s[H + h], gn_ref[...]) * _silu(z_ref[:, hs])
        if st_ref is not None:
            st_ref[0, h] = st_s[h]
            st_ref[1, h] = st_s[H + h]


def _gdn(qkv, z, ab, conv_w, a_log, dt_bias, gn, s0, nb, T, row0):
    n = T // CHUNK
    blk0 = row0 // T
    has_init = s0 is not None
    H, W = GDN_H, GDN_H * LANE
    smem = pl.BlockSpec(memory_space=pltpu.SMEM)
    in_specs = [smem, smem,
                pl.BlockSpec((T, 3 * W), lambda b: (blk0 + b, 0), pipeline_mode=pl.Buffered(1)),
                pl.BlockSpec((CONV_W, 3 * W), lambda b: (0, 0)),
                pl.BlockSpec((T, W), lambda b: (blk0 + b, 0), pipeline_mode=pl.Buffered(1)),
                pl.BlockSpec((T, LANE), lambda b: (blk0 + b, 0)),
                pl.BlockSpec((1, LANE), lambda b: (0, 0))]
    args = [a_log, dt_bias, qkv, conv_w, z, ab, gn]
    st_spec = pl.BlockSpec((None, 2, H, GDN_DK, GDN_DV), lambda b: (b, 0, 0, 0, 0))
    o_spec = pl.BlockSpec((T, W), lambda b: (b, 0))
    o_shape = jax.ShapeDtypeStruct((nb * T, W), F32)
    if has_init:
        in_specs.append(st_spec)
        args.append(s0)
        out_specs, out_shape = o_spec, o_shape
    else:
        out_specs = [o_spec, st_spec]
        out_shape = [o_shape, jax.ShapeDtypeStruct((nb, 2, H, GDN_DK, GDN_DV), F32)]
    scratch = [pltpu.VMEM((T + 16, LANE), F32), pltpu.VMEM((T, W), F32), pltpu.VMEM((T, W), F32),
               pltpu.VMEM((T, W), F32), pltpu.VMEM((16, T), F32),
               pltpu.VMEM((2 * H, T, LANE), F32), pltpu.VMEM((2 * H, T, LANE), BF16),
               pltpu.VMEM((2 * H, T, LANE), BF16), pltpu.VMEM((H, n, LANE, LANE), BF16),
               pltpu.VMEM((H, n, CHUNK, LANE), BF16), pltpu.VMEM((2 * H, n, 8, LANE), F32),
               pltpu.VMEM((2 * H, GDN_DK, GDN_DV), F32)]
    return pl.pallas_call(
        functools.partial(_gdn_kernel, has_init, T),
        grid=(nb,),
        in_specs=in_specs, out_specs=out_specs, out_shape=out_shape, scratch_shapes=scratch,
        compiler_params=_cparams(("arbitrary",), 56 * 1024 * 1024),
        name="gdn_lat" if has_init else "gdn_ctx",
    )(*args)


def _rope(x, cos, sin_signed):
    lane = lax.broadcasted_iota(jnp.int32, x.shape, 1)
    up = pltpu.roll(x, LANE - 16, axis=1)
    dn = pltpu.roll(x, 16, axis=1)
    rot = jnp.where((lane % 32) < 16, up, dn)
    return x * cos + rot * sin_signed


def _rope_tables():
    rows = T_LAT // GRID_W
    r = jnp.repeat(jnp.arange(rows, dtype=F32), GRID_W)
    col = jnp.tile(jnp.arange(GRID_W, dtype=F32), rows)
    quarter = 16
    inv = ROPE_BASE ** (-jnp.arange(quarter, dtype=F32) / quarter)
    ar = r[:, None] * inv
    ac = col[:, None] * inv
    ang = jnp.concatenate([ar, ar, ac, ac], axis=-1)
    sign = jnp.tile(jnp.concatenate([-jnp.ones(16, F32), jnp.ones(16, F32)]), 2)
    cos, sin = jnp.cos(ang), jnp.sin(ang) * sign
    return cos, sin


LOG2E = math.log2(math.e)


def _attend(parts, values):
    m = functools.reduce(jnp.maximum, [jnp.max(s, axis=-1, keepdims=True) for s in parts])
    es = [jnp.exp2(s - m) for s in parts]
    den = functools.reduce(lambda a, b: a + b, [jnp.sum(e, axis=-1, keepdims=True) for e in es])
    o = functools.reduce(lambda a, b: a + b, [_dot(e.astype(BF16), v) for e, v in zip(es, values)])
    return o * (1.0 / den)


def _diff_kernel(latent, T, lam_init, q_ref, k_ref, v_ref, lam_ref, gn_ref, *rest):
    if latent:
        cos_ref, sin_ref, ck_ref, cv_ref, o_ref = rest
    else:
        (o_ref,) = rest
    scale = DIFF_DH ** -0.5
    lp = lam_ref[...]
    lam = (jnp.exp(jnp.sum(lp[0:1] * lp[1:2], axis=-1, keepdims=True))
           - jnp.exp(jnp.sum(lp[2:3] * lp[3:4], axis=-1, keepdims=True)) + lam_init)
    tq = 256
    lane = lax.broadcasted_iota(jnp.int32, (tq, LANE), 1)
    for hh in range(q_ref.shape[1] // LANE):
        cs = slice(hh * LANE, (hh + 1) * LANE)
        k = k_ref[:, cs]
        if latent:
            k = _rope(k, cos_ref[...], sin_ref[...])
        k16 = k.astype(BF16)
        v16 = v_ref[:, cs].astype(BF16)
        if latent:
            ck16 = ck_ref[:, cs].astype(BF16)
            cv16 = cv_ref[:, cs].astype(BF16)
        def scores(qb):
            rows = slice(qb * tq, (qb + 1) * tq)
            q = q_ref[rows, cs]
            if latent:
                q = _rope(q, cos_ref[rows, :], sin_ref[rows, :])
            qs = q * (scale * LOG2E)
            q2 = jnp.concatenate([jnp.where((lane // DIFF_DH) == m, qs, 0.0) for m in range(2)], axis=0)
            q2 = q2.astype(BF16)
            return [_dot_nt(q2, ck16), _dot_nt(q2, k16)] if latent else [_dot_nt(q2, k16)]

        nxt = scores(0)
        for qb in range(T // tq):
            parts = nxt
            if qb + 1 < T // tq:
                nxt = scores(qb + 1)
            o2 = _attend(parts, [cv16, v16] if latent else [v16])
            o = o2[:tq] - lam * o2[tq:]
            o_ref[qb * tq:(qb + 1) * tq, cs] = _rms(o, gn_ref[...]) * (1.0 - lam_init)


def _diff_attn(qd, kd, vd, lam_p, gn, lam_init, nb, T, row0, cache=None, rope=None):
    blk0 = row0 // T
    latent = cache is not None
    hps = 1 if latent else DIFF_H
    hcol = pl.BlockSpec((T, hps * LANE), lambda b, h: (blk0 + b, h))
    in_specs = [hcol, hcol, hcol, pl.BlockSpec((4, DIFF_DH), lambda b, h: (0, 0)),
                pl.BlockSpec((1, LANE), lambda b, h: (0, 0))]
    args = [qd, kd, vd, lam_p, gn]
    if latent:
        tab = pl.BlockSpec((T, LANE), lambda b, h: (0, 0))
        cch = pl.BlockSpec((None, PAST, hps * LANE), lambda b, h: (b, 0, h))
        in_specs += [tab, tab, cch, cch]
        args += [rope[0], rope[1], cache[0], cache[1]]
    return pl.pallas_call(
        functools.partial(_diff_kernel, latent, T, lam_init),
        grid=(nb, DIFF_H // hps),
        in_specs=in_specs,
        out_specs=pl.BlockSpec((T, hps * LANE), lambda b, h: (b, h)),
        out_shape=jax.ShapeDtypeStruct((nb * T, DIFF_H * 2 * DIFF_DH), F32),
        compiler_params=_cparams(("arbitrary", "arbitrary")),
        name="diff_lat" if latent else "diff_ctx",
    )(*args)


ODD_W = Q_LORA + KV_LORA + LANE
MLA_QW = 2 * LANE


def _in_odd_kernel(x_ref, g_ref, sh_ref, sc_ref, w_ref, qg_ref, wuq_ref, kvg_ref, wukv_ref,
                   q_ref, kv_ref, ckv_ref, kpe_ref):
    h = _rms(x_ref[...], g_ref[...]) * (1.0 + sc_ref[...]) + sh_ref[...]
    y = _dot(h.astype(BF16), w_ref[...])
    qn = _rms(y[:, 0:Q_LORA], qg_ref[...])
    q_ref[...] = _dot(qn.astype(BF16), wuq_ref[...])
    ckv = _rms(y[:, Q_LORA:Q_LORA + KV_LORA], kvg_ref[...])
    ckv_ref[...] = ckv
    kv_ref[...] = _dot(ckv.astype(BF16), wukv_ref[...])
    kpe_ref[...] = y[:, Q_LORA + KV_LORA:ODD_W]


def _in_odd(x, g, mod4, layer, w, qg, wuq, kvg, wukv):
    row = lambda c: pl.BlockSpec((TM, c), lambda t: (t, 0))
    full = lambda a: pl.BlockSpec(a.shape, lambda t: (0, 0))
    shp = lambda c: jax.ShapeDtypeStruct((N_TOK, c), F32)
    return pl.pallas_call(
        _in_odd_kernel,
        grid=(N_TILES,),
        in_specs=[row(D), full(g), _mod_spec(layer, 0), _mod_spec(layer, 1), full(w), full(qg), full(wuq),
                  full(kvg), full(wukv)],
        out_specs=[row(MLA_H * MLA_QW), row(MLA_H * MLA_QW), row(KV_LORA), row(LANE)],
        out_shape=[shp(MLA_H * MLA_QW), shp(MLA_H * MLA_QW), shp(KV_LORA), shp(LANE)],
        compiler_params=_cparams(("arbitrary",)),
        name="in_proj_odd",
    )(x, g, mod4, mod4, w, qg, wuq, kvg, wukv)


def _mm_kernel(x_ref, w_ref, o_ref):
    o_ref[...] = _dot(x_ref[...].astype(BF16), w_ref[...])


def _matmul(x, w):
    m, k = x.shape
    n = w.shape[1]
    return pl.pallas_call(
        _mm_kernel,
        grid=(m // TM,),
        in_specs=[pl.BlockSpec((TM, k), lambda i: (i, 0)), pl.BlockSpec((k, n), lambda i: (0, 0))],
        out_specs=pl.BlockSpec((TM, n), lambda i: (i, 0)),
        out_shape=jax.ShapeDtypeStruct((m, n), F32),
        compiler_params=_cparams(("arbitrary",)),
        name="cache_kv_proj",
    )(x, w)


def _mla_kernel(latent, T, q_ref, kv_ref, kpe_ref, *rest):
    if latent:
        cos_ref, sin_ref, ckv_ref, ckpe_ref, o_ref = rest
    else:
        (o_ref,) = rest
    scale = (QK_NOPE + QK_ROPE) ** -0.5
    kpe = kpe_ref[...]
    if latent:
        kpe = _rope(kpe, cos_ref[...], sin_ref[...])
    tq = 256
    for hh in range(q_ref.shape[1] // MLA_QW):
        cs = slice(hh * MLA_QW, (hh + 1) * MLA_QW)
        kv = kv_ref[:, cs]
        v16 = kv[:, LANE:].astype(BF16)
        if latent:
            ckv = ckv_ref[:, cs]
            ck16 = jnp.concatenate([ckv[:, :LANE], ckpe_ref[...]], axis=1).astype(BF16)
            cv16 = ckv[:, LANE:].astype(BF16)
        k16 = jnp.concatenate([kv[:, :LANE], kpe], axis=1).astype(BF16)

        def scores(qb):
            rows = slice(qb * tq, (qb + 1) * tq)
            q = q_ref[rows, cs]
            qp = q[:, LANE:]
            if latent:
                qp = _rope(qp, cos_ref[rows, :], sin_ref[rows, :])
            q16 = (jnp.concatenate([q[:, :LANE], qp], axis=1) * (scale * LOG2E)).astype(BF16)
            return [_dot_nt(q16, ck16), _dot_nt(q16, k16)] if latent else [_dot_nt(q16, k16)]

        nxt = scores(0)
        for qb in range(T // tq):
            parts = nxt
            if qb + 1 < T // tq:
                nxt = scores(qb + 1)
            o_ref[qb * tq:(qb + 1) * tq, hh * V_HEAD:(hh + 1) * V_HEAD] = _attend(
                parts, [cv16, v16] if latent else [v16])


def _mla_attn(q, kv, kpe, nb, T, row0, cache=None, rope=None):
    blk0 = row0 // T
    latent = cache is not None
    hps = 1 if latent else 4
    head = pl.BlockSpec((T, hps * MLA_QW), lambda b, h: (blk0 + b, h))
    in_specs = [head, head, pl.BlockSpec((T, LANE), lambda b, h: (blk0 + b, 0))]
    args = [q, kv, kpe]
    if latent:
        tab = pl.BlockSpec((T, LANE), lambda b, h: (0, 0))
        in_specs += [tab, tab, pl.BlockSpec((PAST, hps * MLA_QW), lambda b, h: (b, h)),
                     pl.BlockSpec((PAST, LANE), lambda b, h: (b, 0))]
        args += [rope[0], rope[1], cache[0], cache[1]]
    return pl.pallas_call(
        functools.partial(_mla_kernel, latent, T),
        grid=(nb, MLA_H // hps),
        in_specs=in_specs,
        out_specs=pl.BlockSpec((T, hps * V_HEAD), lambda b, h: (b, h)),
        out_shape=jax.ShapeDtypeStruct((nb * T, MLA_H * V_HEAD), F32),
        compiler_params=_cparams(("arbitrary", "arbitrary")),
        name="mla_lat" if latent else "mla_ctx",
    )(*args)


def _route_kernel(n_mix, *refs):
    ctx_refs = refs[:n_mix]
    lat_refs = refs[n_mix:2 * n_mix]
    w_refs = refs[2 * n_mix:3 * n_mix]
    (xc_ref, xl_ref, g1_ref, gate1_ref, g2_ref, sh2_ref, sc2_ref, wrh_ref, wrl_ref, br_ref,
     x1_ref, hx_ref, pw_ref, pwt_ref, cnt_ref, y_s, x_s) = refs[3 * n_mix:]

    def out_proj(group):
        mix_refs, x_ref = group
        y = _dot(mix_refs[0][...].astype(BF16), w_refs[0][...])
        for m in range(1, n_mix):
            y = y + _dot(mix_refs[m][...].astype(BF16), w_refs[m][...])
        y_s[...] = y
        x_s[...] = x_ref[...]

    _per_group(out_proj, (ctx_refs, xc_ref), (lat_refs, xl_ref))
    x1 = x_s[...] + gate1_ref[...] * _rms(y_s[...], g1_ref[...])
    x1_ref[...] = x1
    h2 = _rms(x1, g2_ref[...]) * (1.0 + sc2_ref[...]) + sh2_ref[...]

    logits = _mm3(_split_bf16(h2), (wrh_ref[...], wrl_ref[...])) + br_ref[...]
    lane_i = lax.broadcasted_iota(jnp.int32, (TM, LANE), 1)
    lane = lane_i.astype(F32)
    neg = jnp.float32(-jnp.inf)
    l = jnp.where(lane_i < N_EXP, logits, neg)
    sel = jnp.zeros((TM, LANE), F32)
    comb = jnp.zeros((TM, LANE), F32)
    wsum = jnp.zeros((TM, 1), F32)
    hots, wts = [], []
    for k in range(TOP_K):
        m = jnp.max(l, axis=-1, keepdims=True)
        idx = jnp.min(jnp.where(l == m, lane, float(LANE)), axis=-1, keepdims=True)
        hot = lane == idx
        if k == 0:
            m0 = m
        w = jnp.exp(m - m0)
        wsum = wsum + w
        hots.append(hot)
        wts.append(w)
        sel = sel + jnp.where(hot, 1.0, 0.0)
        comb = comb + jnp.where(hot, w, 0.0)
        l = jnp.where(hot, neg, l)
    inv = 1.0 / wsum
    comb = comb * inv

    cnt = jnp.sum(sel, axis=0, keepdims=True)
    cnt_ref[...] = cnt
    padded = jnp.floor((cnt + (SUB_ROWS - 1)) * (1.0 / SUB_ROWS)) * SUB_ROWS
    li = lax.broadcasted_iota(jnp.int32, (LANE, LANE), 0)
    lj = lax.broadcasted_iota(jnp.int32, (LANE, LANE), 1)
    before = jnp.where(li < lj, 1.0, 0.0).astype(BF16)
    start = _dot(jnp.broadcast_to(padded, (8, LANE)).astype(BF16), before)[0:1, :]
    ti = lax.broadcasted_iota(jnp.int32, (TM, TM), 0)
    tj = lax.broadcasted_iota(jnp.int32, (TM, TM), 1)
    earlier = jnp.where(tj < ti, 1.0, 0.0).astype(BF16)
    pos_full = _dot(earlier, sel.astype(BF16)) + start
    pw = jnp.zeros((TM, LANE), F32)
    for k in range(TOP_K):
        pos_k = jnp.sum(jnp.where(hots[k], pos_full, 0.0), axis=-1, keepdims=True)
        pw = pw + jnp.where(lane_i == k, pos_k, 0.0) + jnp.where(lane_i == TOP_K + k, wts[k] * inv, 0.0)
    pw_ref[...] = pw
    pwt_ref[...] = pw.T[0:8, :]

    chi, clo = _split_bf16(comb)
    ext = chi.astype(F32) + pltpu.roll(clo.astype(F32), N_EXP, axis=1)
    hx_ref[:, 0:D] = h2.astype(BF16)
    hx_ref[:, D:D_EXT] = ext.astype(BF16)


def _route(mixes, w_outs, x, norm_g, mod4, layer, w_router, b_router):
    n_mix = len(mixes)
    wr_hi, wr_lo = _split_bf16(w_router)
    row = lambda c: pl.BlockSpec((TM, c), lambda t: (t, 0))
    full = lambda a: pl.BlockSpec(a.shape, lambda t: (0, 0))
    g = lambda k: pl.BlockSpec((None, 1, D), lambda t: (k, 0, 0))
    xc, xl, first = _token_pair(x)
    in_specs = ([_ctx_spec(m[0].shape[1]) for m in mixes] + [_lat_spec(m[1].shape[1]) for m in mixes]
                + [full(w) for w in w_outs]
                + [_ctx_spec(D), _lat_spec(D, first), g(1), _mod_spec(layer, 2), g(2), _mod_spec(layer, 3),
                   _mod_spec(layer, 4), full(wr_hi), full(wr_lo), full(b_router)])
    ng = norm_g.reshape(4, 1, D)
    return pl.pallas_call(
        functools.partial(_route_kernel, n_mix),
        grid=(N_TILES,),
        in_specs=in_specs,
        out_specs=[row(D), row(D_EXT), row(LANE), pl.BlockSpec((None, 8, TM), lambda t: (t, 0, 0)),
                   pl.BlockSpec((None, 1, LANE), lambda t: (t, 0, 0))],
        out_shape=[jax.ShapeDtypeStruct((N_TOK, D), F32), jax.ShapeDtypeStruct((N_TOK, D_EXT), BF16),
                   jax.ShapeDtypeStruct((N_TOK, LANE), F32), jax.ShapeDtypeStruct((N_TILES, 8, TM), F32),
                   jax.ShapeDtypeStruct((N_TILES, 1, LANE), F32)],
        scratch_shapes=[pltpu.VMEM((TM, D), F32), pltpu.VMEM((TM, D), F32)],
        compiler_params=_cparams(("arbitrary",)),
        name="out_proj_route",
    )(*[m[0] for m in mixes], *[m[1] for m in mixes], *w_outs, xc, xl, ng, mod4, ng, mod4, mod4, wr_hi, wr_lo,
      b_router)


def _dispatch_kernel(hx_ref, pwt_ref, s_ref):
    j = pl.program_id(0)

    @pl.when(j < N_TILES)
    def _():
        rio = lax.broadcasted_iota(jnp.int32, (SORT_BLOCK, TM), 0).astype(F32)

        def one_hot(rb):
            p = jnp.zeros((SORT_BLOCK, TM), F32)
            for k in range(TOP_K):
                p = p + jnp.where(rio == pwt_ref[k:k + 1, :] - float(rb * SORT_BLOCK), 1.0, 0.0)
            return p.astype(BF16)

        nxt = one_hot(0)
        for rb in range(TILE_ROWS // SORT_BLOCK):
            p = nxt
            if rb + 1 < TILE_ROWS // SORT_BLOCK:
                nxt = one_hot(rb + 1)
            s_ref[rb * SORT_BLOCK:(rb + 1) * SORT_BLOCK, :] = _dot(p, hx_ref[...]).astype(BF16)

    @pl.when(j == N_TILES)
    def _():
        s_ref[...] = jnp.zeros((TILE_ROWS, D_EXT), BF16)


def _dispatch(hx, pwt):
    clamp = lambda j: jnp.minimum(j, N_TILES - 1)
    return pl.pallas_call(
        _dispatch_kernel,
        grid=(N_TILES + 1,),
        in_specs=[pl.BlockSpec((TM, D_EXT), lambda j: (clamp(j), 0)),
                  pl.BlockSpec((None, 8, TM), lambda j: (clamp(j), 0, 0))],
        out_specs=pl.BlockSpec((TILE_ROWS, D_EXT), lambda j: (j, 0)),
        out_shape=jax.ShapeDtypeStruct(((N_TILES + 1) * TILE_ROWS, D_EXT), BF16),
        compiler_params=_cparams(("arbitrary",)),
        name="moe_dispatch",
    )(hx, pwt)


def _expert_kernel(layer, blk_e_ref, next_e_ref, src_ref, nact_ref, tail_ref, s_hbm, wgu_hbm, bgu_ref, wd_hbm,
                   bd_ref, y_hbm, xbuf, ybuf, wgu_f, wd_f, wgu_s, wd_s, zbuf, gsem, ssem, zsem, wsem):
    i = pl.program_id(0)
    nsteps = pl.num_programs(0)
    nact = nact_ref[0]
    slot = i % 2

    def gather_copy(step, sl, s):
        src = pl.multiple_of(src_ref[step * SUB + s], SUB_ROWS)
        return pltpu.make_async_copy(s_hbm.at[pl.ds(src, SUB_ROWS)],
                                     xbuf.at[sl, pl.ds(s * SUB_ROWS, SUB_ROWS)], gsem.at[sl])

    def scatter_copy(step, sl, s):
        dst = pl.multiple_of(src_ref[step * SUB + s], SUB_ROWS)
        return pltpu.make_async_copy(ybuf.at[sl, pl.ds(s * SUB_ROWS, SUB_ROWS)],
                                     y_hbm.at[pl.ds(dst, SUB_ROWS)], ssem.at[sl])

    def gather_start(step, sl):
        for s in range(SUB):
            gather_copy(step, sl, s).start()

    def gather_wait(step, sl):
        for s in range(SUB):
            gather_copy(step, sl, s).wait()

    def scatter_start(step, sl):
        for s in range(SUB):
            scatter_copy(step, sl, s).start()

    def scatter_wait(step, sl):
        for s in range(SUB):
            scatter_copy(step, sl, s).wait()

    def spare_fill():
        copies = [pltpu.make_async_copy(zbuf, y_hbm.at[pl.ds(N_TILES * TILE_ROWS + b * SUB_ROWS, SUB_ROWS)],
                                        zsem.at[1]) for b in range(2 * SUB)]
        for cp in copies:
            cp.start()
        for cp in copies:
            cp.wait()

    def tail_fill(wait):
        def tile(j, carry):
            first = tail_ref[j]

            def block(b, c):
                dst = pl.multiple_of(first + b * SUB_ROWS, SUB_ROWS)
                cp = pltpu.make_async_copy(zbuf, y_hbm.at[pl.ds(dst, SUB_ROWS)], zsem.at[0])
                if wait:
                    cp.wait()
                else:
                    cp.start()
                return c

            return lax.fori_loop(0, ((j + 1) * TILE_ROWS - first) // SUB_ROWS, block, carry)

        lax.fori_loop(0, N_TILES, tile, 0)

    def weight_copies(e, wsl):
        return (pltpu.make_async_copy(wgu_hbm.at[layer, e], wgu_f.at[wsl], wsem.at[wsl]),
                pltpu.make_async_copy(wd_hbm.at[layer, e], wd_f.at[wsl], wsem.at[wsl]))

    @pl.when(i == 0)
    def _():
        for cp in weight_copies(blk_e_ref[0], next_e_ref[0] % 2):
            cp.start()
        gather_start(0, 0)
        zbuf[...] = jnp.zeros((SUB_ROWS, D), BF16)
        spare_fill()
        tail_fill(wait=False)

    @pl.when(i < nact)
    def _():
        gather_wait(i, slot)

        @pl.when(i >= 2)
        def _():
            scatter_wait(i - 2, slot)

        @pl.when(i + 1 < nact)
        def _():
            gather_start(i + 1, 1 - slot)

        e = blk_e_ref[i]
        e_prev = blk_e_ref[jnp.maximum(i - 1, 0)]

        @pl.when((i == 0) | (e != e_prev))
        def _():
            wsl = next_e_ref[i] % 2
            for cp in weight_copies(e, wsl):
                cp.wait()
            nxt = next_e_ref[i] // 2

            @pl.when(nxt < N_EXP)
            def _():
                for cp in weight_copies(nxt, 1 - wsl):
                    cp.start()

            wgu_s[...] = wgu_f[wsl].astype(BF16)
            wd_s[...] = wd_f[wsl].astype(BF16)

        x = xbuf[slot]
        ext = x[:, D:D_EXT].astype(F32)
        lane = lax.broadcasted_iota(jnp.int32, (MB, LANE), 1)
        wr = jnp.sum(jnp.where((lane == e) | (lane == e + N_EXP), ext, 0.0), axis=-1, keepdims=True)
        xt = x[:, 0:D]
        n_chunks = D_FF // FF_CHUNK

        def gate_up(j):
            gc = slice(j * FF_CHUNK, (j + 1) * FF_CHUNK)
            uc = slice(D_FF + j * FF_CHUNK, D_FF + (j + 1) * FF_CHUNK)
            return _dot(xt, wgu_s[:, gc]) + bgu_ref[:, gc], _dot(xt, wgu_s[:, uc]) + bgu_ref[:, uc]

        y = bd_ref[...]
        nxt = gate_up(0)
        for j in range(n_chunks):
            gate, up = nxt
            if j + 1 < n_chunks:
                nxt = gate_up(j + 1)
            gate = jnp.minimum(gate, SWIGLU_LIMIT)
            up = jnp.clip(up, -SWIGLU_LIMIT, SWIGLU_LIMIT)
            act = (up + 1.0) * gate * jax.nn.sigmoid(SWIGLU_ALPHA * gate)
            y = y + _dot(act.astype(BF16), wd_s[j * FF_CHUNK:(j + 1) * FF_CHUNK, :])
        y = y * wr
        ybuf[slot] = y.astype(BF16)
        scatter_start(i, slot)

    @pl.when(i == nsteps - 1)
    def _():
        @pl.when(nact >= 2)
        def _():
            scatter_wait(nact - 2, nact % 2)

        scatter_wait(nact - 1, (nact - 1) % 2)
        tail_fill(wait=True)


def _experts(blk_e, next_e, src, nact, tail, s_sorted, layer, w_gu, b_gu, w_dn, b_dn):
    rows = N_TILES * TILE_ROWS + 2 * MB
    per_expert = lambda c: pl.BlockSpec((None, None, 1, c), lambda i, be, ne, sr, na, tl: (layer, be[i], 0, 0))
    hbm = pl.BlockSpec(memory_space=pl.ANY)
    grid_spec = pltpu.PrefetchScalarGridSpec(
        num_scalar_prefetch=5,
        grid=(N_MACRO,),
        in_specs=[hbm, hbm, per_expert(2 * D_FF), hbm, per_expert(D)],
        out_specs=hbm,
        scratch_shapes=[pltpu.VMEM((2, MB, D_EXT), BF16), pltpu.VMEM((2, MB, D), BF16),
                        pltpu.VMEM((2, D, 2 * D_FF), F32), pltpu.VMEM((2, D_FF, D), F32),
                        pltpu.VMEM((D, 2 * D_FF), BF16), pltpu.VMEM((D_FF, D), BF16),
                        pltpu.VMEM((SUB_ROWS, D), BF16),
                        pltpu.SemaphoreType.DMA((2,)), pltpu.SemaphoreType.DMA((2,)),
                        pltpu.SemaphoreType.DMA((2,)), pltpu.SemaphoreType.DMA((2,))])
    return pl.pallas_call(
        functools.partial(_expert_kernel, layer),
        grid_spec=grid_spec,
        out_shape=jax.ShapeDtypeStruct((rows, D), BF16),
        compiler_params=_cparams(("arbitrary",), 56 * 1024 * 1024),
        name="moe_experts",
    )(blk_e, next_e, src, nact, tail, s_sorted, w_gu, b_gu.reshape(DEPTH, N_EXP, 1, 2 * D_FF), w_dn,
      b_dn.reshape(DEPTH, N_EXP, 1, D))


def _combine_kernel(y_ref, pw_ref, x1_ref, g_ref, gate_ref, *o_refs):
    lio = lax.broadcasted_iota(jnp.int32, (TM, TILE_ROWS), 1).astype(F32)
    pw = pw_ref[...]
    p = jnp.zeros((TM, TILE_ROWS), F32)
    for k in range(TOP_K):
        p = p + jnp.where(lio == pw[:, k:k + 1], 1.0, 0.0)
    f = _dot(p.astype(BF16), y_ref[...])
    x2 =x1_ref[...] + gate_ref[...] * _rms(f, g_ref[...])
    if len(o_refs) == 1:
        o_refs[0][...] = x2
    else:
        def store(o_ref):
            o_ref[...] = x2

        _per_group(store, *o_refs)


def _combine(y_sorted, pw, x1, norm_g, mod4, layer, split):
    row = lambda c: pl.BlockSpec((TM, c), lambda t: (t, 0))
    if split:
        out_specs = [_ctx_spec(D), _lat_spec(D)]
        out_shape = [jax.ShapeDtypeStruct((N_CTX, D), F32), jax.ShapeDtypeStruct((N_LAT, D), F32)]
    else:
        out_specs, out_shape = row(D), jax.ShapeDtypeStruct((N_TOK, D), F32)
    return pl.pallas_call(
        _combine_kernel,
        grid=(N_TILES,),
        in_specs=[pl.BlockSpec((TILE_ROWS, D), lambda t: (t, 0)), row(LANE), row(D),
                  pl.BlockSpec((None, 1, D), lambda t: (3, 0, 0)), _mod_spec(layer, 5)],
        out_specs=out_specs,
        out_shape=out_shape,
        compiler_params=_cparams(("arbitrary",)),
        name="moe_combine",
    )(y_sorted, pw, x1, norm_g.reshape(4, 1, D), mod4)


N_MACRO_PAD = -(-N_MACRO // 8) * 8


def _dot_exact(sel, tab):
    hi, lo = _split_bf16(tab)
    s16 = sel.astype(BF16)
    return _dot_nt(s16, hi) + _dot_nt(s16, lo)


def _tables_kernel(cnt_ref, blk_ref, src_ref, meta_ref):
    cnt = cnt_ref[...]
    nb = jnp.floor((cnt + (SUB_ROWS - 1)) * (1.0 / SUB_ROWS))
    li = lax.broadcasted_iota(jnp.int32, (LANE, LANE), 0)
    lj = lax.broadcasted_iota(jnp.int32, (LANE, LANE), 1)
    nb16 = nb.astype(BF16)
    lstart = _dot(nb16, jnp.where(li < lj, 1.0, 0.0).astype(BF16)) * SUB_ROWS
    ti = lax.broadcasted_iota(jnp.int32, (N_TILES, N_TILES), 0)
    tj = lax.broadcasted_iota(jnp.int32, (N_TILES, N_TILES), 1)
    cum_nb = _dot(jnp.where(tj <= ti, 1.0, 0.0).astype(BF16), nb16)
    nb_e = cum_nb[N_TILES - 1:N_TILES, :]
    mb_e = jnp.floor((nb_e + (SUB - 1)) * (1.0 / SUB))
    mb_hi, mb_lo = _split_bf16(jnp.broadcast_to(mb_e, (8, LANE)))
    upto = jnp.where(li <= lj, 1.0, 0.0).astype(BF16)
    mend = (_dot(mb_hi, upto) + _dot(mb_lo, upto))[0:1, :]
    lane_row = lax.broadcasted_iota(jnp.int32, (1, LANE), 1)
    nact = jnp.sum(jnp.where(lane_row == N_EXP - 1, mend, 0.0), axis=1, keepdims=True)
    mstart = mend - mb_e

    lane = lax.broadcasted_iota(jnp.int32, (N_MACRO_PAD, LANE), 1)
    i_col = lax.broadcasted_iota(jnp.int32, (N_MACRO_PAD, 1), 0).astype(F32)
    real = lane < N_EXP
    e_i = jnp.sum(jnp.where(real & (mend <= i_col), 1.0, 0.0), axis=1, keepdims=True)
    e_last = jnp.sum(jnp.where((lane_row < N_EXP) & (mend <= nact - 1.0), 1.0, 0.0), axis=1, keepdims=True)
    blk = jnp.where(i_col < nact, jnp.minimum(e_i, N_EXP - 1.0), e_last)
    hot = jnp.where(lane.astype(F32) == blk, 1.0, 0.0)
    mstart_i = jnp.sum(hot * mstart, axis=1, keepdims=True)
    nbe_i = jnp.sum(hot * nb_e, axis=1, keepdims=True)
    cum_t = _dot_exact(hot, cum_nb)
    nb_t = _dot_exact(hot, nb)
    lst_t = _dot_exact(hot, lstart)
    tile_lane = lax.broadcasted_iota(jnp.int32, (N_MACRO_PAD, N_TILES), 1).astype(F32)
    parity = i_col - 2.0 * jnp.floor(i_col * 0.5)
    src = jnp.zeros((N_MACRO_PAD, LANE), F32)
    for s in range(SUB):
        q = (i_col - mstart_i) * SUB + s
        valid = (i_col < nact) & (q < nbe_i)
        le = cum_t <= q
        j = jnp.minimum(jnp.sum(jnp.where(le, 1.0, 0.0), axis=1, keepdims=True), N_TILES - 1.0)
        prev = jnp.sum(jnp.where(le, nb_t, 0.0), axis=1, keepdims=True)
        lst = jnp.sum(jnp.where(tile_lane == j, lst_t, 0.0), axis=1, keepdims=True)
        row = j * TILE_ROWS + lst + (q - prev) * SUB_ROWS
        spare = N_TILES * TILE_ROWS + (parity * SUB + s) * SUB_ROWS
        src = jnp.where(lane == s, jnp.where(valid, row, spare), src)
    src_ref[...] = src.astype(jnp.int32)
    used = real & (mb_e > 0.0)
    lane_f = lane.astype(F32)
    rank = jnp.sum(jnp.where(used & (lane_f < blk), 1.0, 0.0), axis=1, keepdims=True)
    nxt = jnp.min(jnp.where(used & (lane_f > blk), lane_f, float(N_EXP)), axis=1, keepdims=True)
    next_code = 2.0 * nxt + (rank - 2.0 * jnp.floor(rank * 0.5))
    blk_ref[...] = jnp.where(lane == 0, blk, jnp.where(lane == 1, next_code, 0.0)).astype(jnp.int32)
    tile_col = lax.broadcasted_iota(jnp.int32, (N_TILES, 1), 0).astype(F32)
    tail = tile_col * TILE_ROWS + jnp.sum(nb, axis=1, keepdims=True) * SUB_ROWS
    lane_t = lax.broadcasted_iota(jnp.int32, (N_TILES, LANE), 1)
    meta_ref[...] = jnp.where(lane_t == 0, tail, jnp.where(lane_t == 1, nact, 0.0)).astype(jnp.int32)


def _block_tables(cnt):
    i32 = lambda r: jax.ShapeDtypeStruct((r, LANE), jnp.int32)
    blk, src, meta = pl.pallas_call(
        _tables_kernel,
        out_shape=[i32(N_MACRO_PAD), i32(N_MACRO_PAD), i32(N_TILES)],
        compiler_params=_cparams(None),
        name="moe_tables",
    )(cnt.reshape(N_TILES, LANE))
    return blk[:N_MACRO, 0], blk[:N_MACRO, 1], src[:N_MACRO, :SUB].reshape(-1), meta[0:1, 1].reshape(1), meta[:, 0]


def _moe(mixes, w_outs, x, norm_g, mod4, layer, w_router, b_router, w_gu, b_gu, w_dn, b_dn, split=False):
    wr = jnp.pad(w_router, ((0, 0), (0, LANE - N_EXP)))
    br = jnp.pad(b_router, (0, LANE - N_EXP)).reshape(1, LANE)
    x1, hx, pw, pwt, cnt = _route(mixes, w_outs, x, norm_g, mod4, layer, wr, br)
    blk_e, next_e, src, nact, tail = _block_tables(cnt)
    s_sorted = _dispatch(hx, pwt)
    y_sorted = _experts(blk_e, next_e, src, nact, tail, s_sorted, layer, w_gu, b_gu, w_dn, b_dn)
    return _combine(y_sorted, pw, x1, norm_g, mod4, layer, split)


def kernel(x_prompt, x_sample, c, c_ctx, state_delta, cache_diff_k, cache_diff_v, cache_mla_ckv, cache_mla_kpe, w_mod, b_mod, norm_g, w_router, b_router, w_gate_up, b_gate_up, w_down, b_down, w_in_even, conv_w, a_log, dt_bias, gdn_norm_g, diff_lambda, diff_norm_g, w_out_even, w_in_odd, q_norm_g, w_uq, kv_norm_g, w_ukv, w_out_odd):
    x = (x_prompt.reshape(N_CTX, D), x_sample.reshape(N_LAT, D))
    cond =jnp.concatenate([c_ctx[None, :], c, jnp.zeros((16 - 1 - B_LAT, D), F32)], axis=0)
    mod4 = _modulation(cond, w_mod, b_mod).reshape(DEPTH, 16, 1, N_MOD * D)
    cos64, sin64 = _rope_tables()
    rope2 = (jnp.tile(cos64, (1, 2)), jnp.tile(sin64, (1, 2)))
    rope1 = (jnp.pad(cos64, ((0, 0), (0, 64))), jnp.pad(sin64, ((0, 0), (0, 64))))

    we = w_in_even[0]
    w_cat = jnp.concatenate([we[:, 0:2048], we[:, 2064:3600], we[:, 2048:2064],
                             jnp.zeros((D, LANE - 16), F32)], axis=1).astype(BF16)
    qkv, z, qd, kd, vd, ab = _in_even(x, norm_g[0, 0:1], mod4, 0, w_cat)
    gn = gdn_norm_g[0].reshape(1, GDN_DV)
    oa_c, new_state = _gdn(qkv, z, ab, conv_w[0], a_log[0], dt_bias[0], gn, None, B_CTX, T_CTX, 0)
    oa_l = _gdn(qkv, z, ab, conv_w[0], a_log[0], dt_bias[0], gn, state_delta[:, 0], B_LAT, T_LAT, N_CTX)
    lam_init = 0.8 - 0.6 * math.exp(-0.3 * 0)
    dgn = diff_norm_g[0].reshape(1, 2 * DIFF_DH)
    od_c = _diff_attn(qd, kd, vd, diff_lambda[0], dgn, lam_init, B_CTX, T_CTX, 0)
    cache_dk = cache_diff_k[:, 0].reshape(B_LAT, PAST, DIFF_H * 2 * DIFF_DH)
    cache_dv = cache_diff_v[:, 0].reshape(B_LAT, PAST, DIFF_H * 2 * DIFF_DH)
    od_l = _diff_attn(qd, kd, vd, diff_lambda[0], dgn, lam_init, B_LAT, T_LAT, N_CTX,
                      cache=(cache_dk, cache_dv), rope=rope2)
    wo = w_out_even[0].astype(BF16)
    x = _moe([(oa_c, oa_l), (od_c, od_l)], [wo[:512], wo[512:]], x, norm_g[0], mod4, 0, w_router[0], b_router[0],
             w_gate_up, b_gate_up, w_down, b_down)
    new_dk = kd[:N_CTX].reshape(B_CTX, 1, T_CTX, DIFF_H, 2, DIFF_DH)
    new_dv = vd[:N_CTX].reshape(B_CTX, 1, T_CTX, DIFF_H, 2 * DIFF_DH)
    new_state = new_state.reshape(B_CTX, 1, 2, GDN_H, GDN_DK, GDN_DV)

    w_odd = jnp.pad(w_in_odd[0], ((0, 0), (0, ODD_W - (Q_LORA + KV_LORA + QK_ROPE)))).astype(BF16)
    wuq = w_uq[0].reshape(Q_LORA, MLA_H, QK_NOPE + QK_ROPE)
    wuq = jnp.pad(wuq, ((0, 0), (0, 0), (0, MLA_QW - (QK_NOPE + QK_ROPE)))).reshape(Q_LORA, MLA_H * MLA_QW)
    wukv = w_ukv[0].astype(BF16)
    q, kv, ckv, kpe = _in_odd(x, norm_g[1, 0:1], mod4, 1, w_odd, q_norm_g[0].reshape(1, Q_LORA),
                              wuq.astype(BF16), kv_norm_g[0].reshape(1, KV_LORA), wukv)
    om_c = _mla_attn(q, kv, kpe, B_CTX, T_CTX, 0)
    cache_kv = _matmul(cache_mla_ckv[:, 0].reshape(B_LAT * PAST, KV_LORA), wukv)
    cache_kpe = jnp.pad(cache_mla_kpe[:, 0].reshape(B_LAT * PAST, QK_ROPE), ((0, 0), (0, LANE - QK_ROPE)))
    om_l = _mla_attn(q, kv, kpe, B_LAT, T_LAT, N_CTX, cache=(cache_kv, cache_kpe), rope=rope1)
    x_ctx, x_lat = _moe([(om_c, om_l)], [w_out_odd[0].astype(BF16)], x, norm_g[1], mod4, 1, w_router[1],
                        b_router[1], w_gate_up, b_gate_up, w_down, b_down, split=True)
    new_ckv = ckv[:N_CTX].reshape(B_CTX, 1, T_CTX, KV_LORA)
    new_kpe = kpe[:N_CTX, :QK_ROPE].reshape(B_CTX, 1, T_CTX, QK_ROPE)

    return (x_ctx.reshape(B_CTX, T_CTX, D), x_lat.reshape(B_LAT, T_LAT, D),
            new_state, new_dk, new_dv, new_ckv, new_kpe)
```

```python
import functools
import math

import jax
import jax.numpy as jnp
from jax import lax
from jax.experimental import pallas as pl
from jax.experimental.pallas import tpu as pltpu

F32 = jnp.float32
BF16 = jnp.bfloat16
HIGHEST = lax.Precision.HIGHEST

D = 1024
B_CTX, T_CTX = 16, 256
B_LAT, T_LAT = 8, 1024
PAST = 256
N_CTX = B_CTX * T_CTX
N_LAT = B_LAT * T_LAT
N_TOK = N_CTX + N_LAT
DEPTH = 2
N_MOD = 6
GRID_W = 64
GDN_H, GDN_DK, GDN_DV = 4, 128, 128
CONV_W = 5
CHUNK = 64
DIFF_H, DIFF_DH = 4, 64
MLA_H, Q_LORA, KV_LORA, QK_NOPE, QK_ROPE, V_HEAD = 8, 256, 256, 128, 64, 128
N_EXP, TOP_K, D_FF = 32, 4, 1024
SWIGLU_LIMIT, SWIGLU_ALPHA = 7.0, 1.702
ROPE_BASE = 10000.0
EPS = 1e-6

LANE = 128
TM = 256
N_TILES = N_TOK // TM
SUB_ROWS = 16
TILE_ROWS = TOP_K * TM + N_EXP * SUB_ROWS
MB = 256
SUB = MB // SUB_ROWS
FF_CHUNK = 256
SORT_BLOCK = 256
GDN_TRIP_ROWS = 256
N_MACRO = -(-((TOP_K * N_TOK) // SUB_ROWS + N_TILES * N_EXP + N_EXP * (SUB - 1)) // SUB)
D_EXT = D + LANE
VMEM_LIMIT = 48 * 1024 * 1024


def _cparams(sem, vmem=VMEM_LIMIT):
    return pltpu.CompilerParams(dimension_semantics=sem, vmem_limit_bytes=vmem)


N_CTX_TILES = N_CTX // TM


def _mod_row(t):
    return jnp.where(t < N_CTX_TILES, 0, 1 + (t - N_CTX_TILES) // (T_LAT // TM))


def _ctx_spec(c):
    return pl.BlockSpec((TM, c), lambda t: (jnp.minimum(t, N_CTX_TILES - 1), 0))


def _lat_spec(c, first=0):
    return pl.BlockSpec((TM, c), lambda t: (jnp.maximum(t - N_CTX_TILES, 0) + first, 0))


def _token_pair(x):
    return (x[0], x[1], 0) if isinstance(x, tuple) else (x, x, N_CTX_TILES)


def _per_group(body, ctx_ref, lat_ref):
    t = pl.program_id(0)
    pl.when(t < N_CTX_TILES)(lambda: body(ctx_ref))
    pl.when(t >= N_CTX_TILES)(lambda: body(lat_ref))


def _rms(x, g):
    return x * lax.rsqrt(jnp.mean(x * x, axis=-1, keepdims=True) + EPS) * g


def _silu(x):
    return x * jax.nn.sigmoid(x)


def _dot(a, b):
    return jnp.dot(a, b, preferred_element_type=F32)


def _dot_nt(a, b):
    return lax.dot_general(a, b, (((1,), (1,)), ((), ())), preferred_element_type=F32)


def _mod_kernel(c_ref, w_ref, b_ref, o_ref):
    s = _silu(c_ref[...])
    o_ref[...] = _dot(s.astype(BF16), w_ref[...].astype(BF16)) + b_ref[...]


def _modulation(cond, w_mod, b_mod):
    nc = 1536
    return pl.pallas_call(
        _mod_kernel,
        grid=(DEPTH, N_MOD * D // nc),
        in_specs=[pl.BlockSpec((16, D), lambda i, j: (0, 0)),
                  pl.BlockSpec((None, D, nc), lambda i, j: (i, 0, j)),
                  pl.BlockSpec((None, 1, nc), lambda i, j: (i, 0, j))],
        out_specs=pl.BlockSpec((None, 16, nc), lambda i, j: (i, 0, j)),
        out_shape=jax.ShapeDtypeStruct((DEPTH, 16, N_MOD * D), F32),
        compiler_params=_cparams(("arbitrary", "arbitrary")),
        name="modulation",
    )(cond, w_mod, b_mod.reshape(DEPTH, 1, N_MOD * D))


def _mod_spec(layer, k):
    return pl.BlockSpec((None, None, 1, D), lambda t: (layer, _mod_row(t), 0, k))


EVEN_W = 2 * 512 + 512 + 512 + 3 * 512 + LANE


def _in_even_kernel(xc_ref, xl_ref, g_ref, sh_ref, sc_ref, w_ref, qkv_ref, z_ref, qd_ref, kd_ref, vd_ref, ab_ref):
    def body(x_ref):
        h = _rms(x_ref[...], g_ref[...]) * (1.0 + sc_ref[...]) + sh_ref[...]
        y = _dot(h.astype(BF16), w_ref[...])
        qkv_ref[...] = y[:, 0:1536]
        z_ref[...] = y[:, 1536:2048]
        qd_ref[...] = y[:, 2048:2560]
        kd_ref[...] = y[:, 2560:3072]
        vd_ref[...] = y[:, 3072:3584]
        ab_ref[...] = y[:, 3584:3712]

    _per_group(body, xc_ref, xl_ref)


def _in_even(x, g, mod4, layer, w):
    xc, xl, first = _token_pair(x)
    row = lambda c: pl.BlockSpec((TM, c), lambda t: (t, 0))
    shp = lambda c: jax.ShapeDtypeStruct((N_TOK, c), F32)
    return pl.pallas_call(
        _in_even_kernel,
        grid=(N_TILES,),
        in_specs=[_ctx_spec(D), _lat_spec(D, first), pl.BlockSpec((1, D), lambda t: (0, 0)), _mod_spec(layer, 0),
                  _mod_spec(layer, 1), pl.BlockSpec((D, EVEN_W), lambda t: (0, 0))],
        out_specs=[row(1536), row(512), row(512), row(512), row(512), row(LANE)],
        out_shape=[shp(1536), shp(512), shp(512), shp(512), shp(512), shp(LANE)],
        compiler_params=_cparams(("arbitrary",)),
        name="in_proj_even",
    )(xc, xl, g, mod4, mod4, w)


def _split_bf16(a):
    hi = a.astype(BF16)
    lo = (a - hi.astype(F32)).astype(BF16)
    return hi, lo


def _mm3(a, b):
    m = a[0].shape[0]
    top = _dot(jnp.concatenate([a[0], a[1]], axis=0), b[0])
    return top[:m] + (top[m:] + _dot(a[0], b[1]))


def _unit_tri_inverses(ls, eye, blk, bd):
    sp = _split_bf16
    d8 = [jnp.where(blk(8), l, 0.0) for l in ls]
    d8s = [sp(d) for d in d8]
    x2s = [sp(_mm3(d, bd(d))) for d in d8s]
    x2r = [bd(x) for x in x2s]
    ts = [eye - d for d in d8]
    ts = [t + _mm3(sp(t), xr) for t, xr in zip(ts, x2r)]
    x4 = [_mm3(x, xr) for x, xr in zip(x2s, x2r)]
    ts = [t + _mm3(sp(t), bd(sp(x))) for t, x in zip(ts, x4)]
    for s in (8, 16, 32):
        offs = [sp(jnp.where(blk(2 * s) & jnp.logical_not(blk(s)), l, 0.0)) for l in ls]
        tss = [sp(t) for t in ts]
        ys = [_mm3(o, bd(t)) for o, t in zip(offs, tss)]
        ts = [t - _mm3(t2, bd(sp(y))) for t, t2, y in zip(ts, tss, ys)]
    return ts


def _softplus(x):
    return jnp.maximum(x, 0.0) + jnp.log1p(jnp.exp(-jnp.abs(x)))


def _gdn_kernel(has_init, T, alog_ref, dtb_ref, qkv_ref, cw_ref, z_ref, ab_ref, gn_ref, *rest):
    if has_init:
        s0_ref, o_ref = rest[:2]
        st_ref = None
    else:
        o_ref, st_ref = rest[:2]
        s0_ref = None
    xp, qs, ks, vs, abt_s, uo_s, w_s, qg_s, kgt_s, in_s, gl_s, st_s = rest[2:]
    n = T // CHUNK
    H = GDN_H

    xp[0:8, :] = jnp.zeros((8, LANE), F32)
    xp[T + 8:T + 16, :] = jnp.zeros((8, LANE), F32)

    def l2n(x):
        return x * lax.rsqrt(jnp.sum(x * x, axis=-1, keepdims=True) + EPS)

    for blk in range(3 * H):
        cols = slice(blk * LANE, (blk + 1) * LANE)
        dst = slice((blk % H) * LANE, (blk % H + 1) * LANE)
        xp[8:T + 8, :] = qkv_ref[:, cols]
        w = cw_ref[:, cols]
        acc = xp[6:T + 6, :] * w[0:1, :]
        for i in range(1, CONV_W):
            acc = acc + xp[6 + i:T + 6 + i, :] * w[i:i + 1, :]
        y = _silu(acc)
        if blk < H:
            qs[:, dst] = l2n(y) * (GDN_DK ** -0.5)
        elif blk < 2 * H:
            ks[:, dst] = l2n(y)
        else:
            vs[:, dst] = y
    abt_s[...] = ab_ref[...].T[0:16, :]

    ii = lax.broadcasted_iota(jnp.int32, (CHUNK, LANE), 0)
    lane = lax.broadcasted_iota(jnp.int32, (CHUNK, LANE), 1)
    jm = lane & (CHUNK - 1)
    fwd = lane < CHUNK
    order = jnp.where(fwd, ii - jm, jm - ii)
    incl = order >= 0
    strict = order > 0
    incl_t = order <= 0
    eye = jnp.where(order == 0, 1.0, 0.0).astype(F32)
    blk_mask = lambda s: (ii // s) == (jm // s)
    zeros_c = jnp.zeros((CHUNK, LANE), F32)
    zeros_c16 = jnp.zeros((CHUNK, LANE), BF16)

    keep_f = jnp.where(fwd, 1.0, 0.0).astype(BF16)
    keep_b = jnp.where(fwd, 0.0, 1.0).astype(BF16)

    def bd(parts):
        return tuple(jnp.concatenate([a * keep_f, a * keep_b], axis=0) for a in parts)

    lane_row = lax.broadcasted_iota(jnp.int32, (1, LANE), 1)
    sub_col = lax.broadcasted_iota(jnp.int32, (16, 1), 0)
    a_lane, dt_lane = jnp.zeros((1, LANE), F32), jnp.zeros((1, LANE), F32)
    a_col, dt_col = jnp.zeros((16, 1), F32), jnp.zeros((16, 1), F32)
    for d in range(2):
        for h in range(H):
            a_lane = jnp.where(lane_row == d * H + h, alog_ref[d, h], a_lane)
            dt_lane = jnp.where(lane_row == d * H + h, dtb_ref[d, h], dt_lane)
            a_col = jnp.where(sub_col == d * H + h, alog_ref[d, h], a_col)
            dt_col = jnp.where(sub_col == d * H + h, dtb_ref[d, h], dt_col)
    nega_lane, nega_col = -jnp.exp(a_lane), -jnp.exp(a_col)

    def prep(p, carry):
        rbase = pl.multiple_of(p * GDN_TRIP_ROWS, GDN_TRIP_ROWS)
        g_rows = nega_col * _softplus(abt_s[:, pl.ds(rbase, GDN_TRIP_ROWS)] + dt_col)
        sysm = []
        for cc in range(GDN_TRIP_ROWS // CHUNK):
            r0 = pl.multiple_of(rbase + cc * CHUNK, CHUNK)
            abc = ab_ref[pl.ds(r0, CHUNK), :]
            g_cols = nega_lane * _softplus(abc + dt_lane)
            b_cols = jax.nn.sigmoid(abc)
            for h in range(H):
                hs = slice(h * LANE, (h + 1) * LANE)
                tok = slice(cc * CHUNK, (cc + 1) * CHUNK)
                sysm.append(dict(
                    c=(GDN_TRIP_ROWS // CHUNK) * p + cc, rows=pl.ds(r0, CHUNK), h=h,
                    qc=qs[pl.ds(r0, CHUNK), hs], kc=ks[pl.ds(r0, CHUNK), hs], vc=vs[pl.ds(r0, CHUNK), hs],
                    g_col=jnp.where(fwd, g_cols[:, h:h + 1], g_cols[:, H + h:H + h + 1]),
                    g_row=jnp.concatenate([g_rows[h:h + 1, tok], g_rows[H + h:H + h + 1, tok]], axis=1),
                    beta=(b_cols[:, 2 * H + h:2 * H + h + 1], b_cols[:, 3 * H + h:3 * H + h + 1])))
        for s in sysm:
            k16 = s["kc"].astype(BF16)
            kk16 = jnp.concatenate([k16, k16], axis=0)
            kq = _dot_nt(jnp.concatenate([k16, s["qc"].astype(BF16)], axis=0), kk16)
            s["kk"], s["qk"] = kq[:CHUNK], kq[CHUNK:]
        ls = []
        for s in sysm:
            m = jnp.where(incl, s["g_row"], 0.0)
            s["gcf"] = jnp.sum(jnp.where(fwd, m, 0.0), axis=1, keepdims=True)
            s["gcb"] = jnp.sum(jnp.where(fwd, 0.0, m), axis=1, keepdims=True)
            gc_col = jnp.where(fwd, s["gcf"], s["gcb"])
            gc_row = jnp.sum(jnp.where(incl_t, s["g_col"], 0.0), axis=0, keepdims=True)
            decay = jnp.where(incl, jnp.exp(jnp.where(incl, gc_col - gc_row, 0.0)), 0.0)
            beta = jnp.where(fwd, s["beta"][0], s["beta"][1])
            ls.append(jnp.where(strict, s["kk"] * beta * decay, 0.0))
            s["intra"] = s["qk"] * decay
        ts = _unit_tri_inverses(ls, eye, blk_mask, bd)
        for s, t in zip(sysm, ts):
            h, c, rows = s["h"], s["c"], s["rows"]
            egs = (jnp.exp(s["gcf"]), jnp.exp(s["gcb"]))
            r_f = jnp.concatenate([s["vc"] * s["beta"][0], s["kc"] * (s["beta"][0] * egs[0]), zeros_c, zeros_c], axis=1)
            r_b = jnp.concatenate([zeros_c, zeros_c, s["vc"] * s["beta"][1], s["kc"] * (s["beta"][1] * egs[1])], axis=1)
            uw = _dot(t.astype(BF16), jnp.concatenate([r_f, r_b], axis=0).astype(BF16))
            g_last = (s["gcf"][CHUNK - 1:CHUNK, :], s["gcb"][0:1, :])
            kgs = []
            for d in range(2):
                k = d * H + h
                uo_s[k, rows, :] = uw[:, 2 * d * LANE:(2 * d + 1) * LANE]
                w_s[k, rows, :] = uw[:, (2 * d + 1) * LANE:(2 * d + 2) * LANE].astype(BF16)
                qg_s[k, rows, :] = (s["qc"] * egs[d]).astype(BF16)
                gc = s["gcf"] if d == 0 else s["gcb"]
                kgs.append(s["kc"] * jnp.exp(g_last[d] - gc))
                gl_s[k, c] = jnp.broadcast_to(jnp.exp(g_last[d]), (8, LANE))
            in_s[h, c] = s["intra"].astype(BF16)
            kgt_s[h, c] = jnp.concatenate(kgs, axis=0).T.astype(BF16)
        return carry

    lax.fori_loop(0, T // GDN_TRIP_ROWS, prep, 0)

    for d in range(2):
        for h in range(H):
            st_s[d * H + h] = s0_ref[d, h] if has_init else jnp.zeros((GDN_DK, GDN_DV), F32)

    def scan(i, carry):
        cs = (i, n - 1 - i)
        ks_ = range(2 * H)
        c_of = [cs[k // H] for k in ks_]
        rows = [pl.ds(pl.multiple_of(c * CHUNK, CHUNK), CHUNK) for c in c_of]
        s16 = [st_s[k].astype(BF16) for k in ks_]
        wq = [_dot(jnp.concatenate([w_s[k, rows[k], :], qg_s[k, rows[k], :]], axis=0), s16[k]) for k in ks_]
        vn = [(uo_s[k, rows[k], :] - wq[k][:CHUNK]).astype(BF16) for k in ks_]
        vpad = [jnp.concatenate([vn[k], zeros_c16] if k < H else [zeros_c16, vn[k]], axis=0) for k in ks_]
        ik = [_dot(jnp.concatenate([in_s[k % H, c_of[k]], kgt_s[k % H, c_of[k]]], axis=0), vpad[k]) for k in ks_]
        for k in ks_:
            uo_s[k, rows[k], :] = wq[k][CHUNK:] + ik[k][:CHUNK]
            st_s[k] = st_s[k] * gl_s[k, c_of[k]][0:1, :] + ik[k][CHUNK:]
        return carry

    lax.fori_loop(0, n, scan, 0)
    for h in range(H):
        hs = slice(h * LANE, (h + 1) * LANE)
        o_ref[:, hs] = _rms(uo_s[h] + uo_s[H + h], gn_ref[...]) * _silu(z_ref[:, hs])
        if st_ref is not None:
            st_ref[0, h] = st_s[h]
            st_ref[1, h] = st_s[H + h]


def _gdn(qkv, z, ab, conv_w, a_log, dt_bias, gn, s0, nb, T, row0):
    n = T // CHUNK
    blk0 = row0 // T
    has_init = s0 is not None
    H, W = GDN_H, GDN_H * LANE
    smem = pl.BlockSpec(memory_space=pltpu.SMEM)
    in_specs = [smem, smem,
                pl.BlockSpec((T, 3 * W), lambda b: (blk0 + b, 0), pipeline_mode=pl.Buffered(1)),
                pl.BlockSpec((CONV_W, 3 * W), lambda b: (0, 0)),
                pl.BlockSpec((T, W), lambda b: (blk0 + b, 0), pipeline_mode=pl.Buffered(1)),
                pl.BlockSpec((T, LANE), lambda b: (blk0 + b, 0)),
                pl.BlockSpec((1, LANE), lambda b: (0, 0))]
    args = [a_log, dt_bias, qkv, conv_w, z, ab, gn]
    st_spec = pl.BlockSpec((None, 2, H, GDN_DK, GDN_DV), lambda b: (b, 0, 0, 0, 0))
    o_spec = pl.BlockSpec((T, W), lambda b: (b, 0))
    o_shape = jax.ShapeDtypeStruct((nb * T, W), F32)
    if has_init:
        in_specs.append(st_spec)
        args.append(s0)
        out_specs, out_shape = o_spec, o_shape
    else:
        out_specs = [o_spec, st_spec]
        out_shape = [o_shape, jax.ShapeDtypeStruct((nb, 2, H, GDN_DK, GDN_DV), F32)]
    scratch = [pltpu.VMEM((T + 16, LANE), F32), pltpu.VMEM((T, W), F32), pltpu.VMEM((T, W), F32),
               pltpu.VMEM((T, W), F32), pltpu.VMEM((16, T), F32),
               pltpu.VMEM((2 * H, T, LANE), F32), pltpu.VMEM((2 * H, T, LANE), BF16),
               pltpu.VMEM((2 * H, T, LANE), BF16), pltpu.VMEM((H, n, LANE, LANE), BF16),
               pltpu.VMEM((H, n, CHUNK, LANE), BF16), pltpu.VMEM((2 * H, n, 8, LANE), F32),
               pltpu.VMEM((2 * H, GDN_DK, GDN_DV), F32)]
    return pl.pallas_call(
        functools.partial(_gdn_kernel, has_init, T),
        grid=(nb,),
        in_specs=in_specs, out_specs=out_specs, out_shape=out_shape, scratch_shapes=scratch,
        compiler_params=_cparams(("arbitrary",), 56 * 1024 * 1024),
        name="gdn_lat" if has_init else "gdn_ctx",
    )(*args)


def _rope(x, cos, sin_signed):
    lane = lax.broadcasted_iota(jnp.int32, x.shape, 1)
    up = pltpu.roll(x, LANE - 16, axis=1)
    dn = pltpu.roll(x, 16, axis=1)
    rot = jnp.where((lane % 32) < 16, up, dn)
    return x * cos + rot * sin_signed


def _rope_tables():
    rows = T_LAT // GRID_W
    r = jnp.repeat(jnp.arange(rows, dtype=F32), GRID_W)
    col = jnp.tile(jnp.arange(GRID_W, dtype=F32), rows)
    quarter = 16
    inv = ROPE_BASE ** (-jnp.arange(quarter, dtype=F32) / quarter)
    ar = r[:, None] * inv
    ac = col[:, None] * inv
    ang = jnp.concatenate([ar, ar, ac, ac], axis=-1)
    sign = jnp.tile(jnp.concatenate([-jnp.ones(16, F32), jnp.ones(16, F32)]), 2)
    cos, sin = jnp.cos(ang), jnp.sin(ang) * sign
    return cos, sin


LOG2E = math.log2(math.e)


def _attend(parts, values):
    m = functools.reduce(jnp.maximum, [jnp.max(s, axis=-1, keepdims=True) for s in parts])
    es = [jnp.exp2(s - m) for s in parts]
    den = functools.reduce(lambda a, b: a + b, [jnp.sum(e, axis=-1, keepdims=True) for e in es])
    o = functools.reduce(lambda a, b: a + b, [_dot(e.astype(BF16), v) for e, v in zip(es, values)])
    return o * (1.0 / den)


def _diff_kernel(latent, T, lam_init, q_ref, k_ref, v_ref, lam_ref, gn_ref, *rest):
    if latent:
        cos_ref, sin_ref, ck_ref, cv_ref, o_ref = rest
    else:
        (o_ref,) = rest
    scale = DIFF_DH ** -0.5
    lp = lam_ref[...]
    lam = (jnp.exp(jnp.sum(lp[0:1] * lp[1:2], axis=-1, keepdims=True))
           - jnp.exp(jnp.sum(lp[2:3] * lp[3:4], axis=-1, keepdims=True)) + lam_init)
    tq = 256
    lane = lax.broadcasted_iota(jnp.int32, (tq, LANE), 1)
    for hh in range(q_ref.shape[1] // LANE):
        cs = slice(hh * LANE, (hh + 1) * LANE)
        k = k_ref[:, cs]
        if latent:
            k = _rope(k, cos_ref[...], sin_ref[...])
        k16 = k.astype(BF16)
        v16 = v_ref[:, cs].astype(BF16)
        if latent:
            ck16 = ck_ref[:, cs].astype(BF16)
            cv16 = cv_ref[:, cs].astype(BF16)
        def scores(qb):
            rows = slice(qb * tq, (qb + 1) * tq)
            q = q_ref[rows, cs]
            if latent:
                q = _rope(q, cos_ref[rows, :], sin_ref[rows, :])
            qs = q * (scale * LOG2E)
            q2 = jnp.concatenate([jnp.where((lane // DIFF_DH) == m, qs, 0.0) for m in range(2)], axis=0)
            q2 = q2.astype(BF16)
            return [_dot_nt(q2, ck16), _dot_nt(q2, k16)] if latent else [_dot_nt(q2, k16)]

        nxt = scores(0)
        for qb in range(T // tq):
            parts = nxt
            if qb + 1 < T // tq:
                nxt = scores(qb + 1)
            o2 = _attend(parts, [cv16, v16] if latent else [v16])
            o = o2[:tq] - lam * o2[tq:]
            o_ref[qb * tq:(qb + 1) * tq, cs] = _rms(o, gn_ref[...]) * (1.0 - lam_init)


def _diff_attn(qd, kd, vd, lam_p, gn, lam_init, nb, T, row0, cache=None, rope=None):
    blk0 = row0 // T
    latent = cache is not None
    hps = 2 if latent else DIFF_H
    hcol = pl.BlockSpec((T, hps * LANE), lambda b, h: (blk0 + b, h))
    in_specs = [hcol, hcol, hcol, pl.BlockSpec((4, DIFF_DH), lambda b, h: (0, 0)),
                pl.BlockSpec((1, LANE), lambda b, h: (0, 0))]
    args = [qd, kd, vd, lam_p, gn]
    if latent:
        tab = pl.BlockSpec((T, LANE), lambda b, h: (0, 0))
        cch = pl.BlockSpec((None, PAST, hps * LANE), lambda b, h: (b, 0, h))
        in_specs += [tab, tab, cch, cch]
        args += [rope[0], rope[1], cache[0], cache[1]]
    return pl.pallas_call(
        functools.partial(_diff_kernel, latent, T, lam_init),
        grid=(nb, DIFF_H // hps),
        in_specs=in_specs,
        out_specs=pl.BlockSpec((T, hps * LANE), lambda b, h: (b, h)),
        out_shape=jax.ShapeDtypeStruct((nb * T, DIFF_H * 2 * DIFF_DH), F32),
        compiler_params=_cparams(("arbitrary", "arbitrary")),
        name="diff_lat" if latent else "diff_ctx",
    )(*args)


ODD_W = Q_LORA + KV_LORA + LANE
MLA_QW = 2 * LANE


def _in_odd_kernel(x_ref, g_ref, sh_ref, sc_ref, w_ref, qg_ref, wuq_ref, kvg_ref, wukv_ref,
                   q_ref, kv_ref, ckv_ref, kpe_ref):
    h = _rms(x_ref[...], g_ref[...]) * (1.0 + sc_ref[...]) + sh_ref[...]
    y = _dot(h.astype(BF16), w_ref[...])
    qn = _rms(y[:, 0:Q_LORA], qg_ref[...])
    q_ref[...] = _dot(qn.astype(BF16), wuq_ref[...])
    ckv = _rms(y[:, Q_LORA:Q_LORA + KV_LORA], kvg_ref[...])
    ckv_ref[...] = ckv
    kv_ref[...] = _dot(ckv.astype(BF16), wukv_ref[...])
    kpe_ref[...] = y[:, Q_LORA + KV_LORA:ODD_W]


def _in_odd(x, g, mod4, layer, w, qg, wuq, kvg, wukv):
    row = lambda c: pl.BlockSpec((TM, c), lambda t: (t, 0))
    full = lambda a: pl.BlockSpec(a.shape, lambda t: (0, 0))
    shp = lambda c: jax.ShapeDtypeStruct((N_TOK, c), F32)
    return pl.pallas_call(
        _in_odd_kernel,
        grid=(N_TILES,),
        in_specs=[row(D), full(g), _mod_spec(layer, 0), _mod_spec(layer, 1), full(w), full(qg), full(wuq),
                  full(kvg), full(wukv)],
        out_specs=[row(MLA_H * MLA_QW), row(MLA_H * MLA_QW), row(KV_LORA), row(LANE)],
        out_shape=[shp(MLA_H * MLA_QW), shp(MLA_H * MLA_QW), shp(KV_LORA), shp(LANE)],
        compiler_params=_cparams(("arbitrary",)),
        name="in_proj_odd",
    )(x, g, mod4, mod4, w, qg, wuq, kvg, wukv)


def _mm_kernel(x_ref, w_ref, o_ref):
    o_ref[...] = _dot(x_ref[...].astype(BF16), w_ref[...])


def _matmul(x, w):
    m, k = x.shape
    n = w.shape[1]
    return pl.pallas_call(
        _mm_kernel,
        grid=(m // TM,),
        in_specs=[pl.BlockSpec((TM, k), lambda i: (i, 0)), pl.BlockSpec((k, n), lambda i: (0, 0))],
        out_specs=pl.BlockSpec((TM, n), lambda i: (i, 0)),
        out_shape=jax.ShapeDtypeStruct((m, n), F32),
        compiler_params=_cparams(("arbitrary",)),
        name="cache_kv_proj",
    )(x, w)


def _mla_kernel(latent, T, q_ref, kv_ref, kpe_ref, *rest):
    if latent:
        cos_ref, sin_ref, ckv_ref, ckpe_ref, o_ref = rest
    else:
        (o_ref,) = rest
    scale = (QK_NOPE + QK_ROPE) ** -0.5
    kpe = kpe_ref[...]
    if latent:
        kpe = _rope(kpe, cos_ref[...], sin_ref[...])
    tq = 256
    for hh in range(q_ref.shape[1] // MLA_QW):
        cs = slice(hh * MLA_QW, (hh + 1) * MLA_QW)
        kv = kv_ref[:, cs]
        v16 = kv[:, LANE:].astype(BF16)
        if latent:
            ckv = ckv_ref[:, cs]
            ck16 = jnp.concatenate([ckv[:, :LANE], ckpe_ref[...]], axis=1).astype(BF16)
            cv16 = ckv[:, LANE:].astype(BF16)
        k16 = jnp.concatenate([kv[:, :LANE], kpe], axis=1).astype(BF16)

        def scores(qb):
            rows = slice(qb * tq, (qb + 1) * tq)
            q = q_ref[rows, cs]
            qp = q[:, LANE:]
            if latent:
                qp = _rope(qp, cos_ref[rows, :], sin_ref[rows, :])
            q16 = (jnp.concatenate([q[:, :LANE], qp], axis=1) * (scale * LOG2E)).astype(BF16)
            return [_dot_nt(q16, ck16), _dot_nt(q16, k16)] if latent else [_dot_nt(q16, k16)]

        nxt = scores(0)
        for qb in range(T // tq):
            parts = nxt
            if qb + 1 < T // tq:
                nxt = scores(qb + 1)
            o_ref[qb * tq:(qb + 1) * tq, hh * V_HEAD:(hh + 1) * V_HEAD] = _attend(
                parts, [cv16, v16] if latent else [v16])


def _mla_attn(q, kv, kpe, nb, T, row0, cache=None, rope=None):
    blk0 = row0 // T
    latent = cache is not None
    hps = 2 if latent else 4
    head = pl.BlockSpec((T, hps * MLA_QW), lambda b, h: (blk0 + b, h))
    in_specs = [head, head, pl.BlockSpec((T, LANE), lambda b, h: (blk0 + b, 0))]
    args = [q, kv, kpe]
    if latent:
        tab = pl.BlockSpec((T, LANE), lambda b, h: (0, 0))
        in_specs += [tab, tab, pl.BlockSpec((PAST, hps * MLA_QW), lambda b, h: (b, h)),
                     pl.BlockSpec((PAST, LANE), lambda b, h: (b, 0))]
        args += [rope[0], rope[1], cache[0], cache[1]]
    return pl.pallas_call(
        functools.partial(_mla_kernel, latent, T),
        grid=(nb, MLA_H // hps),
        in_specs=in_specs,
        out_specs=pl.BlockSpec((T, hps * V_HEAD), lambda b, h: (b, h)),
        out_shape=jax.ShapeDtypeStruct((nb * T, MLA_H * V_HEAD), F32),
        compiler_params=_cparams(("arbitrary", "arbitrary")),
        name="mla_lat" if latent else "mla_ctx",
    )(*args)


def _route_kernel(n_mix, *refs):
    ctx_refs = refs[:n_mix]
    lat_refs = refs[n_mix:2 * n_mix]
    w_refs = refs[2 * n_mix:3 * n_mix]
    (xc_ref, xl_ref, g1_ref, gate1_ref, g2_ref, sh2_ref, sc2_ref, wrh_ref, wrl_ref, br_ref,
     x1_ref, hx_ref, pw_ref, pwt_ref, cnt_ref, y_s, x_s) = refs[3 * n_mix:]

    def out_proj(group):
        mix_refs, x_ref = group
        y = _dot(mix_refs[0][...].astype(BF16), w_refs[0][...])
        for m in range(1, n_mix):
            y = y + _dot(mix_refs[m][...].astype(BF16), w_refs[m][...])
        y_s[...] = y
        x_s[...] = x_ref[...]

    _per_group(out_proj, (ctx_refs, xc_ref), (lat_refs, xl_ref))
    x1 = x_s[...] + gate1_ref[...] * _rms(y_s[...], g1_ref[...])
    x1_ref[...] = x1
    h2 = _rms(x1, g2_ref[...]) * (1.0 + sc2_ref[...]) + sh2_ref[...]

    logits = _mm3(_split_bf16(h2), (wrh_ref[...], wrl_ref[...])) + br_ref[...]
    lane_i = lax.broadcasted_iota(jnp.int32, (TM, LANE), 1)
    lane = lane_i.astype(F32)
    neg = jnp.float32(-jnp.inf)
    l = jnp.where(lane_i < N_EXP, logits, neg)
    sel = jnp.zeros((TM, LANE), F32)
    comb = jnp.zeros((TM, LANE), F32)
    wsum = jnp.zeros((TM, 1), F32)
    hots, wts = [], []
    for k in range(TOP_K):
        m = jnp.max(l, axis=-1, keepdims=True)
        idx = jnp.min(jnp.where(l == m, lane, float(LANE)), axis=-1, keepdims=True)
        hot = lane == idx
        if k == 0:
            m0 = m
        w = jnp.exp(m - m0)
        wsum = wsum + w
        hots.append(hot)
        wts.append(w)
        sel = sel + jnp.where(hot, 1.0, 0.0)
        comb = comb + jnp.where(hot, w, 0.0)
        l = jnp.where(hot, neg, l)
    inv = 1.0 / wsum
    comb = comb * inv

    cnt = jnp.sum(sel, axis=0, keepdims=True)
    cnt_ref[...] = cnt
    padded = jnp.floor((cnt + (SUB_ROWS - 1)) * (1.0 / SUB_ROWS)) * SUB_ROWS
    li = lax.broadcasted_iota(jnp.int32, (LANE, LANE), 0)
    lj = lax.broadcasted_iota(jnp.int32, (LANE, LANE), 1)
    before = jnp.where(li < lj, 1.0, 0.0).astype(BF16)
    start = _dot(jnp.broadcast_to(padded, (8, LANE)).astype(BF16), before)[0:1, :]
    ti = lax.broadcasted_iota(jnp.int32, (TM, TM), 0)
    tj = lax.broadcasted_iota(jnp.int32, (TM, TM), 1)
    earlier = jnp.where(tj < ti, 1.0, 0.0).astype(BF16)
    pos_full = _dot(earlier, sel.astype(BF16)) + start
    pw = jnp.zeros((TM, LANE), F32)
    for k in range(TOP_K):
        pos_k = jnp.sum(jnp.where(hots[k], pos_full, 0.0), axis=-1, keepdims=True)
        pw = pw + jnp.where(lane_i == k, pos_k, 0.0) + jnp.where(lane_i == TOP_K + k, wts[k] * inv, 0.0)
    pw_ref[...] = pw
    pwt_ref[...] = pw.T[0:8, :]

    chi, clo = _split_bf16(comb)
    ext = chi.astype(F32) + pltpu.roll(clo.astype(F32), N_EXP, axis=1)
    hx_ref[:, 0:D] = h2.astype(BF16)
    hx_ref[:, D:D_EXT] = ext.astype(BF16)


def _route(mixes, w_outs, x, norm_g, mod4, layer, w_router, b_router):
    n_mix = len(mixes)
    wr_hi, wr_lo = _split_bf16(w_router)
    row = lambda c: pl.BlockSpec((TM, c), lambda t: (t, 0))
    full = lambda a: pl.BlockSpec(a.shape, lambda t: (0, 0))
    g = lambda k: pl.BlockSpec((None, 1, D), lambda t: (k, 0, 0))
    xc, xl, first = _token_pair(x)
    in_specs = ([_ctx_spec(m[0].shape[1]) for m in mixes] + [_lat_spec(m[1].shape[1]) for m in mixes]
                + [full(w) for w in w_outs]
                + [_ctx_spec(D), _lat_spec(D, first), g(1), _mod_spec(layer, 2), g(2), _mod_spec(layer, 3),
                   _mod_spec(layer, 4), full(wr_hi), full(wr_lo), full(b_router)])
    ng = norm_g.reshape(4, 1, D)
    return pl.pallas_call(
        functools.partial(_route_kernel, n_mix),
        grid=(N_TILES,),
        in_specs=in_specs,
        out_specs=[row(D), row(D_EXT), row(LANE), pl.BlockSpec((None, 8, TM), lambda t: (t, 0, 0)),
                   pl.BlockSpec((None, 1, LANE), lambda t: (t, 0, 0))],
        out_shape=[jax.ShapeDtypeStruct((N_TOK, D), F32), jax.ShapeDtypeStruct((N_TOK, D_EXT), BF16),
                   jax.ShapeDtypeStruct((N_TOK, LANE), F32), jax.ShapeDtypeStruct((N_TILES, 8, TM), F32),
                   jax.ShapeDtypeStruct((N_TILES, 1, LANE), F32)],
        scratch_shapes=[pltpu.VMEM((TM, D), F32), pltpu.VMEM((TM, D), F32)],
        compiler_params=_cparams(("arbitrary",)),
        name="out_proj_route",
    )(*[m[0] for m in mixes], *[m[1] for m in mixes], *w_outs, xc, xl, ng, mod4, ng, mod4, mod4, wr_hi, wr_lo,
      b_router)


def _dispatch_kernel(hx_ref, pwt_ref, s_ref):
    j = pl.program_id(0)

    @pl.when(j < N_TILES)
    def _():
        rio = lax.broadcasted_iota(jnp.int32, (SORT_BLOCK, TM), 0).astype(F32)

        def one_hot(rb):
            p = jnp.zeros((SORT_BLOCK, TM), F32)
            for k in range(TOP_K):
                p = p + jnp.where(rio == pwt_ref[k:k + 1, :] - float(rb * SORT_BLOCK), 1.0, 0.0)
            return p.astype(BF16)

        nxt = one_hot(0)
        for rb in range(TILE_ROWS // SORT_BLOCK):
            p = nxt
            if rb + 1 < TILE_ROWS // SORT_BLOCK:
                nxt = one_hot(rb + 1)
            s_ref[rb * SORT_BLOCK:(rb + 1) * SORT_BLOCK, :] = _dot(p, hx_ref[...]).astype(BF16)

    @pl.when(j == N_TILES)
    def _():
        s_ref[...] = jnp.zeros((TILE_ROWS, D_EXT), BF16)


def _dispatch(hx, pwt):
    clamp = lambda j: jnp.minimum(j, N_TILES - 1)
    return pl.pallas_call(
        _dispatch_kernel,
        grid=(N_TILES + 1,),
        in_specs=[pl.BlockSpec((TM, D_EXT), lambda j: (clamp(j), 0)),
                  pl.BlockSpec((None, 8, TM), lambda j: (clamp(j), 0, 0))],
        out_specs=pl.BlockSpec((TILE_ROWS, D_EXT), lambda j: (j, 0)),
        out_shape=jax.ShapeDtypeStruct(((N_TILES + 1) * TILE_ROWS, D_EXT), BF16),
        compiler_params=_cparams(("arbitrary",)),
        name="moe_dispatch",
    )(hx, pwt)


def _expert_kernel(layer, blk_e_ref, next_e_ref, src_ref, nact_ref, tail_ref, s_hbm, wgu_hbm, bgu_ref, wd_hbm,
                   bd_ref, y_hbm, xbuf, ybuf, wgu_f, wd_f, wgu_s, wd_s, zbuf, gsem, ssem, zsem, wsem):
    i = pl.program_id(0)
    nsteps = pl.num_programs(0)
    nact = nact_ref[0]
    slot = i % 2

    def gather_copy(step, sl, s):
        src = pl.multiple_of(src_ref[step * SUB + s], SUB_ROWS)
        return pltpu.make_async_copy(s_hbm.at[pl.ds(src, SUB_ROWS)],
                                     xbuf.at[sl, pl.ds(s * SUB_ROWS, SUB_ROWS)], gsem.at[sl])

    def scatter_copy(step, sl, s):
        dst = pl.multiple_of(src_ref[step * SUB + s], SUB_ROWS)
        return pltpu.make_async_copy(ybuf.at[sl, pl.ds(s * SUB_ROWS, SUB_ROWS)],
                                     y_hbm.at[pl.ds(dst, SUB_ROWS)], ssem.at[sl])

    def gather_start(step, sl):
        for s in range(SUB):
            gather_copy(step, sl, s).start()

    def gather_wait(step, sl):
        for s in range(SUB):
            gather_copy(step, sl, s).wait()

    def scatter_start(step, sl):
        for s in range(SUB):
            scatter_copy(step, sl, s).start()

    def scatter_wait(step, sl):
        for s in range(SUB):
            scatter_copy(step, sl, s).wait()

    def spare_fill():
        copies = [pltpu.make_async_copy(zbuf, y_hbm.at[pl.ds(N_TILES * TILE_ROWS + b * SUB_ROWS, SUB_ROWS)],
                                        zsem.at[1]) for b in range(2 * SUB)]
        for cp in copies:
            cp.start()
        for cp in copies:
            cp.wait()

    def tail_fill(wait):
        def tile(j, carry):
            first = tail_ref[j]

            def block(b, c):
                dst = pl.multiple_of(first + b * SUB_ROWS, SUB_ROWS)
                cp = pltpu.make_async_copy(zbuf, y_hbm.at[pl.ds(dst, SUB_ROWS)], zsem.at[0])
                if wait:
                    cp.wait()
                else:
                    cp.start()
                return c

            return lax.fori_loop(0, ((j + 1) * TILE_ROWS - first) // SUB_ROWS, block, carry)

        lax.fori_loop(0, N_TILES, tile, 0)

    def weight_copies(e, wsl):
        return (pltpu.make_async_copy(wgu_hbm.at[layer, e], wgu_f.at[wsl], wsem.at[wsl]),
                pltpu.make_async_copy(wd_hbm.at[layer, e], wd_f.at[wsl], wsem.at[wsl]))

    @pl.when(i == 0)
    def _():
        for cp in weight_copies(blk_e_ref[0], next_e_ref[0] % 2):
            cp.start()
        gather_start(0, 0)
        zbuf[...] = jnp.zeros((SUB_ROWS, D), BF16)
        spare_fill()
        tail_fill(wait=False)

    @pl.when(i < nact)
    def _():
        gather_wait(i, slot)

        @pl.when(i >= 2)
        def _():
            scatter_wait(i - 2, slot)

        @pl.when(i + 1 < nact)
        def _():
            gather_start(i + 1, 1 - slot)

        e = blk_e_ref[i]
        e_prev = blk_e_ref[jnp.maximum(i - 1, 0)]

        @pl.when((i == 0) | (e != e_prev))
        def _():
            wsl = next_e_ref[i] % 2
            for cp in weight_copies(e, wsl):
                cp.wait()
            nxt = next_e_ref[i] // 2

            @pl.when(nxt < N_EXP)
            def _():
                for cp in weight_copies(nxt, 1 - wsl):
                    cp.start()

            wgu_s[...] = wgu_f[wsl].astype(BF16)
            wd_s[...] = wd_f[wsl].astype(BF16)

        x = xbuf[slot]
        ext = x[:, D:D_EXT].astype(F32)
        lane = lax.broadcasted_iota(jnp.int32, (MB, LANE), 1)
        wr = jnp.sum(jnp.where((lane == e) | (lane == e + N_EXP), ext, 0.0), axis=-1, keepdims=True)
        xt = x[:, 0:D]
        n_chunks = D_FF // FF_CHUNK

        def gate_up(j):
            gc = slice(j * FF_CHUNK, (j + 1) * FF_CHUNK)
            uc = slice(D_FF + j * FF_CHUNK, D_FF + (j + 1) * FF_CHUNK)
            return _dot(xt, wgu_s[:, gc]) + bgu_ref[:, gc], _dot(xt, wgu_s[:, uc]) + bgu_ref[:, uc]

        acts = []
        nxt = gate_up(0)
        for j in range(n_chunks):
            gate, up = nxt
            if j + 1 < n_chunks:
                nxt = gate_up(j + 1)
            gate = jnp.minimum(gate, SWIGLU_LIMIT)
            up = jnp.clip(up, -SWIGLU_LIMIT, SWIGLU_LIMIT)
            acts.append(((up + 1.0) * gate * jax.nn.sigmoid(SWIGLU_ALPHA * gate)).astype(BF16))
        for half in range(2):
            cols = slice(half * (D // 2), (half + 1) * (D // 2))
            y = bd_ref[:, cols]
            for j in range(n_chunks):
                y = y + _dot(acts[j], wd_s[j * FF_CHUNK:(j + 1) * FF_CHUNK, cols])
            ybuf[slot, :, cols] = (y * wr).astype(BF16)
        scatter_start(i, slot)

    @pl.when(i == nsteps - 1)
    def _():
        @pl.when(nact >= 2)
        def _():
            scatter_wait(nact - 2, nact % 2)

        scatter_wait(nact - 1, (nact - 1) % 2)
        tail_fill(wait=True)


def _experts(blk_e, next_e, src, nact, tail, s_sorted, layer, w_gu, b_gu, w_dn, b_dn):
    rows = N_TILES * TILE_ROWS + 2 * MB
    per_expert = lambda c: pl.BlockSpec((None, None, 1, c), lambda i, be, ne, sr, na, tl: (layer, be[i], 0, 0))
    hbm = pl.BlockSpec(memory_space=pl.ANY)
    grid_spec = pltpu.PrefetchScalarGridSpec(
        num_scalar_prefetch=5,
        grid=(N_MACRO,),
        in_specs=[hbm, hbm, per_expert(2 * D_FF), hbm, per_expert(D)],
        out_specs=hbm,
        scratch_shapes=[pltpu.VMEM((2, MB, D_EXT), BF16), pltpu.VMEM((2, MB, D), BF16),
                        pltpu.VMEM((2, D, 2 * D_FF), F32), pltpu.VMEM((2, D_FF, D), F32),
                        pltpu.VMEM((D, 2 * D_FF), BF16), pltpu.VMEM((D_FF, D), BF16),
                        pltpu.VMEM((SUB_ROWS, D), BF16),
                        pltpu.SemaphoreType.DMA((2,)), pltpu.SemaphoreType.DMA((2,)),
                        pltpu.SemaphoreType.DMA((2,)), pltpu.SemaphoreType.DMA((2,))])
    return pl.pallas_call(
        functools.partial(_expert_kernel, layer),
        grid_spec=grid_spec,
        out_shape=jax.ShapeDtypeStruct((rows, D), BF16),
        compiler_params=_cparams(("arbitrary",), 56 * 1024 * 1024),
        name="moe_experts",
    )(blk_e, next_e, src, nact, tail, s_sorted, w_gu, b_gu.reshape(DEPTH, N_EXP, 1, 2 * D_FF), w_dn,
      b_dn.reshape(DEPTH, N_EXP, 1, D))


def _combine_kernel(y_ref, pw_ref, x1_ref, g_ref, gate_ref, *o_refs):
    lio = lax.broadcasted_iota(jnp.int32, (TM, TILE_ROWS), 1).astype(F32)
    pw = pw_ref[...]
    p = jnp.zeros((TM, TILE_ROWS), F32)
    for k in range(TOP_K):
        p = p + jnp.where(lio == pw[:, k:k + 1], 1.0, 0.0)
    f = _dot(p.astype(BF16), y_ref[...])
    x2 =x1_ref[...] + gate_ref[...] * _rms(f, g_ref[...])
    if len(o_refs) == 1:
        o_refs[0][...] = x2
    else:
        def store(o_ref):
            o_ref[...] = x2

        _per_group(store, *o_refs)


def _combine(y_sorted, pw, x1, norm_g, mod4, layer, split):
    row = lambda c: pl.BlockSpec((TM, c), lambda t: (t, 0))
    if split:
        out_specs = [_ctx_spec(D), _lat_spec(D)]
        out_shape = [jax.ShapeDtypeStruct((N_CTX, D), F32), jax.ShapeDtypeStruct((N_LAT, D), F32)]
    else:
        out_specs, out_shape = row(D), jax.ShapeDtypeStruct((N_TOK, D), F32)
    return pl.pallas_call(
        _combine_kernel,
        grid=(N_TILES,),
        in_specs=[pl.BlockSpec((TILE_ROWS, D), lambda t: (t, 0)), row(LANE), row(D),
                  pl.BlockSpec((None, 1, D), lambda t: (3, 0, 0)), _mod_spec(layer, 5)],
        out_specs=out_specs,
        out_shape=out_shape,
        compiler_params=_cparams(("arbitrary",)),
        name="moe_combine",
    )(y_sorted, pw, x1, norm_g.reshape(4, 1, D), mod4)


N_MACRO_PAD = -(-N_MACRO // 8) * 8


def _dot_exact(sel, tab):
    hi, lo = _split_bf16(tab)
    s16 = sel.astype(BF16)
    return _dot_nt(s16, hi) + _dot_nt(s16, lo)


def _tables_kernel(cnt_ref, blk_ref, src_ref, meta_ref):
    cnt = cnt_ref[...]
    nb = jnp.floor((cnt + (SUB_ROWS - 1)) * (1.0 / SUB_ROWS))
    li = lax.broadcasted_iota(jnp.int32, (LANE, LANE), 0)
    lj = lax.broadcasted_iota(jnp.int32, (LANE, LANE), 1)
    nb16 = nb.astype(BF16)
    lstart = _dot(nb16, jnp.where(li < lj, 1.0, 0.0).astype(BF16)) * SUB_ROWS
    ti = lax.broadcasted_iota(jnp.int32, (N_TILES, N_TILES), 0)
    tj = lax.broadcasted_iota(jnp.int32, (N_TILES, N_TILES), 1)
    cum_nb = _dot(jnp.where(tj <= ti, 1.0, 0.0).astype(BF16), nb16)
    nb_e = cum_nb[N_TILES - 1:N_TILES, :]
    mb_e = jnp.floor((nb_e + (SUB - 1)) * (1.0 / SUB))
    mb_hi, mb_lo = _split_bf16(jnp.broadcast_to(mb_e, (8, LANE)))
    upto = jnp.where(li <= lj, 1.0, 0.0).astype(BF16)
    mend = (_dot(mb_hi, upto) + _dot(mb_lo, upto))[0:1, :]
    lane_row = lax.broadcasted_iota(jnp.int32, (1, LANE), 1)
    nact = jnp.sum(jnp.where(lane_row == N_EXP - 1, mend, 0.0), axis=1, keepdims=True)
    mstart = mend - mb_e

    lane = lax.broadcasted_iota(jnp.int32, (N_MACRO_PAD, LANE), 1)
    i_col = lax.broadcasted_iota(jnp.int32, (N_MACRO_PAD, 1), 0).astype(F32)
    real = lane < N_EXP
    e_i = jnp.sum(jnp.where(real & (mend <= i_col), 1.0, 0.0), axis=1, keepdims=True)
    e_last = jnp.sum(jnp.where((lane_row < N_EXP) & (mend <= nact - 1.0), 1.0, 0.0), axis=1, keepdims=True)
    blk = jnp.where(i_col < nact, jnp.minimum(e_i, N_EXP - 1.0), e_last)
    hot = jnp.where(lane.astype(F32) == blk, 1.0, 0.0)
    mstart_i = jnp.sum(hot * mstart, axis=1, keepdims=True)
    nbe_i = jnp.sum(hot * nb_e, axis=1, keepdims=True)
    cum_t = _dot_exact(hot, cum_nb)
    nb_t = _dot_exact(hot, nb)
    lst_t = _dot_exact(hot, lstart)
    tile_lane = lax.broadcasted_iota(jnp.int32, (N_MACRO_PAD, N_TILES), 1).astype(F32)
    parity = i_col - 2.0 * jnp.floor(i_col * 0.5)
    src = jnp.zeros((N_MACRO_PAD, LANE), F32)
    for s in range(SUB):
        q = (i_col - mstart_i) * SUB + s
        valid = (i_col < nact) & (q < nbe_i)
        le = cum_t <= q
        j = jnp.minimum(jnp.sum(jnp.where(le, 1.0, 0.0), axis=1, keepdims=True), N_TILES - 1.0)
        prev = jnp.sum(jnp.where(le, nb_t, 0.0), axis=1, keepdims=True)
        lst = jnp.sum(jnp.where(tile_lane == j, lst_t, 0.0), axis=1, keepdims=True)
        row = j * TILE_ROWS + lst + (q - prev) * SUB_ROWS
        spare = N_TILES * TILE_ROWS + (parity * SUB + s) * SUB_ROWS
        src = jnp.where(lane == s, jnp.where(valid, row, spare), src)
    src_ref[...] = src.astype(jnp.int32)
    used = real & (mb_e > 0.0)
    lane_f = lane.astype(F32)
    rank = jnp.sum(jnp.where(used & (lane_f < blk), 1.0, 0.0), axis=1, keepdims=True)
    nxt = jnp.min(jnp.where(used & (lane_f > blk), lane_f, float(N_EXP)), axis=1, keepdims=True)
    next_code = 2.0 * nxt + (rank - 2.0 * jnp.floor(rank * 0.5))
    blk_ref[...] = jnp.where(lane == 0, blk, jnp.where(lane == 1, next_code, 0.0)).astype(jnp.int32)
    tile_col = lax.broadcasted_iota(jnp.int32, (N_TILES, 1), 0).astype(F32)
    tail = tile_col * TILE_ROWS + jnp.sum(nb, axis=1, keepdims=True) * SUB_ROWS
    lane_t = lax.broadcasted_iota(jnp.int32, (N_TILES, LANE), 1)
    meta_ref[...] = jnp.where(lane_t == 0, tail, jnp.where(lane_t == 1, nact, 0.0)).astype(jnp.int32)


def _block_tables(cnt):
    i32 = lambda r: jax.ShapeDtypeStruct((r, LANE), jnp.int32)
    blk, src, meta = pl.pallas_call(
        _tables_kernel,
        out_shape=[i32(N_MACRO_PAD), i32(N_MACRO_PAD), i32(N_TILES)],
        compiler_params=_cparams(None),
        name="moe_tables",
    )(cnt.reshape(N_TILES, LANE))
    return blk[:N_MACRO, 0], blk[:N_MACRO, 1], src[:N_MACRO, :SUB].reshape(-1), meta[0:1, 1].reshape(1), meta[:, 0]


def _moe(mixes, w_outs, x, norm_g, mod4, layer, w_router, b_router, w_gu, b_gu, w_dn, b_dn, split=False):
    wr = jnp.pad(w_router, ((0, 0), (0, LANE - N_EXP)))
    br = jnp.pad(b_router, (0, LANE - N_EXP)).reshape(1, LANE)
    x1, hx, pw, pwt, cnt = _route(mixes, w_outs, x, norm_g, mod4, layer, wr, br)
    blk_e, next_e, src, nact, tail = _block_tables(cnt)
    s_sorted = _dispatch(hx, pwt)
    y_sorted = _experts(blk_e, next_e, src, nact, tail, s_sorted, layer, w_gu, b_gu, w_dn, b_dn)
    return _combine(y_sorted, pw, x1, norm_g, mod4, layer, split)


def kernel(x_prompt, x_sample, c, c_ctx, state_delta, cache_diff_k, cache_diff_v, cache_mla_ckv, cache_mla_kpe, w_mod, b_mod, norm_g, w_router, b_router, w_gate_up, b_gate_up, w_down, b_down, w_in_even, conv_w, a_log, dt_bias, gdn_norm_g, diff_lambda, diff_norm_g, w_out_even, w_in_odd, q_norm_g, w_uq, kv_norm_g, w_ukv, w_out_odd):
    x = (x_prompt.reshape(N_CTX, D), x_sample.reshape(N_LAT, D))
    cond =jnp.concatenate([c_ctx[None, :], c, jnp.zeros((16 - 1 - B_LAT, D), F32)], axis=0)
    mod4 = _modulation(cond, w_mod, b_mod).reshape(DEPTH, 16, 1, N_MOD * D)
    cos64, sin64 = _rope_tables()
    rope2 = (jnp.tile(cos64, (1, 2)), jnp.tile(sin64, (1, 2)))
    rope1 = (jnp.pad(cos64, ((0, 0), (0, 64))), jnp.pad(sin64, ((0, 0), (0, 64))))

    we = w_in_even[0]
    w_cat = jnp.concatenate([we[:, 0:2048], we[:, 2064:3600], we[:, 2048:2064],
                             jnp.zeros((D, LANE - 16), F32)], axis=1).astype(BF16)
    qkv, z, qd, kd, vd, ab = _in_even(x, norm_g[0, 0:1], mod4, 0, w_cat)
    gn = gdn_norm_g[0].reshape(1, GDN_DV)
    oa_c, new_state = _gdn(qkv, z, ab, conv_w[0], a_log[0], dt_bias[0], gn, None, B_CTX, T_CTX, 0)
    oa_l = _gdn(qkv, z, ab, conv_w[0], a_log[0], dt_bias[0], gn, state_delta[:, 0], B_LAT, T_LAT, N_CTX)
    lam_init = 0.8 - 0.6 * math.exp(-0.3 * 0)
    dgn = diff_norm_g[0].reshape(1, 2 * DIFF_DH)
    od_c = _diff_attn(qd, kd, vd, diff_lambda[0], dgn, lam_init, B_CTX, T_CTX, 0)
    cache_dk = cache_diff_k[:, 0].reshape(B_LAT, PAST, DIFF_H * 2 * DIFF_DH)
    cache_dv = cache_diff_v[:, 0].reshape(B_LAT, PAST, DIFF_H * 2 * DIFF_DH)
    od_l = _diff_attn(qd, kd, vd, diff_lambda[0], dgn, lam_init, B_LAT, T_LAT, N_CTX,
                      cache=(cache_dk, cache_dv), rope=rope2)
    wo = w_out_even[0].astype(BF16)
    x = _moe([(oa_c, oa_l), (od_c, od_l)], [wo[:512], wo[512:]], x, norm_g[0], mod4, 0, w_router[0], b_router[0],
             w_gate_up, b_gate_up, w_down, b_down)
    new_dk = kd[:N_CTX].reshape(B_CTX, 1, T_CTX, DIFF_H, 2, DIFF_DH)
    new_dv = vd[:N_CTX].reshape(B_CTX, 1, T_CTX, DIFF_H, 2 * DIFF_DH)
    new_state = new_state.reshape(B_CTX, 1, 2, GDN_H, GDN_DK, GDN_DV)

    w_odd = jnp.pad(w_in_odd[0], ((0, 0), (0, ODD_W - (Q_LORA + KV_LORA + QK_ROPE)))).astype(BF16)
    wuq = w_uq[0].reshape(Q_LORA, MLA_H, QK_NOPE + QK_ROPE)
    wuq = jnp.pad(wuq, ((0, 0), (0, 0), (0, MLA_QW - (QK_NOPE + QK_ROPE)))).reshape(Q_LORA, MLA_H * MLA_QW)
    wukv = w_ukv[0].astype(BF16)
    q, kv, ckv, kpe = _in_odd(x, norm_g[1, 0:1], mod4, 1, w_odd, q_norm_g[0].reshape(1, Q_LORA),
                              wuq.astype(BF16), kv_norm_g[0].reshape(1, KV_LORA), wukv)
    om_c = _mla_attn(q, kv, kpe, B_CTX, T_CTX, 0)
    cache_kv = _matmul(cache_mla_ckv[:, 0].reshape(B_LAT * PAST, KV_LORA), wukv)
    cache_kpe = jnp.pad(cache_mla_kpe[:, 0].reshape(B_LAT * PAST, QK_ROPE), ((0, 0), (0, LANE - QK_ROPE)))
    om_l = _mla_attn(q, kv, kpe, B_LAT, T_LAT, N_CTX, cache=(cache_kv, cache_kpe), rope=rope1)
    x_ctx, x_lat = _moe([(om_c, om_l)], [w_out_odd[0].astype(BF16)], x, norm_g[1], mod4, 1, w_router[1],
                        b_router[1], w_gate_up, b_gate_up, w_down, b_down, split=True)
    new_ckv = ckv[:N_CTX].reshape(B_CTX, 1, T_CTX, KV_LORA)
    new_kpe = kpe[:N_CTX, :QK_ROPE].reshape(B_CTX, 1, T_CTX, QK_ROPE)

    return (x_ctx.reshape(B_CTX, T_CTX, D), x_lat.reshape(B_LAT, T_LAT, D),
            new_state, new_dk, new_dv, new_ckv, new_kpe)
```

```python
import functools
import math

import jax
import jax.numpy as jnp
from jax import lax
from jax.experimental import pallas as pl
from jax.experimental.pallas import tpu as pltpu

F32 = jnp.float32
BF16 = jnp.bfloat16
HIGHEST = lax.Precision.HIGHEST

D = 1024
B_CTX, T_CTX = 16, 256
B_LAT, T_LAT = 8, 1024
PAST = 256
N_CTX = B_CTX * T_CTX
N_LAT = B_LAT * T_LAT
N_TOK = N_CTX + N_LAT
DEPTH = 2
N_MOD = 6
GRID_W = 64
GDN_H, GDN_DK, GDN_DV = 4, 128, 128
CONV_W = 5
CHUNK = 64
DIFF_H, DIFF_DH = 4, 64
MLA_H, Q_LORA, KV_LORA, QK_NOPE, QK_ROPE, V_HEAD = 8, 256, 256, 128, 64, 128
N_EXP, TOP_K, D_FF = 32, 4, 1024
SWIGLU_LIMIT, SWIGLU_ALPHA = 7.0, 1.702
ROPE_BASE = 10000.0
EPS = 1e-6

LANE = 128
TM = 256
N_TILES = N_TOK // TM
SUB_ROWS = 16
TILE_ROWS = TOP_K * TM + N_EXP * SUB_ROWS
MB = 256
SUB = MB // SUB_ROWS
FF_CHUNK = 256
SORT_BLOCK = 256
GDN_TRIP_ROWS = 256
N_MACRO = -(-((TOP_K * N_TOK) // SUB_ROWS + N_TILES * N_EXP + N_EXP * (SUB - 1)) // SUB)
D_EXT = D + LANE
VMEM_LIMIT = 48 * 1024 * 1024


def _cparams(sem, vmem=VMEM_LIMIT):
    return pltpu.CompilerParams(dimension_semantics=sem, vmem_limit_bytes=vmem)


N_CTX_TILES = N_CTX // TM


def _mod_row(t):
    return jnp.where(t < N_CTX_TILES, 0, 1 + (t - N_CTX_TILES) // (T_LAT // TM))


def _ctx_spec(c):
    return pl.BlockSpec((TM, c), lambda t: (jnp.minimum(t, N_CTX_TILES - 1), 0))


def _lat_spec(c, first=0):
    return pl.BlockSpec((TM, c), lambda t: (jnp.maximum(t - N_CTX_TILES, 0) + first, 0))


def _token_pair(x):
    return (x[0], x[1], 0) if isinstance(x, tuple) else (x, x, N_CTX_TILES)


def _per_group(body, ctx_ref, lat_ref):
    t = pl.program_id(0)
    pl.when(t < N_CTX_TILES)(lambda: body(ctx_ref))
    pl.when(t >= N_CTX_TILES)(lambda: body(lat_ref))


def _rms(x, g):
    return x * lax.rsqrt(jnp.mean(x * x, axis=-1, keepdims=True) + EPS) * g


def _silu(x):
    return x * jax.nn.sigmoid(x)


def _dot(a, b):
    return jnp.dot(a, b, preferred_element_type=F32)


def _dot_nt(a, b):
    return lax.dot_general(a, b, (((1,), (1,)), ((), ())), preferred_element_type=F32)


def _mod_kernel(c_ref, w_ref, b_ref, o_ref):
    s = _silu(c_ref[...])
    o_ref[...] = _dot(s.astype(BF16), w_ref[...].astype(BF16)) + b_ref[...]


def _modulation(cond, w_mod, b_mod):
    nc = 1536
    return pl.pallas_call(
        _mod_kernel,
        grid=(DEPTH, N_MOD * D // nc),
        in_specs=[pl.BlockSpec((16, D), lambda i, j: (0, 0)),
                  pl.BlockSpec((None, D, nc), lambda i, j: (i, 0, j)),
                  pl.BlockSpec((None, 1, nc), lambda i, j: (i, 0, j))],
        out_specs=pl.BlockSpec((None, 16, nc), lambda i, j: (i, 0, j)),
        out_shape=jax.ShapeDtypeStruct((DEPTH, 16, N_MOD * D), F32),
        compiler_params=_cparams(("arbitrary", "arbitrary")),
        name="modulation",
    )(cond, w_mod, b_mod.reshape(DEPTH, 1, N_MOD * D))


def _mod_spec(layer, k):
    return pl.BlockSpec((None, None, 1, D), lambda t: (layer, _mod_row(t), 0, k))


EVEN_W = 2 * 512 + 512 + 512 + 3 * 512 + LANE


def _in_even_kernel(xc_ref, xl_ref, g_ref, sh_ref, sc_ref, w_ref, qkv_ref, z_ref, qd_ref, kd_ref, vd_ref, ab_ref):
    def body(x_ref):
        h = _rms(x_ref[...], g_ref[...]) * (1.0 + sc_ref[...]) + sh_ref[...]
        y = _dot(h.astype(BF16), w_ref[...])
        qkv_ref[...] = y[:, 0:1536]
        z_ref[...] = y[:, 1536:2048]
        qd_ref[...] = y[:, 2048:2560]
        kd_ref[...] = y[:, 2560:3072]
        vd_ref[...] = y[:, 3072:3584]
        ab_ref[...] = y[:, 3584:3712]

    _per_group(body, xc_ref, xl_ref)


def _in_even(x, g, mod4, layer, w):
    xc, xl, first = _token_pair(x)
    row = lambda c: pl.BlockSpec((TM, c), lambda t: (t, 0))
    shp = lambda c: jax.ShapeDtypeStruct((N_TOK, c), F32)
    return pl.pallas_call(
        _in_even_kernel,
        grid=(N_TILES,),
        in_specs=[_ctx_spec(D), _lat_spec(D, first), pl.BlockSpec((1, D), lambda t: (0, 0)), _mod_spec(layer, 0),
                  _mod_spec(layer, 1), pl.BlockSpec((D, EVEN_W), lambda t: (0, 0))],
        out_specs=[row(1536), row(512), row(512), row(512), row(512), row(LANE)],
        out_shape=[shp(1536), shp(512), shp(512), shp(512), shp(512), shp(LANE)],
        compiler_params=_cparams(("arbitrary",)),
        name="in_proj_even",
    )(xc, xl, g, mod4, mod4, w)


def _split_bf16(a):
    hi = a.astype(BF16)
    lo = (a - hi.astype(F32)).astype(BF16)
    return hi, lo


def _mm3(a, b):
    m = a[0].shape[0]
    top = _dot(jnp.concatenate([a[0], a[1]], axis=0), b[0])
    return top[:m] + (top[m:] + _dot(a[0], b[1]))


def _unit_tri_inverses(ls, eye, blk, bd):
    sp = _split_bf16
    d8 = [jnp.where(blk(8), l, 0.0) for l in ls]
    d8s = [sp(d) for d in d8]
    x2s = [sp(_mm3(d, bd(d))) for d in d8s]
    x2r = [bd(x) for x in x2s]
    ts = [eye - d for d in d8]
    ts = [t + _mm3(sp(t), xr) for t, xr in zip(ts, x2r)]
    x4 = [_mm3(x, xr) for x, xr in zip(x2s, x2r)]
    ts = [t + _mm3(sp(t), bd(sp(x))) for t, x in zip(ts, x4)]
    for s in (8, 16, 32):
        offs = [sp(jnp.where(blk(2 * s) & jnp.logical_not(blk(s)), l, 0.0)) for l in ls]
        tss = [sp(t) for t in ts]
        ys = [_mm3(o, bd(t)) for o, t in zip(offs, tss)]
        ts = [t - _mm3(t2, bd(sp(y))) for t, t2, y in zip(ts, tss, ys)]
    return ts


def _softplus(x):
    return jnp.maximum(x, 0.0) + jnp.log1p(jnp.exp(-jnp.abs(x)))


def _gdn_kernel(has_init, T, alog_ref, dtb_ref, qkv_ref, cw_ref, z_ref, ab_ref, gn_ref, *rest):
    if has_init:
        s0_ref, o_ref = rest[:2]
        st_ref = None
    else:
        o_ref, st_ref = rest[:2]
        s0_ref = None
    xp, qs, ks, vs, abt_s, uo_s, w_s, qg_s, kgt_s, in_s, gl_s, st_s = rest[2:]
    n = T // CHUNK
    H = GDN_H

    xp[0:8, :] = jnp.zeros((8, LANE), F32)
    xp[T + 8:T + 16, :] = jnp.zeros((8, LANE), F32)

    def l2n(x):
        return x * lax.rsqrt(jnp.sum(x * x, axis=-1, keepdims=True) + EPS)

    for blk in range(3 * H):
        cols = slice(blk * LANE, (blk + 1) * LANE)
        dst = slice((blk % H) * LANE, (blk % H + 1) * LANE)
        xp[8:T + 8, :] = qkv_ref[:, cols]
        w = cw_ref[:, cols]
        acc = xp[6:T + 6, :] * w[0:1, :]
        for i in range(1, CONV_W):
            acc = acc + xp[6 + i:T + 6 + i, :] * w[i:i + 1, :]
        y = _silu(acc)
        if blk < H:
            qs[:, dst] = l2n(y) * (GDN_DK ** -0.5)
        elif blk < 2 * H:
            ks[:, dst] = l2n(y)
        else:
            vs[:, dst] = y
    abt_s[...] = ab_ref[...].T[0:16, :]

    ii = lax.broadcasted_iota(jnp.int32, (CHUNK, LANE), 0)
    lane = lax.broadcasted_iota(jnp.int32, (CHUNK, LANE), 1)
    jm = lane & (CHUNK - 1)
    fwd = lane < CHUNK
    order = jnp.where(fwd, ii - jm, jm - ii)
    incl = order >= 0
    strict = order > 0
    incl_t = order <= 0
    eye = jnp.where(order == 0, 1.0, 0.0).astype(F32)
    blk_mask = lambda s: (ii // s) == (jm // s)
    zeros_c = jnp.zeros((CHUNK, LANE), F32)
    zeros_c16 = jnp.zeros((CHUNK, LANE), BF16)

    keep_f = jnp.where(fwd, 1.0, 0.0).astype(BF16)
    keep_b = jnp.where(fwd, 0.0, 1.0).astype(BF16)

    def bd(parts):
        return tuple(jnp.concatenate([a * keep_f, a * keep_b], axis=0) for a in parts)

    lane_row = lax.broadcasted_iota(jnp.int32, (1, LANE), 1)
    sub_col = lax.broadcasted_iota(jnp.int32, (16, 1), 0)
    a_lane, dt_lane = jnp.zeros((1, LANE), F32), jnp.zeros((1, LANE), F32)
    a_col, dt_col = jnp.zeros((16, 1), F32), jnp.zeros((16, 1), F32)
    for d in range(2):
        for h in range(H):
            a_lane = jnp.where(lane_row == d * H + h, alog_ref[d, h], a_lane)
            dt_lane = jnp.where(lane_row == d * H + h, dtb_ref[d, h], dt_lane)
            a_col = jnp.where(sub_col == d * H + h, alog_ref[d, h], a_col)
            dt_col = jnp.where(sub_col == d * H + h, dtb_ref[d, h], dt_col)
    nega_lane, nega_col = -jnp.exp(a_lane), -jnp.exp(a_col)

    def prep(p, carry):
        rbase = pl.multiple_of(p * GDN_TRIP_ROWS, GDN_TRIP_ROWS)
        g_rows = nega_col * _softplus(abt_s[:, pl.ds(rbase, GDN_TRIP_ROWS)] + dt_col)
        sysm = []
        for cc in range(GDN_TRIP_ROWS // CHUNK):
            r0 = pl.multiple_of(rbase + cc * CHUNK, CHUNK)
            abc = ab_ref[pl.ds(r0, CHUNK), :]
            g_cols = nega_lane * _softplus(abc + dt_lane)
            b_cols = jax.nn.sigmoid(abc)
            for h in range(H):
                hs = slice(h * LANE, (h + 1) * LANE)
                tok = slice(cc * CHUNK, (cc + 1) * CHUNK)
                sysm.append(dict(
                    c=(GDN_TRIP_ROWS // CHUNK) * p + cc, rows=pl.ds(r0, CHUNK), h=h,
                    qc=qs[pl.ds(r0, CHUNK), hs], kc=ks[pl.ds(r0, CHUNK), hs], vc=vs[pl.ds(r0, CHUNK), hs],
                    g_col=jnp.where(fwd, g_cols[:, h:h + 1], g_cols[:, H + h:H + h + 1]),
                    g_row=jnp.concatenate([g_rows[h:h + 1, tok], g_rows[H + h:H + h + 1, tok]], axis=1),
                    beta=(b_cols[:, 2 * H + h:2 * H + h + 1], b_cols[:, 3 * H + h:3 * H + h + 1])))
        for s in sysm:
            k16 = s["kc"].astype(BF16)
            kk16 = jnp.concatenate([k16, k16], axis=0)
            kq = _dot_nt(jnp.concatenate([k16, s["qc"].astype(BF16)], axis=0), kk16)
            s["kk"], s["qk"] = kq[:CHUNK], kq[CHUNK:]
        ls = []
        for s in sysm:
            m = jnp.where(incl, s["g_row"], 0.0)
            s["gcf"] = jnp.sum(jnp.where(fwd, m, 0.0), axis=1, keepdims=True)
            s["gcb"] = jnp.sum(jnp.where(fwd, 0.0, m), axis=1, keepdims=True)
            gc_col = jnp.where(fwd, s["gcf"], s["gcb"])
            gc_row = jnp.sum(jnp.where(incl_t, s["g_col"], 0.0), axis=0, keepdims=True)
            decay = jnp.where(incl, jnp.exp(jnp.where(incl, gc_col - gc_row, 0.0)), 0.0)
            beta = jnp.where(fwd, s["beta"][0], s["beta"][1])
            ls.append(jnp.where(strict, s["kk"] * beta * decay, 0.0))
            s["intra"] = s["qk"] * decay
        ts = _unit_tri_inverses(ls, eye, blk_mask, bd)
        for s, t in zip(sysm, ts):
            h, c, rows = s["h"], s["c"], s["rows"]
            egs = (jnp.exp(s["gcf"]), jnp.exp(s["gcb"]))
            r_f = jnp.concatenate([s["vc"] * s["beta"][0], s["kc"] * (s["beta"][0] * egs[0]), zeros_c, zeros_c], axis=1)
            r_b = jnp.concatenate([zeros_c, zeros_c, s["vc"] * s["beta"][1], s["kc"] * (s["beta"][1] * egs[1])], axis=1)
            uw = _dot(t.astype(BF16), jnp.concatenate([r_f, r_b], axis=0).astype(BF16))
            g_last = (s["gcf"][CHUNK - 1:CHUNK, :], s["gcb"][0:1, :])
            kgs = []
            for d in range(2):
                k = d * H + h
                uo_s[k, rows, :] = uw[:, 2 * d * LANE:(2 * d + 1) * LANE]
                w_s[k, rows, :] = uw[:, (2 * d + 1) * LANE:(2 * d + 2) * LANE].astype(BF16)
                qg_s[k, rows, :] = (s["qc"] * egs[d]).astype(BF16)
                gc = s["gcf"] if d == 0 else s["gcb"]
                kgs.append(s["kc"] * jnp.exp(g_last[d] - gc))
                gl_s[k, c] = jnp.broadcast_to(jnp.exp(g_last[d]), (8, LANE))
            in_s[h, c] = s["intra"].astype(BF16)
            kgt_s[h, c] = jnp.concatenate(kgs, axis=0).T.astype(BF16)
        return carry

    lax.fori_loop(0, T // GDN_TRIP_ROWS, prep, 0)

    for d in range(2):
        for h in range(H):
            st_s[d * H + h] = s0_ref[d, h] if has_init else jnp.zeros((GDN_DK, GDN_DV), F32)

    def scan(i, carry):
        cs = (i, n - 1 - i)
        ks_ = range(2 * H)
        c_of = [cs[k // H] for k in ks_]
        rows = [pl.ds(pl.multiple_of(c * CHUNK, CHUNK), CHUNK) for c in c_of]
        s16 = [st_s[k].astype(BF16) for k in ks_]
        wq = [_dot(jnp.concatenate([w_s[k, rows[k], :], qg_s[k, rows[k], :]], axis=0), s16[k]) for k in ks_]
        vn = [(uo_s[k, rows[k], :] - wq[k][:CHUNK]).astype(BF16) for k in ks_]
        vpad = [jnp.concatenate([vn[k], zeros_c16] if k < H else [zeros_c16, vn[k]], axis=0) for k in ks_]
        ik = [_dot(jnp.concatenate([in_s[k % H, c_of[k]], kgt_s[k % H, c_of[k]]], axis=0), vpad[k]) for k in ks_]
        for k in ks_:
            uo_s[k, rows[k], :] = wq[k][CHUNK:] + ik[k][:CHUNK]
            st_s[k] = st_s[k] * gl_s[k, c_of[k]][0:1, :] + ik[k][CHUNK:]
        return carry

    lax.fori_loop(0, n, scan, 0)
    for h in range(H):
        hs = slice(h * LANE, (h + 1) * LANE)
        o_ref[:, hs] = _rms(uo_s[h] + uo_s[H + h], gn_ref[...]) * _silu(z_ref[:, hs])
        if st_ref is not None:
            st_ref[0, h] = st_s[h]
            st_ref[1, h] = st_s[H + h]


def _gdn(qkv, z, ab, conv_w, a_log, dt_bias, gn, s0, nb, T, row0):
    n = T // CHUNK
    blk0 = row0 // T
    has_init = s0 is not None
    H, W = GDN_H, GDN_H * LANE
    smem = pl.BlockSpec(memory_space=pltpu.SMEM)
    in_specs = [smem, smem,
                pl.BlockSpec((T, 3 * W), lambda b: (blk0 + b, 0), pipeline_mode=pl.Buffered(1)),
                pl.BlockSpec((CONV_W, 3 * W), lambda b: (0, 0)),
                pl.BlockSpec((T, W), lambda b: (blk0 + b, 0), pipeline_mode=pl.Buffered(1)),
                pl.BlockSpec((T, LANE), lambda b: (blk0 + b, 0)),
                pl.BlockSpec((1, LANE), lambda b: (0, 0))]
    args = [a_log, dt_bias, qkv, conv_w, z, ab, gn]
    st_spec = pl.BlockSpec((None, 2, H, GDN_DK, GDN_DV), lambda b: (b, 0, 0, 0, 0))
    o_spec = pl.BlockSpec((T, W), lambda b: (b, 0))
    o_shape = jax.ShapeDtypeStruct((nb * T, W), F32)
    if has_init:
        in_specs.append(st_spec)
        args.append(s0)
        out_specs, out_shape = o_spec, o_shape
    else:
        out_specs = [o_spec, st_spec]
        out_shape = [o_shape, jax.ShapeDtypeStruct((nb, 2, H, GDN_DK, GDN_DV), F32)]
    scratch = [pltpu.VMEM((T + 16, LANE), F32), pltpu.VMEM((T, W), F32), pltpu.VMEM((T, W), F32),
               pltpu.VMEM((T, W), F32), pltpu.VMEM((16, T), F32),
               pltpu.VMEM((2 * H, T, LANE), F32), pltpu.VMEM((2 * H, T, LANE), BF16),
               pltpu.VMEM((2 * H, T, LANE), BF16), pltpu.VMEM((H, n, LANE, LANE), BF16),
               pltpu.VMEM((H, n, CHUNK, LANE), BF16), pltpu.VMEM((2 * H, n, 8, LANE), F32),
               pltpu.VMEM((2 * H, GDN_DK, GDN_DV), F32)]
    return pl.pallas_call(
        functools.partial(_gdn_kernel, has_init, T),
        grid=(nb,),
        in_specs=in_specs, out_specs=out_specs, out_shape=out_shape, scratch_shapes=scratch,
        compiler_params=_cparams(("arbitrary",), 56 * 1024 * 1024),
        name="gdn_lat" if has_init else "gdn_ctx",
    )(*args)


def _rope(x, cos, sin_signed):
    lane = lax.broadcasted_iota(jnp.int32, x.shape, 1)
    up = pltpu.roll(x, LANE - 16, axis=1)
    dn = pltpu.roll(x, 16, axis=1)
    rot = jnp.where((lane % 32) < 16, up, dn)
    return x * cos + rot * sin_signed


def _rope_tables():
    rows = T_LAT // GRID_W
    r = jnp.repeat(jnp.arange(rows, dtype=F32), GRID_W)
    col = jnp.tile(jnp.arange(GRID_W, dtype=F32), rows)
    quarter = 16
    inv = ROPE_BASE ** (-jnp.arange(quarter, dtype=F32) / quarter)
    ar = r[:, None] * inv
    ac = col[:, None] * inv
    ang = jnp.concatenate([ar, ar, ac, ac], axis=-1)
    sign = jnp.tile(jnp.concatenate([-jnp.ones(16, F32), jnp.ones(16, F32)]), 2)
    cos, sin = jnp.cos(ang), jnp.sin(ang) * sign
    return cos, sin


LOG2E = math.log2(math.e)


def _attend(parts, values):
    m = functools.reduce(jnp.maximum, [jnp.max(s, axis=-1, keepdims=True) for s in parts])
    es = [jnp.exp2(s - m) for s in parts]
    den = functools.reduce(lambda a, b: a + b, [jnp.sum(e, axis=-1, keepdims=True) for e in es])
    o = functools.reduce(lambda a, b: a + b, [_dot(e.astype(BF16), v) for e, v in zip(es, values)])
    return o * (1.0 / den)


def _diff_kernel(latent, T, lam_init, q_ref, k_ref, v_ref, lam_ref, gn_ref, *rest):
    if latent:
        cos_ref, sin_ref, ck_ref, cv_ref, o_ref = rest
    else:
        (o_ref,) = rest
    scale = DIFF_DH ** -0.5
    lp = lam_ref[...]
    lam = (jnp.exp(jnp.sum(lp[0:1] * lp[1:2], axis=-1, keepdims=True))
           - jnp.exp(jnp.sum(lp[2:3] * lp[3:4], axis=-1, keepdims=True)) + lam_init)
    tq = 256
    lane = lax.broadcasted_iota(jnp.int32, (tq, LANE), 1)
    for hh in range(q_ref.shape[1] // LANE):
        cs = slice(hh * LANE, (hh + 1) * LANE)
        k = k_ref[:, cs]
        if latent:
            k = _rope(k, cos_ref[...], sin_ref[...])
        k16 = k.astype(BF16)
        v16 = v_ref[:, cs].astype(BF16)
        if latent:
            ck16 = ck_ref[:, cs].astype(BF16)
            cv16 = cv_ref[:, cs].astype(BF16)
        def scores(qb):
            rows = slice(qb * tq, (qb + 1) * tq)
            q = q_ref[rows, cs]
            if latent:
                q = _rope(q, cos_ref[rows, :], sin_ref[rows, :])
            qs = q * (scale * LOG2E)
            q2 = jnp.concatenate([jnp.where((lane // DIFF_DH) == m, qs, 0.0) for m in range(2)], axis=0)
            q2 = q2.astype(BF16)
            return [_dot_nt(q2, ck16), _dot_nt(q2, k16)] if latent else [_dot_nt(q2, k16)]

        nxt = scores(0)
        for qb in range(T // tq):
            parts = nxt
            if qb + 1 < T // tq:
                nxt = scores(qb + 1)
            o2 = _attend(parts, [cv16, v16] if latent else [v16])
            o = o2[:tq] - lam * o2[tq:]
            o_ref[qb * tq:(qb + 1) * tq, cs] = _rms(o, gn_ref[...]) * (1.0 - lam_init)


def _diff_attn(qd, kd, vd, lam_p, gn, lam_init, nb, T, row0, cache=None, rope=None):
    blk0 = row0 // T
    latent = cache is not None
    hps = 2 if latent else DIFF_H
    hcol = pl.BlockSpec((T, hps * LANE), lambda b, h: (blk0 + b, h))
    in_specs = [hcol, hcol, hcol, pl.BlockSpec((4, DIFF_DH), lambda b, h: (0, 0)),
                pl.BlockSpec((1, LANE), lambda b, h: (0, 0))]
    args = [qd, kd, vd, lam_p, gn]
    if latent:
        tab = pl.BlockSpec((T, LANE), lambda b, h: (0, 0))
        cch = pl.BlockSpec((None, PAST, hps * LANE), lambda b, h: (b, 0, h))
        in_specs += [tab, tab, cch, cch]
        args += [rope[0], rope[1], cache[0], cache[1]]
    return pl.pallas_call(
        functools.partial(_diff_kernel, latent, T, lam_init),
        grid=(nb, DIFF_H // hps),
        in_specs=in_specs,
        out_specs=pl.BlockSpec((T, hps * LANE), lambda b, h: (b, h)),
        out_shape=jax.ShapeDtypeStruct((nb * T, DIFF_H * 2 * DIFF_DH), F32),
        compiler_params=_cparams(("arbitrary", "arbitrary")),
        name="diff_lat" if latent else "diff_ctx",
    )(*args)


ODD_W = Q_LORA + KV_LORA + LANE
MLA_QW = 2 * LANE


def _in_odd_kernel(x_ref, g_ref, sh_ref, sc_ref, w_ref, qg_ref, wuq_ref, kvg_ref, wukv_ref,
                   q_ref, kv_ref, ckv_ref, kpe_ref):
    h = _rms(x_ref[...], g_ref[...]) * (1.0 + sc_ref[...]) + sh_ref[...]
    y = _dot(h.astype(BF16), w_ref[...])
    qn = _rms(y[:, 0:Q_LORA], qg_ref[...])
    q_ref[...] = _dot(qn.astype(BF16), wuq_ref[...])
    ckv = _rms(y[:, Q_LORA:Q_LORA + KV_LORA], kvg_ref[...])
    ckv_ref[...] = ckv
    kv_ref[...] = _dot(ckv.astype(BF16), wukv_ref[...])
    kpe_ref[...] = y[:, Q_LORA + KV_LORA:ODD_W]


def _in_odd(x, g, mod4, layer, w, qg, wuq, kvg, wukv):
    row = lambda c: pl.BlockSpec((TM, c), lambda t: (t, 0))
    full = lambda a: pl.BlockSpec(a.shape, lambda t: (0, 0))
    shp = lambda c: jax.ShapeDtypeStruct((N_TOK, c), F32)
    return pl.pallas_call(
        _in_odd_kernel,
        grid=(N_TILES,),
        in_specs=[row(D), full(g), _mod_spec(layer, 0), _mod_spec(layer, 1), full(w), full(qg), full(wuq),
                  full(kvg), full(wukv)],
        out_specs=[row(MLA_H * MLA_QW), row(MLA_H * MLA_QW), row(KV_LORA), row(LANE)],
        out_shape=[shp(MLA_H * MLA_QW), shp(MLA_H * MLA_QW), shp(KV_LORA), shp(LANE)],
        compiler_params=_cparams(("arbitrary",)),
        name="in_proj_odd",
    )(x, g, mod4, mod4, w, qg, wuq, kvg, wukv)


def _mm_kernel(x_ref, w_ref, o_ref):
    o_ref[...] = _dot(x_ref[...].astype(BF16), w_ref[...])


def _matmul(x, w):
    m, k = x.shape
    n = w.shape[1]
    return pl.pallas_call(
        _mm_kernel,
        grid=(m // TM,),
        in_specs=[pl.BlockSpec((TM, k), lambda i: (i, 0)), pl.BlockSpec((k, n), lambda i: (0, 0))],
        out_specs=pl.BlockSpec((TM, n), lambda i: (i, 0)),
        out_shape=jax.ShapeDtypeStruct((m, n), F32),
        compiler_params=_cparams(("arbitrary",)),
        name="cache_kv_proj",
    )(x, w)


def _mla_kernel(latent, T, q_ref, kv_ref, kpe_ref, *rest):
    if latent:
        cos_ref, sin_ref, ckv_ref, ckpe_ref, o_ref = rest
    else:
        (o_ref,) = rest
    scale = (QK_NOPE + QK_ROPE) ** -0.5
    kpe = kpe_ref[...]
    if latent:
        kpe = _rope(kpe, cos_ref[...], sin_ref[...])
    tq = 256
    for hh in range(q_ref.shape[1] // MLA_QW):
        cs = slice(hh * MLA_QW, (hh + 1) * MLA_QW)
        kv = kv_ref[:, cs]
        v16 = kv[:, LANE:].astype(BF16)
        if latent:
            ckv = ckv_ref[:, cs]
            ck16 = jnp.concatenate([ckv[:, :LANE], ckpe_ref[...]], axis=1).astype(BF16)
            cv16 = ckv[:, LANE:].astype(BF16)
        k16 = jnp.concatenate([kv[:, :LANE], kpe], axis=1).astype(BF16)

        def scores(qb):
            rows = slice(qb * tq, (qb + 1) * tq)
            q = q_ref[rows, cs]
            qp = q[:, LANE:]
            if latent:
                qp = _rope(qp, cos_ref[rows, :], sin_ref[rows, :])
            q16 = (jnp.concatenate([q[:, :LANE], qp], axis=1) * (scale * LOG2E)).astype(BF16)
            return [_dot_nt(q16, ck16), _dot_nt(q16, k16)] if latent else [_dot_nt(q16, k16)]

        nxt = scores(0)
        for qb in range(T // tq):
            parts = nxt
            if qb + 1 < T // tq:
                nxt = scores(qb + 1)
            o_ref[qb * tq:(qb + 1) * tq, hh * V_HEAD:(hh + 1) * V_HEAD] = _attend(
                parts, [cv16, v16] if latent else [v16])


def _mla_attn(q, kv, kpe, nb, T, row0, cache=None, rope=None):
    blk0 = row0 // T
    latent = cache is not None
    hps = 2 if latent else 4
    head = pl.BlockSpec((T, hps * MLA_QW), lambda b, h: (blk0 + b, h))
    in_specs = [head, head, pl.BlockSpec((T, LANE), lambda b, h: (blk0 + b, 0))]
    args = [q, kv, kpe]
    if latent:
        tab = pl.BlockSpec((T, LANE), lambda b, h: (0, 0))
        in_specs += [tab, tab, pl.BlockSpec((PAST, hps * MLA_QW), lambda b, h: (b, h)),
                     pl.BlockSpec((PAST, LANE), lambda b, h: (b, 0))]
        args += [rope[0], rope[1], cache[0], cache[1]]
    return pl.pallas_call(
        functools.partial(_mla_kernel, latent, T),
        grid=(nb, MLA_H // hps),
        in_specs=in_specs,
        out_specs=pl.BlockSpec((T, hps * V_HEAD), lambda b, h: (b, h)),
        out_shape=jax.ShapeDtypeStruct((nb * T, MLA_H * V_HEAD), F32),
        compiler_params=_cparams(("arbitrary", "arbitrary")),
        name="mla_lat" if latent else "mla_ctx",
    )(*args)


def _route_kernel(n_mix, *refs):
    ctx_refs = refs[:n_mix]
    lat_refs = refs[n_mix:2 * n_mix]
    w_refs = refs[2 * n_mix:3 * n_mix]
    (xc_ref, xl_ref, g1_ref, gate1_ref, g2_ref, sh2_ref, sc2_ref, wrh_ref, wrl_ref, br_ref,
     x1_ref, hx_ref, pw_ref, pwt_ref, cnt_ref, y_s, x_s) = refs[3 * n_mix:]

    def out_proj(group):
        mix_refs, x_ref = group
        y = _dot(mix_refs[0][...].astype(BF16), w_refs[0][...])
        for m in range(1, n_mix):
            y = y + _dot(mix_refs[m][...].astype(BF16), w_refs[m][...])
        y_s[...] = y
        x_s[...] = x_ref[...]

    _per_group(out_proj, (ctx_refs, xc_ref), (lat_refs, xl_ref))
    x1 = x_s[...] + gate1_ref[...] * _rms(y_s[...], g1_ref[...])
    x1_ref[...] = x1
    h2 = _rms(x1, g2_ref[...]) * (1.0 + sc2_ref[...]) + sh2_ref[...]

    logits = _mm3(_split_bf16(h2), (wrh_ref[...], wrl_ref[...])) + br_ref[...]
    lane_i = lax.broadcasted_iota(jnp.int32, (TM, LANE), 1)
    lane = lane_i.astype(F32)
    neg = jnp.float32(-jnp.inf)
    l = jnp.where(lane_i < N_EXP, logits, neg)
    sel = jnp.zeros((TM, LANE), F32)
    comb = jnp.zeros((TM, LANE), F32)
    wsum = jnp.zeros((TM, 1), F32)
    hots, wts = [], []
    for k in range(TOP_K):
        m = jnp.max(l, axis=-1, keepdims=True)
        idx = jnp.min(jnp.where(l == m, lane, float(LANE)), axis=-1, keepdims=True)
        hot = lane == idx
        if k == 0:
            m0 = m
        w = jnp.exp(m - m0)
        wsum = wsum + w
        hots.append(hot)
        wts.append(w)
        sel = sel + jnp.where(hot, 1.0, 0.0)
        comb = comb + jnp.where(hot, w, 0.0)
        l = jnp.where(hot, neg, l)
    inv = 1.0 / wsum
    comb = comb * inv

    cnt = jnp.sum(sel, axis=0, keepdims=True)
    cnt_ref[...] = cnt
    padded = jnp.floor((cnt + (SUB_ROWS - 1)) * (1.0 / SUB_ROWS)) * SUB_ROWS
    li = lax.broadcasted_iota(jnp.int32, (LANE, LANE), 0)
    lj = lax.broadcasted_iota(jnp.int32, (LANE, LANE), 1)
    before = jnp.where(li < lj, 1.0, 0.0).astype(BF16)
    start = _dot(jnp.broadcast_to(padded, (8, LANE)).astype(BF16), before)[0:1, :]
    ti = lax.broadcasted_iota(jnp.int32, (TM, TM), 0)
    tj = lax.broadcasted_iota(jnp.int32, (TM, TM), 1)
    earlier = jnp.where(tj < ti, 1.0, 0.0).astype(BF16)
    pos_full = _dot(earlier, sel.astype(BF16)) + start
    pw = jnp.zeros((TM, LANE), F32)
    for k in range(TOP_K):
        pos_k = jnp.sum(jnp.where(hots[k], pos_full, 0.0), axis=-1, keepdims=True)
        pw = pw + jnp.where(lane_i == k, pos_k, 0.0) + jnp.where(lane_i == TOP_K + k, wts[k] * inv, 0.0)
    pw_ref[...] = pw
    pwt_ref[...] = pw.T[0:8, :]

    chi, clo = _split_bf16(comb)
    ext = chi.astype(F32) + pltpu.roll(clo.astype(F32), N_EXP, axis=1)
    hx_ref[:, 0:D] = h2.astype(BF16)
    hx_ref[:, D:D_EXT] = ext.astype(BF16)


def _route(mixes, w_outs, x, norm_g, mod4, layer, w_router, b_router):
    n_mix = len(mixes)
    wr_hi, wr_lo = _split_bf16(w_router)
    row = lambda c: pl.BlockSpec((TM, c), lambda t: (t, 0))
    full = lambda a: pl.BlockSpec(a.shape, lambda t: (0, 0))
    g = lambda k: pl.BlockSpec((None, 1, D), lambda t: (k, 0, 0))
    xc, xl, first = _token_pair(x)
    in_specs = ([_ctx_spec(m[0].shape[1]) for m in mixes] + [_lat_spec(m[1].shape[1]) for m in mixes]
                + [full(w) for w in w_outs]
                + [_ctx_spec(D), _lat_spec(D, first), g(1), _mod_spec(layer, 2), g(2), _mod_spec(layer, 3),
                   _mod_spec(layer, 4), full(wr_hi), full(wr_lo), full(b_router)])
    ng = norm_g.reshape(4, 1, D)
    return pl.pallas_call(
        functools.partial(_route_kernel, n_mix),
        grid=(N_TILES,),
        in_specs=in_specs,
        out_specs=[row(D), row(D_EXT), row(LANE), pl.BlockSpec((None, 8, TM), lambda t: (t, 0, 0)),
                   pl.BlockSpec((None, 1, LANE), lambda t: (t, 0, 0))],
        out_shape=[jax.ShapeDtypeStruct((N_TOK, D), F32), jax.ShapeDtypeStruct((N_TOK, D_EXT), BF16),
                   jax.ShapeDtypeStruct((N_TOK, LANE), F32), jax.ShapeDtypeStruct((N_TILES, 8, TM), F32),
                   jax.ShapeDtypeStruct((N_TILES, 1, LANE), F32)],
        scratch_shapes=[pltpu.VMEM((TM, D), F32), pltpu.VMEM((TM, D), F32)],
        compiler_params=_cparams(("arbitrary",)),
        name="out_proj_route",
    )(*[m[0] for m in mixes], *[m[1] for m in mixes], *w_outs, xc, xl, ng, mod4, ng, mod4, mod4, wr_hi, wr_lo,
      b_router)


def _dispatch_kernel(hx_ref, pwt_ref, s_ref):
    j = pl.program_id(0)

    @pl.when(j < N_TILES)
    def _():
        rio = lax.broadcasted_iota(jnp.int32, (SORT_BLOCK, TM), 0).astype(F32)

        def one_hot(rb):
            p = jnp.zeros((SORT_BLOCK, TM), F32)
            for k in range(TOP_K):
                p = p + jnp.where(rio == pwt_ref[k:k + 1, :] - float(rb * SORT_BLOCK), 1.0, 0.0)
            return p.astype(BF16)

        nxt = one_hot(0)
        for rb in range(TILE_ROWS // SORT_BLOCK):
            p = nxt
            if rb + 1 < TILE_ROWS // SORT_BLOCK:
                nxt = one_hot(rb + 1)
            s_ref[rb * SORT_BLOCK:(rb + 1) * SORT_BLOCK, :] = _dot(p, hx_ref[...]).astype(BF16)

    @pl.when(j == N_TILES)
    def _():
        s_ref[...] = jnp.zeros((TILE_ROWS, D_EXT), BF16)


def _dispatch(hx, pwt):
    clamp = lambda j: jnp.minimum(j, N_TILES - 1)
    return pl.pallas_call(
        _dispatch_kernel,
        grid=(N_TILES + 1,),
        in_specs=[pl.BlockSpec((TM, D_EXT), lambda j: (clamp(j), 0)),
                  pl.BlockSpec((None, 8, TM), lambda j: (clamp(j), 0, 0))],
        out_specs=pl.BlockSpec((TILE_ROWS, D_EXT), lambda j: (j, 0)),
        out_shape=jax.ShapeDtypeStruct(((N_TILES + 1) * TILE_ROWS, D_EXT), BF16),
        compiler_params=_cparams(("arbitrary",)),
        name="moe_dispatch",
    )(hx, pwt)


def _expert_kernel(layer, blk_e_ref, next_e_ref, src_ref, nact_ref, tail_ref, s_hbm, wgu_hbm, bgu_ref, wd_hbm,
                   bd_ref, y_hbm, xbuf, ybuf, wgu_f, wd_f, wgu_s, wd_s, zbuf, gsem, ssem, zsem, wsem):
    i = pl.program_id(0)
    nsteps = pl.num_programs(0)
    nact = nact_ref[0]
    slot = i % 2

    def gather_copy(step, sl, s):
        src = pl.multiple_of(src_ref[step * SUB + s], SUB_ROWS)
        return pltpu.make_async_copy(s_hbm.at[pl.ds(src, SUB_ROWS)],
                                     xbuf.at[sl, pl.ds(s * SUB_ROWS, SUB_ROWS)], gsem.at[sl])

    def scatter_copy(step, sl, s):
        dst = pl.multiple_of(src_ref[step * SUB + s], SUB_ROWS)
        return pltpu.make_async_copy(ybuf.at[sl, pl.ds(s * SUB_ROWS, SUB_ROWS)],
                                     y_hbm.at[pl.ds(dst, SUB_ROWS)], ssem.at[sl])

    def gather_start(step, sl):
        for s in range(SUB):
            gather_copy(step, sl, s).start()

    def gather_wait(step, sl):
        for s in range(SUB):
            gather_copy(step, sl, s).wait()

    def scatter_start(step, sl):
        for s in range(SUB):
            scatter_copy(step, sl, s).start()

    def scatter_wait(step, sl):
        for s in range(SUB):
            scatter_copy(step, sl, s).wait()

    def spare_fill():
        copies = [pltpu.make_async_copy(zbuf, y_hbm.at[pl.ds(N_TILES * TILE_ROWS + b * SUB_ROWS, SUB_ROWS)],
                                        zsem.at[1]) for b in range(2 * SUB)]
        for cp in copies:
            cp.start()
        for cp in copies:
            cp.wait()

    def tail_fill(wait):
        def tile(j, carry):
            first = tail_ref[j]

            def block(b, c):
                dst = pl.multiple_of(first + b * SUB_ROWS, SUB_ROWS)
                cp = pltpu.make_async_copy(zbuf, y_hbm.at[pl.ds(dst, SUB_ROWS)], zsem.at[0])
                if wait:
                    cp.wait()
                else:
                    cp.start()
                return c

            return lax.fori_loop(0, ((j + 1) * TILE_ROWS - first) // SUB_ROWS, block, carry)

        lax.fori_loop(0, N_TILES, tile, 0)

    def weight_copies(e, wsl):
        return (pltpu.make_async_copy(wgu_hbm.at[layer, e], wgu_f.at[wsl], wsem.at[wsl]),
                pltpu.make_async_copy(wd_hbm.at[layer, e], wd_f.at[wsl], wsem.at[wsl]))

    @pl.when(i == 0)
    def _():
        for cp in weight_copies(blk_e_ref[0], next_e_ref[0] % 2):
            cp.start()
        gather_start(0, 0)
        zbuf[...] = jnp.zeros((SUB_ROWS, D), BF16)
        spare_fill()
        tail_fill(wait=False)

    @pl.when(i < nact)
    def _():
        gather_wait(i, slot)

        @pl.when(i >= 2)
        def _():
            scatter_wait(i - 2, slot)

        @pl.when(i + 1 < nact)
        def _():
            gather_start(i + 1, 1 - slot)

        e = blk_e_ref[i]
        e_prev = blk_e_ref[jnp.maximum(i - 1, 0)]

        @pl.when((i == 0) | (e != e_prev))
        def _():
            wsl = next_e_ref[i] % 2
            for cp in weight_copies(e, wsl):
                cp.wait()
            nxt = next_e_ref[i] // 2

            @pl.when(nxt < N_EXP)
            def _():
                for cp in weight_copies(nxt, 1 - wsl):
                    cp.start()

            wgu_s[...] = wgu_f[wsl].astype(BF16)
            wd_s[...] = wd_f[wsl].astype(BF16)

        x = xbuf[slot]
        ext = x[:, D:D_EXT].astype(F32)
        lane = lax.broadcasted_iota(jnp.int32, (MB, LANE), 1)
        wr = jnp.sum(jnp.where((lane == e) | (lane == e + N_EXP), ext, 0.0), axis=-1, keepdims=True)
        xt = x[:, 0:D]
        n_chunks = D_FF // FF_CHUNK

        def gate_up(j):
            gc = slice(j * FF_CHUNK, (j + 1) * FF_CHUNK)
            uc = slice(D_FF + j * FF_CHUNK, D_FF + (j + 1) * FF_CHUNK)
            return _dot(xt, wgu_s[:, gc]) + bgu_ref[:, gc], _dot(xt, wgu_s[:, uc]) + bgu_ref[:, uc]

        y = bd_ref[...]
        nxt = gate_up(0)
        for j in range(n_chunks):
            gate, up = nxt
            if j + 1 < n_chunks:
                nxt = gate_up(j + 1)
            gate = jnp.minimum(gate, SWIGLU_LIMIT)
            up = jnp.clip(up, -SWIGLU_LIMIT, SWIGLU_LIMIT)
            act = (up + 1.0) * gate * jax.nn.sigmoid(SWIGLU_ALPHA * gate)
            y = y + _dot(act.astype(BF16), wd_s[j * FF_CHUNK:(j + 1) * FF_CHUNK, :])
        y = y * wr
        ybuf[slot] = y.astype(BF16)
        scatter_start(i, slot)

    @pl.when(i == nsteps - 1)
    def _():
        @pl.when(nact >= 2)
        def _():
            scatter_wait(nact - 2, nact % 2)

        scatter_wait(nact - 1, (nact - 1) % 2)
        tail_fill(wait=True)


def _experts(blk_e, next_e, src, nact, tail, s_sorted, layer, w_gu, b_gu, w_dn, b_dn):
    rows = N_TILES * TILE_ROWS + 2 * MB
    per_expert = lambda c: pl.BlockSpec((None, None, 1, c), lambda i, be, ne, sr, na, tl: (layer, be[i], 0, 0))
    hbm = pl.BlockSpec(memory_space=pl.ANY)
    grid_spec = pltpu.PrefetchScalarGridSpec(
        num_scalar_prefetch=5,
        grid=(N_MACRO,),
        in_specs=[hbm, hbm, per_expert(2 * D_FF), hbm, per_expert(D)],
        out_specs=hbm,
        scratch_shapes=[pltpu.VMEM((2, MB, D_EXT), BF16), pltpu.VMEM((2, MB, D), BF16),
                        pltpu.VMEM((2, D, 2 * D_FF), F32), pltpu.VMEM((2, D_FF, D), F32),
                        pltpu.VMEM((D, 2 * D_FF), BF16), pltpu.VMEM((D_FF, D), BF16),
                        pltpu.VMEM((SUB_ROWS, D), BF16),
                        pltpu.SemaphoreType.DMA((2,)), pltpu.SemaphoreType.DMA((2,)),
                        pltpu.SemaphoreType.DMA((2,)), pltpu.SemaphoreType.DMA((2,))])
    return pl.pallas_call(
        functools.partial(_expert_kernel, layer),
        grid_spec=grid_spec,
        out_shape=jax.ShapeDtypeStruct((rows, D), BF16),
        compiler_params=_cparams(("arbitrary",), 56 * 1024 * 1024),
        name="moe_experts",
    )(blk_e, next_e, src, nact, tail, s_sorted, w_gu, b_gu.reshape(DEPTH, N_EXP, 1, 2 * D_FF), w_dn,
      b_dn.reshape(DEPTH, N_EXP, 1, D))


def _combine_kernel(y_ref, pw_ref, x1_ref, g_ref, gate_ref, *o_refs):
    lio = lax.broadcasted_iota(jnp.int32, (TM, TILE_ROWS), 1).astype(F32)
    pw = pw_ref[...]
    p = jnp.zeros((TM, TILE_ROWS), F32)
    for k in range(TOP_K):
        p = p + jnp.where(lio == pw[:, k:k + 1], 1.0, 0.0)
    f = _dot(p.astype(BF16), y_ref[...])
    x2 =x1_ref[...] + gate_ref[...] * _rms(f, g_ref[...])
    if len(o_refs) == 1:
        o_refs[0][...] = x2
    else:
        def store(o_ref):
            o_ref[...] = x2

        _per_group(store, *o_refs)


def _combine(y_sorted, pw, x1, norm_g, mod4, layer, split):
    row = lambda c: pl.BlockSpec((TM, c), lambda t: (t, 0))
    if split:
        out_specs = [_ctx_spec(D), _lat_spec(D)]
        out_shape = [jax.ShapeDtypeStruct((N_CTX, D), F32), jax.ShapeDtypeStruct((N_LAT, D), F32)]
    else:
        out_specs, out_shape = row(D), jax.ShapeDtypeStruct((N_TOK, D), F32)
    return pl.pallas_call(
        _combine_kernel,
        grid=(N_TILES,),
        in_specs=[pl.BlockSpec((TILE_ROWS, D), lambda t: (t, 0)), row(LANE), row(D),
                  pl.BlockSpec((None, 1, D), lambda t: (3, 0, 0)), _mod_spec(layer, 5)],
        out_specs=out_specs,
        out_shape=out_shape,
        compiler_params=_cparams(("arbitrary",)),
        name="moe_combine",
    )(y_sorted, pw, x1, norm_g.reshape(4, 1, D), mod4)


N_MACRO_PAD = -(-N_MACRO // 8) * 8


def _dot_exact(sel, tab):
    hi, lo = _split_bf16(tab)
    s16 = sel.astype(BF16)
    return _dot_nt(s16, hi) + _dot_nt(s16, lo)


def _tables_kernel(cnt_ref, blk_ref, src_ref, meta_ref):
    cnt = cnt_ref[...]
    nb = jnp.floor((cnt + (SUB_ROWS - 1)) * (1.0 / SUB_ROWS))
    li = lax.broadcasted_iota(jnp.int32, (LANE, LANE), 0)
    lj = lax.broadcasted_iota(jnp.int32, (LANE, LANE), 1)
    nb16 = nb.astype(BF16)
    lstart = _dot(nb16, jnp.where(li < lj, 1.0, 0.0).astype(BF16)) * SUB_ROWS
    ti = lax.broadcasted_iota(jnp.int32, (N_TILES, N_TILES), 0)
    tj = lax.broadcasted_iota(jnp.int32, (N_TILES, N_TILES), 1)
    cum_nb = _dot(jnp.where(tj <= ti, 1.0, 0.0).astype(BF16), nb16)
    nb_e = cum_nb[N_TILES - 1:N_TILES, :]
    mb_e = jnp.floor((nb_e + (SUB - 1)) * (1.0 / SUB))
    mb_hi, mb_lo = _split_bf16(jnp.broadcast_to(mb_e, (8, LANE)))
    upto = jnp.where(li <= lj, 1.0, 0.0).astype(BF16)
    mend = (_dot(mb_hi, upto) + _dot(mb_lo, upto))[0:1, :]
    lane_row = lax.broadcasted_iota(jnp.int32, (1, LANE), 1)
    nact = jnp.sum(jnp.where(lane_row == N_EXP - 1, mend, 0.0), axis=1, keepdims=True)
    mstart = mend - mb_e

    lane = lax.broadcasted_iota(jnp.int32, (N_MACRO_PAD, LANE), 1)
    i_col = lax.broadcasted_iota(jnp.int32, (N_MACRO_PAD, 1), 0).astype(F32)
    real = lane < N_EXP
    e_i = jnp.sum(jnp.where(real & (mend <= i_col), 1.0, 0.0), axis=1, keepdims=True)
    e_last = jnp.sum(jnp.where((lane_row < N_EXP) & (mend <= nact - 1.0), 1.0, 0.0), axis=1, keepdims=True)
    blk = jnp.where(i_col < nact, jnp.minimum(e_i, N_EXP - 1.0), e_last)
    hot = jnp.where(lane.astype(F32) == blk, 1.0, 0.0)
    mstart_i = jnp.sum(hot * mstart, axis=1, keepdims=True)
    nbe_i = jnp.sum(hot * nb_e, axis=1, keepdims=True)
    cum_t = _dot_exact(hot, cum_nb)
    nb_t = _dot_exact(hot, nb)
    lst_t = _dot_exact(hot, lstart)
    tile_lane = lax.broadcasted_iota(jnp.int32, (N_MACRO_PAD, N_TILES), 1).astype(F32)
    parity = i_col - 2.0 * jnp.floor(i_col * 0.5)
    src = jnp.zeros((N_MACRO_PAD, LANE), F32)
    for s in range(SUB):
        q = (i_col - mstart_i) * SUB + s
        valid = (i_col < nact) & (q < nbe_i)
        le = cum_t <= q
        j = jnp.minimum(jnp.sum(jnp.where(le, 1.0, 0.0), axis=1, keepdims=True), N_TILES - 1.0)
        prev = jnp.sum(jnp.where(le, nb_t, 0.0), axis=1, keepdims=True)
        lst = jnp.sum(jnp.where(tile_lane == j, lst_t, 0.0), axis=1, keepdims=True)
        row = j * TILE_ROWS + lst + (q - prev) * SUB_ROWS
        spare = N_TILES * TILE_ROWS + (parity * SUB + s) * SUB_ROWS
        src = jnp.where(lane == s, jnp.where(valid, row, spare), src)
    src_ref[...] = src.astype(jnp.int32)
    used = real & (mb_e > 0.0)
    lane_f = lane.astype(F32)
    rank = jnp.sum(jnp.where(used & (lane_f < blk), 1.0, 0.0), axis=1, keepdims=True)
    nxt = jnp.min(jnp.where(used & (lane_f > blk), lane_f, float(N_EXP)), axis=1, keepdims=True)
    next_code = 2.0 * nxt + (rank - 2.0 * jnp.floor(rank * 0.5))
    blk_ref[...] = jnp.where(lane == 0, blk, jnp.where(lane == 1, next_code, 0.0)).astype(jnp.int32)
    tile_col = lax.broadcasted_iota(jnp.int32, (N_TILES, 1), 0).astype(F32)
    tail = tile_col * TILE_ROWS + jnp.sum(nb, axis=1, keepdims=True) * SUB_ROWS
    lane_t = lax.broadcasted_iota(jnp.int32, (N_TILES, LANE), 1)
    meta_ref[...] = jnp.where(lane_t == 0, tail, jnp.where(lane_t == 1, nact, 0.0)).astype(jnp.int32)


def _block_tables(cnt):
    i32 = lambda r: jax.ShapeDtypeStruct((r, LANE), jnp.int32)
    blk, src, meta = pl.pallas_call(
        _tables_kernel,
        out_shape=[i32(N_MACRO_PAD), i32(N_MACRO_PAD), i32(N_TILES)],
        compiler_params=_cparams(None),
        name="moe_tables",
    )(cnt.reshape(N_TILES, LANE))
    return blk[:N_MACRO, 0], blk[:N_MACRO, 1], src[:N_MACRO, :SUB].reshape(-1), meta[0:1, 1].reshape(1), meta[:, 0]


def _moe(mixes, w_outs, x, norm_g, mod4, layer, w_router, b_router, w_gu, b_gu, w_dn, b_dn, split=False):
    wr = jnp.pad(w_router, ((0, 0), (0, LANE - N_EXP)))
    br = jnp.pad(b_router, (0, LANE - N_EXP)).reshape(1, LANE)
    x1, hx, pw, pwt, cnt = _route(mixes, w_outs, x, norm_g, mod4, layer, wr, br)
    blk_e, next_e, src, nact, tail = _block_tables(cnt)
    s_sorted = _dispatch(hx, pwt)
    y_sorted = _experts(blk_e, next_e, src, nact, tail, s_sorted, layer, w_gu, b_gu, w_dn, b_dn)
    return _combine(y_sorted, pw, x1, norm_g, mod4, layer, split)


def kernel(x_prompt, x_sample, c, c_ctx, state_delta, cache_diff_k, cache_diff_v, cache_mla_ckv, cache_mla_kpe, w_mod, b_mod, norm_g, w_router, b_router, w_gate_up, b_gate_up, w_down, b_down, w_in_even, conv_w, a_log, dt_bias, gdn_norm_g, diff_lambda, diff_norm_g, w_out_even, w_in_odd, q_norm_g, w_uq, kv_norm_g, w_ukv, w_out_odd):
    x = (x_prompt.reshape(N_CTX, D), x_sample.reshape(N_LAT, D))
    cond =jnp.concatenate([c_ctx[None, :], c, jnp.zeros((16 - 1 - B_LAT, D), F32)], axis=0)
    mod4 = _modulation(cond, w_mod, b_mod).reshape(DEPTH, 16, 1, N_MOD * D)
    cos64, sin64 = _rope_tables()
    rope2 = (jnp.tile(cos64, (1, 2)), jnp.tile(sin64, (1, 2)))
    rope1 = (jnp.pad(cos64, ((0, 0), (0, 64))), jnp.pad(sin64, ((0, 0), (0, 64))))

    we = w_in_even[0]
    w_cat = jnp.concatenate([we[:, 0:2048], we[:, 2064:3600], we[:, 2048:2064],
                             jnp.zeros((D, LANE - 16), F32)], axis=1).astype(BF16)
    qkv, z, qd, kd, vd, ab = _in_even(x, norm_g[0, 0:1], mod4, 0, w_cat)
    gn = gdn_norm_g[0].reshape(1, GDN_DV)
    oa_c, new_state = _gdn(qkv, z, ab, conv_w[0], a_log[0], dt_bias[0], gn, None, B_CTX, T_CTX, 0)
    oa_l = _gdn(qkv, z, ab, conv_w[0], a_log[0], dt_bias[0], gn, state_delta[:, 0], B_LAT, T_LAT, N_CTX)
    lam_init = 0.8 - 0.6 * math.exp(-0.3 * 0)
    dgn = diff_norm_g[0].reshape(1, 2 * DIFF_DH)
    od_c = _diff_attn(qd, kd, vd, diff_lambda[0], dgn, lam_init, B_CTX, T_CTX, 0)
    cache_dk = cache_diff_k[:, 0].reshape(B_LAT, PAST, DIFF_H * 2 * DIFF_DH)
    cache_dv = cache_diff_v[:, 0].reshape(B_LAT, PAST, DIFF_H * 2 * DIFF_DH)
    od_l = _diff_attn(qd, kd, vd, diff_lambda[0], dgn, lam_init, B_LAT, T_LAT, N_CTX,
                      cache=(cache_dk, cache_dv), rope=rope2)
    wo = w_out_even[0].astype(BF16)
    x = _moe([(oa_c, oa_l), (od_c, od_l)], [wo[:512], wo[512:]], x, norm_g[0], mod4, 0, w_router[0], b_router[0],
             w_gate_up, b_gate_up, w_down, b_down)
    new_dk = kd[:N_CTX].reshape(B_CTX, 1, T_CTX, DIFF_H, 2, DIFF_DH)
    new_dv = vd[:N_CTX].reshape(B_CTX, 1, T_CTX, DIFF_H, 2 * DIFF_DH)
    new_state = new_state.reshape(B_CTX, 1, 2, GDN_H, GDN_DK, GDN_DV)

    w_odd = jnp.pad(w_in_odd[0], ((0, 0), (0, ODD_W - (Q_LORA + KV_LORA + QK_ROPE)))).astype(BF16)
    wuq = w_uq[0].reshape(Q_LORA, MLA_H, QK_NOPE + QK_ROPE)
    wuq = jnp.pad(wuq, ((0, 0), (0, 0), (0, MLA_QW - (QK_NOPE + QK_ROPE)))).reshape(Q_LORA, MLA_H * MLA_QW)
    wukv = w_ukv[0].astype(BF16)
    q, kv, ckv, kpe = _in_odd(x, norm_g[1, 0:1], mod4, 1, w_odd, q_norm_g[0].reshape(1, Q_LORA),
                              wuq.astype(BF16), kv_norm_g[0].reshape(1, KV_LORA), wukv)
    om_c = _mla_attn(q, kv, kpe, B_CTX, T_CTX, 0)
    cache_kv = _matmul(cache_mla_ckv[:, 0].reshape(B_LAT * PAST, KV_LORA), wukv)
    cache_kpe = jnp.pad(cache_mla_kpe[:, 0].reshape(B_LAT * PAST, QK_ROPE), ((0, 0), (0, LANE - QK_ROPE)))
    om_l = _mla_attn(q, kv, kpe, B_LAT, T_LAT, N_CTX, cache=(cache_kv, cache_kpe), rope=rope1)
    x_ctx, x_lat = _moe([(om_c, om_l)], [w_out_odd[0].astype(BF16)], x, norm_g[1], mod4, 1, w_router[1],
                        b_router[1], w_gate_up, b_gate_up, w_down, b_down, split=True)
    new_ckv = ckv[:N_CTX].reshape(B_CTX, 1, T_CTX, KV_LORA)
    new_kpe = kpe[:N_CTX, :QK_ROPE].reshape(B_CTX, 1, T_CTX, QK_ROPE)

    return (x_ctx.reshape(B_CTX, T_CTX, D), x_lat.reshape(B_LAT, T_LAT, D),
            new_state, new_dk, new_dv, new_ckv, new_kpe)
```

```python
import functools
import math

import jax
import jax.numpy as jnp
from jax import lax
from jax.experimental import pallas as pl
from jax.experimental.pallas import tpu as pltpu

F32 = jnp.float32
BF16 = jnp.bfloat16
HIGHEST = lax.Precision.HIGHEST

D = 1024
B_CTX, T_CTX = 16, 256
B_LAT, T_LAT = 8, 1024
PAST = 256
N_CTX = B_CTX * T_CTX
N_LAT = B_LAT * T_LAT
N_TOK = N_CTX + N_LAT
DEPTH = 2
N_MOD = 6
GRID_W = 64
GDN_H, GDN_DK, GDN_DV = 4, 128, 128
CONV_W = 5
CHUNK = 64
DIFF_H, DIFF_DH = 4, 64
MLA_H, Q_LORA, KV_LORA, QK_NOPE, QK_ROPE, V_HEAD = 8, 256, 256, 128, 64, 128
N_EXP, TOP_K, D_FF = 32, 4, 1024
SWIGLU_LIMIT, SWIGLU_ALPHA = 7.0, 1.702
ROPE_BASE = 10000.0
EPS = 1e-6

LANE = 128
TM = 256
N_TILES = N_TOK // TM
SUB_ROWS = 16
TILE_ROWS = TOP_K * TM + N_EXP * SUB_ROWS
MB = 256
SUB = MB // SUB_ROWS
FF_CHUNK = 256
SORT_BLOCK = 256
GDN_TRIP_ROWS = 256
N_MACRO = -(-((TOP_K * N_TOK) // SUB_ROWS + N_TILES * N_EXP + N_EXP * (SUB - 1)) // SUB)
D_EXT = D + LANE
VMEM_LIMIT = 48 * 1024 * 1024


def _cparams(sem, vmem=VMEM_LIMIT):
    return pltpu.CompilerParams(dimension_semantics=sem, vmem_limit_bytes=vmem)


N_CTX_TILES = N_CTX // TM


def _mod_row(t):
    return jnp.where(t < N_CTX_TILES, 0, 1 + (t - N_CTX_TILES) // (T_LAT // TM))


def _ctx_spec(c):
    return pl.BlockSpec((TM, c), lambda t: (jnp.minimum(t, N_CTX_TILES - 1), 0))


def _lat_spec(c, first=0):
    return pl.BlockSpec((TM, c), lambda t: (jnp.maximum(t - N_CTX_TILES, 0) + first, 0))


def _token_pair(x):
    return (x[0], x[1], 0) if isinstance(x, tuple) else (x, x, N_CTX_TILES)


def _per_group(body, ctx_ref, lat_ref):
    t = pl.program_id(0)
    pl.when(t < N_CTX_TILES)(lambda: body(ctx_ref))
    pl.when(t >= N_CTX_TILES)(lambda: body(lat_ref))


def _rms(x, g):
    return x * lax.rsqrt(jnp.mean(x * x, axis=-1, keepdims=True) + EPS) * g


def _silu(x):
    return x * jax.nn.sigmoid(x)


def _dot(a, b):
    return jnp.dot(a, b, preferred_element_type=F32)


def _dot_nt(a, b):
    return lax.dot_general(a, b, (((1,), (1,)), ((), ())), preferred_element_type=F32)


def _mod_kernel(c_ref, w_ref, b_ref, o_ref):
    s = _silu(c_ref[...])
    o_ref[...] = _dot(s.astype(BF16), w_ref[...].astype(BF16)) + b_ref[...]


def _modulation(cond, w_mod, b_mod):
    nc = 1536
    return pl.pallas_call(
        _mod_kernel,
        grid=(DEPTH, N_MOD * D // nc),
        in_specs=[pl.BlockSpec((16, D), lambda i, j: (0, 0)),
                  pl.BlockSpec((None, D, nc), lambda i, j: (i, 0, j)),
                  pl.BlockSpec((None, 1, nc), lambda i, j: (i, 0, j))],
        out_specs=pl.BlockSpec((None, 16, nc), lambda i, j: (i, 0, j)),
        out_shape=jax.ShapeDtypeStruct((DEPTH, 16, N_MOD * D), F32),
        compiler_params=_cparams(("arbitrary", "arbitrary")),
        name="modulation",
    )(cond, w_mod, b_mod.reshape(DEPTH, 1, N_MOD * D))


def _mod_spec(layer, k):
    return pl.BlockSpec((None, None, 1, D), lambda t: (layer, _mod_row(t), 0, k))


EVEN_W = 2 * 512 + 512 + 512 + 3 * 512 + LANE


def _in_even_kernel(xc_ref, xl_ref, g_ref, sh_ref, sc_ref, w_ref, qkv_ref, z_ref, qd_ref, kd_ref, vd_ref, ab_ref):
    def body(x_ref):
        h = _rms(x_ref[...], g_ref[...]) * (1.0 + sc_ref[...]) + sh_ref[...]
        y = _dot(h.astype(BF16), w_ref[...])
        qkv_ref[...] = y[:, 0:1536]
        z_ref[...] = y[:, 1536:2048]
        qd_ref[...] = y[:, 2048:2560]
        kd_ref[...] = y[:, 2560:3072]
        vd_ref[...] = y[:, 3072:3584]
        ab_ref[...] = y[:, 3584:3712]

    _per_group(body, xc_ref, xl_ref)


def _in_even(x, g, mod4, layer, w):
    xc, xl, first = _token_pair(x)
    row = lambda c: pl.BlockSpec((TM, c), lambda t: (t, 0))
    shp = lambda c: jax.ShapeDtypeStruct((N_TOK, c), F32)
    return pl.pallas_call(
        _in_even_kernel,
        grid=(N_TILES,),
        in_specs=[_ctx_spec(D), _lat_spec(D, first), pl.BlockSpec((1, D), lambda t: (0, 0)), _mod_spec(layer, 0),
                  _mod_spec(layer, 1), pl.BlockSpec((D, EVEN_W), lambda t: (0, 0))],
        out_specs=[row(1536), row(512), row(512), row(512), row(512), row(LANE)],
        out_shape=[shp(1536), shp(512), shp(512), shp(512), shp(512), shp(LANE)],
        compiler_params=_cparams(("arbitrary",)),
        name="in_proj_even",
    )(xc, xl, g, mod4, mod4, w)


def _split_bf16(a):
    hi = a.astype(BF16)
    lo = (a - hi.astype(F32)).astype(BF16)
    return hi, lo


def _mm3(a, b):
    m = a[0].shape[0]
    top = _dot(jnp.concatenate([a[0], a[1]], axis=0), b[0])
    return top[:m] + (top[m:] + _dot(a[0], b[1]))


def _unit_tri_inverses(ls, eye, blk, bd):
    sp = _split_bf16
    d8 = [jnp.where(blk(8), l, 0.0) for l in ls]
    d8s = [sp(d) for d in d8]
    x2s = [sp(_mm3(d, bd(d))) for d in d8s]
    x2r = [bd(x) for x in x2s]
    ts = [eye - d for d in d8]
    ts = [t + _mm3(sp(t), xr) for t, xr in zip(ts, x2r)]
    x4 = [_mm3(x, xr) for x, xr in zip(x2s, x2r)]
    ts = [t + _mm3(sp(t), bd(sp(x))) for t, x in zip(ts, x4)]
    for s in (8, 16, 32):
        offs = [sp(jnp.where(blk(2 * s) & jnp.logical_not(blk(s)), l, 0.0)) for l in ls]
        tss = [sp(t) for t in ts]
        ys = [_mm3(o, bd(t)) for o, t in zip(offs, tss)]
        ts = [t - _mm3(t2, bd(sp(y))) for t, t2, y in zip(ts, tss, ys)]
    return ts


def _softplus(x):
    return jnp.maximum(x, 0.0) + jnp.log1p(jnp.exp(-jnp.abs(x)))


def _gdn_kernel(has_init, T, alog_ref, dtb_ref, qkv_ref, cw_ref, z_ref, ab_ref, gn_ref, *rest):
    if has_init:
        s0_ref, o_ref = rest[:2]
        st_ref = None
    else:
        o_ref, st_ref = rest[:2]
        s0_ref = None
    xp, qs, ks, vs, abt_s, uo_s, w_s, qg_s, kgt_s, in_s, gl_s, st_s = rest[2:]
    n = T // CHUNK
    H = GDN_H

    xp[0:8, :] = jnp.zeros((8, LANE), F32)
    xp[T + 8:T + 16, :] = jnp.zeros((8, LANE), F32)

    def l2n(x):
        return x * lax.rsqrt(jnp.sum(x * x, axis=-1, keepdims=True) + EPS)

    for blk in range(3 * H):
        cols = slice(blk * LANE, (blk + 1) * LANE)
        dst = slice((blk % H) * LANE, (blk % H + 1) * LANE)
        xp[8:T + 8, :] = qkv_ref[:, cols]
        w = cw_ref[:, cols]
        acc = xp[6:T + 6, :] * w[0:1, :]
        for i in range(1, CONV_W):
            acc = acc + xp[6 + i:T + 6 + i, :] * w[i:i + 1, :]
        y = _silu(acc)
        if blk < H:
            qs[:, dst] = l2n(y) * (GDN_DK ** -0.5)
        elif blk < 2 * H:
            ks[:, dst] = l2n(y)
        else:
            vs[:, dst] = y
    abt_s[...] = ab_ref[...].T[0:16, :]

    ii = lax.broadcasted_iota(jnp.int32, (CHUNK, LANE), 0)
    lane = lax.broadcasted_iota(jnp.int32, (CHUNK, LANE), 1)
    jm = lane & (CHUNK - 1)
    fwd = lane < CHUNK
    order = jnp.where(fwd, ii - jm, jm - ii)
    incl = order >= 0
    strict = order > 0
    incl_t = order <= 0
    eye = jnp.where(order == 0, 1.0, 0.0).astype(F32)
    blk_mask = lambda s: (ii // s) == (jm // s)
    zeros_c = jnp.zeros((CHUNK, LANE), F32)
    zeros_c16 = jnp.zeros((CHUNK, LANE), BF16)

    keep_f = jnp.where(fwd, 1.0, 0.0).astype(BF16)
    keep_b = jnp.where(fwd, 0.0, 1.0).astype(BF16)

    def bd(parts):
        return tuple(jnp.concatenate([a * keep_f, a * keep_b], axis=0) for a in parts)

    lane_row = lax.broadcasted_iota(jnp.int32, (1, LANE), 1)
    sub_col = lax.broadcasted_iota(jnp.int32, (16, 1), 0)
    a_lane, dt_lane = jnp.zeros((1, LANE), F32), jnp.zeros((1, LANE), F32)
    a_col, dt_col = jnp.zeros((16, 1), F32), jnp.zeros((16, 1), F32)
    for d in range(2):
        for h in range(H):
            a_lane = jnp.where(lane_row == d * H + h, alog_ref[d, h], a_lane)
            dt_lane = jnp.where(lane_row == d * H + h, dtb_ref[d, h], dt_lane)
            a_col = jnp.where(sub_col == d * H + h, alog_ref[d, h], a_col)
            dt_col = jnp.where(sub_col == d * H + h, dtb_ref[d, h], dt_col)
    nega_lane, nega_col = -jnp.exp(a_lane), -jnp.exp(a_col)

    def prep(p, carry):
        rbase = pl.multiple_of(p * GDN_TRIP_ROWS, GDN_TRIP_ROWS)
        g_rows = nega_col * _softplus(abt_s[:, pl.ds(rbase, GDN_TRIP_ROWS)] + dt_col)
        sysm = []
        for cc in range(GDN_TRIP_ROWS // CHUNK):
            r0 = pl.multiple_of(rbase + cc * CHUNK, CHUNK)
            abc = ab_ref[pl.ds(r0, CHUNK), :]
            g_cols = nega_lane * _softplus(abc + dt_lane)
            b_cols = jax.nn.sigmoid(abc)
            for h in range(H):
                hs = slice(h * LANE, (h + 1) * LANE)
                tok = slice(cc * CHUNK, (cc + 1) * CHUNK)
                sysm.append(dict(
                    c=(GDN_TRIP_ROWS // CHUNK) * p + cc, rows=pl.ds(r0, CHUNK), h=h,
                    qc=qs[pl.ds(r0, CHUNK), hs], kc=ks[pl.ds(r0, CHUNK), hs], vc=vs[pl.ds(r0, CHUNK), hs],
                    g_col=jnp.where(fwd, g_cols[:, h:h + 1], g_cols[:, H + h:H + h + 1]),
                    g_row=jnp.concatenate([g_rows[h:h + 1, tok], g_rows[H + h:H + h + 1, tok]], axis=1),
                    beta=(b_cols[:, 2 * H + h:2 * H + h + 1], b_cols[:, 3 * H + h:3 * H + h + 1])))
        for s in sysm:
            k16 = s["kc"].astype(BF16)
            kk16 = jnp.concatenate([k16, k16], axis=0)
            kq = _dot_nt(jnp.concatenate([k16, s["qc"].astype(BF16)], axis=0), kk16)
            s["kk"], s["qk"] = kq[:CHUNK], kq[CHUNK:]
        ls = []
        for s in sysm:
            m = jnp.where(incl, s["g_row"], 0.0)
            s["gcf"] = jnp.sum(jnp.where(fwd, m, 0.0), axis=1, keepdims=True)
            s["gcb"] = jnp.sum(jnp.where(fwd, 0.0, m), axis=1, keepdims=True)
            gc_col = jnp.where(fwd, s["gcf"], s["gcb"])
            gc_row = jnp.sum(jnp.where(incl_t, s["g_col"], 0.0), axis=0, keepdims=True)
            decay = jnp.where(incl, jnp.exp(jnp.where(incl, gc_col - gc_row, 0.0)), 0.0)
            beta = jnp.where(fwd, s["beta"][0], s["beta"][1])
            ls.append(jnp.where(strict, s["kk"] * beta * decay, 0.0))
            s["intra"] = s["qk"] * decay
        ts = _unit_tri_inverses(ls, eye, blk_mask, bd)
        for s, t in zip(sysm, ts):
            h, c, rows = s["h"], s["c"], s["rows"]
            egs = (jnp.exp(s["gcf"]), jnp.exp(s["gcb"]))
            r_f = jnp.concatenate([s["vc"] * s["beta"][0], s["kc"] * (s["beta"][0] * egs[0]), zeros_c, zeros_c], axis=1)
            r_b = jnp.concatenate([zeros_c, zeros_c, s["vc"] * s["beta"][1], s["kc"] * (s["beta"][1] * egs[1])], axis=1)
            uw = _dot(t.astype(BF16), jnp.concatenate([r_f, r_b], axis=0).astype(BF16))
            g_last = (s["gcf"][CHUNK - 1:CHUNK, :], s["gcb"][0:1, :])
            kgs = []
            for d in range(2):
                k = d * H + h
                uo_s[k, rows, :] = uw[:, 2 * d * LANE:(2 * d + 1) * LANE]
                w_s[k, rows, :] = uw[:, (2 * d + 1) * LANE:(2 * d + 2) * LANE].astype(BF16)
                qg_s[k, rows, :] = (s["qc"] * egs[d]).astype(BF16)
                gc = s["gcf"] if d == 0 else s["gcb"]
                kgs.append(s["kc"] * jnp.exp(g_last[d] - gc))
                gl_s[k, c] = jnp.broadcast_to(jnp.exp(g_last[d]), (8, LANE))
            in_s[h, c] = s["intra"].astype(BF16)
            kgt_s[h, c] = jnp.concatenate(kgs, axis=0).T.astype(BF16)
        return carry

    lax.fori_loop(0, T // GDN_TRIP_ROWS, prep, 0)

    for d in range(2):
        for h in range(H):
            st_s[d * H + h] = s0_ref[d, h] if has_init else jnp.zeros((GDN_DK, GDN_DV), F32)

    def scan(i, carry):
        cs = (i, n - 1 - i)
        ks_ = range(2 * H)
        c_of = [cs[k // H] for k in ks_]
        rows = [pl.ds(pl.multiple_of(c * CHUNK, CHUNK), CHUNK) for c in c_of]
        s16 = [st_s[k].astype(BF16) for k in ks_]
        wq = [_dot(jnp.concatenate([w_s[k, rows[k], :], qg_s[k, rows[k], :]], axis=0), s16[k]) for k in ks_]
        vn = [(uo_s[k, rows[k], :] - wq[k][:CHUNK]).astype(BF16) for k in ks_]
        vpad = [jnp.concatenate([vn[k], zeros_c16] if k < H else [zeros_c16, vn[k]], axis=0) for k in ks_]
        ik = [_dot(jnp.concatenate([in_s[k % H, c_of[k]], kgt_s[k % H, c_of[k]]], axis=0), vpad[k]) for k in ks_]
        for k in ks_:
            uo_s[k, rows[k], :] = wq[k][CHUNK:] + ik[k][:CHUNK]
            st_s[k] = st_s[k] * gl_s[k, c_of[k]][0:1, :] + ik[k][CHUNK:]
        return carry

    lax.fori_loop(0, n, scan, 0)
    for h in range(H):
        hs = slice(h * LANE, (h + 1) * LANE)
        o_ref[:, hs] = _rms(uo_s[h] + uo_s[H + h], gn_ref[...]) * _silu(z_ref[:, hs])
        if st_ref is not None:
            st_ref[0, h] = st_s[h]
            st_ref[1, h] = st_s[H + h]


def _gdn(qkv, z, ab, conv_w, a_log, dt_bias, gn, s0, nb, T, row0):
    n = T // CHUNK
    blk0 = row0 // T
    has_init = s0 is not None
    H, W = GDN_H, GDN_H * LANE
    smem = pl.BlockSpec(memory_space=pltpu.SMEM)
    in_specs = [smem, smem,
                pl.BlockSpec((T, 3 * W), lambda b: (blk0 + b, 0), pipeline_mode=pl.Buffered(1)),
                pl.BlockSpec((CONV_W, 3 * W), lambda b: (0, 0)),
                pl.BlockSpec((T, W), lambda b: (blk0 + b, 0), pipeline_mode=pl.Buffered(1)),
                pl.BlockSpec((T, LANE), lambda b: (blk0 + b, 0)),
                pl.BlockSpec((1, LANE), lambda b: (0, 0))]
    args = [a_log, dt_bias, qkv, conv_w, z, ab, gn]
    st_spec = pl.BlockSpec((None, 2, H, GDN_DK, GDN_DV), lambda b: (b, 0, 0, 0, 0))
    o_spec = pl.BlockSpec((T, W), lambda b: (b, 0))
    o_shape = jax.ShapeDtypeStruct((nb * T, W), F32)
    if has_init:
        in_specs.append(st_spec)
        args.append(s0)
        out_specs, out_shape = o_spec, o_shape
    else:
        out_specs = [o_spec, st_spec]
        out_shape = [o_shape, jax.ShapeDtypeStruct((nb, 2, H, GDN_DK, GDN_DV), F32)]
    scratch = [pltpu.VMEM((T + 16, LANE), F32), pltpu.VMEM((T, W), F32), pltpu.VMEM((T, W), F32),
               pltpu.VMEM((T, W), F32), pltpu.VMEM((16, T), F32),
               pltpu.VMEM((2 * H, T, LANE), F32), pltpu.VMEM((2 * H, T, LANE), BF16),
               pltpu.VMEM((2 * H, T, LANE), BF16), pltpu.VMEM((H, n, LANE, LANE), BF16),
               pltpu.VMEM((H, n, CHUNK, LANE), BF16), pltpu.VMEM((2 * H, n, 8, LANE), F32),
               pltpu.VMEM((2 * H, GDN_DK, GDN_DV), F32)]
    return pl.pallas_call(
        functools.partial(_gdn_kernel, has_init, T),
        grid=(nb,),
        in_specs=in_specs, out_specs=out_specs, out_shape=out_shape, scratch_shapes=scratch,
        compiler_params=_cparams(("arbitrary",), 56 * 1024 * 1024),
        name="gdn_lat" if has_init else "gdn_ctx",
    )(*args)


def _rope(x, cos, sin_signed):
    lane = lax.broadcasted_iota(jnp.int32, x.shape, 1)
    up = pltpu.roll(x, LANE - 16, axis=1)
    dn = pltpu.roll(x, 16, axis=1)
    rot = jnp.where((lane % 32) < 16, up, dn)
    return x * cos + rot * sin_signed


def _rope_tables():
    rows = T_LAT // GRID_W
    r = jnp.repeat(jnp.arange(rows, dtype=F32), GRID_W)
    col = jnp.tile(jnp.arange(GRID_W, dtype=F32), rows)
    quarter = 16
    inv = ROPE_BASE ** (-jnp.arange(quarter, dtype=F32) / quarter)
    ar = r[:, None] * inv
    ac = col[:, None] * inv
    ang = jnp.concatenate([ar, ar, ac, ac], axis=-1)
    sign = jnp.tile(jnp.concatenate([-jnp.ones(16, F32), jnp.ones(16, F32)]), 2)
    cos, sin = jnp.cos(ang), jnp.sin(ang) * sign
    return cos, sin


LOG2E = math.log2(math.e)


def _attend(parts, values):
    m = functools.reduce(jnp.maximum, [jnp.max(s, axis=-1, keepdims=True) for s in parts])
    es = [jnp.exp2(s - m) for s in parts]
    den = functools.reduce(lambda a, b: a + b, [jnp.sum(e, axis=-1, keepdims=True) for e in es])
    o = functools.reduce(lambda a, b: a + b, [_dot(e.astype(BF16), v) for e, v in zip(es, values)])
    return o * (1.0 / den)


def _diff_kernel(latent, T, lam_init, q_ref, k_ref, v_ref, lam_ref, gn_ref, *rest):
    if latent:
        cos_ref, sin_ref, ck_ref, cv_ref, o_ref = rest
    else:
        (o_ref,) = rest
    scale = DIFF_DH ** -0.5
    lp = lam_ref[...]
    lam = (jnp.exp(jnp.sum(lp[0:1] * lp[1:2], axis=-1, keepdims=True))
           - jnp.exp(jnp.sum(lp[2:3] * lp[3:4], axis=-1, keepdims=True)) + lam_init)
    tq = 256
    lane = lax.broadcasted_iota(jnp.int32, (tq, LANE), 1)
    for hh in range(q_ref.shape[1] // LANE):
        cs = slice(hh * LANE, (hh + 1) * LANE)
        k = k_ref[:, cs]
        if latent:
            k = _rope(k, cos_ref[...], sin_ref[...])
        k16 = k.astype(BF16)
        v16 = v_ref[:, cs].astype(BF16)
        if latent:
            ck16 = ck_ref[:, cs].astype(BF16)
            cv16 = cv_ref[:, cs].astype(BF16)
        def scores(qb):
            rows = slice(qb * tq, (qb + 1) * tq)
            q = q_ref[rows, cs]
            if latent:
                q = _rope(q, cos_ref[rows, :], sin_ref[rows, :])
            qs = q * (scale * LOG2E)
            q2 = jnp.concatenate([jnp.where((lane // DIFF_DH) == m, qs, 0.0) for m in range(2)], axis=0)
            q2 = q2.astype(BF16)
            return [_dot_nt(q2, ck16), _dot_nt(q2, k16)] if latent else [_dot_nt(q2, k16)]

        nxt = scores(0)
        for qb in range(T // tq):
            parts = nxt
            if qb + 1 < T // tq:
                nxt = scores(qb + 1)
            o2 = _attend(parts, [cv16, v16] if latent else [v16])
            o = o2[:tq] - lam * o2[tq:]
            o_ref[qb * tq:(qb + 1) * tq, cs] = _rms(o, gn_ref[...]) * (1.0 - lam_init)


def _diff_attn(qd, kd, vd, lam_p, gn, lam_init, nb, T, row0, cache=None, rope=None):
    blk0 = row0 // T
    latent = cache is not None
    hps = 2 if latent else DIFF_H
    hcol = pl.BlockSpec((T, hps * LANE), lambda b, h: (blk0 + b, h))
    in_specs = [hcol, hcol, hcol, pl.BlockSpec((4, DIFF_DH), lambda b, h: (0, 0)),
                pl.BlockSpec((1, LANE), lambda b, h: (0, 0))]
    args = [qd, kd, vd, lam_p, gn]
    if latent:
        tab = pl.BlockSpec((T, LANE), lambda b, h: (0, 0))
        cch = pl.BlockSpec((None, PAST, hps * LANE), lambda b, h: (b, 0, h))
        in_specs += [tab, tab, cch, cch]
        args += [rope[0], rope[1], cache[0], cache[1]]
    return pl.pallas_call(
        functools.partial(_diff_kernel, latent, T, lam_init),
        grid=(nb, DIFF_H // hps),
        in_specs=in_specs,
        out_specs=pl.BlockSpec((T, hps * LANE), lambda b, h: (b, h)),
        out_shape=jax.ShapeDtypeStruct((nb * T, DIFF_H * 2 * DIFF_DH), F32),
        compiler_params=_cparams(("arbitrary", "arbitrary")),
        name="diff_lat" if latent else "diff_ctx",
    )(*args)


ODD_W = Q_LORA + KV_LORA + LANE
MLA_QW = 2 * LANE


def _in_odd_kernel(x_ref, g_ref, sh_ref, sc_ref, w_ref, qg_ref, wuq_ref, kvg_ref, wukv_ref,
                   q_ref, kv_ref, ckv_ref, kpe_ref):
    h = _rms(x_ref[...], g_ref[...]) * (1.0 + sc_ref[...]) + sh_ref[...]
    y = _dot(h.astype(BF16), w_ref[...])
    qn = _rms(y[:, 0:Q_LORA], qg_ref[...])
    q_ref[...] = _dot(qn.astype(BF16), wuq_ref[...])
    ckv = _rms(y[:, Q_LORA:Q_LORA + KV_LORA], kvg_ref[...])
    ckv_ref[...] = ckv
    kv_ref[...] = _dot(ckv.astype(BF16), wukv_ref[...])
    kpe_ref[...] = y[:, Q_LORA + KV_LORA:ODD_W]


def _in_odd(x, g, mod4, layer, w, qg, wuq, kvg, wukv):
    row = lambda c: pl.BlockSpec((TM, c), lambda t: (t, 0))
    full = lambda a: pl.BlockSpec(a.shape, lambda t: (0, 0))
    shp = lambda c: jax.ShapeDtypeStruct((N_TOK, c), F32)
    return pl.pallas_call(
        _in_odd_kernel,
        grid=(N_TILES,),
        in_specs=[row(D), full(g), _mod_spec(layer, 0), _mod_spec(layer, 1), full(w), full(qg), full(wuq),
                  full(kvg), full(wukv)],
        out_specs=[row(MLA_H * MLA_QW), row(MLA_H * MLA_QW), row(KV_LORA), row(LANE)],
        out_shape=[shp(MLA_H * MLA_QW), shp(MLA_H * MLA_QW), shp(KV_LORA), shp(LANE)],
        compiler_params=_cparams(("arbitrary",)),
        name="in_proj_odd",
    )(x, g, mod4, mod4, w, qg, wuq, kvg, wukv)


def _mm_kernel(x_ref, w_ref, o_ref):
    o_ref[...] = _dot(x_ref[...].astype(BF16), w_ref[...])


def _matmul(x, w):
    m, k = x.shape
    n = w.shape[1]
    return pl.pallas_call(
        _mm_kernel,
        grid=(m // TM,),
        in_specs=[pl.BlockSpec((TM, k), lambda i: (i, 0)), pl.BlockSpec((k, n), lambda i: (0, 0))],
        out_specs=pl.BlockSpec((TM, n), lambda i: (i, 0)),
        out_shape=jax.ShapeDtypeStruct((m, n), F32),
        compiler_params=_cparams(("arbitrary",)),
        name="cache_kv_proj",
    )(x, w)


def _mla_kernel(latent, T, q_ref, kv_ref, kpe_ref, *rest):
    if latent:
        cos_ref, sin_ref, ckv_ref, ckpe_ref, o_ref = rest
    else:
        (o_ref,) = rest
    scale = (QK_NOPE + QK_ROPE) ** -0.5
    kpe = kpe_ref[...]
    if latent:
        kpe = _rope(kpe, cos_ref[...], sin_ref[...])
    tq = 256
    for hh in range(q_ref.shape[1] // MLA_QW):
        cs = slice(hh * MLA_QW, (hh + 1) * MLA_QW)
        kv = kv_ref[:, cs]
        v16 = kv[:, LANE:].astype(BF16)
        if latent:
            ckv = ckv_ref[:, cs]
            ck16 = jnp.concatenate([ckv[:, :LANE], ckpe_ref[...]], axis=1).astype(BF16)
            cv16 = ckv[:, LANE:].astype(BF16)
        k16 = jnp.concatenate([kv[:, :LANE], kpe], axis=1).astype(BF16)

        def scores(qb):
            rows = slice(qb * tq, (qb + 1) * tq)
            q = q_ref[rows, cs]
            qp = q[:, LANE:]
            if latent:
                qp = _rope(qp, cos_ref[rows, :], sin_ref[rows, :])
            q16 = (jnp.concatenate([q[:, :LANE], qp], axis=1) * (scale * LOG2E)).astype(BF16)
            return [_dot_nt(q16, ck16), _dot_nt(q16, k16)] if latent else [_dot_nt(q16, k16)]

        nxt = scores(0)
        for qb in range(T // tq):
            parts = nxt
            if qb + 1 < T // tq:
                nxt = scores(qb + 1)
            o_ref[qb * tq:(qb + 1) * tq, hh * V_HEAD:(hh + 1) * V_HEAD] = _attend(
                parts, [cv16, v16] if latent else [v16])


def _mla_attn(q, kv, kpe, nb, T, row0, cache=None, rope=None):
    blk0 = row0 // T
    latent = cache is not None
    hps = 2 if latent else 4
    head = pl.BlockSpec((T, hps * MLA_QW), lambda b, h: (blk0 + b, h))
    in_specs = [head, head, pl.BlockSpec((T, LANE), lambda b, h: (blk0 + b, 0))]
    args = [q, kv, kpe]
    if latent:
        tab = pl.BlockSpec((T, LANE), lambda b, h: (0, 0))
        in_specs += [tab, tab, pl.BlockSpec((PAST, hps * MLA_QW), lambda b, h: (b, h)),
                     pl.BlockSpec((PAST, LANE), lambda b, h: (b, 0))]
        args += [rope[0], rope[1], cache[0], cache[1]]
    return pl.pallas_call(
        functools.partial(_mla_kernel, latent, T),
        grid=(nb, MLA_H // hps),
        in_specs=in_specs,
        out_specs=pl.BlockSpec((T, hps * V_HEAD), lambda b, h: (b, h)),
        out_shape=jax.ShapeDtypeStruct((nb * T, MLA_H * V_HEAD), F32),
        compiler_params=_cparams(("arbitrary", "arbitrary")),
        name="mla_lat" if latent else "mla_ctx",
    )(*args)


def _route_kernel(n_mix, *refs):
    ctx_refs = refs[:n_mix]
    lat_refs = refs[n_mix:2 * n_mix]
    w_refs = refs[2 * n_mix:3 * n_mix]
    (xc_ref, xl_ref, g1_ref, gate1_ref, g2_ref, sh2_ref, sc2_ref, wrh_ref, wrl_ref, br_ref,
     x1_ref, hx_ref, pw_ref, pwt_ref, cnt_ref, y_s, x_s) = refs[3 * n_mix:]

    def out_proj(group):
        mix_refs, x_ref = group
        y = _dot(mix_refs[0][...].astype(BF16), w_refs[0][...])
        for m in range(1, n_mix):
            y = y + _dot(mix_refs[m][...].astype(BF16), w_refs[m][...])
        y_s[...] = y
        x_s[...] = x_ref[...]

    _per_group(out_proj, (ctx_refs, xc_ref), (lat_refs, xl_ref))
    x1 = x_s[...] + gate1_ref[...] * _rms(y_s[...], g1_ref[...])
    x1_ref[...] = x1
    h2 = _rms(x1, g2_ref[...]) * (1.0 + sc2_ref[...]) + sh2_ref[...]

    logits = _mm3(_split_bf16(h2), (wrh_ref[...], wrl_ref[...])) + br_ref[...]
    lane_i = lax.broadcasted_iota(jnp.int32, (TM, LANE), 1)
    lane = lane_i.astype(F32)
    neg = jnp.float32(-jnp.inf)
    l = jnp.where(lane_i < N_EXP, logits, neg)
    sel = jnp.zeros((TM, LANE), F32)
    comb = jnp.zeros((TM, LANE), F32)
    wsum = jnp.zeros((TM, 1), F32)
    hots, wts = [], []
    for k in range(TOP_K):
        m = jnp.max(l, axis=-1, keepdims=True)
        idx = jnp.min(jnp.where(l == m, lane, float(LANE)), axis=-1, keepdims=True)
        hot = lane == idx
        if k == 0:
            m0 = m
        w = jnp.exp(m - m0)
        wsum = wsum + w
        hots.append(hot)
        wts.append(w)
        sel = sel + jnp.where(hot, 1.0, 0.0)
        comb = comb + jnp.where(hot, w, 0.0)
        l = jnp.where(hot, neg, l)
    inv = 1.0 / wsum
    comb = comb * inv

    cnt = jnp.sum(sel, axis=0, keepdims=True)
    cnt_ref[...] = cnt
    padded = jnp.floor((cnt + (SUB_ROWS - 1)) * (1.0 / SUB_ROWS)) * SUB_ROWS
    li = lax.broadcasted_iota(jnp.int32, (LANE, LANE), 0)
    lj = lax.broadcasted_iota(jnp.int32, (LANE, LANE), 1)
    before = jnp.where(li < lj, 1.0, 0.0).astype(BF16)
    start = _dot(jnp.broadcast_to(padded, (8, LANE)).astype(BF16), before)[0:1, :]
    ti = lax.broadcasted_iota(jnp.int32, (TM, TM), 0)
    tj = lax.broadcasted_iota(jnp.int32, (TM, TM), 1)
    earlier = jnp.where(tj < ti, 1.0, 0.0).astype(BF16)
    pos_full = _dot(earlier, sel.astype(BF16)) + start
    pw = jnp.zeros((TM, LANE), F32)
    for k in range(TOP_K):
        pos_k = jnp.sum(jnp.where(hots[k], pos_full, 0.0), axis=-1, keepdims=True)
        pw = pw + jnp.where(lane_i == k, pos_k, 0.0) + jnp.where(lane_i == TOP_K + k, wts[k] * inv, 0.0)
    pw_ref[...] = pw
    pwt_ref[...] = pw.T[0:8, :]

    chi, clo = _split_bf16(comb)
    ext = chi.astype(F32) + pltpu.roll(clo.astype(F32), N_EXP, axis=1)
    hx_ref[:, 0:D] = h2.astype(BF16)
    hx_ref[:, D:D_EXT] = ext.astype(BF16)


def _route(mixes, w_outs, x, norm_g, mod4, layer, w_router, b_router):
    n_mix = len(mixes)
    wr_hi, wr_lo = _split_bf16(w_router)
    row = lambda c: pl.BlockSpec((TM, c), lambda t: (t, 0))
    full = lambda a: pl.BlockSpec(a.shape, lambda t: (0, 0))
    g = lambda k: pl.BlockSpec((None, 1, D), lambda t: (k, 0, 0))
    xc, xl, first = _token_pair(x)
    in_specs = ([_ctx_spec(m[0].shape[1]) for m in mixes] + [_lat_spec(m[1].shape[1]) for m in mixes]
                + [full(w) for w in w_outs]
                + [_ctx_spec(D), _lat_spec(D, first), g(1), _mod_spec(layer, 2), g(2), _mod_spec(layer, 3),
                   _mod_spec(layer, 4), full(wr_hi), full(wr_lo), full(b_router)])
    ng = norm_g.reshape(4, 1, D)
    return pl.pallas_call(
        functools.partial(_route_kernel, n_mix),
        grid=(N_TILES,),
        in_specs=in_specs,
        out_specs=[row(D), row(D_EXT), row(LANE), pl.BlockSpec((None, 8, TM), lambda t: (t, 0, 0)),
                   pl.BlockSpec((None, 1, LANE), lambda t: (t, 0, 0))],
        out_shape=[jax.ShapeDtypeStruct((N_TOK, D), F32), jax.ShapeDtypeStruct((N_TOK, D_EXT), BF16),
                   jax.ShapeDtypeStruct((N_TOK, LANE), F32), jax.ShapeDtypeStruct((N_TILES, 8, TM), F32),
                   jax.ShapeDtypeStruct((N_TILES, 1, LANE), F32)],
        scratch_shapes=[pltpu.VMEM((TM, D), F32), pltpu.VMEM((TM, D), F32)],
        compiler_params=_cparams(("arbitrary",)),
        name="out_proj_route",
    )(*[m[0] for m in mixes], *[m[1] for m in mixes], *w_outs, xc, xl, ng, mod4, ng, mod4, mod4, wr_hi, wr_lo,
      b_router)


def _dispatch_kernel(hx_ref, pwt_ref, s_ref):
    j = pl.program_id(0)

    @pl.when(j < N_TILES)
    def _():
        rio = lax.broadcasted_iota(jnp.int32, (SORT_BLOCK, TM), 0).astype(F32)

        def one_hot(rb):
            p = jnp.zeros((SORT_BLOCK, TM), F32)
            for k in range(TOP_K):
                p = jnp.where(rio == pwt_ref[k:k + 1, :] - float(rb * SORT_BLOCK), 1.0, p)
            return p.astype(BF16)

        nxt = one_hot(0)
        for rb in range(TILE_ROWS // SORT_BLOCK):
            p = nxt
            if rb + 1 < TILE_ROWS // SORT_BLOCK:
                nxt = one_hot(rb + 1)
            s_ref[rb * SORT_BLOCK:(rb + 1) * SORT_BLOCK, :] = _dot(p, hx_ref[...]).astype(BF16)

    @pl.when(j == N_TILES)
    def _():
        s_ref[...] = jnp.zeros((TILE_ROWS, D_EXT), BF16)


def _dispatch(hx, pwt):
    clamp = lambda j: jnp.minimum(j, N_TILES - 1)
    return pl.pallas_call(
        _dispatch_kernel,
        grid=(N_TILES + 1,),
        in_specs=[pl.BlockSpec((TM, D_EXT), lambda j: (clamp(j), 0)),
                  pl.BlockSpec((None, 8, TM), lambda j: (clamp(j), 0, 0))],
        out_specs=pl.BlockSpec((TILE_ROWS, D_EXT), lambda j: (j, 0)),
        out_shape=jax.ShapeDtypeStruct(((N_TILES + 1) * TILE_ROWS, D_EXT), BF16),
        compiler_params=_cparams(("arbitrary",)),
        name="moe_dispatch",
    )(hx, pwt)


def _expert_kernel(layer, blk_e_ref, next_e_ref, src_ref, nact_ref, tail_ref, s_hbm, wgu_hbm, bgu_ref, wd_hbm,
                   bd_ref, y_hbm, xbuf, ybuf, wgu_f, wd_f, wgu_s, wd_s, zbuf, gsem, ssem, zsem, wsem):
    i = pl.program_id(0)
    nsteps = pl.num_programs(0)
    nact = nact_ref[0]
    slot = i % 2

    def gather_copy(step, sl, s):
        src = pl.multiple_of(src_ref[step * SUB + s], SUB_ROWS)
        return pltpu.make_async_copy(s_hbm.at[pl.ds(src, SUB_ROWS)],
                                     xbuf.at[sl, pl.ds(s * SUB_ROWS, SUB_ROWS)], gsem.at[sl])

    def scatter_copy(step, sl, s):
        dst = pl.multiple_of(src_ref[step * SUB + s], SUB_ROWS)
        return pltpu.make_async_copy(ybuf.at[sl, pl.ds(s * SUB_ROWS, SUB_ROWS)],
                                     y_hbm.at[pl.ds(dst, SUB_ROWS)], ssem.at[sl])

    def gather_start(step, sl):
        for s in range(SUB):
            gather_copy(step, sl, s).start()

    def gather_wait(step, sl):
        for s in range(SUB):
            gather_copy(step, sl, s).wait()

    def scatter_start(step, sl):
        for s in range(SUB):
            scatter_copy(step, sl, s).start()

    def scatter_wait(step, sl):
        for s in range(SUB):
            scatter_copy(step, sl, s).wait()

    def spare_fill():
        copies = [pltpu.make_async_copy(zbuf, y_hbm.at[pl.ds(N_TILES * TILE_ROWS + b * SUB_ROWS, SUB_ROWS)],
                                        zsem.at[1]) for b in range(2 * SUB)]
        for cp in copies:
            cp.start()
        for cp in copies:
            cp.wait()

    def tail_fill(wait):
        def tile(j, carry):
            first = tail_ref[j]

            def block(b, c):
                dst = pl.multiple_of(first + b * SUB_ROWS, SUB_ROWS)
                cp = pltpu.make_async_copy(zbuf, y_hbm.at[pl.ds(dst, SUB_ROWS)], zsem.at[0])
                if wait:
                    cp.wait()
                else:
                    cp.start()
                return c

            return lax.fori_loop(0, ((j + 1) * TILE_ROWS - first) // SUB_ROWS, block, carry)

        lax.fori_loop(0, N_TILES, tile, 0)

    def weight_copies(e, wsl):
        return (pltpu.make_async_copy(wgu_hbm.at[layer, e], wgu_f.at[wsl], wsem.at[wsl]),
                pltpu.make_async_copy(wd_hbm.at[layer, e], wd_f.at[wsl], wsem.at[wsl]))

    @pl.when(i == 0)
    def _():
        for cp in weight_copies(blk_e_ref[0], next_e_ref[0] % 2):
            cp.start()
        gather_start(0, 0)
        zbuf[...] = jnp.zeros((SUB_ROWS, D), BF16)
        spare_fill()
        tail_fill(wait=False)

    @pl.when(i < nact)
    def _():
        gather_wait(i, slot)

        @pl.when(i >= 2)
        def _():
            scatter_wait(i - 2, slot)

        @pl.when(i + 1 < nact)
        def _():
            gather_start(i + 1, 1 - slot)

        e = blk_e_ref[i]
        e_prev = blk_e_ref[jnp.maximum(i - 1, 0)]

        @pl.when((i == 0) | (e != e_prev))
        def _():
            wsl = next_e_ref[i] % 2
            for cp in weight_copies(e, wsl):
                cp.wait()
            nxt = next_e_ref[i] // 2

            @pl.when(nxt < N_EXP)
            def _():
                for cp in weight_copies(nxt, 1 - wsl):
                    cp.start()

            wgu_s[...] = wgu_f[wsl].astype(BF16)
            wd_s[...] = wd_f[wsl].astype(BF16)

        x = xbuf[slot]
        ext = x[:, D:D_EXT].astype(F32)
        lane = lax.broadcasted_iota(jnp.int32, (MB, LANE), 1)
        wr = jnp.sum(jnp.where((lane == e) | (lane == e + N_EXP), ext, 0.0), axis=-1, keepdims=True)
        xt = x[:, 0:D]
        n_chunks = D_FF // FF_CHUNK

        def gate_up(j):
            gc = slice(j * FF_CHUNK, (j + 1) * FF_CHUNK)
            uc = slice(D_FF + j * FF_CHUNK, D_FF + (j + 1) * FF_CHUNK)
            return _dot(xt, wgu_s[:, gc]) + bgu_ref[:, gc], _dot(xt, wgu_s[:, uc]) + bgu_ref[:, uc]

        y = bd_ref[...]
        nxt = gate_up(0)
        for j in range(n_chunks):
            gate, up = nxt
            if j + 1 < n_chunks:
                nxt = gate_up(j + 1)
            gate = jnp.minimum(gate, SWIGLU_LIMIT)
            up = jnp.clip(up, -SWIGLU_LIMIT, SWIGLU_LIMIT)
            act = (up + 1.0) * gate * jax.nn.sigmoid(SWIGLU_ALPHA * gate)
            y = y + _dot(act.astype(BF16), wd_s[j * FF_CHUNK:(j + 1) * FF_CHUNK, :])
        y = y * wr
        ybuf[slot] = y.astype(BF16)
        scatter_start(i, slot)

    @pl.when(i == nsteps - 1)
    def _():
        @pl.when(nact >= 2)
        def _():
            scatter_wait(nact - 2, nact % 2)

        scatter_wait(nact - 1, (nact - 1) % 2)
        tail_fill(wait=True)


def _experts(blk_e, next_e, src, nact, tail, s_sorted, layer, w_gu, b_gu, w_dn, b_dn):
    rows = N_TILES * TILE_ROWS + 2 * MB
    per_expert = lambda c: pl.BlockSpec((None, None, 1, c), lambda i, be, ne, sr, na, tl: (layer, be[i], 0, 0))
    hbm = pl.BlockSpec(memory_space=pl.ANY)
    grid_spec = pltpu.PrefetchScalarGridSpec(
        num_scalar_prefetch=5,
        grid=(N_MACRO,),
        in_specs=[hbm, hbm, per_expert(2 * D_FF), hbm, per_expert(D)],
        out_specs=hbm,
        scratch_shapes=[pltpu.VMEM((2, MB, D_EXT), BF16), pltpu.VMEM((2, MB, D), BF16),
                        pltpu.VMEM((2, D, 2 * D_FF), F32), pltpu.VMEM((2, D_FF, D), F32),
                        pltpu.VMEM((D, 2 * D_FF), BF16), pltpu.VMEM((D_FF, D), BF16),
                        pltpu.VMEM((SUB_ROWS, D), BF16),
                        pltpu.SemaphoreType.DMA((2,)), pltpu.SemaphoreType.DMA((2,)),
                        pltpu.SemaphoreType.DMA((2,)), pltpu.SemaphoreType.DMA((2,))])
    return pl.pallas_call(
        functools.partial(_expert_kernel, layer),
        grid_spec=grid_spec,
        out_shape=jax.ShapeDtypeStruct((rows, D), BF16),
        compiler_params=_cparams(("arbitrary",), 56 * 1024 * 1024),
        name="moe_experts",
    )(blk_e, next_e, src, nact, tail, s_sorted, w_gu, b_gu.reshape(DEPTH, N_EXP, 1, 2 * D_FF), w_dn,
      b_dn.reshape(DEPTH, N_EXP, 1, D))


def _combine_kernel(y_ref, pw_ref, x1_ref, g_ref, gate_ref, *o_refs):
    lio = lax.broadcasted_iota(jnp.int32, (TM, TILE_ROWS), 1).astype(F32)
    pw = pw_ref[...]
    p = jnp.zeros((TM, TILE_ROWS), F32)
    for k in range(TOP_K):
        p = jnp.where(lio == pw[:, k:k + 1], 1.0, p)
    f = _dot(p.astype(BF16), y_ref[...])
    x2 =x1_ref[...] + gate_ref[...] * _rms(f, g_ref[...])
    if len(o_refs) == 1:
        o_refs[0][...] = x2
    else:
        def store(o_ref):
            o_ref[...] = x2

        _per_group(store, *o_refs)


def _combine(y_sorted, pw, x1, norm_g, mod4, layer, split):
    row = lambda c: pl.BlockSpec((TM, c), lambda t: (t, 0))
    if split:
        out_specs = [_ctx_spec(D), _lat_spec(D)]
        out_shape = [jax.ShapeDtypeStruct((N_CTX, D), F32), jax.ShapeDtypeStruct((N_LAT, D), F32)]
    else:
        out_specs, out_shape = row(D), jax.ShapeDtypeStruct((N_TOK, D), F32)
    return pl.pallas_call(
        _combine_kernel,
        grid=(N_TILES,),
        in_specs=[pl.BlockSpec((TILE_ROWS, D), lambda t: (t, 0)), row(LANE), row(D),
                  pl.BlockSpec((None, 1, D), lambda t: (3, 0, 0)), _mod_spec(layer, 5)],
        out_specs=out_specs,
        out_shape=out_shape,
        compiler_params=_cparams(("arbitrary",)),
        name="moe_combine",
    )(y_sorted, pw, x1, norm_g.reshape(4, 1, D), mod4)


N_MACRO_PAD = -(-N_MACRO // 8) * 8


def _dot_exact(sel, tab):
    hi, lo = _split_bf16(tab)
    s16 = sel.astype(BF16)
    return _dot_nt(s16, hi) + _dot_nt(s16, lo)


def _tables_kernel(cnt_ref, blk_ref, src_ref, meta_ref):
    cnt = cnt_ref[...]
    nb = jnp.floor((cnt + (SUB_ROWS - 1)) * (1.0 / SUB_ROWS))
    li = lax.broadcasted_iota(jnp.int32, (LANE, LANE), 0)
    lj = lax.broadcasted_iota(jnp.int32, (LANE, LANE), 1)
    nb16 = nb.astype(BF16)
    lstart = _dot(nb16, jnp.where(li < lj, 1.0, 0.0).astype(BF16)) * SUB_ROWS
    ti = lax.broadcasted_iota(jnp.int32, (N_TILES, N_TILES), 0)
    tj = lax.broadcasted_iota(jnp.int32, (N_TILES, N_TILES), 1)
    cum_nb = _dot(jnp.where(tj <= ti, 1.0, 0.0).astype(BF16), nb16)
    nb_e = cum_nb[N_TILES - 1:N_TILES, :]
    mb_e = jnp.floor((nb_e + (SUB - 1)) * (1.0 / SUB))
    mb_hi, mb_lo = _split_bf16(jnp.broadcast_to(mb_e, (8, LANE)))
    upto = jnp.where(li <= lj, 1.0, 0.0).astype(BF16)
    mend = (_dot(mb_hi, upto) + _dot(mb_lo, upto))[0:1, :]
    lane_row = lax.broadcasted_iota(jnp.int32, (1, LANE), 1)
    nact = jnp.sum(jnp.where(lane_row == N_EXP - 1, mend, 0.0), axis=1, keepdims=True)
    mstart = mend - mb_e

    lane = lax.broadcasted_iota(jnp.int32, (N_MACRO_PAD, LANE), 1)
    i_col = lax.broadcasted_iota(jnp.int32, (N_MACRO_PAD, 1), 0).astype(F32)
    real = lane < N_EXP
    e_i = jnp.sum(jnp.where(real & (mend <= i_col), 1.0, 0.0), axis=1, keepdims=True)
    e_last = jnp.sum(jnp.where((lane_row < N_EXP) & (mend <= nact - 1.0), 1.0, 0.0), axis=1, keepdims=True)
    blk = jnp.where(i_col < nact, jnp.minimum(e_i, N_EXP - 1.0), e_last)
    hot = jnp.where(lane.astype(F32) == blk, 1.0, 0.0)
    mstart_i = jnp.sum(hot * mstart, axis=1, keepdims=True)
    nbe_i = jnp.sum(hot * nb_e, axis=1, keepdims=True)
    cum_t = _dot_exact(hot, cum_nb)
    nb_t = _dot_exact(hot, nb)
    lst_t = _dot_exact(hot, lstart)
    tile_lane = lax.broadcasted_iota(jnp.int32, (N_MACRO_PAD, N_TILES), 1).astype(F32)
    parity = i_col - 2.0 * jnp.floor(i_col * 0.5)
    src = jnp.zeros((N_MACRO_PAD, LANE), F32)
    for s in range(SUB):
        q = (i_col - mstart_i) * SUB + s
        valid = (i_col < nact) & (q < nbe_i)
        le = cum_t <= q
        j = jnp.minimum(jnp.sum(jnp.where(le, 1.0, 0.0), axis=1, keepdims=True), N_TILES - 1.0)
        prev = jnp.sum(jnp.where(le, nb_t, 0.0), axis=1, keepdims=True)
        lst = jnp.sum(jnp.where(tile_lane == j, lst_t, 0.0), axis=1, keepdims=True)
        row = j * TILE_ROWS + lst + (q - prev) * SUB_ROWS
        spare = N_TILES * TILE_ROWS + (parity * SUB + s) * SUB_ROWS
        src = jnp.where(lane == s, jnp.where(valid, row, spare), src)
    src_ref[...] = src.astype(jnp.int32)
    used = real & (mb_e > 0.0)
    lane_f = lane.astype(F32)
    rank = jnp.sum(jnp.where(used & (lane_f < blk), 1.0, 0.0), axis=1, keepdims=True)
    nxt = jnp.min(jnp.where(used & (lane_f > blk), lane_f, float(N_EXP)), axis=1, keepdims=True)
    next_code = 2.0 * nxt + (rank - 2.0 * jnp.floor(rank * 0.5))
    blk_ref[...] = jnp.where(lane == 0, blk, jnp.where(lane == 1, next_code, 0.0)).astype(jnp.int32)
    tile_col = lax.broadcasted_iota(jnp.int32, (N_TILES, 1), 0).astype(F32)
    tail = tile_col * TILE_ROWS + jnp.sum(nb, axis=1, keepdims=True) * SUB_ROWS
    lane_t = lax.broadcasted_iota(jnp.int32, (N_TILES, LANE), 1)
    meta_ref[...] = jnp.where(lane_t == 0, tail, jnp.where(lane_t == 1, nact, 0.0)).astype(jnp.int32)


def _block_tables(cnt):
    i32 = lambda r: jax.ShapeDtypeStruct((r, LANE), jnp.int32)
    blk, src, meta = pl.pallas_call(
        _tables_kernel,
        out_shape=[i32(N_MACRO_PAD), i32(N_MACRO_PAD), i32(N_TILES)],
        compiler_params=_cparams(None),
        name="moe_tables",
    )(cnt.reshape(N_TILES, LANE))
    return blk[:N_MACRO, 0], blk[:N_MACRO, 1], src[:N_MACRO, :SUB].reshape(-1), meta[0:1, 1].reshape(1), meta[:, 0]


def _moe(mixes, w_outs, x, norm_g, mod4, layer, w_router, b_router, w_gu, b_gu, w_dn, b_dn, split=False):
    wr = jnp.pad(w_router, ((0, 0), (0, LANE - N_EXP)))
    br = jnp.pad(b_router, (0, LANE - N_EXP)).reshape(1, LANE)
    x1, hx, pw, pwt, cnt = _route(mixes, w_outs, x, norm_g, mod4, layer, wr, br)
    blk_e, next_e, src, nact, tail = _block_tables(cnt)
    s_sorted = _dispatch(hx, pwt)
    y_sorted = _experts(blk_e, next_e, src, nact, tail, s_sorted, layer, w_gu, b_gu, w_dn, b_dn)
    return _combine(y_sorted, pw, x1, norm_g, mod4, layer, split)


def kernel(x_prompt, x_sample, c, c_ctx, state_delta, cache_diff_k, cache_diff_v, cache_mla_ckv, cache_mla_kpe, w_mod, b_mod, norm_g, w_router, b_router, w_gate_up, b_gate_up, w_down, b_down, w_in_even, conv_w, a_log, dt_bias, gdn_norm_g, diff_lambda, diff_norm_g, w_out_even, w_in_odd, q_norm_g, w_uq, kv_norm_g, w_ukv, w_out_odd):
    x = (x_prompt.reshape(N_CTX, D), x_sample.reshape(N_LAT, D))
    cond =jnp.concatenate([c_ctx[None, :], c, jnp.zeros((16 - 1 - B_LAT, D), F32)], axis=0)
    mod4 = _modulation(cond, w_mod, b_mod).reshape(DEPTH, 16, 1, N_MOD * D)
    cos64, sin64 = _rope_tables()
    rope2 = (jnp.tile(cos64, (1, 2)), jnp.tile(sin64, (1, 2)))
    rope1 = (jnp.pad(cos64, ((0, 0), (0, 64))), jnp.pad(sin64, ((0, 0), (0, 64))))

    we = w_in_even[0]
    w_cat = jnp.concatenate([we[:, 0:2048], we[:, 2064:3600], we[:, 2048:2064],
                             jnp.zeros((D, LANE - 16), F32)], axis=1).astype(BF16)
    qkv, z, qd, kd, vd, ab = _in_even(x, norm_g[0, 0:1], mod4, 0, w_cat)
    gn = gdn_norm_g[0].reshape(1, GDN_DV)
    oa_c, new_state = _gdn(qkv, z, ab, conv_w[0], a_log[0], dt_bias[0], gn, None, B_CTX, T_CTX, 0)
    oa_l = _gdn(qkv, z, ab, conv_w[0], a_log[0], dt_bias[0], gn, state_delta[:, 0], B_LAT, T_LAT, N_CTX)
    lam_init = 0.8 - 0.6 * math.exp(-0.3 * 0)
    dgn = diff_norm_g[0].reshape(1, 2 * DIFF_DH)
    od_c = _diff_attn(qd, kd, vd, diff_lambda[0], dgn, lam_init, B_CTX, T_CTX, 0)
    cache_dk = cache_diff_k[:, 0].reshape(B_LAT, PAST, DIFF_H * 2 * DIFF_DH)
    cache_dv = cache_diff_v[:, 0].reshape(B_LAT, PAST, DIFF_H * 2 * DIFF_DH)
    od_l = _diff_attn(qd, kd, vd, diff_lambda[0], dgn, lam_init, B_LAT, T_LAT, N_CTX,
                      cache=(cache_dk, cache_dv), rope=rope2)
    wo = w_out_even[0].astype(BF16)
    x = _moe([(oa_c, oa_l), (od_c, od_l)], [wo[:512], wo[512:]], x, norm_g[0], mod4, 0, w_router[0], b_router[0],
             w_gate_up, b_gate_up, w_down, b_down)
    new_dk = kd[:N_CTX].reshape(B_CTX, 1, T_CTX, DIFF_H, 2, DIFF_DH)
    new_dv = vd[:N_CTX].reshape(B_CTX, 1, T_CTX, DIFF_H, 2 * DIFF_DH)
    new_state = new_state.reshape(B_CTX, 1, 2, GDN_H, GDN_DK, GDN_DV)

    w_odd = jnp.pad(w_in_odd[0], ((0, 0), (0, ODD_W - (Q_LORA + KV_LORA + QK_ROPE)))).astype(BF16)
    wuq = w_uq[0].reshape(Q_LORA, MLA_H, QK_NOPE + QK_ROPE)
    wuq = jnp.pad(wuq, ((0, 0), (0, 0), (0, MLA_QW - (QK_NOPE + QK_ROPE)))).reshape(Q_LORA, MLA_H * MLA_QW)
    wukv = w_ukv[0].astype(BF16)
    q, kv, ckv, kpe = _in_odd(x, norm_g[1, 0:1], mod4, 1, w_odd, q_norm_g[0].reshape(1, Q_LORA),
                              wuq.astype(BF16), kv_norm_g[0].reshape(1, KV_LORA), wukv)
    om_c = _mla_attn(q, kv, kpe, B_CTX, T_CTX, 0)
    cache_kv = _matmul(cache_mla_ckv[:, 0].reshape(B_LAT * PAST, KV_LORA), wukv)
    cache_kpe = jnp.pad(cache_mla_kpe[:, 0].reshape(B_LAT * PAST, QK_ROPE), ((0, 0), (0, LANE - QK_ROPE)))
    om_l = _mla_attn(q, kv, kpe, B_LAT, T_LAT, N_CTX, cache=(cache_kv, cache_kpe), rope=rope1)
    x_ctx, x_lat = _moe([(om_c, om_l)], [w_out_odd[0].astype(BF16)], x, norm_g[1], mod4, 1, w_router[1],
                        b_router[1], w_gate_up, b_gate_up, w_down, b_down, split=True)
    new_ckv = ckv[:N_CTX].reshape(B_CTX, 1, T_CTX, KV_LORA)
    new_kpe = kpe[:N_CTX, :QK_ROPE].reshape(B_CTX, 1, T_CTX, QK_ROPE)

    return (x_ctx.reshape(B_CTX, T_CTX, D), x_lat.reshape(B_LAT, T_LAT, D),
            new_state, new_dk, new_dv, new_ckv, new_kpe)
```
